```python
import math
import jax, jax.numpy as jnp
from jax import lax
import numpy as np

D_MODEL = 1024
BATCH = 4
SEQ = 4096
DEPTH = 2

N_EVEN = (DEPTH + 1) // 2
N_ODD = DEPTH // 2
NORM_EPS = 1e-6

SSM_WIDTH = 512
SSM_GROUP = 16
SSM_GROUPS = SSM_WIDTH // SSM_GROUP
SSM_STATE = 64
DT_MIN = 1e-3
DT_MAX = 1e-1
SGU_WIDTH = 512
SGU_HEADS = 8
SGU_HEAD_DIM = SGU_WIDTH // SGU_HEADS
SGU_CHUNK = 128
EVEN_IN = SSM_WIDTH + 2 * SGU_WIDTH
EVEN_MIX = SSM_WIDTH + SGU_WIDTH
CONV_CH = 512
CONV_TAPS = 31
MLA_HEADS = 8
MLA_Q_RANK = 256
MLA_KV_RANK = 128
MLA_NOPE = 64
MLA_ROPE = 32
MLA_V = 64
ROPE_THETA = 10000.0
ATTN_BLOCK = 128
ODD_IN = 2 * CONV_CH + MLA_Q_RANK + MLA_KV_RANK + MLA_ROPE
ODD_MIX = CONV_CH + MLA_HEADS * MLA_V
FF_DENSE = 4096
N_EXPERTS = 8
TOP_K = 2
FF_EXPERT = 3584

kernel_name = 'hybrid_s5_sgu_conformer_mla_moe'


def rms_norm(x, g):
    x32 = x.astype(jnp.float32)
    y = x32 * lax.rsqrt(jnp.mean(x32 * x32, axis=-1, keepdims=True) + NORM_EPS)
    return (y * g.astype(jnp.float32)).astype(x.dtype)


def layer_norm(x, g, b):
    x32 = x.astype(jnp.float32)
    mu = jnp.mean(x32, axis=-1, keepdims=True)
    xc = x32 - mu
    y = xc * lax.rsqrt(jnp.mean(xc * xc, axis=-1, keepdims=True) + NORM_EPS)
    return (y * g.astype(jnp.float32) + b.astype(jnp.float32)).astype(x.dtype)


def swiglu(x, w_gate, w_up, w_down):
    return (jax.nn.silu(x @ w_gate) * (x @ w_up)) @ w_down


def s5_mixer(u, lam_re, lam_im, log_dt, b_re, b_im, c_re, c_im, d, w_glu):
    f32 = jnp.float32
    bsz, seq, _ = u.shape
    u32 = u.astype(f32)
    ug = u32.reshape(bsz, seq, SSM_GROUPS, SSM_GROUP)
    lr = jnp.minimum(lam_re.astype(f32), -1e-4)
    li = lam_im.astype(f32)
    dt = jnp.exp(log_dt.astype(f32))[:, None]
    mag = jnp.exp(lr * dt)
    lb_re = mag * jnp.cos(li * dt)
    lb_im = mag * jnp.sin(li * dt)
    den = lr * lr + li * li
    nr = lb_re - 1.0
    coef_re = (nr * lr + lb_im * li) / den
    coef_im = (lb_im * lr - nr * li) / den
    br = b_re.astype(f32)
    bi = b_im.astype(f32)
    bb_re = coef_re[..., None] * br - coef_im[..., None] * bi
    bb_im = coef_re[..., None] * bi + coef_im[..., None] * br
    bu_re = jnp.einsum('gpn,blgn->blgp', bb_re, ug)
    bu_im = jnp.einsum('gpn,blgn->blgp', bb_im, ug)
    a_re = jnp.broadcast_to(lb_re, bu_re.shape)
    a_im = jnp.broadcast_to(lb_im, bu_im.shape)

    def combine(e1, e2):
        a1r, a1i, b1r, b1i = e1
        a2r, a2i, b2r, b2i = e2
        return (a2r * a1r - a2i * a1i,
                a2r * a1i + a2i * a1r,
                a2r * b1r - a2i * b1i + b2r,
                a2r * b1i + a2i * b1r + b2i)

    _, _, h_re, h_im = lax.associative_scan(combine, (a_re, a_im, bu_re, bu_im), axis=1)
    y = (jnp.einsum('gnp,blgp->blgn', c_re.astype(f32), h_re)
         - jnp.einsum('gnp,blgp->blgn', c_im.astype(f32), h_im))
    y = y.reshape(bsz, seq, SSM_WIDTH) + d.astype(f32) * u32
    y = jax.nn.gelu(y).astype(u.dtype)
    return y * jax.nn.sigmoid(y @ w_glu)


def sgu_mixer(u, v, ln_g, ln_b, w_s, b_s):
    bsz, seq, _ = u.shape
    u = jax.nn.gelu(u)
    v = layer_norm(jax.nn.gelu(v), ln_g, ln_b)
    n_chunks = seq // SGU_CHUNK
    v = v.reshape(bsz, n_chunks, SGU_CHUNK, SGU_HEADS, SGU_HEAD_DIM)
    causal = jnp.tril(jnp.ones((SGU_CHUNK, SGU_CHUNK), dtype=bool))
    w = jnp.where(causal[None], w_s, jnp.zeros((), w_s.dtype))
    s = jnp.einsum('hts,bcshd->bcthd', w, v) + b_s.T[None, None, :, :, None]
    return u * s.reshape(bsz, seq, SGU_WIDTH)


def conv_mixer(z, w, b, ln_g, ln_b):
    a, g = jnp.split(z, 2, axis=-1)
    h = a * jax.nn.sigmoid(g)
    h = lax.conv_general_dilated(h, w[:, None, :], window_strides=(1,),
                                 padding=[(CONV_TAPS - 1, 0)],
                                 dimension_numbers=('NWC', 'WIO', 'NWC'),
                                 feature_group_count=CONV_CH) + b
    h = layer_norm(h, ln_g, ln_b)
    return jax.nn.silu(h)


def rope_tables(seq):
    inv = 1.0 / (ROPE_THETA ** (jnp.arange(0, MLA_ROPE, 2, dtype=jnp.float32) / MLA_ROPE))
    ang = jnp.arange(seq, dtype=jnp.float32)[:, None] * inv[None, :]
    return jnp.cos(ang), jnp.sin(ang)


def apply_rope(x, cos, sin):
    half = x.shape[-1] // 2
    x32 = x.astype(jnp.float32)
    x1 = x32[..., :half]
    x2 = x32[..., half:]
    return jnp.concatenate([x1 * cos - x2 * sin, x2 * cos + x1 * sin], axis=-1).astype(x.dtype)


def mla_mixer(c_q, c_kv, k_rope, q_norm_g, w_uq, kv_norm_g, w_ukv, cos, sin):
    bsz, seq, _ = c_q.shape
    dk = MLA_NOPE + MLA_ROPE
    q = (rms_norm(c_q, q_norm_g) @ w_uq).reshape(bsz, seq, MLA_HEADS, dk)
    q_rope = apply_rope(q[..., MLA_NOPE:], cos[None, :, None, :], sin[None, :, None, :])
    q = jnp.concatenate([q[..., :MLA_NOPE], q_rope], axis=-1)
    kv = (rms_norm(c_kv, kv_norm_g) @ w_ukv).reshape(bsz, seq, MLA_HEADS, MLA_NOPE + MLA_V)
    k_r = apply_rope(k_rope, cos[None], sin[None])
    k = jnp.concatenate([kv[..., :MLA_NOPE],
                         jnp.broadcast_to(k_r[:, :, None, :], (bsz, seq, MLA_HEADS, MLA_ROPE))], axis=-1)
    kh = k.transpose(0, 2, 1, 3)
    vh = kv[..., MLA_NOPE:].transpose(0, 2, 1, 3)
    n_blocks = seq // ATTN_BLOCK
    qb = q.reshape(bsz, n_blocks, ATTN_BLOCK, MLA_HEADS, dk).transpose(1, 0, 3, 2, 4)
    scale = dk ** -0.5
    k_pos = jnp.arange(seq)

    def attend(args):
        q_blk, blk = args
        s = jnp.einsum('bhqd,bhkd->bhqk', q_blk, kh).astype(jnp.float32) * scale
        q_pos = blk * ATTN_BLOCK + jnp.arange(ATTN_BLOCK)
        s = jnp.where(k_pos[None, :] <= q_pos[:, None], s, -1e30)
        p = jax.nn.softmax(s, axis=-1).astype(vh.dtype)
        return jnp.einsum('bhqk,bhkd->bhqd', p, vh)

    o = lax.map(attend, (qb, jnp.arange(n_blocks)))
    return o.transpose(1, 0, 3, 2, 4).reshape(bsz, seq, MLA_HEADS * MLA_V)


def moe_ffn(x, w_router, w_gate, w_up, w_down):
    bsz, seq, d = x.shape
    xt = x.reshape(-1, d)
    logits = (xt @ w_router).astype(jnp.float32)
    top_val, top_idx = lax.top_k(logits, TOP_K)
    top_w = jax.nn.softmax(top_val, axis=-1)
    gates = jnp.sum(jax.nn.one_hot(top_idx, N_EXPERTS, dtype=jnp.float32) * top_w[..., None], axis=1)
    out = jnp.zeros_like(xt)
    for e in range(N_EXPERTS):
        out = out + gates[:, e:e + 1].astype(x.dtype) * swiglu(xt, w_gate[e], w_up[e], w_down[e])
    return out.reshape(bsz, seq, d)


def setup_inputs(seed: int = 0) -> dict:
    key = jax.random.key(seed)
    ks = iter(jax.random.split(key, 48))

    def nrm(shape, scale):
        return scale * jax.random.normal(next(ks), shape, jnp.float32)

    def gain(shape):
        return 1.0 + nrm(shape, 0.05)

    x = nrm((BATCH, SEQ, D_MODEL), 1.0)
    norm_g = gain((DEPTH, 4, D_MODEL))
    ev_w_in = nrm((N_EVEN, D_MODEL, EVEN_IN), D_MODEL ** -0.5)
    ssm_lambda_re = -0.5 + nrm((N_EVEN, SSM_GROUPS, SSM_STATE), 0.01)
    ssm_lambda_im = jnp.pi * jnp.arange(SSM_STATE, dtype=jnp.float32) + nrm((N_EVEN, SSM_GROUPS, SSM_STATE), 0.01)
    ssm_log_dt = jax.random.uniform(next(ks), (N_EVEN, SSM_GROUPS), jnp.float32, math.log(DT_MIN), math.log(DT_MAX))
    ssm_b_re = nrm((N_EVEN, SSM_GROUPS, SSM_STATE, SSM_GROUP), (2 * SSM_GROUP) ** -0.5)
    ssm_b_im = nrm((N_EVEN, SSM_GROUPS, SSM_STATE, SSM_GROUP), (2 * SSM_GROUP) ** -0.5)
    ssm_c_re = nrm((N_EVEN, SSM_GROUPS, SSM_GROUP, SSM_STATE), SSM_STATE ** -0.5)
    ssm_c_im = nrm((N_EVEN, SSM_GROUPS, SSM_GROUP, SSM_STATE), SSM_STATE ** -0.5)
    ssm_d = nrm((N_EVEN, SSM_WIDTH), 0.1)
    ssm_w_glu = nrm((N_EVEN, SSM_WIDTH, SSM_WIDTH), SSM_WIDTH ** -0.5)
    sgu_ln_g = gain((N_EVEN, SGU_WIDTH))
    sgu_ln_b = nrm((N_EVEN, SGU_WIDTH), 0.02)
    sgu_w = nrm((N_EVEN, SGU_HEADS, SGU_CHUNK, SGU_CHUNK), 0.5 * SGU_CHUNK ** -0.5)
    sgu_b = gain((N_EVEN, SGU_HEADS, SGU_CHUNK))
    ev_w_out = nrm((N_EVEN, EVEN_MIX, D_MODEL), EVEN_MIX ** -0.5)
    ffn_w_gate = nrm((N_EVEN, D_MODEL, FF_DENSE), D_MODEL ** -0.5)
    ffn_w_up = nrm((N_EVEN, D_MODEL, FF_DENSE), D_MODEL ** -0.5)
    ffn_w_down = nrm((N_EVEN, FF_DENSE, D_MODEL), FF_DENSE ** -0.5)
    od_w_in = nrm((N_ODD, D_MODEL, ODD_IN), D_MODEL ** -0.5)
    conv_w = nrm((N_ODD, CONV_TAPS, CONV_CH), CONV_TAPS ** -0.5)
    conv_b = nrm((N_ODD, CONV_CH), 0.02)
    conv_ln_g = gain((N_ODD, CONV_CH))
    conv_ln_b = nrm((N_ODD, CONV_CH), 0.02)
    mla_q_norm_g = gain((N_ODD, MLA_Q_RANK))
    mla_w_uq = nrm((N_ODD, MLA_Q_RANK, MLA_HEADS * (MLA_NOPE + MLA_ROPE)), MLA_Q_RANK ** -0.5)
    mla_kv_norm_g = gain((N_ODD, MLA_KV_RANK))
    mla_w_ukv = nrm((N_ODD, MLA_KV_RANK, MLA_HEADS * (MLA_NOPE + MLA_V)), MLA_KV_RANK ** -0.5)
    od_w_out = nrm((N_ODD, ODD_MIX, D_MODEL), ODD_MIX ** -0.5)
    moe_w_router = nrm((N_ODD, D_MODEL, N_EXPERTS), D_MODEL ** -0.5)
    moe_w_gate = nrm((N_ODD, N_EXPERTS, D_MODEL, FF_EXPERT), D_MODEL ** -0.5)
    moe_w_up = nrm((N_ODD, N_EXPERTS, D_MODEL, FF_EXPERT), D_MODEL ** -0.5)
    moe_w_down = nrm((N_ODD, N_EXPERTS, FF_EXPERT, D_MODEL), FF_EXPERT ** -0.5)
    return {
        'x': x, 'norm_g': norm_g, 'ev_w_in': ev_w_in,
        'ssm_lambda_re': ssm_lambda_re, 'ssm_lambda_im': ssm_lambda_im, 'ssm_log_dt': ssm_log_dt,
        'ssm_b_re': ssm_b_re, 'ssm_b_im': ssm_b_im, 'ssm_c_re': ssm_c_re, 'ssm_c_im': ssm_c_im,
        'ssm_d': ssm_d, 'ssm_w_glu': ssm_w_glu,
        'sgu_ln_g': sgu_ln_g, 'sgu_ln_b': sgu_ln_b, 'sgu_w': sgu_w, 'sgu_b': sgu_b,
        'ev_w_out': ev_w_out, 'ffn_w_gate': ffn_w_gate, 'ffn_w_up': ffn_w_up, 'ffn_w_down': ffn_w_down,
        'od_w_in': od_w_in, 'conv_w': conv_w, 'conv_b': conv_b, 'conv_ln_g': conv_ln_g, 'conv_ln_b': conv_ln_b,
        'mla_q_norm_g': mla_q_norm_g, 'mla_w_uq': mla_w_uq, 'mla_kv_norm_g': mla_kv_norm_g, 'mla_w_ukv': mla_w_ukv,
        'od_w_out': od_w_out, 'moe_w_router': moe_w_router,
        'moe_w_gate': moe_w_gate, 'moe_w_up': moe_w_up, 'moe_w_down': moe_w_down,
    }


def reference(x, norm_g, ev_w_in, ssm_lambda_re, ssm_lambda_im, ssm_log_dt, ssm_b_re, ssm_b_im,
              ssm_c_re, ssm_c_im, ssm_d, ssm_w_glu, sgu_ln_g, sgu_ln_b, sgu_w, sgu_b, ev_w_out,
              ffn_w_gate, ffn_w_up, ffn_w_down, od_w_in, conv_w, conv_b, conv_ln_g, conv_ln_b,
              mla_q_norm_g, mla_w_uq, mla_kv_norm_g, mla_w_ukv, od_w_out, moe_w_router,
              moe_w_gate, moe_w_up, moe_w_down):
    seq = x.shape[1]
    cos, sin = rope_tables(seq)
    h = x
    for layer in range(DEPTH):
        i = layer // 2
        g = norm_g[layer]
        z = rms_norm(h, g[0])
        if layer % 2 == 0:
            proj = z @ ev_w_in[i]
            a_in = proj[..., :SSM_WIDTH]
            b_u = proj[..., SSM_WIDTH:SSM_WIDTH + SGU_WIDTH]
            b_v = proj[..., SSM_WIDTH + SGU_WIDTH:]
            y_a = s5_mixer(a_in, ssm_lambda_re[i], ssm_lambda_im[i], ssm_log_dt[i], ssm_b_re[i], ssm_b_im[i],
                           ssm_c_re[i], ssm_c_im[i], ssm_d[i], ssm_w_glu[i])
            y_b = sgu_mixer(b_u, b_v, sgu_ln_g[i], sgu_ln_b[i], sgu_w[i], sgu_b[i])
            mix = jnp.concatenate([y_a, y_b], axis=-1) @ ev_w_out[i]
        else:
            proj = z @ od_w_in[i]
            o1 = 2 * CONV_CH
            o2 = o1 + MLA_Q_RANK
            o3 = o2 + MLA_KV_RANK
            y_c = conv_mixer(proj[..., :o1], conv_w[i], conv_b[i], conv_ln_g[i], conv_ln_b[i])
            y_d = mla_mixer(proj[..., o1:o2], proj[..., o2:o3], proj[..., o3:], mla_q_norm_g[i], mla_w_uq[i],
                            mla_kv_norm_g[i], mla_w_ukv[i], cos, sin)
            mix = jnp.concatenate([y_c, y_d], axis=-1) @ od_w_out[i]
        h = h + rms_norm(mix, g[1])
        z = rms_norm(h, g[2])
        if layer % 2 == 0:
            f = swiglu(z, ffn_w_gate[i], ffn_w_up[i], ffn_w_down[i])
        else:
            f = moe_ffn(z, moe_w_router[i], moe_w_gate[i], moe_w_up[i], moe_w_down[i])
        h = h + rms_norm(f, g[3])
    return h
```

```python
import functools
import math

import jax
import jax.numpy as jnp
from jax import lax
from jax.experimental import pallas as pl
from jax.experimental.pallas import tpu as pltpu

F32 = jnp.float32
BF16 = jnp.bfloat16

D_MODEL = 1024
NORM_EPS = 1e-6
SSM_WIDTH = 512
SSM_GROUP = 16
SSM_GROUPS = 32
SSM_STATE = 64
SSM_CHUNK = 16
SSM_PAIR = 2 * SSM_GROUP * SSM_CHUNK
SGU_WIDTH = 512
SGU_HEADS = 8
SGU_HEAD_DIM = 64
SGU_CHUNK = 128
CONV_CH = 512
CONV_TAPS = 31
CONV_HALO = 32
MLA_HEADS = 8
MLA_Q_RANK = 256
MLA_KV_RANK = 128
MLA_NOPE = 64
MLA_ROPE = 32
MLA_V = 64
MLA_PAD = 128
ROPE_THETA = 10000.0
FF_DENSE = 4096
N_EXPERTS = 8
FF_EXPERT = 3584
LANES = 128
SUBLANES = 8
VMEM_LIMIT = 56 * 1024 * 1024


def _params(sem, vmem=VMEM_LIMIT):
    return pltpu.CompilerParams(dimension_semantics=sem, vmem_limit_bytes=vmem)


def _rms(x, g):
    return x * lax.rsqrt(jnp.mean(x * x, axis=-1, keepdims=True) + NORM_EPS) * g


def _layer_norm(x, g, b):
    mu = jnp.mean(x, axis=-1, keepdims=True)
    xc = x - mu
    return xc * lax.rsqrt(jnp.mean(xc * xc, axis=-1, keepdims=True) + NORM_EPS) * g + b


def _dot(a, b):
    return jnp.dot(a, b, preferred_element_type=F32)


def _norm_proj_kernel(h_ref, g_ref, w_ref, o_ref):
    z = _rms(h_ref[...], g_ref[...])
    o_ref[...] = _dot(z.astype(BF16), w_ref[...]).astype(o_ref.dtype)


def _norm_proj(h, g, w, tm):
    n, d = h.shape
    cols = w.shape[1]
    return pl.pallas_call(
        _norm_proj_kernel,
        grid=(n // tm,),
        in_specs=[pl.BlockSpec((tm, d), lambda i: (i, 0)),
                  pl.BlockSpec((1, d), lambda i: (0, 0)),
                  pl.BlockSpec((d, cols), lambda i: (0, 0))],
        out_specs=pl.BlockSpec((tm, cols), lambda i: (i, 0)),
        out_shape=jax.ShapeDtypeStruct((n, cols), BF16),
        compiler_params=_params(("parallel",)),
        name="even_in_proj",
    )(h, g, w)


def _s5_matrices(lam_re, lam_im, log_dt, b_re, b_im, c_re, c_im):
    t = SSM_CHUNK
    lr = jnp.minimum(lam_re.astype(F32), -1e-4)
    li = lam_im.astype(F32)
    dt = jnp.exp(log_dt.astype(F32))[:, None]
    mag = jnp.exp(lr * dt)
    a_re = mag * jnp.cos(li * dt)
    a_im = mag * jnp.sin(li * dt)
    den = lr * lr + li * li
    nr = a_re - 1.0
    coef_re = (nr * lr + a_im * li) / den
    coef_im = (a_im * lr - nr * li) / den
    br = b_re.astype(F32)
    bi = b_im.astype(F32)
    bb_re = coef_re[..., None] * br - coef_im[..., None] * bi
    bb_im = coef_re[..., None] * bi + coef_im[..., None] * br
    cr = c_re.astype(F32)
    ci = c_im.astype(F32)
    pw_re = [jnp.ones_like(a_re)]
    pw_im = [jnp.zeros_like(a_im)]
    for _ in range(t):
        pr, pi = pw_re[-1], pw_im[-1]
        pw_re.append(pr * a_re - pi * a_im)
        pw_im.append(pr * a_im + pi * a_re)
    pw_re = jnp.stack(pw_re)
    pw_im = jnp.stack(pw_im)
    ab_re = pw_re[:t, :, :, None] * bb_re[None] - pw_im[:t, :, :, None] * bb_im[None]
    ab_im = pw_re[:t, :, :, None] * bb_im[None] + pw_im[:t, :, :, None] * bb_re[None]
    hi = lax.Precision.HIGHEST
    k_lag = (jnp.einsum('gnp,tgpm->tgnm', cr, ab_re, precision=hi)
             - jnp.einsum('gnp,tgpm->tgnm', ci, ab_im, precision=hi))
    s_idx = jnp.arange(t)[:, None]
    t_idx = jnp.arange(t)[None, :]
    lag = jnp.clip(t_idx - s_idx, 0, t - 1)
    w_intra = k_lag[lag]
    w_intra = jnp.where((t_idx >= s_idx)[:, :, None, None, None], w_intra, 0.0)
    w_intra = w_intra.transpose(2, 0, 4, 1, 3).reshape(SSM_GROUPS, t * SSM_GROUP, t * SSM_GROUP)
    rev_re = pw_re[:t][::-1]
    rev_im = pw_im[:t][::-1]
    ws_re = rev_re[..., None] * bb_re[None] - rev_im[..., None] * bb_im[None]
    ws_im = rev_re[..., None] * bb_im[None] + rev_im[..., None] * bb_re[None]
    ws_re = ws_re.transpose(1, 0, 3, 2).reshape(SSM_GROUPS, t * SSM_GROUP, SSM_STATE)
    ws_im = ws_im.transpose(1, 0, 3, 2).reshape(SSM_GROUPS, t * SSM_GROUP, SSM_STATE)
    ca_re = cr[None] * pw_re[1:, :, None, :] - ci[None] * pw_im[1:, :, None, :]
    ca_im = cr[None] * pw_im[1:, :, None, :] + ci[None] * pw_re[1:, :, None, :]
    wo_re = ca_re.transpose(1, 3, 0, 2).reshape(SSM_GROUPS, SSM_STATE, t * SSM_GROUP)
    wo_im = (-ca_im).transpose(1, 3, 0, 2).reshape(SSM_GROUPS, SSM_STATE, t * SSM_GROUP)

    def pair_diag(w):
        g, r, c = w.shape
        w = w.reshape(g // 2, 2, r, c)
        z = jnp.zeros_like(w[:, 0])
        top = jnp.concatenate([w[:, 0], z], axis=2)
        bot = jnp.concatenate([z, w[:, 1]], axis=2)
        return jnp.concatenate([top, bot], axis=1)

    return dict(
        w_intra=pair_diag(w_intra).astype(BF16),
        ws_re=pair_diag(ws_re).astype(BF16), ws_im=pair_diag(ws_im).astype(BF16),
        wo_re=pair_diag(wo_re).astype(BF16), wo_im=pair_diag(wo_im).astype(BF16),
        at_re=pw_re[t].reshape(1, SSM_GROUPS * SSM_STATE), at_im=pw_im[t].reshape(1, SSM_GROUPS * SSM_STATE))


def _s5_state_kernel(x_ref, wre_ref, wim_ref, sre_ref, sim_ref):
    x = x_ref[...]
    sre_ref[...] = _dot(x, wre_ref[0])
    sim_ref[...] = _dot(x, wim_ref[0])


def _s5_scan_kernel(sre_ref, sim_ref, are_ref, aim_ref, hre_ref, him_ref, *, n_tiles, batch):
    ar = are_ref[...]
    ai = aim_ref[...]
    lanes = sre_ref.shape[1]
    row = lax.broadcasted_iota(jnp.int32, (SUBLANES, lanes), 0)
    per_tile = SUBLANES // batch

    def body(k, carry):
        r0 = pl.multiple_of(k * SUBLANES, SUBLANES)
        sr = sre_ref[pl.ds(r0, SUBLANES), :]
        si = sim_ref[pl.ds(r0, SUBLANES), :]
        out_r, out_i = carry
        for i in range(1, per_tile + 1):
            tr = ar * out_r - ai * out_i + sr
            ti = ar * out_i + ai * out_r + si
            tr = pltpu.roll(tr, batch, 0)
            ti = pltpu.roll(ti, batch, 0)
            if i < per_tile:
                sel = (row >= i * batch) & (row < (i + 1) * batch)
                out_r = jnp.where(sel, tr, out_r)
                out_i = jnp.where(sel, ti, out_i)
        hre_ref[pl.ds(r0, SUBLANES), :] = out_r
        him_ref[pl.ds(r0, SUBLANES), :] = out_i
        return tr, ti

    zero = jnp.zeros((SUBLANES, lanes), F32)
    lax.fori_loop(0, n_tiles, body, (zero, zero))


def _s5_out_kernel(x_ref, wi_ref, hre_ref, him_ref, wore_ref, woim_ref, d_ref, y_ref):
    x = x_ref[...]
    y = _dot(x, wi_ref[0])
    y += _dot(hre_ref[...].astype(BF16), wore_ref[0])
    y += _dot(him_ref[...].astype(BF16), woim_ref[0])
    y += d_ref[...] * x.astype(F32)
    y_ref[...] = jax.nn.gelu(y).astype(y_ref.dtype)


def _s5_mixer(u, mats, d, batch, seq):
    t = SSM_CHUNK
    n_chunks = seq // t
    rows = n_chunks * batch
    cols = SSM_GROUPS * t * SSM_GROUP
    n_pairs = SSM_GROUPS // 2
    st = 2 * SSM_STATE
    x = (u.reshape(batch, n_chunks, t, SSM_GROUPS, SSM_GROUP)
         .transpose(1, 0, 3, 2, 4).reshape(rows, cols))
    s_re, s_im = pl.pallas_call(
        _s5_state_kernel,
        grid=(n_pairs,),
        in_specs=[pl.BlockSpec((rows, SSM_PAIR), lambda q: (0, q)),
                  pl.BlockSpec((1, SSM_PAIR, st), lambda q: (q, 0, 0)),
                  pl.BlockSpec((1, SSM_PAIR, st), lambda q: (q, 0, 0))],
        out_specs=[pl.BlockSpec((rows, st), lambda q: (0, q)),
                   pl.BlockSpec((rows, st), lambda q: (0, q))],
        out_shape=[jax.ShapeDtypeStruct((rows, n_pairs * st), F32)] * 2,
        compiler_params=_params(("parallel",)),
        name="s5_state",
    )(x, mats['ws_re'], mats['ws_im'])
    tl = 512
    n_state = n_pairs * st
    h_re, h_im = pl.pallas_call(
        functools.partial(_s5_scan_kernel, n_tiles=rows // SUBLANES, batch=batch),
        grid=(n_state // tl,),
        in_specs=[pl.BlockSpec((rows, tl), lambda q: (0, q)),
                  pl.BlockSpec((rows, tl), lambda q: (0, q)),
                  pl.BlockSpec((1, tl), lambda q: (0, q)),
                  pl.BlockSpec((1, tl), lambda q: (0, q))],
        out_specs=[pl.BlockSpec((rows, tl), lambda q: (0, q)),
                   pl.BlockSpec((rows, tl), lambda q: (0, q))],
        out_shape=[jax.ShapeDtypeStruct((rows, n_state), F32)] * 2,
        compiler_params=_params(("parallel",)),
        name="s5_scan",
    )(s_re, s_im, mats['at_re'], mats['at_im'])
    d_cols = jnp.broadcast_to(d.astype(F32).reshape(SSM_GROUPS, 1, SSM_GROUP),
                              (SSM_GROUPS, t, SSM_GROUP)).reshape(1, cols)
    y = pl.pallas_call(
        _s5_out_kernel,
        grid=(n_pairs,),
        in_specs=[pl.BlockSpec((rows, SSM_PAIR), lambda q: (0, q)),
                  pl.BlockSpec((1, SSM_PAIR, SSM_PAIR), lambda q: (q, 0, 0)),
                  pl.BlockSpec((rows, st), lambda q: (0, q)),
                  pl.BlockSpec((rows, st), lambda q: (0, q)),
                  pl.BlockSpec((1, st, SSM_PAIR), lambda q: (q, 0, 0)),
                  pl.BlockSpec((1, st, SSM_PAIR), lambda q: (q, 0, 0)),
                  pl.BlockSpec((1, SSM_PAIR), lambda q: (0, q))],
        out_specs=pl.BlockSpec((rows, SSM_PAIR), lambda q: (0, q)),
        out_shape=jax.ShapeDtypeStruct((rows, cols), BF16),
        compiler_params=_params(("parallel",)),
        name="s5_out",
    )(x, mats['w_intra'], h_re, h_im, mats['wo_re'], mats['wo_im'], d_cols)
    return (y.reshape(n_chunks, batch, SSM_GROUPS, t, SSM_GROUP)
            .transpose(1, 0, 3, 2, 4).reshape(batch * seq, SSM_WIDTH))


def _even_mix_kernel(ys_ref, bu_ref, bv_ref, h_ref, wglu_ref, lng_ref, lnb_ref, ws_ref, bias_ref,
                     wo_a_ref, wo_b_ref, g1_ref, g2_ref, hout_ref, z_ref, s_scr):
    tm = ys_ref.shape[0]
    ys = ys_ref[...]
    ya = ys.astype(F32) * jax.nn.sigmoid(_dot(ys, wglu_ref[...]))
    u = jax.nn.gelu(bu_ref[...].astype(F32))
    v = _layer_norm(jax.nn.gelu(bv_ref[...].astype(F32)), lng_ref[...], lnb_ref[...])
    lane = lax.broadcasted_iota(jnp.int32, v.shape, 1)
    left = (lane % LANES) < SGU_HEAD_DIM
    v_l = jnp.where(left, v, 0.0).astype(BF16)
    v_r = jnp.where(left, 0.0, v).astype(BF16)
    for c in range(tm // SGU_CHUNK):
        rows = slice(c * SGU_CHUNK, (c + 1) * SGU_CHUNK)
        for p in range(SGU_HEADS // 2):
            cols = slice(p * LANES, (p + 1) * LANES)
            s_scr[rows, cols] = (_dot(ws_ref[2 * p], v_l[rows, cols]) + _dot(ws_ref[2 * p + 1], v_r[rows, cols]))
    bias = jnp.concatenate([bias_ref[...]] * (tm // SGU_CHUNK), axis=0)
    yb = u * (s_scr[...] + bias)
    mix = _dot(ya.astype(BF16), wo_a_ref[...]) + _dot(yb.astype(BF16), wo_b_ref[...])
    h_new = h_ref[...] + _rms(mix, g1_ref[...])
    hout_ref[...] = h_new
    z_ref[...] = _rms(h_new, g2_ref[...]).astype(z_ref.dtype)


def _even_mix(ys, proj, h, wglu, lng, lnb, ws, bias, wo_a, wo_b, g1, g2, tm):
    n, d = h.shape
    w = SGU_WIDTH
    const = lambda *shape: pl.BlockSpec(shape, lambda i: (0,) * len(shape))
    return pl.pallas_call(
        _even_mix_kernel,
        grid=(n // tm,),
        in_specs=[pl.BlockSpec((tm, w), lambda i: (i, 0)),
                  pl.BlockSpec((tm, w), lambda i: (i, 1)),
                  pl.BlockSpec((tm, w), lambda i: (i, 2)),
                  pl.BlockSpec((tm, d), lambda i: (i, 0)),
                  const(w, w), const(1, w), const(1, w),
                  const(SGU_HEADS, SGU_CHUNK, SGU_CHUNK), const(SGU_CHUNK, w),
                  const(w, d), const(w, d), const(1, d), const(1, d)],
        out_specs=[pl.BlockSpec((tm, d), lambda i: (i, 0)),
                   pl.BlockSpec((tm, d), lambda i: (i, 0))],
        out_shape=[jax.ShapeDtypeStruct((n, d), F32), jax.ShapeDtypeStruct((n, d), BF16)],
        scratch_shapes=[pltpu.VMEM((tm, w), F32)],
        compiler_params=_params(("parallel",)),
        name="even_mix",
    )(ys, proj, proj, h, wglu, lng, lnb, ws, bias, wo_a, wo_b, g1, g2)


def _ffn_kernel(z_ref, wg_ref, wu_ref, wd_ref, h_ref, g3_ref, gn_ref, hout_ref, zout_ref, acc_ref):
    j = pl.program_id(1)

    @pl.when(j == 0)
    def _():
        acc_ref[...] = jnp.zeros_like(acc_ref)

    z = z_ref[...]
    a = jax.nn.silu(_dot(z, wg_ref[...])) * _dot(z, wu_ref[...])
    acc_ref[...] += _dot(a.astype(BF16), wd_ref[...])

    @pl.when(j == pl.num_programs(1) - 1)
    def _():
        h_new = h_ref[...] + _rms(acc_ref[...], g3_ref[...])
        hout_ref[...] = h_new
        zout_ref[...] = _rms(h_new, gn_ref[...]).astype(zout_ref.dtype)


def _dense_ffn(z, wg, wu, wd, h, g3, g_next, tm, tf):
    n, d = h.shape
    ff = wg.shape[1]
    return pl.pallas_call(
        _ffn_kernel,
        grid=(n // tm, ff // tf),
        in_specs=[pl.BlockSpec((tm, d), lambda i, j: (i, 0)),
                  pl.BlockSpec((d, tf), lambda i, j: (0, j)),
                  pl.BlockSpec((d, tf), lambda i, j: (0, j)),
                  pl.BlockSpec((tf, d), lambda i, j: (j, 0)),
                  pl.BlockSpec((tm, d), lambda i, j: (i, 0)),
                  pl.BlockSpec((1, d), lambda i, j: (0, 0)),
                  pl.BlockSpec((1, d), lambda i, j: (0, 0))],
        out_specs=[pl.BlockSpec((tm, d), lambda i, j: (i, 0)),
                   pl.BlockSpec((tm, d), lambda i, j: (i, 0))],
        out_shape=[jax.ShapeDtypeStruct((n, d), F32), jax.ShapeDtypeStruct((n, d), BF16)],
        scratch_shapes=[pltpu.VMEM((tm, d), F32)],
        compiler_params=_params(("parallel", "arbitrary")),
        name="dense_ffn",
    )(z, wg, wu, wd, h, g3, g_next)


def _odd_proj_kernel(z_ref, win_ref, gq_ref, gkv_ref, wuq_ref, wuqs_ref, wuk_ref, wuv_ref, cos_ref, sin_ref,
                     zc_ref, q_ref, k_ref, v_ref, *, scale):
    z = z_ref[...]
    proj = _dot(z, win_ref[...])
    c0 = 2 * CONV_CH
    c1 = c0 + MLA_Q_RANK
    c2 = c1 + MLA_KV_RANK
    c3 = c2 + MLA_PAD
    zc_ref[...] = proj[:, :c0].astype(zc_ref.dtype)
    cq = _rms(proj[:, c0:c1], gq_ref[...]).astype(BF16)
    ckv = _rms(proj[:, c1:c2], gkv_ref[...]).astype(BF16)
    cos = cos_ref[...]
    sin = sin_ref[...]
    cos_h = jnp.concatenate([cos] * MLA_HEADS, axis=1)
    sin_h = jnp.concatenate([sin] * MLA_HEADS, axis=1)
    q = _dot(cq, wuq_ref[...]) * cos_h + _dot(cq, wuqs_ref[...]) * sin_h
    q_ref[...] = (q * scale).astype(q_ref.dtype)
    kr = proj[:, c2:c3] * cos + proj[:, c3:] * sin
    k = _dot(ckv, wuk_ref[...]) + jnp.concatenate([kr] * MLA_HEADS, axis=1)
    k_ref[...] = k.astype(k_ref.dtype)
    v_ref[...] = _dot(ckv, wuv_ref[...]).astype(v_ref.dtype)


def _odd_proj(z, win, gq, gkv, wuq, wuqs, wuk, wuv, cos_t, sin_t, seq, tm):
    n, d = z.shape
    hp = MLA_HEADS * MLA_PAD
    n_l = seq // tm
    const = lambda *shape: pl.BlockSpec(shape, lambda i: (0,) * len(shape))
    out = jax.ShapeDtypeStruct((n, hp), BF16)
    return pl.pallas_call(
        functools.partial(_odd_proj_kernel, scale=float((MLA_NOPE + MLA_ROPE) ** -0.5)),
        grid=(n // tm,),
        in_specs=[pl.BlockSpec((tm, d), lambda i: (i, 0)),
                  const(d, win.shape[1]), const(1, MLA_Q_RANK), const(1, MLA_KV_RANK),
                  const(MLA_Q_RANK, hp), const(MLA_Q_RANK, hp), const(MLA_KV_RANK, hp), const(MLA_KV_RANK, hp),
                  pl.BlockSpec((tm, MLA_PAD), lambda i: (i % n_l, 0)),
                  pl.BlockSpec((tm, MLA_PAD), lambda i: (i % n_l, 0))],
        out_specs=[pl.BlockSpec((tm, 2 * CONV_CH), lambda i: (i, 0)),
                   pl.BlockSpec((tm, hp), lambda i: (i, 0)),
                   pl.BlockSpec((tm, hp), lambda i: (i, 0)),
                   pl.BlockSpec((tm, hp), lambda i: (i, 0))],
        out_shape=[jax.ShapeDtypeStruct((n, 2 * CONV_CH), BF16), out, out, out],
        compiler_params=_params(("parallel",)),
        name="odd_in_proj",
    )(z, win, gq, gkv, wuq, wuqs, wuk, wuv, cos_t, sin_t)


def _attn_kernel(q_ref, k_ref, v_ref, o_ref, acc_ref, m_ref, l_ref, *, blk):
    i = pl.program_id(2)
    q = q_ref[0]
    acc_ref[...] = jnp.zeros_like(acc_ref)
    m_ref[...] = jnp.full_like(m_ref, -1e30)
    l_ref[...] = jnp.zeros_like(l_ref)

    def step(j, masked):
        r0 = pl.multiple_of(j * blk, blk)
        k = k_ref[0, pl.ds(r0, blk), :]
        v = v_ref[0, pl.ds(r0, blk), :]
        s = lax.dot_general(q, k, (((1,), (1,)), ((), ())), preferred_element_type=F32)
        if masked:
            row = lax.broadcasted_iota(jnp.int32, s.shape, 0)
            col = lax.broadcasted_iota(jnp.int32, s.shape, 1)
            s = jnp.where(col <= row, s, -1e30)
        m_prev = m_ref[...]
        m_new = jnp.maximum(m_prev, jnp.max(s, axis=-1, keepdims=True))
        alpha = jnp.exp(m_prev - m_new)
        p = jnp.exp(s - m_new)
        l_ref[...] = alpha * l_ref[...] + jnp.sum(p, axis=-1, keepdims=True)
        acc_ref[...] = alpha * acc_ref[...] + _dot(p.astype(BF16), v)
        m_ref[...] = m_new

    def body(j, carry):
        step(j, False)
        return carry

    lax.fori_loop(0, i, body, 0)
    step(i, True)
    o_ref[0] = (acc_ref[...] / l_ref[...]).astype(o_ref.dtype)


def _attention(q, k, v, blk):
    b, seq, hp = q.shape
    return pl.pallas_call(
        functools.partial(_attn_kernel, blk=blk),
        grid=(b, MLA_HEADS, seq // blk),
        in_specs=[pl.BlockSpec((1, blk, MLA_PAD), lambda bi, h, i: (bi, i, h)),
                  pl.BlockSpec((1, seq, MLA_PAD), lambda bi, h, i: (bi, 0, h)),
                  pl.BlockSpec((1, seq, MLA_PAD), lambda bi, h, i: (bi, 0, h))],
        out_specs=pl.BlockSpec((1, blk, MLA_PAD), lambda bi, h, i: (bi, i, h)),
        out_shape=jax.ShapeDtypeStruct((b, seq, hp), BF16),
        scratch_shapes=[pltpu.VMEM((blk, MLA_PAD), F32), pltpu.VMEM((blk, 1), F32), pltpu.VMEM((blk, 1), F32)],
        compiler_params=_params(("parallel", "parallel", "parallel")),
        name="mla_attention",
    )(q, k, v)


def _conv_kernel(zc_ref, w_ref, b_ref, lng_ref, lnb_ref, y_ref, buf_ref):
    tm = zc_ref.shape[1]

    @pl.when(pl.program_id(1) == 0)
    def _():
        buf_ref[pl.ds(0, CONV_HALO), :] = jnp.zeros((CONV_HALO, CONV_CH), F32)

    zc = zc_ref[0].astype(F32)
    hh = zc[:, :CONV_CH] * jax.nn.sigmoid(zc[:, CONV_CH:])
    buf_ref[pl.ds(CONV_HALO, tm), :] = hh
    off = CONV_HALO - (CONV_TAPS - 1)
    acc = jnp.zeros((tm, CONV_CH), F32) + b_ref[...]
    for kk in range(CONV_TAPS):
        acc = acc + w_ref[pl.ds(kk, 1), :] * buf_ref[pl.ds(off + kk, tm), :]
    buf_ref[pl.ds(0, CONV_HALO), :] = buf_ref[pl.ds(tm, CONV_HALO), :]
    y_ref[0] = jax.nn.silu(_layer_norm(acc, lng_ref[...], lnb_ref[...])).astype(y_ref.dtype)


def _conv_mixer(zc, w, b, lng, lnb, tm):
    bsz, seq, _ = zc.shape
    const = lambda *shape: pl.BlockSpec(shape, lambda bi, i: (0,) * len(shape))
    return pl.pallas_call(
        _conv_kernel,
        grid=(bsz, seq // tm),
        in_specs=[pl.BlockSpec((1, tm, 2 * CONV_CH), lambda bi, i: (bi, i, 0)),
                  const(CONV_HALO, CONV_CH), const(1, CONV_CH), const(1, CONV_CH), const(1, CONV_CH)],
        out_specs=pl.BlockSpec((1, tm, CONV_CH), lambda bi, i: (bi, i, 0)),
        out_shape=jax.ShapeDtypeStruct((bsz, seq, CONV_CH), BF16),
        scratch_shapes=[pltpu.VMEM((tm + CONV_HALO, CONV_CH), F32)],
        compiler_params=_params(("arbitrary", "arbitrary")),
        name="conv_module",
    )(zc, w, b, lng, lnb)


def _odd_mix_kernel(yc_ref, yd_ref, h_ref, wo_a_ref, wo_b_ref, g1_ref, g2_ref, wr_ref, hout_ref, z_ref, route_ref):
    mix = _dot(yc_ref[...], wo_a_ref[...]) + _dot(yd_ref[...], wo_b_ref[...])
    h_new = h_ref[...] + _rms(mix, g1_ref[...])
    hout_ref[...] = h_new
    z = _rms(h_new, g2_ref[...])
    z_ref[...] = z
    logits = _dot(z.astype(BF16), wr_ref[...])
    lane = lax.broadcasted_iota(jnp.int32, logits.shape, 1)
    neg = -jnp.inf
    logits = jnp.where(lane < N_EXPERTS, logits, neg)
    m1 = jnp.max(logits, axis=-1, keepdims=True)
    i1 = jnp.min(jnp.where(logits == m1, lane, LANES), axis=-1, keepdims=True)
    rest = jnp.where(lane == i1, neg, logits)
    m2 = jnp.max(rest, axis=-1, keepdims=True)
    i2 = jnp.min(jnp.where(rest == m2, lane, LANES), axis=-1, keepdims=True)
    e = jnp.exp(m2 - m1)
    w1 = 1.0 / (1.0 + e)
    w2 = e / (1.0 + e)
    route = jnp.where(lane == 0, i1.astype(F32),
                      jnp.where(lane == 1, i2.astype(F32),
                                jnp.where(lane == 2, w1, jnp.where(lane == 3, w2, 0.0))))
    route_ref[...] = route


def _odd_mix(yc, yd, h, wo_a, wo_b, g1, g2, wr, tm):
    n, d = h.shape
    const = lambda *shape: pl.BlockSpec(shape, lambda i: (0,) * len(shape))
    return pl.pallas_call(
        _odd_mix_kernel,
        grid=(n // tm,),
        in_specs=[pl.BlockSpec((tm, yc.shape[1]), lambda i: (i, 0)),
                  pl.BlockSpec((tm, yd.shape[1]), lambda i: (i, 0)),
                  pl.BlockSpec((tm, d), lambda i: (i, 0)),
                  const(*wo_a.shape), const(*wo_b.shape), const(1, d), const(1, d), const(d, LANES)],
        out_specs=[pl.BlockSpec((tm, d), lambda i: (i, 0)),
                   pl.BlockSpec((tm, d), lambda i: (i, 0)),
                   pl.BlockSpec((tm, LANES), lambda i: (i, 0))],
        out_shape=[jax.ShapeDtypeStruct((n, d), F32), jax.ShapeDtypeStruct((n, d), F32),
                   jax.ShapeDtypeStruct((n, LANES), F32)],
        compiler_params=_params(("parallel",)),
        name="odd_mix_router",
    )(yc, yd, h, wo_a, wo_b, g1, g2, wr)


def _row_copy(src_hbm, row, dst_ref, r, sem):
    return pltpu.make_async_copy(src_hbm.at[pl.ds(row, 1), :], dst_ref.at[pl.ds(r, 1), :], sem)


def _dispatch_kernel(tok_ref, z_hbm, o_ref, sem):
    tg = o_ref.shape[0]
    base = pl.program_id(0) * tg

    def issue(r, c):
        _row_copy(z_hbm, tok_ref[base + r], o_ref, r, sem).start()
        return c

    lax.fori_loop(0, tg, issue, 0)

    def drain(r, c):
        _row_copy(z_hbm, 0, o_ref, r, sem).wait()
        return c

    lax.fori_loop(0, tg, drain, 0)


def _dispatch(tok_of_slot, z, tg):
    n_slots = tok_of_slot.shape[0]
    d = z.shape[1]
    return pl.pallas_call(
        _dispatch_kernel,
        grid_spec=pltpu.PrefetchScalarGridSpec(
            num_scalar_prefetch=1,
            grid=(n_slots // tg,),
            in_specs=[pl.BlockSpec(memory_space=pl.ANY)],
            out_specs=pl.BlockSpec((tg, d), lambda i, tok: (i, 0)),
            scratch_shapes=[pltpu.SemaphoreType.DMA(())]),
        out_shape=jax.ShapeDtypeStruct((n_slots, d), z.dtype),
        compiler_params=_params(("arbitrary",)),
        name="moe_dispatch",
    )(tok_of_slot, z)


def _moe_ffn_kernel(te_ref, nu_ref, x_ref, wg_ref, wu_ref, wd_ref, y_ref, xb_ref, acc_ref):
    i = pl.program_id(0)
    j = pl.program_id(1)
    used = i < nu_ref[0]

    @pl.when(j == 0)
    def _():
        xb_ref[...] = x_ref[...].astype(BF16)
        acc_ref[...] = jnp.zeros_like(acc_ref)

    @pl.when(used)
    def _():
        x = xb_ref[...]
        a = jax.nn.silu(_dot(x, wg_ref[0].astype(BF16))) * _dot(x, wu_ref[0].astype(BF16))
        acc_ref[...] += _dot(a.astype(BF16), wd_ref[0].astype(BF16))

    @pl.when(j == pl.num_programs(1) - 1)
    def _():
        y_ref[...] = acc_ref[...]


def _moe_ffn(tile_expert, n_used, xs, wg, wu, wd, tm, tf):
    n_slots, d = xs.shape
    ff = wg.shape[2]
    n_f = ff // tf

    def col(i, j, nu):
        return jnp.where(i < nu[0], j, n_f - 1)

    return pl.pallas_call(
        _moe_ffn_kernel,
        grid_spec=pltpu.PrefetchScalarGridSpec(
            num_scalar_prefetch=2,
            grid=(n_slots // tm, n_f),
            in_specs=[pl.BlockSpec((tm, d), lambda i, j, te, nu: (i, 0)),
                      pl.BlockSpec((1, d, tf), lambda i, j, te, nu: (te[i], 0, col(i, j, nu))),
                      pl.BlockSpec((1, d, tf), lambda i, j, te, nu: (te[i], 0, col(i, j, nu))),
                      pl.BlockSpec((1, tf, d), lambda i, j, te, nu: (te[i], col(i, j, nu), 0))],
            out_specs=pl.BlockSpec((tm, d), lambda i, j, te, nu: (i, 0)),
            scratch_shapes=[pltpu.VMEM((tm, d), BF16), pltpu.VMEM((tm, d), F32)]),
        out_shape=jax.ShapeDtypeStruct((n_slots, d), F32),
        compiler_params=_params(("arbitrary", "arbitrary")),
        name="moe_grouped_ffn",
    )(tile_expert, n_used, xs, wg, wu, wd)


def _combine_kernel(sa_ref, sb_ref, y_hbm, route_ref, h_ref, g_ref, o_ref, a_buf, b_buf, sem):
    tm = h_ref.shape[0]
    base = pl.program_id(0) * tm

    def issue(r, c):
        _row_copy(y_hbm, sa_ref[base + r], a_buf, r, sem.at[0]).start()
        _row_copy(y_hbm, sb_ref[base + r], b_buf, r, sem.at[1]).start()
        return c

    lax.fori_loop(0, tm, issue, 0)

    def drain(r, c):
        _row_copy(y_hbm, 0, a_buf, r, sem.at[0]).wait()
        _row_copy(y_hbm, 0, b_buf, r, sem.at[1]).wait()
        return c

    lax.fori_loop(0, tm, drain, 0)
    route = route_ref[...]
    f = route[:, 2:3] * a_buf[...] + route[:, 3:4] * b_buf[...]
    o_ref[...] = h_ref[...] + _rms(f, g_ref[...])


def _combine(slot_a, slot_b, y, route, h, g, tm):
    n, d = h.shape
    return pl.pallas_call(
        _combine_kernel,
        grid_spec=pltpu.PrefetchScalarGridSpec(
            num_scalar_prefetch=2,
            grid=(n // tm,),
            in_specs=[pl.BlockSpec(memory_space=pl.ANY),
                      pl.BlockSpec((tm, LANES), lambda i, sa, sb: (i, 0)),
                      pl.BlockSpec((tm, d), lambda i, sa, sb: (i, 0)),
                      pl.BlockSpec((1, d), lambda i, sa, sb: (0, 0))],
            out_specs=pl.BlockSpec((tm, d), lambda i, sa, sb: (i, 0)),
            scratch_shapes=[pltpu.VMEM((tm, d), F32), pltpu.VMEM((tm, d), F32), pltpu.SemaphoreType.DMA((2,))]),
        out_shape=jax.ShapeDtypeStruct((n, d), F32),
        compiler_params=_params(("arbitrary",)),
        name="moe_combine",
    )(slot_a, slot_b, y, route, h, g)


def _moe_plan(route, tm):
    n = route.shape[0]
    eids = jnp.concatenate([route[:, 0], route[:, 1]]).astype(jnp.int32)
    onehot = (eids[:, None] == jnp.arange(N_EXPERTS, dtype=jnp.int32)[None, :]).astype(jnp.int32)
    csum = jnp.cumsum(onehot, axis=0)
    rank = jnp.sum(csum * onehot, axis=1) - 1
    counts = csum[-1]
    padded = ((counts + tm - 1) // tm) * tm
    ends = jnp.cumsum(padded)
    starts = ends - padded
    slot = jnp.sum(onehot * starts[None, :], axis=1) + rank
    n_tiles = 2 * n // tm + N_EXPERTS
    tok = jnp.concatenate([jnp.arange(n, dtype=jnp.int32)] * 2)
    tok_of_slot = jnp.zeros((n_tiles * tm,), jnp.int32).at[slot].set(tok)
    n_used = (ends[-1] // tm).astype(jnp.int32)
    tile_start = jnp.minimum(jnp.arange(n_tiles, dtype=jnp.int32), n_used - 1) * tm
    tile_expert = jnp.sum((tile_start[:, None] >= ends[None, :]).astype(jnp.int32), axis=1)
    return tok_of_slot, slot[:n], slot[n:], tile_expert, n_used.reshape(1)


def _odd_weights(od_w_in, mla_w_uq, mla_w_ukv):
    c2 = 2 * CONV_CH + MLA_Q_RANK + MLA_KV_RANK
    half = MLA_ROPE // 2
    w_kr = od_w_in[:, c2:]
    w_kr_sw = jnp.concatenate([w_kr[:, half:], w_kr[:, :half]], axis=1)
    zl = jnp.zeros((D_MODEL, MLA_NOPE), F32)
    zr = jnp.zeros((D_MODEL, MLA_PAD - MLA_NOPE - MLA_ROPE), F32)
    win = jnp.concatenate([od_w_in[:, :c2], zl, w_kr, zr, zl, w_kr_sw, zr], axis=1)
    dk = MLA_NOPE + MLA_ROPE
    wq = mla_w_uq.reshape(MLA_Q_RANK, MLA_HEADS, dk)
    zq = jnp.zeros((MLA_Q_RANK, MLA_HEADS, MLA_PAD - dk), F32)
    wuq = jnp.concatenate([wq, zq], axis=2).reshape(MLA_Q_RANK, MLA_HEADS * MLA_PAD)
    wq_sw = jnp.concatenate([jnp.zeros_like(wq[:, :, :MLA_NOPE]), wq[:, :, MLA_NOPE + half:],
                             wq[:, :, MLA_NOPE:MLA_NOPE + half], zq], axis=2)
    wuqs = wq_sw.reshape(MLA_Q_RANK, MLA_HEADS * MLA_PAD)
    wkv = mla_w_ukv.reshape(MLA_KV_RANK, MLA_HEADS, MLA_NOPE + MLA_V)
    zk = jnp.zeros((MLA_KV_RANK, MLA_HEADS, MLA_PAD - MLA_NOPE), F32)
    wuk = jnp.concatenate([wkv[:, :, :MLA_NOPE], zk], axis=2).reshape(MLA_KV_RANK, MLA_HEADS * MLA_PAD)
    zv = jnp.zeros((MLA_KV_RANK, MLA_HEADS, MLA_PAD - MLA_V), F32)
    wuv = jnp.concatenate([wkv[:, :, MLA_NOPE:], zv], axis=2).reshape(MLA_KV_RANK, MLA_HEADS * MLA_PAD)
    return win.astype(BF16), wuq.astype(BF16), wuqs.astype(BF16), wuk.astype(BF16), wuv.astype(BF16)


def _rope_tables(seq):
    inv = 1.0 / (ROPE_THETA ** (jnp.arange(0, MLA_ROPE, 2, dtype=F32) / MLA_ROPE))
    ang = jnp.arange(seq, dtype=F32)[:, None] * inv[None, :]
    cos, sin = jnp.cos(ang), jnp.sin(ang)
    ones = jnp.ones((seq, MLA_NOPE), F32)
    zl = jnp.zeros((seq, MLA_NOPE), F32)
    zr = jnp.zeros((seq, MLA_PAD - MLA_NOPE - MLA_ROPE), F32)
    return (jnp.concatenate([ones, cos, cos, zr], axis=1), jnp.concatenate([zl, -sin, sin, zr], axis=1))


def kernel(x, norm_g, ev_w_in, ssm_lambda_re, ssm_lambda_im, ssm_log_dt, ssm_b_re, ssm_b_im, ssm_c_re, ssm_c_im, ssm_d, ssm_w_glu, sgu_ln_g, sgu_ln_b, sgu_w, sgu_b, ev_w_out, ffn_w_gate, ffn_w_up, ffn_w_down, od_w_in, conv_w, conv_b, conv_ln_g, conv_ln_b, mla_q_norm_g, mla_w_uq, mla_kv_norm_g, mla_w_ukv, od_w_out, moe_w_router, moe_w_gate, moe_w_up, moe_w_down):
    bsz, seq, d = x.shape
    n = bsz * seq
    assert d == D_MODEL and SUBLANES % bsz == 0 and seq % 512 == 0
    row = lambda v: v.astype(F32).reshape(1, -1)
    h = x.astype(F32).reshape(n, d)
    tm = 512

    g = norm_g[0]
    proj = _norm_proj(h, row(g[0]), ev_w_in[0].astype(BF16), tm)
    mats = _s5_matrices(ssm_lambda_re[0], ssm_lambda_im[0], ssm_log_dt[0], ssm_b_re[0], ssm_b_im[0],
                        ssm_c_re[0], ssm_c_im[0])
    ys = _s5_mixer(proj[:, :SSM_WIDTH], mats, ssm_d[0], bsz, seq)
    causal = jnp.tril(jnp.ones((SGU_CHUNK, SGU_CHUNK), dtype=bool))
    ws = jnp.where(causal[None], sgu_w[0], 0.0).astype(BF16)
    bias = jnp.repeat(sgu_b[0].astype(F32).T, SGU_HEAD_DIM, axis=1)
    wo = ev_w_out[0].astype(BF16)
    h, z = _even_mix(ys, proj, h, ssm_w_glu[0].astype(BF16), row(sgu_ln_g[0]), row(sgu_ln_b[0]), ws, bias,
                     wo[:SSM_WIDTH], wo[SSM_WIDTH:], row(g[1]), row(g[2]), tm)
    h, z = _dense_ffn(z, ffn_w_gate[0].astype(BF16), ffn_w_up[0].astype(BF16), ffn_w_down[0].astype(BF16),
                      h, row(g[3]), row(norm_g[1][0]), 1024, 512)

    g = norm_g[1]
    win, wuq, wuqs, wuk, wuv = _odd_weights(od_w_in[0], mla_w_uq[0], mla_w_ukv[0])
    cos_t, sin_t = _rope_tables(seq)
    zc, q, k, v = _odd_proj(z, win, row(mla_q_norm_g[0]), row(mla_kv_norm_g[0]), wuq, wuqs, wuk, wuv,
                            cos_t, sin_t, seq, tm)
    hp = MLA_HEADS * MLA_PAD
    yd = _attention(q.reshape(bsz, seq, hp), k.reshape(bsz, seq, hp), v.reshape(bsz, seq, hp), 512)
    conv_w_pad = jnp.concatenate([conv_w[0].astype(F32), jnp.zeros((CONV_HALO - CONV_TAPS, CONV_CH), F32)], axis=0)
    yc = _conv_mixer(zc.reshape(bsz, seq, 2 * CONV_CH), conv_w_pad, row(conv_b[0]), row(conv_ln_g[0]),
                     row(conv_ln_b[0]), tm)
    wo = od_w_out[0].astype(F32)
    wo_b = jnp.concatenate([wo[CONV_CH:].reshape(MLA_HEADS, MLA_V, d),
                            jnp.zeros((MLA_HEADS, MLA_PAD - MLA_V, d), F32)], axis=1).reshape(hp, d)
    wr = jnp.concatenate([moe_w_router[0].astype(F32), jnp.zeros((d, LANES - N_EXPERTS), F32)], axis=1)
    h, z, route = _odd_mix(yc.reshape(n, CONV_CH), yd.reshape(n, hp), h, wo[:CONV_CH].astype(BF16),
                           wo_b.astype(BF16), row(g[1]), row(g[2]), wr.astype(BF16), tm)
    tm_moe = 1024
    tok_of_slot, slot_a, slot_b, tile_expert, n_used = _moe_plan(route, tm_moe)
    xs = _dispatch(tok_of_slot, z, 256)
    y = _moe_ffn(tile_expert, n_used, xs, moe_w_gate[0], moe_w_up[0], moe_w_down[0], tm_moe, 512)
    h = _combine(slot_a, slot_b, y, route, h, row(g[3]), 256)
    return h.reshape(bsz, seq, d).astype(x.dtype)
```

```python
import functools
import math

import jax
import jax.numpy as jnp
from jax import lax
from jax.experimental import pallas as pl
from jax.experimental.pallas import tpu as pltpu

F32 = jnp.float32
BF16 = jnp.bfloat16

D_MODEL = 1024
NORM_EPS = 1e-6
SSM_WIDTH = 512
SSM_GROUP = 16
SSM_GROUPS = 32
SSM_STATE = 64
SSM_CHUNK = 16
SSM_PAIR = 2 * SSM_GROUP * SSM_CHUNK
SGU_WIDTH = 512
SGU_HEADS = 8
SGU_HEAD_DIM = 64
SGU_CHUNK = 128
CONV_CH = 512
CONV_TAPS = 31
CONV_HALO = 32
MLA_HEADS = 8
MLA_Q_RANK = 256
MLA_KV_RANK = 128
MLA_NOPE = 64
MLA_ROPE = 32
MLA_V = 64
MLA_PAD = 128
MLA_VROWS = 80
ROPE_THETA = 10000.0
FF_DENSE = 4096
N_EXPERTS = 8
FF_EXPERT = 3584
LANES = 128
SUBLANES = 8
VMEM_LIMIT = 56 * 1024 * 1024


def _params(sem, vmem=VMEM_LIMIT):
    return pltpu.CompilerParams(dimension_semantics=sem, vmem_limit_bytes=vmem)


def _rms(x, g):
    return x * lax.rsqrt(jnp.mean(x * x, axis=-1, keepdims=True) + NORM_EPS) * g


def _layer_norm(x, g, b):
    mu = jnp.mean(x, axis=-1, keepdims=True)
    xc = x - mu
    return xc * lax.rsqrt(jnp.mean(xc * xc, axis=-1, keepdims=True) + NORM_EPS) * g + b


def _dot(a, b):
    return jnp.dot(a, b, preferred_element_type=F32)


def _norm_proj_kernel(h_ref, g_ref, w_ref, a_ref, b_ref):
    z = _rms(h_ref[...], g_ref[...])
    proj = _dot(z.astype(BF16), w_ref[...])
    a_ref[...] = proj[:, :SSM_WIDTH]
    b_ref[...] = proj[:, SSM_WIDTH:].astype(b_ref.dtype)


def _norm_proj(h, g, w, tm):
    n, d = h.shape
    cols = w.shape[1]
    return pl.pallas_call(
        _norm_proj_kernel,
        grid=(n // tm,),
        in_specs=[pl.BlockSpec((tm, d), lambda i: (i, 0)),
                  pl.BlockSpec((1, d), lambda i: (0, 0)),
                  pl.BlockSpec((d, cols), lambda i: (0, 0))],
        out_specs=[pl.BlockSpec((tm, SSM_WIDTH), lambda i: (i, 0)),
                   pl.BlockSpec((tm, cols - SSM_WIDTH), lambda i: (i, 0))],
        out_shape=[jax.ShapeDtypeStruct((n, SSM_WIDTH), F32), jax.ShapeDtypeStruct((n, cols - SSM_WIDTH), BF16)],
        compiler_params=_params(("parallel",)),
        name="even_in_proj",
    )(h, g, w)


def _s5_matrices(lam_re, lam_im, log_dt, b_re, b_im, c_re, c_im):
    t = SSM_CHUNK
    lr = jnp.minimum(lam_re.astype(F32), -1e-4)
    li = lam_im.astype(F32)
    dt = jnp.exp(log_dt.astype(F32))[:, None]
    mag = jnp.exp(lr * dt)
    a_re = mag * jnp.cos(li * dt)
    a_im = mag * jnp.sin(li * dt)
    den = lr * lr + li * li
    nr = a_re - 1.0
    coef_re = (nr * lr + a_im * li) / den
    coef_im = (a_im * lr - nr * li) / den
    br = b_re.astype(F32)
    bi = b_im.astype(F32)
    bb_re = coef_re[..., None] * br - coef_im[..., None] * bi
    bb_im = coef_re[..., None] * bi + coef_im[..., None] * br
    cr = c_re.astype(F32)
    ci = c_im.astype(F32)
    pw_re = [jnp.ones_like(a_re)]
    pw_im = [jnp.zeros_like(a_im)]
    for _ in range(t):
        pr, pi = pw_re[-1], pw_im[-1]
        pw_re.append(pr * a_re - pi * a_im)
        pw_im.append(pr * a_im + pi * a_re)
    pw_re = jnp.stack(pw_re)
    pw_im = jnp.stack(pw_im)
    ab_re = pw_re[:t, :, :, None] * bb_re[None] - pw_im[:t, :, :, None] * bb_im[None]
    ab_im = pw_re[:t, :, :, None] * bb_im[None] + pw_im[:t, :, :, None] * bb_re[None]
    hi = lax.Precision.HIGHEST
    k_lag = (jnp.einsum('gnp,tgpm->tgnm', cr, ab_re, precision=hi)
             - jnp.einsum('gnp,tgpm->tgnm', ci, ab_im, precision=hi))
    s_idx = jnp.arange(t)[:, None]
    t_idx = jnp.arange(t)[None, :]
    lag = jnp.clip(t_idx - s_idx, 0, t - 1)
    w_intra = k_lag[lag]
    w_intra = jnp.where((t_idx >= s_idx)[:, :, None, None, None], w_intra, 0.0)
    w_intra = w_intra.transpose(2, 0, 4, 1, 3)
    rev_re = pw_re[:t][::-1]
    rev_im = pw_im[:t][::-1]
    ws_re = rev_re[..., None] * bb_re[None] - rev_im[..., None] * bb_im[None]
    ws_im = rev_re[..., None] * bb_im[None] + rev_im[..., None] * bb_re[None]
    ws_re = ws_re.transpose(1, 0, 3, 2)
    ws_im = ws_im.transpose(1, 0, 3, 2)
    ca_re = cr[None] * pw_re[1:, :, None, :] - ci[None] * pw_im[1:, :, None, :]
    ca_im = cr[None] * pw_im[1:, :, None, :] + ci[None] * pw_re[1:, :, None, :]
    wo_re = ca_re.transpose(1, 3, 0, 2)
    wo_im = (-ca_im).transpose(1, 3, 0, 2)

    eye = jnp.eye(2, dtype=F32)
    n_pairs = SSM_GROUPS // 2
    pair = lambda w: w.reshape((n_pairs, 2) + w.shape[1:])
    st = 2 * SSM_STATE

    def intra(w):
        w = w.transpose(0, 2, 1, 3, 4, 5)[:, :, :, :, :, None, :] * eye[None, None, :, None, None, :, None]
        return w.reshape(n_pairs, SSM_PAIR, SSM_PAIR)

    def state(w):
        w = w.transpose(0, 2, 1, 3, 4)[:, :, :, :, None, :] * eye[None, None, :, None, :, None]
        return w.reshape(n_pairs, SSM_PAIR, st)

    def readout(w):
        w = w[:, :, :, :, None, :] * eye[None, :, None, None, :, None]
        return w.reshape(n_pairs, st, SSM_PAIR)

    w_intra = intra(pair(w_intra))
    ws_re, ws_im = state(pair(ws_re)), state(pair(ws_im))
    wo_re, wo_im = readout(pair(wo_re)), readout(pair(wo_im))
    return dict(
        w_intra=w_intra.astype(BF16), ws_re=ws_re.astype(BF16), ws_im=ws_im.astype(BF16),
        wo_re=wo_re.astype(BF16), wo_im=wo_im.astype(BF16),
        at_re=pw_re[t].reshape(1, SSM_GROUPS * SSM_STATE), at_im=pw_im[t].reshape(1, SSM_GROUPS * SSM_STATE))


S5_LANE_PAIRS = LANES // (2 * SSM_GROUP)
S5_SCAN_LANES = 512


def _s5_state_kernel(u0_ref, u1_ref, u2_ref, u3_ref, wre_ref, wim_ref, are_ref, aim_ref,
                     x_ref, hre_ref, him_ref, sre_ref, sim_ref):
    n_chunks = x_ref.shape[0]
    pw = 2 * SSM_GROUP
    u_refs = (u0_ref, u1_ref, u2_ref, u3_ref)
    for t in range(SSM_CHUNK):
        for j, u_ref in enumerate(u_refs):
            ut = u_ref[pl.ds(t, n_chunks, stride=SSM_CHUNK), :]
            for qq in range(S5_LANE_PAIRS):
                q = j * S5_LANE_PAIRS + qq
                x_ref[:, q * SSM_PAIR + t * pw: q * SSM_PAIR + (t + 1) * pw] = (
                    ut[:, qq * pw:(qq + 1) * pw].astype(x_ref.dtype))
    st = 2 * SSM_STATE
    for q in range(SSM_GROUPS // 2):
        xq = x_ref[:, q * SSM_PAIR:(q + 1) * SSM_PAIR]
        sre_ref[:, q * st:(q + 1) * st] = _dot(xq, wre_ref[q])
        sim_ref[:, q * st:(q + 1) * st] = _dot(xq, wim_ref[q])

    row = lax.broadcasted_iota(jnp.int32, (SUBLANES, S5_SCAN_LANES), 0)
    zero = jnp.zeros((SUBLANES, S5_SCAN_LANES), F32)
    for c0 in range(0, sre_ref.shape[1], S5_SCAN_LANES):
        cols = pl.ds(c0, S5_SCAN_LANES)
        ar = are_ref[:, cols]
        ai = aim_ref[:, cols]

        def body(k, carry, cols=cols, ar=ar, ai=ai):
            r0 = pl.multiple_of(k * SUBLANES, SUBLANES)
            sr = sre_ref[pl.ds(r0, SUBLANES), cols]
            si = sim_ref[pl.ds(r0, SUBLANES), cols]
            out_r, out_i = carry
            for i in range(1, SUBLANES + 1):
                tr = ar * out_r - ai * out_i + sr
                ti = ar * out_i + ai * out_r + si
                tr = pltpu.roll(tr, 1, 0)
                ti = pltpu.roll(ti, 1, 0)
                if i < SUBLANES:
                    out_r = jnp.where(row == i, tr, out_r)
                    out_i = jnp.where(row == i, ti, out_i)
            hre_ref[pl.ds(r0, SUBLANES), cols] = out_r
            him_ref[pl.ds(r0, SUBLANES), cols] = out_i
            return tr, ti

        lax.fori_loop(0, n_chunks // SUBLANES, body, (zero, zero))


def _s5_out_kernel(x_ref, wi_ref, hre_ref, him_ref, wore_ref, woim_ref, d_ref, y_ref, yt_ref):
    n_chunks = x_ref.shape[0]
    pw = 2 * SSM_GROUP
    st = 2 * SSM_STATE
    for qq in range(S5_LANE_PAIRS):
        x = x_ref[:, qq * SSM_PAIR:(qq + 1) * SSM_PAIR]
        y = _dot(x, wi_ref[qq])
        y += _dot(hre_ref[:, qq * st:(qq + 1) * st].astype(BF16), wore_ref[qq])
        y += _dot(him_ref[:, qq * st:(qq + 1) * st].astype(BF16), woim_ref[qq])
        y += d_ref[:, qq * SSM_PAIR:(qq + 1) * SSM_PAIR] * x.astype(F32)
        y = jax.nn.gelu(y)
        for t in range(SSM_CHUNK):
            yt_ref[t, :, qq * pw:(qq + 1) * pw] = y[:, t * pw:(t + 1) * pw]
    for t in range(SSM_CHUNK):
        y_ref[pl.ds(t, n_chunks, stride=SSM_CHUNK), :] = yt_ref[t]


def _s5_mixer(u, mats, d, batch, seq):
    t = SSM_CHUNK
    n_chunks = seq // t
    n_pairs = SSM_GROUPS // 2
    cols = n_pairs * SSM_PAIR
    st = 2 * SSM_STATE
    n_state = n_pairs * st
    n_blk = SSM_WIDTH // LANES
    assert n_blk == 4 and n_chunks % SUBLANES == 0
    once = pl.Buffered(1)
    x, h_re, h_im = pl.pallas_call(
        _s5_state_kernel,
        grid=(batch,),
        in_specs=[pl.BlockSpec((seq, LANES), lambda b, j=j: (b, j)) for j in range(n_blk)] + [
            pl.BlockSpec((n_pairs, SSM_PAIR, st), lambda b: (0, 0, 0), pipeline_mode=once),
            pl.BlockSpec((n_pairs, SSM_PAIR, st), lambda b: (0, 0, 0), pipeline_mode=once),
            pl.BlockSpec((1, n_state), lambda b: (0, 0)),
            pl.BlockSpec((1, n_state), lambda b: (0, 0))],
        out_specs=[pl.BlockSpec((n_chunks, cols), lambda b: (b, 0)),
                   pl.BlockSpec((n_chunks, n_state), lambda b: (b, 0)),
                   pl.BlockSpec((n_chunks, n_state), lambda b: (b, 0))],
        out_shape=[jax.ShapeDtypeStruct((batch * n_chunks, cols), BF16),
                   jax.ShapeDtypeStruct((batch * n_chunks, n_state), F32),
                   jax.ShapeDtypeStruct((batch * n_chunks, n_state), F32)],
        scratch_shapes=[pltpu.VMEM((n_chunks, n_state), F32), pltpu.VMEM((n_chunks, n_state), F32)],
        compiler_params=_params(("parallel",)),
        name="s5_state_scan",
    )(u, u, u, u, mats['ws_re'], mats['ws_im'], mats['at_re'], mats['at_im'])
    lp = S5_LANE_PAIRS
    d_cols = jnp.broadcast_to(d.astype(F32).reshape(n_pairs, 1, 2 * SSM_GROUP),
                              (n_pairs, t, 2 * SSM_GROUP)).reshape(1, cols)
    return pl.pallas_call(
        _s5_out_kernel,
        grid=(batch, n_blk),
        in_specs=[pl.BlockSpec((n_chunks, lp * SSM_PAIR), lambda b, j: (b, j)),
                  pl.BlockSpec((lp, SSM_PAIR, SSM_PAIR), lambda b, j: (j, 0, 0)),
                  pl.BlockSpec((n_chunks, lp * st), lambda b, j: (b, j)),
                  pl.BlockSpec((n_chunks, lp * st), lambda b, j: (b, j)),
                  pl.BlockSpec((lp, st, SSM_PAIR), lambda b, j: (j, 0, 0)),
                  pl.BlockSpec((lp, st, SSM_PAIR), lambda b, j: (j, 0, 0)),
                  pl.BlockSpec((1, lp * SSM_PAIR), lambda b, j: (0, j))],
        out_specs=pl.BlockSpec((seq, LANES), lambda b, j: (b, j)),
        out_shape=jax.ShapeDtypeStruct((batch * seq, SSM_WIDTH), F32),
        scratch_shapes=[pltpu.VMEM((t, n_chunks, LANES), F32)],
        compiler_params=_params(("parallel", "parallel")),
        name="s5_out",
    )(x, mats['w_intra'], h_re, h_im, mats['wo_re'], mats['wo_im'], d_cols)


def _even_mix_kernel(ys_ref, bu_ref, bv_ref, h_ref, wglu_ref, lng_ref, lnb_ref, ws_ref, bias_ref,
                     wo_a_ref, wo_b_ref, g1_ref, g2_ref, hout_ref, z_ref, s_scr):
    tm = ys_ref.shape[0]
    ys = ys_ref[...]
    ya = ys * jax.nn.sigmoid(_dot(ys.astype(BF16), wglu_ref[...]))
    u = jax.nn.gelu(bu_ref[...].astype(F32))
    v = _layer_norm(jax.nn.gelu(bv_ref[...].astype(F32)), lng_ref[...], lnb_ref[...])
    lane = lax.broadcasted_iota(jnp.int32, v.shape, 1)
    left = (lane % LANES) < SGU_HEAD_DIM
    v_l = jnp.where(left, v, 0.0).astype(BF16)
    v_r = jnp.where(left, 0.0, v).astype(BF16)
    for c in range(tm // SGU_CHUNK):
        rows = slice(c * SGU_CHUNK, (c + 1) * SGU_CHUNK)
        for p in range(SGU_HEADS // 2):
            cols = slice(p * LANES, (p + 1) * LANES)
            s_scr[rows, cols] = (_dot(ws_ref[2 * p], v_l[rows, cols]) + _dot(ws_ref[2 * p + 1], v_r[rows, cols]))
    bias = jnp.concatenate([bias_ref[...]] * (tm // SGU_CHUNK), axis=0)
    yb = u * (s_scr[...] + bias)
    mix = _dot(ya.astype(BF16), wo_a_ref[...]) + _dot(yb.astype(BF16), wo_b_ref[...])
    h_new = h_ref[...] + _rms(mix, g1_ref[...])
    hout_ref[...] = h_new
    z_ref[...] = _rms(h_new, g2_ref[...]).astype(z_ref.dtype)


def _even_mix(ys, proj, h, wglu, lng, lnb, ws, bias, wo_a, wo_b, g1, g2, tm):
    n, d = h.shape
    w = SGU_WIDTH
    const = lambda *shape: pl.BlockSpec(shape, lambda i: (0,) * len(shape))
    return pl.pallas_call(
        _even_mix_kernel,
        grid=(n // tm,),
        in_specs=[pl.BlockSpec((tm, w), lambda i: (i, 0)),
                  pl.BlockSpec((tm, w), lambda i: (i, 0)),
                  pl.BlockSpec((tm, w), lambda i: (i, 1)),
                  pl.BlockSpec((tm, d), lambda i: (i, 0)),
                  const(w, w), const(1, w), const(1, w),
                  const(SGU_HEADS, SGU_CHUNK, SGU_CHUNK), const(SGU_CHUNK, w),
                  const(w, d), const(w, d), const(1, d), const(1, d)],
        out_specs=[pl.BlockSpec((tm, d), lambda i: (i, 0)),
                   pl.BlockSpec((tm, d), lambda i: (i, 0))],
        out_shape=[jax.ShapeDtypeStruct((n, d), F32), jax.ShapeDtypeStruct((n, d), BF16)],
        scratch_shapes=[pltpu.VMEM((tm, w), F32)],
        compiler_params=_params(("parallel",)),
        name="even_mix",
    )(ys, proj, proj, h, wglu, lng, lnb, ws, bias, wo_a, wo_b, g1, g2)


def _ffn_kernel(z_ref, wg_ref, wu_ref, wd_ref, h_ref, g3_ref, gn_ref, hout_ref, zout_ref, acc_ref):
    j = pl.program_id(1)

    @pl.when(j == 0)
    def _():
        acc_ref[...] = jnp.zeros_like(acc_ref)

    z = z_ref[...]
    a = jax.nn.silu(_dot(z, wg_ref[...])) * _dot(z, wu_ref[...])
    acc_ref[...] += _dot(a.astype(BF16), wd_ref[...])

    @pl.when(j == pl.num_programs(1) - 1)
    def _():
        h_new = h_ref[...] + _rms(acc_ref[...], g3_ref[...])
        hout_ref[...] = h_new
        zout_ref[...] = _rms(h_new, gn_ref[...]).astype(zout_ref.dtype)


def _dense_ffn(z, wg, wu, wd, h, g3, g_next, tm, tf):
    n, d = h.shape
    ff = wg.shape[1]
    return pl.pallas_call(
        _ffn_kernel,
        grid=(n // tm, ff // tf),
        in_specs=[pl.BlockSpec((tm, d), lambda i, j: (i, 0)),
                  pl.BlockSpec((d, tf), lambda i, j: (0, j)),
                  pl.BlockSpec((d, tf), lambda i, j: (0, j)),
                  pl.BlockSpec((tf, d), lambda i, j: (j, 0)),
                  pl.BlockSpec((tm, d), lambda i, j: (i, 0)),
                  pl.BlockSpec((1, d), lambda i, j: (0, 0)),
                  pl.BlockSpec((1, d), lambda i, j: (0, 0))],
        out_specs=[pl.BlockSpec((tm, d), lambda i, j: (i, 0)),
                   pl.BlockSpec((tm, d), lambda i, j: (i, 0))],
        out_shape=[jax.ShapeDtypeStruct((n, d), F32), jax.ShapeDtypeStruct((n, d), BF16)],
        scratch_shapes=[pltpu.VMEM((tm, d), F32)],
        compiler_params=_params(("parallel", "arbitrary")),
        name="dense_ffn",
    )(z, wg, wu, wd, h, g3, g_next)


def _odd_proj_kernel(z_ref, win_ref, gq_ref, gkv_ref, wuq_ref, wuqs_ref, wuk_ref, wuv_ref, vone_ref, cos_ref, sin_ref,
                     zc_ref, q_ref, k_ref, v_ref, *, scale):
    z = z_ref[...]
    proj = _dot(z, win_ref[...])
    c0 = 2 * CONV_CH
    c1 = c0 + MLA_Q_RANK
    c2 = c1 + MLA_KV_RANK
    c3 = c2 + MLA_PAD
    zc_ref[...] = proj[:, :c0].astype(zc_ref.dtype)
    cq = _rms(proj[:, c0:c1], gq_ref[...]).astype(BF16)
    ckv = _rms(proj[:, c1:c2], gkv_ref[...]).astype(BF16)
    cos = cos_ref[...]
    sin = sin_ref[...]
    cos_h = jnp.concatenate([cos] * MLA_HEADS, axis=1)
    sin_h = jnp.concatenate([sin] * MLA_HEADS, axis=1)
    q = _dot(cq, wuq_ref[...]) * cos_h + _dot(cq, wuqs_ref[...]) * sin_h
    q_ref[...] = (q * scale).astype(q_ref.dtype)
    kr = proj[:, c2:c3] * cos + proj[:, c3:] * sin
    k = _dot(ckv, wuk_ref[...]) + jnp.concatenate([kr] * MLA_HEADS, axis=1)
    k_ref[...] = k.astype(k_ref.dtype)
    vt = lax.dot_general(wuv_ref[...], ckv, (((1,), (1,)), ((), ())), preferred_element_type=F32)
    v_ref[0] = (vt + vone_ref[...]).astype(v_ref.dtype)


def _odd_proj(z, win, gq, gkv, wuq, wuqs, wuk, wuv_t, v_one, cos_t, sin_t, seq, tm):
    n, d = z.shape
    hp = MLA_HEADS * MLA_PAD
    vr = MLA_HEADS * MLA_VROWS
    n_l = seq // tm
    const = lambda *shape: pl.BlockSpec(shape, lambda i: (0,) * len(shape))
    out = jax.ShapeDtypeStruct((n, hp), BF16)
    scale = float((MLA_NOPE + MLA_ROPE) ** -0.5 * math.log2(math.e))
    return pl.pallas_call(
        functools.partial(_odd_proj_kernel, scale=scale),
        grid=(n // tm,),
        in_specs=[pl.BlockSpec((tm, d), lambda i: (i, 0)),
                  const(d, win.shape[1]), const(1, MLA_Q_RANK), const(1, MLA_KV_RANK),
                  const(MLA_Q_RANK, hp), const(MLA_Q_RANK, hp), const(MLA_KV_RANK, hp), const(vr, MLA_KV_RANK),
                  const(vr, 1),
                  pl.BlockSpec((tm, MLA_PAD), lambda i: (i % n_l, 0)),
                  pl.BlockSpec((tm, MLA_PAD), lambda i: (i % n_l, 0))],
        out_specs=[pl.BlockSpec((tm, 2 * CONV_CH), lambda i: (i, 0)),
                   pl.BlockSpec((tm, hp), lambda i: (i, 0)),
                   pl.BlockSpec((tm, hp), lambda i: (i, 0)),
                   pl.BlockSpec((1, vr, tm), lambda i: (i, 0, 0))],
        out_shape=[jax.ShapeDtypeStruct((n, 2 * CONV_CH), BF16), out, out,
                   jax.ShapeDtypeStruct((n // tm, vr, tm), BF16)],
        compiler_params=_params(("parallel",)),
        name="odd_in_proj",
    )(z, win, gq, gkv, wuq, wuqs, wuk, wuv_t, v_one, cos_t, sin_t)


def _attn_kernel(q_ref, k_ref, vt_ref, o_ref, acc_ref, *, blk):
    i = pl.program_id(2)
    acc_ref[...] = jnp.zeros_like(acc_ref)

    def step(j, m, masked):
        r0 = pl.multiple_of(j * blk, blk)
        m_out = []
        for hh in range(2):
            q = q_ref[0, :, hh * MLA_PAD:(hh + 1) * MLA_PAD]
            k = k_ref[0, pl.ds(r0, blk), hh * MLA_PAD:(hh + 1) * MLA_PAD]
            st = lax.dot_general(k, q, (((1,), (1,)), ((), ())), preferred_element_type=F32)
            if masked:
                key = lax.broadcasted_iota(jnp.int32, st.shape, 0)
                qry = lax.broadcasted_iota(jnp.int32, st.shape, 1)
                st = jnp.where(key <= qry, st, -1e30)
            m_new = jnp.maximum(m[hh], jnp.max(st, axis=0, keepdims=True))
            alpha = jnp.exp2(m[hh] - m_new)
            p = jnp.exp2(st - m_new).astype(BF16)
            vt = vt_ref[j, hh * MLA_VROWS:(hh + 1) * MLA_VROWS, :]
            acc_ref[hh] = alpha * acc_ref[hh] + _dot(vt, p)
            m_out.append(m_new)
        return tuple(m_out)

    init = jnp.full((1, blk), -1e30, F32)
    m = lax.fori_loop(0, i, lambda j, m: step(j, m, False), (init, init))
    step(i, m, True)
    a0 = acc_ref[0]
    a1 = acc_ref[1]
    ot = jnp.concatenate([a0[:MLA_V] / a0[MLA_V:MLA_V + 1], a1[:MLA_V] / a1[MLA_V:MLA_V + 1]], axis=0)
    o_ref[0] = ot.T.astype(o_ref.dtype)


def _attention(q, k, vt, blk):
    b, seq, _ = q.shape
    n_blk = seq // blk
    return pl.pallas_call(
        functools.partial(_attn_kernel, blk=blk),
        grid=(b, MLA_HEADS // 2, n_blk),
        in_specs=[pl.BlockSpec((1, blk, 2 * MLA_PAD), lambda bi, p, i: (bi, i, p)),
                  pl.BlockSpec((1, seq, 2 * MLA_PAD), lambda bi, p, i: (bi, 0, p)),
                  pl.BlockSpec((n_blk, 2 * MLA_VROWS, blk), lambda bi, p, i: (bi, p, 0))],
        out_specs=pl.BlockSpec((1, blk, 2 * MLA_V), lambda bi, p, i: (bi, i, p)),
        out_shape=jax.ShapeDtypeStruct((b, seq, MLA_HEADS * MLA_V), BF16),
        scratch_shapes=[pltpu.VMEM((2, MLA_VROWS, blk), F32)],
        compiler_params=_params(("parallel", "parallel", "parallel")),
        name="mla_attention",
    )(q, k, vt)


def _conv_kernel(zc_ref, w_ref, b_ref, lng_ref, lnb_ref, y_ref, buf_ref):
    tm = zc_ref.shape[1]

    @pl.when(pl.program_id(1) == 0)
    def _():
        buf_ref[pl.ds(0, CONV_HALO), :] = jnp.zeros((CONV_HALO, CONV_CH), F32)

    zc = zc_ref[0].astype(F32)
    hh = zc[:, :CONV_CH] * jax.nn.sigmoid(zc[:, CONV_CH:])
    buf_ref[pl.ds(CONV_HALO, tm), :] = hh
    off = CONV_HALO - (CONV_TAPS - 1)
    acc = jnp.zeros((tm, CONV_CH), F32) + b_ref[...]
    for kk in range(CONV_TAPS):
        acc = acc + w_ref[pl.ds(kk, 1), :] * buf_ref[pl.ds(off + kk, tm), :]
    buf_ref[pl.ds(0, CONV_HALO), :] = buf_ref[pl.ds(tm, CONV_HALO), :]
    y_ref[0] = jax.nn.silu(_layer_norm(acc, lng_ref[...], lnb_ref[...])).astype(y_ref.dtype)


def _conv_mixer(zc, w, b, lng, lnb, tm):
    bsz, seq, _ = zc.shape
    const = lambda *shape: pl.BlockSpec(shape, lambda bi, i: (0,) * len(shape))
    return pl.pallas_call(
        _conv_kernel,
        grid=(bsz, seq // tm),
        in_specs=[pl.BlockSpec((1, tm, 2 * CONV_CH), lambda bi, i: (bi, i, 0)),
                  const(CONV_HALO, CONV_CH), const(1, CONV_CH), const(1, CONV_CH), const(1, CONV_CH)],
        out_specs=pl.BlockSpec((1, tm, CONV_CH), lambda bi, i: (bi, i, 0)),
        out_shape=jax.ShapeDtypeStruct((bsz, seq, CONV_CH), BF16),
        scratch_shapes=[pltpu.VMEM((tm + CONV_HALO, CONV_CH), F32)],
        compiler_params=_params(("arbitrary", "arbitrary")),
        name="conv_module",
    )(zc, w, b, lng, lnb)


def _odd_mix_kernel(yc_ref, yd_ref, h_ref, wo_a_ref, wo_b_ref, g1_ref, g2_ref, wr_ref, hout_ref, z_ref, route_ref):
    mix = _dot(yc_ref[...], wo_a_ref[...]) + _dot(yd_ref[...], wo_b_ref[...])
    h_new = h_ref[...] + _rms(mix, g1_ref[...])
    hout_ref[...] = h_new
    z = _rms(h_new, g2_ref[...])
    z_ref[...] = z
    logits = _dot(z.astype(BF16), wr_ref[...])
    lane = lax.broadcasted_iota(jnp.int32, logits.shape, 1)
    neg = -jnp.inf
    logits = jnp.where(lane < N_EXPERTS, logits, neg)
    m1 = jnp.max(logits, axis=-1, keepdims=True)
    i1 = jnp.min(jnp.where(logits == m1, lane, LANES), axis=-1, keepdims=True)
    rest = jnp.where(lane == i1, neg, logits)
    m2 = jnp.max(rest, axis=-1, keepdims=True)
    i2 = jnp.min(jnp.where(rest == m2, lane, LANES), axis=-1, keepdims=True)
    e = jnp.exp(m2 - m1)
    w1 = 1.0 / (1.0 + e)
    w2 = e / (1.0 + e)
    route = jnp.where(lane == 0, i1.astype(F32),
                      jnp.where(lane == 1, i2.astype(F32),
                                jnp.where(lane == 2, w1, jnp.where(lane == 3, w2, 0.0))))
    route_ref[...] = route


def _odd_mix(yc, yd, h, wo_a, wo_b, g1, g2, wr, tm):
    n, d = h.shape
    const = lambda *shape: pl.BlockSpec(shape, lambda i: (0,) * len(shape))
    return pl.pallas_call(
        _odd_mix_kernel,
        grid=(n // tm,),
        in_specs=[pl.BlockSpec((tm, yc.shape[1]), lambda i: (i, 0)),
                  pl.BlockSpec((tm, yd.shape[1]), lambda i: (i, 0)),
                  pl.BlockSpec((tm, d), lambda i: (i, 0)),
                  const(*wo_a.shape), const(*wo_b.shape), const(1, d), const(1, d), const(d, LANES)],
        out_specs=[pl.BlockSpec((tm, d), lambda i: (i, 0)),
                   pl.BlockSpec((tm, d), lambda i: (i, 0)),
                   pl.BlockSpec((tm, LANES), lambda i: (i, 0))],
        out_shape=[jax.ShapeDtypeStruct((n, d), F32), jax.ShapeDtypeStruct((n, d), F32),
                   jax.ShapeDtypeStruct((n, LANES), F32)],
        compiler_params=_params(("parallel",)),
        name="odd_mix_router",
    )(yc, yd, h, wo_a, wo_b, g1, g2, wr)


def _row_copy(src_hbm, row, dst_ref, r, sem):
    return pltpu.make_async_copy(src_hbm.at[pl.ds(row, 1), :], dst_ref.at[pl.ds(r, 1), :], sem)


def _dispatch_kernel(tok_ref, z_hbm, o_ref, sem):
    tg = o_ref.shape[0]
    base = pl.program_id(0) * tg

    def issue(r, c):
        _row_copy(z_hbm, tok_ref[base + r], o_ref, r, sem).start()
        return c

    lax.fori_loop(0, tg, issue, 0)

    def drain(r, c):
        _row_copy(z_hbm, 0, o_ref, r, sem).wait()
        return c

    lax.fori_loop(0, tg, drain, 0)


def _dispatch(tok_of_slot, z, tg):
    n_slots = tok_of_slot.shape[0]
    d = z.shape[1]
    return pl.pallas_call(
        _dispatch_kernel,
        grid_spec=pltpu.PrefetchScalarGridSpec(
            num_scalar_prefetch=1,
            grid=(n_slots // tg,),
            in_specs=[pl.BlockSpec(memory_space=pl.ANY)],
            out_specs=pl.BlockSpec((tg, d), lambda i, tok: (i, 0)),
            scratch_shapes=[pltpu.SemaphoreType.DMA(())]),
        out_shape=jax.ShapeDtypeStruct((n_slots, d), z.dtype),
        compiler_params=_params(("arbitrary",)),
        name="moe_dispatch",
    )(tok_of_slot, z)


def _moe_ffn_kernel(te_ref, nu_ref, x_ref, wg_ref, wu_ref, wd_ref, y_ref, xb_ref, acc_ref):
    i = pl.program_id(0)
    j = pl.program_id(1)
    used = i < nu_ref[0]

    @pl.when(j == 0)
    def _():
        xb_ref[...] = x_ref[...].astype(BF16)
        acc_ref[...] = jnp.zeros_like(acc_ref)

    @pl.when(used)
    def _():
        x = xb_ref[...]
        a = jax.nn.silu(_dot(x, wg_ref[0].astype(BF16))) * _dot(x, wu_ref[0].astype(BF16))
        acc_ref[...] += _dot(a.astype(BF16), wd_ref[0].astype(BF16))

    @pl.when(j == pl.num_programs(1) - 1)
    def _():
        y_ref[...] = acc_ref[...]


def _moe_ffn(tile_expert, n_used, xs, wg, wu, wd, tm, tf):
    n_slots, d = xs.shape
    ff = wg.shape[2]
    n_f = ff // tf

    def col(i, j, nu):
        return jnp.where(i < nu[0], j, n_f - 1)

    return pl.pallas_call(
        _moe_ffn_kernel,
        grid_spec=pltpu.PrefetchScalarGridSpec(
            num_scalar_prefetch=2,
            grid=(n_slots // tm, n_f),
            in_specs=[pl.BlockSpec((tm, d), lambda i, j, te, nu: (i, 0)),
                      pl.BlockSpec((1, d, tf), lambda i, j, te, nu: (te[i], 0, col(i, j, nu))),
                      pl.BlockSpec((1, d, tf), lambda i, j, te, nu: (te[i], 0, col(i, j, nu))),
                      pl.BlockSpec((1, tf, d), lambda i, j, te, nu: (te[i], col(i, j, nu), 0))],
            out_specs=pl.BlockSpec((tm, d), lambda i, j, te, nu: (i, 0)),
            scratch_shapes=[pltpu.VMEM((tm, d), BF16), pltpu.VMEM((tm, d), F32)]),
        out_shape=jax.ShapeDtypeStruct((n_slots, d), F32),
        compiler_params=_params(("arbitrary", "arbitrary")),
        name="moe_grouped_ffn",
    )(tile_expert, n_used, xs, wg, wu, wd)


def _combine_kernel(sa_ref, sb_ref, y_hbm, route_ref, h_ref, g_ref, o_ref, a_buf, b_buf, sem):
    tm = h_ref.shape[0]
    base = pl.program_id(0) * tm

    def issue(r, c):
        _row_copy(y_hbm, sa_ref[base + r], a_buf, r, sem.at[0]).start()
        _row_copy(y_hbm, sb_ref[base + r], b_buf, r, sem.at[1]).start()
        return c

    lax.fori_loop(0, tm, issue, 0)

    def drain(r, c):
        _row_copy(y_hbm, 0, a_buf, r, sem.at[0]).wait()
        _row_copy(y_hbm, 0, b_buf, r, sem.at[1]).wait()
        return c

    lax.fori_loop(0, tm, drain, 0)
    route = route_ref[...]
    f = route[:, 2:3] * a_buf[...] + route[:, 3:4] * b_buf[...]
    o_ref[...] = h_ref[...] + _rms(f, g_ref[...])


def _combine(slot_a, slot_b, y, route, h, g, tm):
    n, d = h.shape
    return pl.pallas_call(
        _combine_kernel,
        grid_spec=pltpu.PrefetchScalarGridSpec(
            num_scalar_prefetch=2,
            grid=(n // tm,),
            in_specs=[pl.BlockSpec(memory_space=pl.ANY),
                      pl.BlockSpec((tm, LANES), lambda i, sa, sb: (i, 0)),
                      pl.BlockSpec((tm, d), lambda i, sa, sb: (i, 0)),
                      pl.BlockSpec((1, d), lambda i, sa, sb: (0, 0))],
            out_specs=pl.BlockSpec((tm, d), lambda i, sa, sb: (i, 0)),
            scratch_shapes=[pltpu.VMEM((tm, d), F32), pltpu.VMEM((tm, d), F32), pltpu.SemaphoreType.DMA((2,))]),
        out_shape=jax.ShapeDtypeStruct((n, d), F32),
        compiler_params=_params(("arbitrary",)),
        name="moe_combine",
    )(slot_a, slot_b, y, route, h, g)


def _moe_plan(route, tm):
    n = route.shape[0]
    eids = jnp.concatenate([route[:, 0], route[:, 1]]).astype(jnp.int32)
    onehot = (eids[:, None] == jnp.arange(N_EXPERTS, dtype=jnp.int32)[None, :]).astype(jnp.int32)
    csum = jnp.cumsum(onehot, axis=0)
    rank = jnp.sum(csum * onehot, axis=1) - 1
    counts = csum[-1]
    padded = ((counts + tm - 1) // tm) * tm
    ends = jnp.cumsum(padded)
    starts = ends - padded
    slot = jnp.sum(onehot * starts[None, :], axis=1) + rank
    n_tiles = 2 * n // tm + N_EXPERTS
    tok = jnp.concatenate([jnp.arange(n, dtype=jnp.int32)] * 2)
    tok_of_slot = jnp.zeros((n_tiles * tm,), jnp.int32).at[slot].set(tok)
    n_used = (ends[-1] // tm).astype(jnp.int32)
    tile_start = jnp.minimum(jnp.arange(n_tiles, dtype=jnp.int32), n_used - 1) * tm
    tile_expert = jnp.sum((tile_start[:, None] >= ends[None, :]).astype(jnp.int32), axis=1)
    return tok_of_slot, slot[:n], slot[n:], tile_expert, n_used.reshape(1)


def _odd_weights(od_w_in, mla_w_uq, mla_w_ukv):
    c2 = 2 * CONV_CH + MLA_Q_RANK + MLA_KV_RANK
    half = MLA_ROPE // 2
    w_kr = od_w_in[:, c2:]
    w_kr_sw = jnp.concatenate([w_kr[:, half:], w_kr[:, :half]], axis=1)
    zl = jnp.zeros((D_MODEL, MLA_NOPE), F32)
    zr = jnp.zeros((D_MODEL, MLA_PAD - MLA_NOPE - MLA_ROPE), F32)
    win = jnp.concatenate([od_w_in[:, :c2], zl, w_kr, zr, zl, w_kr_sw, zr], axis=1)
    dk = MLA_NOPE + MLA_ROPE
    wq = mla_w_uq.reshape(MLA_Q_RANK, MLA_HEADS, dk)
    zq = jnp.zeros((MLA_Q_RANK, MLA_HEADS, MLA_PAD - dk), F32)
    wuq = jnp.concatenate([wq, zq], axis=2).reshape(MLA_Q_RANK, MLA_HEADS * MLA_PAD)
    wq_sw = jnp.concatenate([jnp.zeros_like(wq[:, :, :MLA_NOPE]), wq[:, :, MLA_NOPE + half:],
                             wq[:, :, MLA_NOPE:MLA_NOPE + half], zq], axis=2)
    wuqs = wq_sw.reshape(MLA_Q_RANK, MLA_HEADS * MLA_PAD)
    wkv = mla_w_ukv.reshape(MLA_KV_RANK, MLA_HEADS, MLA_NOPE + MLA_V)
    zk = jnp.zeros((MLA_KV_RANK, MLA_HEADS, MLA_PAD - MLA_NOPE), F32)
    wuk = jnp.concatenate([wkv[:, :, :MLA_NOPE], zk], axis=2).reshape(MLA_KV_RANK, MLA_HEADS * MLA_PAD)
    zv = jnp.zeros((MLA_KV_RANK, MLA_HEADS, MLA_VROWS - MLA_V), F32)
    wuv_t = jnp.concatenate([wkv[:, :, MLA_NOPE:], zv], axis=2).reshape(MLA_KV_RANK, MLA_HEADS * MLA_VROWS).T
    v_one = jnp.zeros((MLA_HEADS, MLA_VROWS), F32).at[:, MLA_V].set(1.0).reshape(MLA_HEADS * MLA_VROWS, 1)
    return win.astype(BF16), wuq.astype(BF16), wuqs.astype(BF16), wuk.astype(BF16), wuv_t.astype(BF16), v_one


def _rope_tables(seq):
    inv = 1.0 / (ROPE_THETA ** (jnp.arange(0, MLA_ROPE, 2, dtype=F32) / MLA_ROPE))
    ang = jnp.arange(seq, dtype=F32)[:, None] * inv[None, :]
    cos, sin = jnp.cos(ang), jnp.sin(ang)
    ones = jnp.ones((seq, MLA_NOPE), F32)
    zl = jnp.zeros((seq, MLA_NOPE), F32)
    zr = jnp.zeros((seq, MLA_PAD - MLA_NOPE - MLA_ROPE), F32)
    return (jnp.concatenate([ones, cos, cos, zr], axis=1), jnp.concatenate([zl, -sin, sin, zr], axis=1))


def kernel(x, norm_g, ev_w_in, ssm_lambda_re, ssm_lambda_im, ssm_log_dt, ssm_b_re, ssm_b_im, ssm_c_re, ssm_c_im, ssm_d, ssm_w_glu, sgu_ln_g, sgu_ln_b, sgu_w, sgu_b, ev_w_out, ffn_w_gate, ffn_w_up, ffn_w_down, od_w_in, conv_w, conv_b, conv_ln_g, conv_ln_b, mla_q_norm_g, mla_w_uq, mla_kv_norm_g, mla_w_ukv, od_w_out, moe_w_router, moe_w_gate, moe_w_up, moe_w_down):
    bsz, seq, d = x.shape
    n = bsz * seq
    assert d == D_MODEL and SUBLANES % bsz == 0 and seq % 512 == 0
    row = lambda v: v.astype(F32).reshape(1, -1)
    h = x.astype(F32).reshape(n, d)
    tm = 512

    g = norm_g[0]
    a_in, proj = _norm_proj(h, row(g[0]), ev_w_in[0].astype(BF16), tm)
    mats = _s5_matrices(ssm_lambda_re[0], ssm_lambda_im[0], ssm_log_dt[0], ssm_b_re[0], ssm_b_im[0],
                        ssm_c_re[0], ssm_c_im[0])
    ys = _s5_mixer(a_in, mats, ssm_d[0], bsz, seq)
    causal = jnp.tril(jnp.ones((SGU_CHUNK, SGU_CHUNK), dtype=bool))
    ws = jnp.where(causal[None], sgu_w[0], 0.0).astype(BF16)
    bias = jnp.repeat(sgu_b[0].astype(F32).T, SGU_HEAD_DIM, axis=1)
    wo = ev_w_out[0].astype(BF16)
    h, z = _even_mix(ys, proj, h, ssm_w_glu[0].astype(BF16), row(sgu_ln_g[0]), row(sgu_ln_b[0]), ws, bias,
                     wo[:SSM_WIDTH], wo[SSM_WIDTH:], row(g[1]), row(g[2]), tm)
    h, z = _dense_ffn(z, ffn_w_gate[0].astype(BF16), ffn_w_up[0].astype(BF16), ffn_w_down[0].astype(BF16),
                      h, row(g[3]), row(norm_g[1][0]), 1024, 512)

    g = norm_g[1]
    win, wuq, wuqs, wuk, wuv_t, v_one = _odd_weights(od_w_in[0], mla_w_uq[0], mla_w_ukv[0])
    cos_t, sin_t = _rope_tables(seq)
    zc, q, k, vt = _odd_proj(z, win, row(mla_q_norm_g[0]), row(mla_kv_norm_g[0]), wuq, wuqs, wuk, wuv_t, v_one,
                             cos_t, sin_t, seq, tm)
    hp = MLA_HEADS * MLA_PAD
    yd = _attention(q.reshape(bsz, seq, hp), k.reshape(bsz, seq, hp), vt, tm)
    conv_w_pad = jnp.concatenate([conv_w[0].astype(F32), jnp.zeros((CONV_HALO - CONV_TAPS, CONV_CH), F32)], axis=0)
    yc = _conv_mixer(zc.reshape(bsz, seq, 2 * CONV_CH), conv_w_pad, row(conv_b[0]), row(conv_ln_g[0]),
                     row(conv_ln_b[0]), tm)
    wo = od_w_out[0].astype(BF16)
    wr = jnp.concatenate([moe_w_router[0].astype(F32), jnp.zeros((d, LANES - N_EXPERTS), F32)], axis=1)
    h, z, route = _odd_mix(yc.reshape(n, CONV_CH), yd.reshape(n, MLA_HEADS * MLA_V), h, wo[:CONV_CH], wo[CONV_CH:],
                           row(g[1]), row(g[2]), wr.astype(BF16), tm)
    tm_moe = 1024
    tok_of_slot, slot_a, slot_b, tile_expert, n_used = _moe_plan(route, tm_moe)
    xs = _dispatch(tok_of_slot, z, 256)
    y = _moe_ffn(tile_expert, n_used, xs, moe_w_gate[0], moe_w_up[0], moe_w_down[0], tm_moe, 512)
    h = _combine(slot_a, slot_b, y, route, h, row(g[3]), 256)
    return h.reshape(bsz, seq, d).astype(x.dtype)
```

```python
import functools
import math

import jax
import jax.numpy as jnp
from jax import lax
from jax.experimental import pallas as pl
from jax.experimental.pallas import tpu as pltpu

F32 = jnp.float32
BF16 = jnp.bfloat16

D_MODEL = 1024
NORM_EPS = 1e-6
SSM_WIDTH = 512
SSM_GROUP = 16
SSM_GROUPS = 32
SSM_STATE = 64
SSM_CHUNK = 16
SSM_PAIR = 2 * SSM_GROUP * SSM_CHUNK
SGU_WIDTH = 512
SGU_HEADS = 8
SGU_HEAD_DIM = 64
SGU_CHUNK = 128
CONV_CH = 512
CONV_TAPS = 31
CONV_HALO = 32
MLA_HEADS = 8
MLA_Q_RANK = 256
MLA_KV_RANK = 128
MLA_NOPE = 64
MLA_ROPE = 32
MLA_V = 64
MLA_PAD = 128
MLA_VROWS = 80
ROPE_THETA = 10000.0
FF_DENSE = 4096
N_EXPERTS = 8
FF_EXPERT = 3584
LANES = 128
SUBLANES = 8
ROW_TILE = D_MODEL // LANES
VMEM_LIMIT = 56 * 1024 * 1024


def _params(sem, vmem=VMEM_LIMIT):
    return pltpu.CompilerParams(dimension_semantics=sem, vmem_limit_bytes=vmem)


def _rms(x, g):
    return x * lax.rsqrt(jnp.mean(x * x, axis=-1, keepdims=True) + NORM_EPS) * g


def _layer_norm(x, g, b):
    mu = jnp.mean(x, axis=-1, keepdims=True)
    xc = x - mu
    return xc * lax.rsqrt(jnp.mean(xc * xc, axis=-1, keepdims=True) + NORM_EPS) * g + b


def _dot(a, b):
    return jnp.dot(a, b, preferred_element_type=F32)


def _norm_proj_kernel(h_ref, g_ref, w_ref, a_ref, b_ref):
    z = _rms(h_ref[...], g_ref[...])
    proj = _dot(z.astype(BF16), w_ref[...])
    a_ref[...] = proj[:, :SSM_WIDTH]
    b_ref[...] = proj[:, SSM_WIDTH:].astype(b_ref.dtype)


def _norm_proj(h, g, w, tm):
    n, d = h.shape
    cols = w.shape[1]
    return pl.pallas_call(
        _norm_proj_kernel,
        grid=(n // tm,),
        in_specs=[pl.BlockSpec((tm, d), lambda i: (i, 0)),
                  pl.BlockSpec((1, d), lambda i: (0, 0)),
                  pl.BlockSpec((d, cols), lambda i: (0, 0))],
        out_specs=[pl.BlockSpec((tm, SSM_WIDTH), lambda i: (i, 0)),
                   pl.BlockSpec((tm, cols - SSM_WIDTH), lambda i: (i, 0))],
        out_shape=[jax.ShapeDtypeStruct((n, SSM_WIDTH), F32), jax.ShapeDtypeStruct((n, cols - SSM_WIDTH), BF16)],
        compiler_params=_params(("parallel",)),
        name="even_in_proj",
    )(h, g, w)


def _s5_matrices(lam_re, lam_im, log_dt, b_re, b_im, c_re, c_im):
    t = SSM_CHUNK
    lr = jnp.minimum(lam_re.astype(F32), -1e-4)
    li = lam_im.astype(F32)
    dt = jnp.exp(log_dt.astype(F32))[:, None]
    mag = jnp.exp(lr * dt)
    a_re = mag * jnp.cos(li * dt)
    a_im = mag * jnp.sin(li * dt)
    den = lr * lr + li * li
    nr = a_re - 1.0
    coef_re = (nr * lr + a_im * li) / den
    coef_im = (a_im * lr - nr * li) / den
    br = b_re.astype(F32)
    bi = b_im.astype(F32)
    bb_re = coef_re[..., None] * br - coef_im[..., None] * bi
    bb_im = coef_re[..., None] * bi + coef_im[..., None] * br
    cr = c_re.astype(F32)
    ci = c_im.astype(F32)
    pw_re = [jnp.ones_like(a_re)]
    pw_im = [jnp.zeros_like(a_im)]
    for _ in range(t):
        pr, pi = pw_re[-1], pw_im[-1]
        pw_re.append(pr * a_re - pi * a_im)
        pw_im.append(pr * a_im + pi * a_re)
    pw_re = jnp.stack(pw_re)
    pw_im = jnp.stack(pw_im)
    ab_re = pw_re[:t, :, :, None] * bb_re[None] - pw_im[:t, :, :, None] * bb_im[None]
    ab_im = pw_re[:t, :, :, None] * bb_im[None] + pw_im[:t, :, :, None] * bb_re[None]
    hi = lax.Precision.HIGHEST
    k_lag = (jnp.einsum('gnp,tgpm->tgnm', cr, ab_re, precision=hi)
             - jnp.einsum('gnp,tgpm->tgnm', ci, ab_im, precision=hi))
    n_pairs = SSM_GROUPS // 2
    st = 2 * SSM_STATE

    def pair_diag(w):
        w = w.reshape((n_pairs, 2) + w.shape[1:])
        z = jnp.zeros_like(w[:, 0])
        top = jnp.concatenate([w[:, 0], z], axis=-1)
        bot = jnp.concatenate([z, w[:, 1]], axis=-1)
        return jnp.concatenate([top, bot], axis=-2)

    k_blk = pair_diag(k_lag.transpose(1, 0, 3, 2)).astype(BF16)
    rev_re = pw_re[:t][::-1]
    rev_im = pw_im[:t][::-1]
    ws_re = rev_re[..., None] * bb_re[None] - rev_im[..., None] * bb_im[None]
    ws_im = rev_re[..., None] * bb_im[None] + rev_im[..., None] * bb_re[None]
    ws_re = pair_diag(ws_re.transpose(1, 0, 3, 2)).reshape(n_pairs, SSM_PAIR, st).astype(BF16)
    ws_im = pair_diag(ws_im.transpose(1, 0, 3, 2)).reshape(n_pairs, SSM_PAIR, st).astype(BF16)
    ca_re = cr[None] * pw_re[1:, :, None, :] - ci[None] * pw_im[1:, :, None, :]
    ca_im = cr[None] * pw_im[1:, :, None, :] + ci[None] * pw_re[1:, :, None, :]
    co_re = pair_diag(ca_re.transpose(1, 0, 3, 2)).astype(BF16)
    co_im = pair_diag((-ca_im).transpose(1, 0, 3, 2)).astype(BF16)
    w_intra, wo_re, wo_im = _s5_expand(k_blk, co_re, co_im)
    return dict(
        w_intra=w_intra, ws_re=ws_re, ws_im=ws_im, wo_re=wo_re, wo_im=wo_im,
        at_re=pw_re[t].reshape(1, SSM_GROUPS * SSM_STATE), at_im=pw_im[t].reshape(1, SSM_GROUPS * SSM_STATE))


def _s5_expand_kernel(k_ref, cre_ref, cim_ref, er_ref, ec_ref, wi_ref, wore_ref, woim_ref):
    pw = 2 * SSM_GROUP
    e_r = er_ref[...]
    e_c = ec_ref[...]
    rb = lax.broadcasted_iota(jnp.int32, (SSM_PAIR, SSM_PAIR), 0) // pw
    cb = lax.broadcasted_iota(jnp.int32, (SSM_PAIR, SSM_PAIR), 1) // pw
    lag = cb - rb
    w = jnp.zeros((SSM_PAIR, SSM_PAIR), F32)
    for tau in range(SSM_CHUNK):
        tiled = _dot(_dot(e_r, k_ref[0, tau]).astype(BF16), e_c)
        w = jnp.where(lag == tau, tiled, w)
    wi_ref[0] = w.astype(wi_ref.dtype)
    cb_o = lax.broadcasted_iota(jnp.int32, (2 * SSM_STATE, SSM_PAIR), 1) // pw
    o_re = jnp.zeros((2 * SSM_STATE, SSM_PAIR), F32)
    o_im = jnp.zeros((2 * SSM_STATE, SSM_PAIR), F32)
    for step in range(SSM_CHUNK):
        o_re = jnp.where(cb_o == step, _dot(cre_ref[0, step], e_c), o_re)
        o_im = jnp.where(cb_o == step, _dot(cim_ref[0, step], e_c), o_im)
    wore_ref[0] = o_re.astype(wore_ref.dtype)
    woim_ref[0] = o_im.astype(woim_ref.dtype)


def _s5_expand(k_blk, co_re, co_im):
    n_pairs = k_blk.shape[0]
    pw = 2 * SSM_GROUP
    st = 2 * SSM_STATE
    eye = jnp.eye(pw, dtype=BF16)
    e_c = jnp.concatenate([eye] * SSM_CHUNK, axis=1)
    e_r = jnp.concatenate([eye] * SSM_CHUNK, axis=0)
    return pl.pallas_call(
        _s5_expand_kernel,
        grid=(n_pairs,),
        in_specs=[pl.BlockSpec((1, SSM_CHUNK, pw, pw), lambda q: (q, 0, 0, 0)),
                  pl.BlockSpec((1, SSM_CHUNK, st, pw), lambda q: (q, 0, 0, 0)),
                  pl.BlockSpec((1, SSM_CHUNK, st, pw), lambda q: (q, 0, 0, 0)),
                  pl.BlockSpec((SSM_PAIR, pw), lambda q: (0, 0)),
                  pl.BlockSpec((pw, SSM_PAIR), lambda q: (0, 0))],
        out_specs=[pl.BlockSpec((1, SSM_PAIR, SSM_PAIR), lambda q: (q, 0, 0)),
                   pl.BlockSpec((1, st, SSM_PAIR), lambda q: (q, 0, 0)),
                   pl.BlockSpec((1, st, SSM_PAIR), lambda q: (q, 0, 0))],
        out_shape=[jax.ShapeDtypeStruct((n_pairs, SSM_PAIR, SSM_PAIR), BF16),
                   jax.ShapeDtypeStruct((n_pairs, st, SSM_PAIR), BF16),
                   jax.ShapeDtypeStruct((n_pairs, st, SSM_PAIR), BF16)],
        compiler_params=_params(("parallel",)),
        name="s5_expand_weights",
    )(k_blk, co_re, co_im, e_r, e_c)


S5_LANE_PAIRS = LANES // (2 * SSM_GROUP)
S5_SCAN_LANES = 512


def _s5_state_kernel(u0_ref, u1_ref, u2_ref, u3_ref, wre_ref, wim_ref, are_ref, aim_ref,
                     x_ref, hre_ref, him_ref, sre_ref, sim_ref):
    n_chunks = x_ref.shape[0]
    pw = 2 * SSM_GROUP
    u_refs = (u0_ref, u1_ref, u2_ref, u3_ref)
    for t in range(SSM_CHUNK):
        for j, u_ref in enumerate(u_refs):
            ut = u_ref[pl.ds(t, n_chunks, stride=SSM_CHUNK), :]
            for qq in range(S5_LANE_PAIRS):
                q = j * S5_LANE_PAIRS + qq
                x_ref[:, q * SSM_PAIR + t * pw: q * SSM_PAIR + (t + 1) * pw] = (
                    ut[:, qq * pw:(qq + 1) * pw].astype(x_ref.dtype))
    st = 2 * SSM_STATE
    for q in range(SSM_GROUPS // 2):
        xq = x_ref[:, q * SSM_PAIR:(q + 1) * SSM_PAIR]
        sre_ref[:, q * st:(q + 1) * st] = _dot(xq, wre_ref[q])
        sim_ref[:, q * st:(q + 1) * st] = _dot(xq, wim_ref[q])

    row = lax.broadcasted_iota(jnp.int32, (SUBLANES, S5_SCAN_LANES), 0)
    zero = jnp.zeros((SUBLANES, S5_SCAN_LANES), F32)
    for c0 in range(0, sre_ref.shape[1], S5_SCAN_LANES):
        cols = pl.ds(c0, S5_SCAN_LANES)
        ar = are_ref[:, cols]
        ai = aim_ref[:, cols]

        def body(k, carry, cols=cols, ar=ar, ai=ai):
            r0 = pl.multiple_of(k * SUBLANES, SUBLANES)
            sr = sre_ref[pl.ds(r0, SUBLANES), cols]
            si = sim_ref[pl.ds(r0, SUBLANES), cols]
            out_r, out_i = carry
            for i in range(1, SUBLANES + 1):
                tr = ar * out_r - ai * out_i + sr
                ti = ar * out_i + ai * out_r + si
                tr = pltpu.roll(tr, 1, 0)
                ti = pltpu.roll(ti, 1, 0)
                if i < SUBLANES:
                    out_r = jnp.where(row == i, tr, out_r)
                    out_i = jnp.where(row == i, ti, out_i)
            hre_ref[pl.ds(r0, SUBLANES), cols] = out_r
            him_ref[pl.ds(r0, SUBLANES), cols] = out_i
            return tr, ti

        lax.fori_loop(0, n_chunks // SUBLANES, body, (zero, zero))


def _s5_out_kernel(x_ref, wi_ref, hre_ref, him_ref, wore_ref, woim_ref, d_ref, y_ref, yt_ref):
    n_chunks = x_ref.shape[0]
    pw = 2 * SSM_GROUP
    st = 2 * SSM_STATE
    for qq in range(S5_LANE_PAIRS):
        x = x_ref[:, qq * SSM_PAIR:(qq + 1) * SSM_PAIR]
        y = _dot(x, wi_ref[qq])
        y += _dot(hre_ref[:, qq * st:(qq + 1) * st].astype(BF16), wore_ref[qq])
        y += _dot(him_ref[:, qq * st:(qq + 1) * st].astype(BF16), woim_ref[qq])
        y += d_ref[:, qq * SSM_PAIR:(qq + 1) * SSM_PAIR] * x.astype(F32)
        y = jax.nn.gelu(y)
        for t in range(SSM_CHUNK):
            yt_ref[t, :, qq * pw:(qq + 1) * pw] = y[:, t * pw:(t + 1) * pw]
    for t in range(SSM_CHUNK):
        y_ref[pl.ds(t, n_chunks, stride=SSM_CHUNK), :] = yt_ref[t]


def _s5_mixer(u, mats, d, batch, seq):
    t = SSM_CHUNK
    n_chunks = seq // t
    n_pairs = SSM_GROUPS // 2
    cols = n_pairs * SSM_PAIR
    st = 2 * SSM_STATE
    n_state = n_pairs * st
    n_blk = SSM_WIDTH // LANES
    assert n_blk == 4 and n_chunks % SUBLANES == 0
    once = pl.Buffered(1)
    x, h_re, h_im = pl.pallas_call(
        _s5_state_kernel,
        grid=(batch,),
        in_specs=[pl.BlockSpec((seq, LANES), lambda b, j=j: (b, j)) for j in range(n_blk)] + [
            pl.BlockSpec((n_pairs, SSM_PAIR, st), lambda b: (0, 0, 0), pipeline_mode=once),
            pl.BlockSpec((n_pairs, SSM_PAIR, st), lambda b: (0, 0, 0), pipeline_mode=once),
            pl.BlockSpec((1, n_state), lambda b: (0, 0)),
            pl.BlockSpec((1, n_state), lambda b: (0, 0))],
        out_specs=[pl.BlockSpec((n_chunks, cols), lambda b: (b, 0)),
                   pl.BlockSpec((n_chunks, n_state), lambda b: (b, 0)),
                   pl.BlockSpec((n_chunks, n_state), lambda b: (b, 0))],
        out_shape=[jax.ShapeDtypeStruct((batch * n_chunks, cols), BF16),
                   jax.ShapeDtypeStruct((batch * n_chunks, n_state), F32),
                   jax.ShapeDtypeStruct((batch * n_chunks, n_state), F32)],
        scratch_shapes=[pltpu.VMEM((n_chunks, n_state), F32), pltpu.VMEM((n_chunks, n_state), F32)],
        compiler_params=_params(("parallel",)),
        name="s5_state_scan",
    )(u, u, u, u, mats['ws_re'], mats['ws_im'], mats['at_re'], mats['at_im'])
    lp = S5_LANE_PAIRS
    d_cols = jnp.broadcast_to(d.astype(F32).reshape(n_pairs, 1, 2 * SSM_GROUP),
                              (n_pairs, t, 2 * SSM_GROUP)).reshape(1, cols)
    return pl.pallas_call(
        _s5_out_kernel,
        grid=(batch, n_blk),
        in_specs=[pl.BlockSpec((n_chunks, lp * SSM_PAIR), lambda b, j: (b, j)),
                  pl.BlockSpec((lp, SSM_PAIR, SSM_PAIR), lambda b, j: (j, 0, 0)),
                  pl.BlockSpec((n_chunks, lp * st), lambda b, j: (b, j)),
                  pl.BlockSpec((n_chunks, lp * st), lambda b, j: (b, j)),
                  pl.BlockSpec((lp, st, SSM_PAIR), lambda b, j: (j, 0, 0)),
                  pl.BlockSpec((lp, st, SSM_PAIR), lambda b, j: (j, 0, 0)),
                  pl.BlockSpec((1, lp * SSM_PAIR), lambda b, j: (0, j))],
        out_specs=pl.BlockSpec((seq, LANES), lambda b, j: (b, j)),
        out_shape=jax.ShapeDtypeStruct((batch * seq, SSM_WIDTH), F32),
        scratch_shapes=[pltpu.VMEM((t, n_chunks, LANES), F32)],
        compiler_params=_params(("parallel", "parallel")),
        name="s5_out",
    )(x, mats['w_intra'], h_re, h_im, mats['wo_re'], mats['wo_im'], d_cols)


def _even_mix_kernel(ys_ref, bu_ref, bv_ref, h_ref, wglu_ref, lng_ref, lnb_ref, ws_ref, bias_ref,
                     wo_a_ref, wo_b_ref, g1_ref, g2_ref, hout_ref, z_ref, s_scr):
    tm = ys_ref.shape[0]
    ys = ys_ref[...]
    ya = ys * jax.nn.sigmoid(_dot(ys.astype(BF16), wglu_ref[...]))
    u = jax.nn.gelu(bu_ref[...].astype(F32))
    v = _layer_norm(jax.nn.gelu(bv_ref[...].astype(F32)), lng_ref[...], lnb_ref[...])
    lane = lax.broadcasted_iota(jnp.int32, v.shape, 1)
    left = (lane % LANES) < SGU_HEAD_DIM
    v_l = jnp.where(left, v, 0.0).astype(BF16)
    v_r = jnp.where(left, 0.0, v).astype(BF16)
    for c in range(tm // SGU_CHUNK):
        rows = slice(c * SGU_CHUNK, (c + 1) * SGU_CHUNK)
        for p in range(SGU_HEADS // 2):
            cols = slice(p * LANES, (p + 1) * LANES)
            s_scr[rows, cols] = (_dot(ws_ref[2 * p], v_l[rows, cols]) + _dot(ws_ref[2 * p + 1], v_r[rows, cols]))
    bias = jnp.concatenate([bias_ref[...]] * (tm // SGU_CHUNK), axis=0)
    yb = u * (s_scr[...] + bias)
    mix = _dot(ya.astype(BF16), wo_a_ref[...]) + _dot(yb.astype(BF16), wo_b_ref[...])
    h_new = h_ref[...] + _rms(mix, g1_ref[...])
    hout_ref[...] = h_new
    z_ref[...] = _rms(h_new, g2_ref[...]).astype(z_ref.dtype)


def _even_mix(ys, proj, h, wglu, lng, lnb, ws, bias, wo_a, wo_b, g1, g2, tm):
    n, d = h.shape
    w = SGU_WIDTH
    const = lambda *shape: pl.BlockSpec(shape, lambda i: (0,) * len(shape))
    return pl.pallas_call(
        _even_mix_kernel,
        grid=(n // tm,),
        in_specs=[pl.BlockSpec((tm, w), lambda i: (i, 0)),
                  pl.BlockSpec((tm, w), lambda i: (i, 0)),
                  pl.BlockSpec((tm, w), lambda i: (i, 1)),
                  pl.BlockSpec((tm, d), lambda i: (i, 0)),
                  const(w, w), const(1, w), const(1, w),
                  const(SGU_HEADS, SGU_CHUNK, SGU_CHUNK), const(SGU_CHUNK, w),
                  const(w, d), const(w, d), const(1, d), const(1, d)],
        out_specs=[pl.BlockSpec((tm, d), lambda i: (i, 0)),
                   pl.BlockSpec((tm, d), lambda i: (i, 0))],
        out_shape=[jax.ShapeDtypeStruct((n, d), F32), jax.ShapeDtypeStruct((n, d), BF16)],
        scratch_shapes=[pltpu.VMEM((tm, w), F32)],
        compiler_params=_params(("parallel",)),
        name="even_mix",
    )(ys, proj, proj, h, wglu, lng, lnb, ws, bias, wo_a, wo_b, g1, g2)


def _ffn_kernel(z_ref, wg_ref, wu_ref, wd_ref, h_ref, g3_ref, gn_ref, hout_ref, zout_ref, acc_ref):
    j = pl.program_id(1)

    @pl.when(j == 0)
    def _():
        acc_ref[...] = jnp.zeros_like(acc_ref)

    z = z_ref[...]
    a = jax.nn.silu(_dot(z, wg_ref[...])) * _dot(z, wu_ref[...])
    acc_ref[...] += _dot(a.astype(BF16), wd_ref[...])

    @pl.when(j == pl.num_programs(1) - 1)
    def _():
        h_new = h_ref[...] + _rms(acc_ref[...], g3_ref[...])
        hout_ref[...] = h_new
        zout_ref[...] = _rms(h_new, gn_ref[...]).astype(zout_ref.dtype)


def _dense_ffn(z, wg, wu, wd, h, g3, g_next, tm, tf):
    n, d = h.shape
    ff = wg.shape[1]
    return pl.pallas_call(
        _ffn_kernel,
        grid=(n // tm, ff // tf),
        in_specs=[pl.BlockSpec((tm, d), lambda i, j: (i, 0)),
                  pl.BlockSpec((d, tf), lambda i, j: (0, j)),
                  pl.BlockSpec((d, tf), lambda i, j: (0, j)),
                  pl.BlockSpec((tf, d), lambda i, j: (j, 0)),
                  pl.BlockSpec((tm, d), lambda i, j: (i, 0)),
                  pl.BlockSpec((1, d), lambda i, j: (0, 0)),
                  pl.BlockSpec((1, d), lambda i, j: (0, 0))],
        out_specs=[pl.BlockSpec((tm, d), lambda i, j: (i, 0)),
                   pl.BlockSpec((tm, d), lambda i, j: (i, 0))],
        out_shape=[jax.ShapeDtypeStruct((n, d), F32), jax.ShapeDtypeStruct((n, d), BF16)],
        scratch_shapes=[pltpu.VMEM((tm, d), F32)],
        compiler_params=_params(("parallel", "arbitrary")),
        name="dense_ffn",
    )(z, wg, wu, wd, h, g3, g_next)


def _odd_proj_kernel(z_ref, win_ref, gq_ref, gkv_ref, wuq_ref, wuqs_ref, wuk_ref, wuv_ref, vone_ref, cos_ref, sin_ref,
                     zc_ref, q_ref, k_ref, v_ref, *, scale):
    z = z_ref[...]
    proj = _dot(z, win_ref[...])
    c0 = 2 * CONV_CH
    c1 = c0 + MLA_Q_RANK
    c2 = c1 + MLA_KV_RANK
    c3 = c2 + MLA_PAD
    zc_ref[...] = proj[:, :c0].astype(zc_ref.dtype)
    cq = _rms(proj[:, c0:c1], gq_ref[...]).astype(BF16)
    ckv = _rms(proj[:, c1:c2], gkv_ref[...]).astype(BF16)
    cos = cos_ref[...]
    sin = sin_ref[...]
    cos_h = jnp.concatenate([cos] * MLA_HEADS, axis=1)
    sin_h = jnp.concatenate([sin] * MLA_HEADS, axis=1)
    q = _dot(cq, wuq_ref[...]) * cos_h + _dot(cq, wuqs_ref[...]) * sin_h
    q_ref[...] = (q * scale).astype(q_ref.dtype)
    kr = proj[:, c2:c3] * cos + proj[:, c3:] * sin
    k = _dot(ckv, wuk_ref[...]) + jnp.concatenate([kr] * MLA_HEADS, axis=1)
    k_ref[...] = k.astype(k_ref.dtype)
    vt = lax.dot_general(wuv_ref[...], ckv, (((1,), (1,)), ((), ())), preferred_element_type=F32)
    v_ref[0] = (vt + vone_ref[...]).astype(v_ref.dtype)


def _odd_proj(z, win, gq, gkv, wuq, wuqs, wuk, wuv_t, v_one, cos_t, sin_t, seq, tm):
    n, d = z.shape
    hp = MLA_HEADS * MLA_PAD
    vr = MLA_HEADS * MLA_VROWS
    n_l = seq // tm
    const = lambda *shape: pl.BlockSpec(shape, lambda i: (0,) * len(shape))
    out = jax.ShapeDtypeStruct((n, hp), BF16)
    scale = float((MLA_NOPE + MLA_ROPE) ** -0.5 * math.log2(math.e))
    return pl.pallas_call(
        functools.partial(_odd_proj_kernel, scale=scale),
        grid=(n // tm,),
        in_specs=[pl.BlockSpec((tm, d), lambda i: (i, 0)),
                  const(d, win.shape[1]), const(1, MLA_Q_RANK), const(1, MLA_KV_RANK),
                  const(MLA_Q_RANK, hp), const(MLA_Q_RANK, hp), const(MLA_KV_RANK, hp), const(vr, MLA_KV_RANK),
                  const(vr, 1),
                  pl.BlockSpec((tm, MLA_PAD), lambda i: (i % n_l, 0)),
                  pl.BlockSpec((tm, MLA_PAD), lambda i: (i % n_l, 0))],
        out_specs=[pl.BlockSpec((tm, 2 * CONV_CH), lambda i: (i, 0)),
                   pl.BlockSpec((tm, hp), lambda i: (i, 0)),
                   pl.BlockSpec((tm, hp), lambda i: (i, 0)),
                   pl.BlockSpec((1, vr, tm), lambda i: (i, 0, 0))],
        out_shape=[jax.ShapeDtypeStruct((n, 2 * CONV_CH), BF16), out, out,
                   jax.ShapeDtypeStruct((n // tm, vr, tm), BF16)],
        compiler_params=_params(("parallel",)),
        name="odd_in_proj",
    )(z, win, gq, gkv, wuq, wuqs, wuk, wuv_t, v_one, cos_t, sin_t)


def _attn_kernel(q_ref, k_ref, vt_ref, o_ref, acc_ref, *, blk):
    i = pl.program_id(2)
    acc_ref[...] = jnp.zeros_like(acc_ref)

    def step(j, m, masked):
        r0 = pl.multiple_of(j * blk, blk)
        m_out = []
        for hh in range(2):
            q = q_ref[0, :, hh * MLA_PAD:(hh + 1) * MLA_PAD]
            k = k_ref[0, pl.ds(r0, blk), hh * MLA_PAD:(hh + 1) * MLA_PAD]
            st = lax.dot_general(k, q, (((1,), (1,)), ((), ())), preferred_element_type=F32)
            if masked:
                key = lax.broadcasted_iota(jnp.int32, st.shape, 0)
                qry = lax.broadcasted_iota(jnp.int32, st.shape, 1)
                st = jnp.where(key <= qry, st, -1e30)
            m_new = jnp.maximum(m[hh], jnp.max(st, axis=0, keepdims=True))
            alpha = jnp.exp2(m[hh] - m_new)
            p = jnp.exp2(st - m_new).astype(BF16)
            vt = vt_ref[j, hh * MLA_VROWS:(hh + 1) * MLA_VROWS, :]
            acc_ref[hh] = alpha * acc_ref[hh] + _dot(vt, p)
            m_out.append(m_new)
        return tuple(m_out)

    init = jnp.full((1, blk), -1e30, F32)
    m = lax.fori_loop(0, i, lambda j, m: step(j, m, False), (init, init))
    step(i, m, True)
    a0 = acc_ref[0]
    a1 = acc_ref[1]
    ot = jnp.concatenate([a0[:MLA_V] / a0[MLA_V:MLA_V + 1], a1[:MLA_V] / a1[MLA_V:MLA_V + 1]], axis=0)
    o_ref[0] = ot.T.astype(o_ref.dtype)


def _attention(q, k, vt, blk):
    b, seq, _ = q.shape
    n_blk = seq // blk
    return pl.pallas_call(
        functools.partial(_attn_kernel, blk=blk),
        grid=(b, MLA_HEADS // 2, n_blk),
        in_specs=[pl.BlockSpec((1, blk, 2 * MLA_PAD), lambda bi, p, i: (bi, i, p)),
                  pl.BlockSpec((1, seq, 2 * MLA_PAD), lambda bi, p, i: (bi, 0, p)),
                  pl.BlockSpec((n_blk, 2 * MLA_VROWS, blk), lambda bi, p, i: (bi, p, 0))],
        out_specs=pl.BlockSpec((1, blk, 2 * MLA_V), lambda bi, p, i: (bi, i, p)),
        out_shape=jax.ShapeDtypeStruct((b, seq, MLA_HEADS * MLA_V), BF16),
        scratch_shapes=[pltpu.VMEM((2, MLA_VROWS, blk), F32)],
        compiler_params=_params(("parallel", "parallel", "parallel")),
        name="mla_attention",
    )(q, k, vt)


def _conv_kernel(zc_ref, w_ref, b_ref, lng_ref, lnb_ref, y_ref, buf_ref):
    tm = zc_ref.shape[1]

    @pl.when(pl.program_id(1) == 0)
    def _():
        buf_ref[pl.ds(0, CONV_HALO), :] = jnp.zeros((CONV_HALO, CONV_CH), F32)

    zc = zc_ref[0].astype(F32)
    hh = zc[:, :CONV_CH] * jax.nn.sigmoid(zc[:, CONV_CH:])
    buf_ref[pl.ds(CONV_HALO, tm), :] = hh
    off = CONV_HALO - (CONV_TAPS - 1)
    acc = jnp.zeros((tm, CONV_CH), F32) + b_ref[...]
    for kk in range(CONV_TAPS):
        acc = acc + w_ref[pl.ds(kk, 1), :] * buf_ref[pl.ds(off + kk, tm), :]
    buf_ref[pl.ds(0, CONV_HALO), :] = buf_ref[pl.ds(tm, CONV_HALO), :]
    y_ref[0] = jax.nn.silu(_layer_norm(acc, lng_ref[...], lnb_ref[...])).astype(y_ref.dtype)


def _conv_mixer(zc, w, b, lng, lnb, tm):
    bsz, seq, _ = zc.shape
    const = lambda *shape: pl.BlockSpec(shape, lambda bi, i: (0,) * len(shape))
    return pl.pallas_call(
        _conv_kernel,
        grid=(bsz, seq // tm),
        in_specs=[pl.BlockSpec((1, tm, 2 * CONV_CH), lambda bi, i: (bi, i, 0)),
                  const(CONV_HALO, CONV_CH), const(1, CONV_CH), const(1, CONV_CH), const(1, CONV_CH)],
        out_specs=pl.BlockSpec((1, tm, CONV_CH), lambda bi, i: (bi, i, 0)),
        out_shape=jax.ShapeDtypeStruct((bsz, seq, CONV_CH), BF16),
        scratch_shapes=[pltpu.VMEM((tm + CONV_HALO, CONV_CH), F32)],
        compiler_params=_params(("arbitrary", "arbitrary")),
        name="conv_module",
    )(zc, w, b, lng, lnb)


def _odd_mix_kernel(yc_ref, yd_ref, h_ref, wo_a_ref, wo_b_ref, g1_ref, g2_ref, wr_ref, hout_ref, z_ref, route_ref):
    mix = _dot(yc_ref[...], wo_a_ref[...]) + _dot(yd_ref[...], wo_b_ref[...])
    h_new = h_ref[...] + _rms(mix, g1_ref[...])
    hout_ref[...] = h_new
    z = _rms(h_new, g2_ref[...])
    _store_row_tiles(z_ref, z)
    logits = _dot(z.astype(BF16), wr_ref[...])
    lane = lax.broadcasted_iota(jnp.int32, logits.shape, 1)
    neg = -jnp.inf
    logits = jnp.where(lane < N_EXPERTS, logits, neg)
    m1 = jnp.max(logits, axis=-1, keepdims=True)
    i1 = jnp.min(jnp.where(logits == m1, lane, LANES), axis=-1, keepdims=True)
    rest = jnp.where(lane == i1, neg, logits)
    m2 = jnp.max(rest, axis=-1, keepdims=True)
    i2 = jnp.min(jnp.where(rest == m2, lane, LANES), axis=-1, keepdims=True)
    e = jnp.exp(m2 - m1)
    w1 = 1.0 / (1.0 + e)
    w2 = e / (1.0 + e)
    route = jnp.where(lane == 0, i1.astype(F32),
                      jnp.where(lane == 1, i2.astype(F32),
                                jnp.where(lane == 2, w1, jnp.where(lane == 3, w2, 0.0))))
    route_ref[...] = route


def _odd_mix(yc, yd, h, wo_a, wo_b, g1, g2, wr, tm):
    n, d = h.shape
    const = lambda *shape: pl.BlockSpec(shape, lambda i: (0,) * len(shape))
    return pl.pallas_call(
        _odd_mix_kernel,
        grid=(n // tm,),
        in_specs=[pl.BlockSpec((tm, yc.shape[1]), lambda i: (i, 0)),
                  pl.BlockSpec((tm, yd.shape[1]), lambda i: (i, 0)),
                  pl.BlockSpec((tm, d), lambda i: (i, 0)),
                  const(*wo_a.shape), const(*wo_b.shape), const(1, d), const(1, d), const(d, LANES)],
        out_specs=[pl.BlockSpec((tm, d), lambda i: (i, 0)),
                   pl.BlockSpec((tm * ROW_TILE, LANES), lambda i: (i, 0)),
                   pl.BlockSpec((tm, LANES), lambda i: (i, 0))],
        out_shape=[jax.ShapeDtypeStruct((n, d), F32), jax.ShapeDtypeStruct((n * ROW_TILE, LANES), F32),
                   jax.ShapeDtypeStruct((n, LANES), F32)],
        compiler_params=_params(("parallel",)),
        name="odd_mix_router",
    )(yc, yd, h, wo_a, wo_b, g1, g2, wr)


def _store_row_tiles(ref, x):
    rows = x.shape[0]
    for s in range(ROW_TILE):
        ref[pl.ds(s, rows, stride=ROW_TILE), :] = x[:, s * LANES:(s + 1) * LANES]


def _load_row_tiles(ref, rows):
    return [ref[pl.ds(s, rows, stride=ROW_TILE), :] for s in range(ROW_TILE)]


def _gather_rows(idx_ref, base, n_rows, src_hbm, dst_ref, sem):
    def body(r, c):
        src = pl.multiple_of(idx_ref[base + r] * ROW_TILE, ROW_TILE)
        dst = pl.multiple_of(r * ROW_TILE, ROW_TILE)
        pltpu.make_async_copy(src_hbm.at[pl.ds(src, ROW_TILE), :], dst_ref.at[pl.ds(dst, ROW_TILE), :], sem).start()
        return c

    lax.fori_loop(0, n_rows, body, 0, unroll=8)


def _wait_rows(src_hbm, dst_ref, sem):
    pltpu.make_async_copy(src_hbm.at[pl.ds(0, dst_ref.shape[0]), :], dst_ref, sem).wait()


def _moe_ffn_kernel(te_ref, nu_ref, tok_ref, z_hbm, wg_ref, wu_ref, wd_ref, y_ref, xraw_ref, xb_ref, acc_ref, sem):
    i = pl.program_id(0)
    j = pl.program_id(1)
    tm = xb_ref.shape[0]
    n_used = nu_ref[0]
    slot = i % 2
    first = j == 0

    @pl.when(first & (i == 0))
    def _():
        _gather_rows(tok_ref, 0, tm, z_hbm, xraw_ref.at[0], sem.at[0])

    @pl.when(first & (i < n_used))
    def _():
        _wait_rows(z_hbm, xraw_ref.at[slot], sem.at[slot])
        for s, blk in enumerate(_load_row_tiles(xraw_ref.at[slot], tm)):
            xb_ref[:, s * LANES:(s + 1) * LANES] = blk.astype(BF16)

    @pl.when(first & (i + 1 < n_used))
    def _():
        _gather_rows(tok_ref, (i + 1) * tm, tm, z_hbm, xraw_ref.at[1 - slot], sem.at[1 - slot])

    @pl.when(first)
    def _():
        acc_ref[...] = jnp.zeros_like(acc_ref)

    @pl.when(i < n_used)
    def _():
        x = xb_ref[...]
        a = jax.nn.silu(_dot(x, wg_ref[0].astype(BF16))) * _dot(x, wu_ref[0].astype(BF16))
        acc_ref[...] += _dot(a.astype(BF16), wd_ref[0].astype(BF16))

    @pl.when(j == pl.num_programs(1) - 1)
    def _():
        _store_row_tiles(y_ref, acc_ref[...])


def _moe_ffn(tile_expert, n_used, tok_of_slot, z_tiles, wg, wu, wd, tm, tf):
    n_slots = tok_of_slot.shape[0]
    d, ff = wg.shape[1], wg.shape[2]
    n_f = ff // tf

    def col(i, j, nu):
        return jnp.where(i < nu[0], j, n_f - 1)

    return pl.pallas_call(
        _moe_ffn_kernel,
        grid_spec=pltpu.PrefetchScalarGridSpec(
            num_scalar_prefetch=3,
            grid=(n_slots // tm, n_f),
            in_specs=[pl.BlockSpec(memory_space=pl.ANY),
                      pl.BlockSpec((1, d, tf), lambda i, j, te, nu, tok: (te[i], 0, col(i, j, nu))),
                      pl.BlockSpec((1, d, tf), lambda i, j, te, nu, tok: (te[i], 0, col(i, j, nu))),
                      pl.BlockSpec((1, tf, d), lambda i, j, te, nu, tok: (te[i], col(i, j, nu), 0))],
            out_specs=pl.BlockSpec((tm * ROW_TILE, LANES), lambda i, j, te, nu, tok: (i, 0)),
            scratch_shapes=[pltpu.VMEM((2, tm * ROW_TILE, LANES), F32), pltpu.VMEM((tm, d), BF16),
                            pltpu.VMEM((tm, d), F32), pltpu.SemaphoreType.DMA((2,))]),
        out_shape=jax.ShapeDtypeStruct((n_slots * ROW_TILE, LANES), F32),
        compiler_params=_params(("arbitrary", "arbitrary")),
        name="moe_grouped_ffn",
    )(tile_expert, n_used, tok_of_slot, z_tiles, wg, wu, wd)


def _combine_kernel(sa_ref, sb_ref, y_hbm, route_ref, h_ref, g_ref, o_ref, a_buf, b_buf, sem):
    i = pl.program_id(0)
    n_steps = pl.num_programs(0)
    tm = h_ref.shape[0]
    slot = i % 2

    def start(step, sl):
        _gather_rows(sa_ref, step * tm, tm, y_hbm, a_buf.at[sl], sem.at[0, sl])
        _gather_rows(sb_ref, step * tm, tm, y_hbm, b_buf.at[sl], sem.at[1, sl])

    @pl.when(i == 0)
    def _():
        start(0, 0)

    @pl.when(i + 1 < n_steps)
    def _():
        start(i + 1, 1 - slot)

    _wait_rows(y_hbm, a_buf.at[slot], sem.at[0, slot])
    _wait_rows(y_hbm, b_buf.at[slot], sem.at[1, slot])
    route = route_ref[...]
    a = jnp.concatenate(_load_row_tiles(a_buf.at[slot], tm), axis=1)
    b = jnp.concatenate(_load_row_tiles(b_buf.at[slot], tm), axis=1)
    f = route[:, 2:3] * a + route[:, 3:4] * b
    o_ref[...] = h_ref[...] + _rms(f, g_ref[...])


def _combine(slot_a, slot_b, y, route, h, g, tm):
    n, d = h.shape
    return pl.pallas_call(
        _combine_kernel,
        grid_spec=pltpu.PrefetchScalarGridSpec(
            num_scalar_prefetch=2,
            grid=(n // tm,),
            in_specs=[pl.BlockSpec(memory_space=pl.ANY),
                      pl.BlockSpec((tm, LANES), lambda i, sa, sb: (i, 0)),
                      pl.BlockSpec((tm, d), lambda i, sa, sb: (i, 0)),
                      pl.BlockSpec((1, d), lambda i, sa, sb: (0, 0))],
            out_specs=pl.BlockSpec((tm, d), lambda i, sa, sb: (i, 0)),
            scratch_shapes=[pltpu.VMEM((2, tm * ROW_TILE, LANES), F32), pltpu.VMEM((2, tm * ROW_TILE, LANES), F32),
                            pltpu.SemaphoreType.DMA((2, 2))]),
        out_shape=jax.ShapeDtypeStruct((n, d), F32),
        compiler_params=_params(("arbitrary",)),
        name="moe_combine",
    )(slot_a, slot_b, y, route, h, g)


def _moe_plan(route, tm):
    n = route.shape[0]
    eids = jnp.concatenate([route[:, 0], route[:, 1]]).astype(jnp.int32)
    onehot = (eids[:, None] == jnp.arange(N_EXPERTS, dtype=jnp.int32)[None, :]).astype(jnp.int32)
    csum = jnp.cumsum(onehot, axis=0)
    rank = jnp.sum(csum * onehot, axis=1) - 1
    counts = csum[-1]
    padded = ((counts + tm - 1) // tm) * tm
    ends = jnp.cumsum(padded)
    starts = ends - padded
    slot = jnp.sum(onehot * starts[None, :], axis=1) + rank
    n_tiles = 2 * n // tm + N_EXPERTS
    tok = jnp.concatenate([jnp.arange(n, dtype=jnp.int32)] * 2)
    tok_of_slot = jnp.zeros((n_tiles * tm,), jnp.int32).at[slot].set(tok)
    n_used = (ends[-1] // tm).astype(jnp.int32)
    tile_start = jnp.minimum(jnp.arange(n_tiles, dtype=jnp.int32), n_used - 1) * tm
    tile_expert = jnp.sum((tile_start[:, None] >= ends[None, :]).astype(jnp.int32), axis=1)
    return tok_of_slot, slot[:n], slot[n:], tile_expert, n_used.reshape(1)


def _odd_weights(od_w_in, mla_w_uq, mla_w_ukv):
    c2 = 2 * CONV_CH + MLA_Q_RANK + MLA_KV_RANK
    half = MLA_ROPE // 2
    w_kr = od_w_in[:, c2:]
    w_kr_sw = jnp.concatenate([w_kr[:, half:], w_kr[:, :half]], axis=1)
    zl = jnp.zeros((D_MODEL, MLA_NOPE), F32)
    zr = jnp.zeros((D_MODEL, MLA_PAD - MLA_NOPE - MLA_ROPE), F32)
    win = jnp.concatenate([od_w_in[:, :c2], zl, w_kr, zr, zl, w_kr_sw, zr], axis=1)
    dk = MLA_NOPE + MLA_ROPE
    wq = mla_w_uq.reshape(MLA_Q_RANK, MLA_HEADS, dk)
    zq = jnp.zeros((MLA_Q_RANK, MLA_HEADS, MLA_PAD - dk), F32)
    wuq = jnp.concatenate([wq, zq], axis=2).reshape(MLA_Q_RANK, MLA_HEADS * MLA_PAD)
    wq_sw = jnp.concatenate([jnp.zeros_like(wq[:, :, :MLA_NOPE]), wq[:, :, MLA_NOPE + half:],
                             wq[:, :, MLA_NOPE:MLA_NOPE + half], zq], axis=2)
    wuqs = wq_sw.reshape(MLA_Q_RANK, MLA_HEADS * MLA_PAD)
    wkv = mla_w_ukv.reshape(MLA_KV_RANK, MLA_HEADS, MLA_NOPE + MLA_V)
    zk = jnp.zeros((MLA_KV_RANK, MLA_HEADS, MLA_PAD - MLA_NOPE), F32)
    wuk = jnp.concatenate([wkv[:, :, :MLA_NOPE], zk], axis=2).reshape(MLA_KV_RANK, MLA_HEADS * MLA_PAD)
    zv = jnp.zeros((MLA_KV_RANK, MLA_HEADS, MLA_VROWS - MLA_V), F32)
    wuv_t = jnp.concatenate([wkv[:, :, MLA_NOPE:], zv], axis=2).reshape(MLA_KV_RANK, MLA_HEADS * MLA_VROWS).T
    v_one = jnp.zeros((MLA_HEADS, MLA_VROWS), F32).at[:, MLA_V].set(1.0).reshape(MLA_HEADS * MLA_VROWS, 1)
    return win.astype(BF16), wuq.astype(BF16), wuqs.astype(BF16), wuk.astype(BF16), wuv_t.astype(BF16), v_one


def _rope_tables(seq):
    inv = 1.0 / (ROPE_THETA ** (jnp.arange(0, MLA_ROPE, 2, dtype=F32) / MLA_ROPE))
    ang = jnp.arange(seq, dtype=F32)[:, None] * inv[None, :]
    cos, sin = jnp.cos(ang), jnp.sin(ang)
    ones = jnp.ones((seq, MLA_NOPE), F32)
    zl = jnp.zeros((seq, MLA_NOPE), F32)
    zr = jnp.zeros((seq, MLA_PAD - MLA_NOPE - MLA_ROPE), F32)
    return (jnp.concatenate([ones, cos, cos, zr], axis=1), jnp.concatenate([zl, -sin, sin, zr], axis=1))


def kernel(x, norm_g, ev_w_in, ssm_lambda_re, ssm_lambda_im, ssm_log_dt, ssm_b_re, ssm_b_im, ssm_c_re, ssm_c_im, ssm_d, ssm_w_glu, sgu_ln_g, sgu_ln_b, sgu_w, sgu_b, ev_w_out, ffn_w_gate, ffn_w_up, ffn_w_down, od_w_in, conv_w, conv_b, conv_ln_g, conv_ln_b, mla_q_norm_g, mla_w_uq, mla_kv_norm_g, mla_w_ukv, od_w_out, moe_w_router, moe_w_gate, moe_w_up, moe_w_down):
    bsz, seq, d = x.shape
    n = bsz * seq
    assert d == D_MODEL and SUBLANES % bsz == 0 and seq % 512 == 0
    row = lambda v: v.astype(F32).reshape(1, -1)
    h = x.astype(F32).reshape(n, d)
    tm = 512

    g = norm_g[0]
    a_in, proj = _norm_proj(h, row(g[0]), ev_w_in[0].astype(BF16), tm)
    mats = _s5_matrices(ssm_lambda_re[0], ssm_lambda_im[0], ssm_log_dt[0], ssm_b_re[0], ssm_b_im[0],
                        ssm_c_re[0], ssm_c_im[0])
    ys = _s5_mixer(a_in, mats, ssm_d[0], bsz, seq)
    causal = jnp.tril(jnp.ones((SGU_CHUNK, SGU_CHUNK), dtype=bool))
    ws = jnp.where(causal[None], sgu_w[0], 0.0).astype(BF16)
    bias = jnp.repeat(sgu_b[0].astype(F32).T, SGU_HEAD_DIM, axis=1)
    wo = ev_w_out[0].astype(BF16)
    h, z = _even_mix(ys, proj, h, ssm_w_glu[0].astype(BF16), row(sgu_ln_g[0]), row(sgu_ln_b[0]), ws, bias,
                     wo[:SSM_WIDTH], wo[SSM_WIDTH:], row(g[1]), row(g[2]), tm)
    h, z = _dense_ffn(z, ffn_w_gate[0].astype(BF16), ffn_w_up[0].astype(BF16), ffn_w_down[0].astype(BF16),
                      h, row(g[3]), row(norm_g[1][0]), 1024, 512)

    g = norm_g[1]
    win, wuq, wuqs, wuk, wuv_t, v_one = _odd_weights(od_w_in[0], mla_w_uq[0], mla_w_ukv[0])
    cos_t, sin_t = _rope_tables(seq)
    zc, q, k, vt = _odd_proj(z, win, row(mla_q_norm_g[0]), row(mla_kv_norm_g[0]), wuq, wuqs, wuk, wuv_t, v_one,
                             cos_t, sin_t, seq, tm)
    hp = MLA_HEADS * MLA_PAD
    yd = _attention(q.reshape(bsz, seq, hp), k.reshape(bsz, seq, hp), vt, tm)
    conv_w_pad = jnp.concatenate([conv_w[0].astype(F32), jnp.zeros((CONV_HALO - CONV_TAPS, CONV_CH), F32)], axis=0)
    yc = _conv_mixer(zc.reshape(bsz, seq, 2 * CONV_CH), conv_w_pad, row(conv_b[0]), row(conv_ln_g[0]),
                     row(conv_ln_b[0]), tm)
    wo = od_w_out[0].astype(BF16)
    wr = jnp.concatenate([moe_w_router[0].astype(F32), jnp.zeros((d, LANES - N_EXPERTS), F32)], axis=1)
    h, z, route = _odd_mix(yc.reshape(n, CONV_CH), yd.reshape(n, MLA_HEADS * MLA_V), h, wo[:CONV_CH], wo[CONV_CH:],
                           row(g[1]), row(g[2]), wr.astype(BF16), tm)
    tm_moe = 1024
    tok_of_slot, slot_a, slot_b, tile_expert, n_used = _moe_plan(route, tm_moe)
    y = _moe_ffn(tile_expert, n_used, tok_of_slot, z, moe_w_gate[0], moe_w_up[0], moe_w_down[0], tm_moe, 512)
    h = _combine(slot_a, slot_b, y, route, h, row(g[3]), 256)
    return h.reshape(bsz, seq, d).astype(x.dtype)
```

```python
import functools
import math

import jax
import jax.numpy as jnp
from jax import lax
from jax.experimental import pallas as pl
from jax.experimental.pallas import tpu as pltpu

F32 = jnp.float32
BF16 = jnp.bfloat16

D_MODEL = 1024
NORM_EPS = 1e-6
SSM_WIDTH = 512
SSM_GROUP = 16
SSM_GROUPS = 32
SSM_STATE = 64
SSM_CHUNK = 16
SSM_PAIR = 2 * SSM_GROUP * SSM_CHUNK
SGU_WIDTH = 512
SGU_HEADS = 8
SGU_HEAD_DIM = 64
SGU_CHUNK = 128
CONV_CH = 512
CONV_TAPS = 31
CONV_HALO = 32
MLA_HEADS = 8
MLA_Q_RANK = 256
MLA_KV_RANK = 128
MLA_NOPE = 64
MLA_ROPE = 32
MLA_V = 64
MLA_PAD = 128
MLA_VROWS = 80
ROPE_THETA = 10000.0
FF_DENSE = 4096
N_EXPERTS = 8
FF_EXPERT = 3584
LANES = 128
SUBLANES = 8
ROW_TILE = D_MODEL // LANES
VMEM_LIMIT = 56 * 1024 * 1024


def _params(sem, vmem=VMEM_LIMIT):
    return pltpu.CompilerParams(dimension_semantics=sem, vmem_limit_bytes=vmem)


def _rms(x, g):
    return x * lax.rsqrt(jnp.mean(x * x, axis=-1, keepdims=True) + NORM_EPS) * g


def _layer_norm(x, g, b):
    mu = jnp.mean(x, axis=-1, keepdims=True)
    xc = x - mu
    return xc * lax.rsqrt(jnp.mean(xc * xc, axis=-1, keepdims=True) + NORM_EPS) * g + b


def _dot(a, b):
    return jnp.dot(a, b, preferred_element_type=F32)


def _norm_proj_kernel(h_ref, g_ref, w_ref, a_ref, b_ref):
    z = _rms(h_ref[...], g_ref[...])
    proj = _dot(z.astype(BF16), w_ref[...])
    a_ref[...] = proj[:, :SSM_WIDTH]
    b_ref[...] = proj[:, SSM_WIDTH:].astype(b_ref.dtype)


def _norm_proj(h, g, w, tm):
    n, d = h.shape
    cols = w.shape[1]
    return pl.pallas_call(
        _norm_proj_kernel,
        grid=(n // tm,),
        in_specs=[pl.BlockSpec((tm, d), lambda i: (i, 0)),
                  pl.BlockSpec((1, d), lambda i: (0, 0)),
                  pl.BlockSpec((d, cols), lambda i: (0, 0))],
        out_specs=[pl.BlockSpec((tm, SSM_WIDTH), lambda i: (i, 0)),
                   pl.BlockSpec((tm, cols - SSM_WIDTH), lambda i: (i, 0))],
        out_shape=[jax.ShapeDtypeStruct((n, SSM_WIDTH), F32), jax.ShapeDtypeStruct((n, cols - SSM_WIDTH), BF16)],
        compiler_params=_params(("parallel",)),
        name="even_in_proj",
    )(h, g, w)


def _s5_matrices(lam_re, lam_im, log_dt, b_re, b_im, c_re, c_im):
    t = SSM_CHUNK
    lr = jnp.minimum(lam_re.astype(F32), -1e-4)
    li = lam_im.astype(F32)
    dt = jnp.exp(log_dt.astype(F32))[:, None]
    mag = jnp.exp(lr * dt)
    a_re = mag * jnp.cos(li * dt)
    a_im = mag * jnp.sin(li * dt)
    den = lr * lr + li * li
    nr = a_re - 1.0
    coef_re = (nr * lr + a_im * li) / den
    coef_im = (a_im * lr - nr * li) / den
    br = b_re.astype(F32)
    bi = b_im.astype(F32)
    bb_re = coef_re[..., None] * br - coef_im[..., None] * bi
    bb_im = coef_re[..., None] * bi + coef_im[..., None] * br
    cr = c_re.astype(F32)
    ci = c_im.astype(F32)
    pw_re = [jnp.ones_like(a_re)]
    pw_im = [jnp.zeros_like(a_im)]
    for _ in range(t):
        pr, pi = pw_re[-1], pw_im[-1]
        pw_re.append(pr * a_re - pi * a_im)
        pw_im.append(pr * a_im + pi * a_re)
    pw_re = jnp.stack(pw_re)
    pw_im = jnp.stack(pw_im)
    ab_re = pw_re[:t, :, :, None] * bb_re[None] - pw_im[:t, :, :, None] * bb_im[None]
    ab_im = pw_re[:t, :, :, None] * bb_im[None] + pw_im[:t, :, :, None] * bb_re[None]
    hi = lax.Precision.HIGHEST
    k_lag = (jnp.einsum('gnp,tgpm->tgnm', cr, ab_re, precision=hi)
             - jnp.einsum('gnp,tgpm->tgnm', ci, ab_im, precision=hi))
    n_pairs = SSM_GROUPS // 2
    st = 2 * SSM_STATE

    def pair_diag(w):
        w = w.reshape((n_pairs, 2) + w.shape[1:])
        z = jnp.zeros_like(w[:, 0])
        top = jnp.concatenate([w[:, 0], z], axis=-1)
        bot = jnp.concatenate([z, w[:, 1]], axis=-1)
        return jnp.concatenate([top, bot], axis=-2)

    k_blk = pair_diag(k_lag.transpose(1, 0, 3, 2)).astype(BF16)
    rev_re = pw_re[:t][::-1]
    rev_im = pw_im[:t][::-1]
    ws_re = rev_re[..., None] * bb_re[None] - rev_im[..., None] * bb_im[None]
    ws_im = rev_re[..., None] * bb_im[None] + rev_im[..., None] * bb_re[None]
    ws_re = pair_diag(ws_re.transpose(1, 0, 3, 2)).reshape(n_pairs, SSM_PAIR, st).astype(BF16)
    ws_im = pair_diag(ws_im.transpose(1, 0, 3, 2)).reshape(n_pairs, SSM_PAIR, st).astype(BF16)
    ca_re = cr[None] * pw_re[1:, :, None, :] - ci[None] * pw_im[1:, :, None, :]
    ca_im = cr[None] * pw_im[1:, :, None, :] + ci[None] * pw_re[1:, :, None, :]
    co_re = pair_diag(ca_re.transpose(1, 0, 3, 2)).astype(BF16)
    co_im = pair_diag((-ca_im).transpose(1, 0, 3, 2)).astype(BF16)
    w_intra, wo_re, wo_im = _s5_expand(k_blk, co_re, co_im)
    return dict(
        w_intra=w_intra, ws_re=ws_re, ws_im=ws_im, wo_re=wo_re, wo_im=wo_im,
        at_re=pw_re[t].reshape(1, SSM_GROUPS * SSM_STATE), at_im=pw_im[t].reshape(1, SSM_GROUPS * SSM_STATE))


def _s5_expand_kernel(k_ref, cre_ref, cim_ref, er_ref, ec_ref, wi_ref, wore_ref, woim_ref):
    pw = 2 * SSM_GROUP
    e_r = er_ref[...]
    e_c = ec_ref[...]
    rb = lax.broadcasted_iota(jnp.int32, (SSM_PAIR, SSM_PAIR), 0) // pw
    cb = lax.broadcasted_iota(jnp.int32, (SSM_PAIR, SSM_PAIR), 1) // pw
    lag = cb - rb
    w = jnp.zeros((SSM_PAIR, SSM_PAIR), F32)
    for tau in range(SSM_CHUNK):
        tiled = _dot(_dot(e_r, k_ref[0, tau]).astype(BF16), e_c)
        w = jnp.where(lag == tau, tiled, w)
    wi_ref[0] = w.astype(wi_ref.dtype)
    cb_o = lax.broadcasted_iota(jnp.int32, (2 * SSM_STATE, SSM_PAIR), 1) // pw
    o_re = jnp.zeros((2 * SSM_STATE, SSM_PAIR), F32)
    o_im = jnp.zeros((2 * SSM_STATE, SSM_PAIR), F32)
    for step in range(SSM_CHUNK):
        o_re = jnp.where(cb_o == step, _dot(cre_ref[0, step], e_c), o_re)
        o_im = jnp.where(cb_o == step, _dot(cim_ref[0, step], e_c), o_im)
    wore_ref[0] = o_re.astype(wore_ref.dtype)
    woim_ref[0] = o_im.astype(woim_ref.dtype)


def _s5_expand(k_blk, co_re, co_im):
    n_pairs = k_blk.shape[0]
    pw = 2 * SSM_GROUP
    st = 2 * SSM_STATE
    eye = jnp.eye(pw, dtype=BF16)
    e_c = jnp.concatenate([eye] * SSM_CHUNK, axis=1)
    e_r = jnp.concatenate([eye] * SSM_CHUNK, axis=0)
    return pl.pallas_call(
        _s5_expand_kernel,
        grid=(n_pairs,),
        in_specs=[pl.BlockSpec((1, SSM_CHUNK, pw, pw), lambda q: (q, 0, 0, 0)),
                  pl.BlockSpec((1, SSM_CHUNK, st, pw), lambda q: (q, 0, 0, 0)),
                  pl.BlockSpec((1, SSM_CHUNK, st, pw), lambda q: (q, 0, 0, 0)),
                  pl.BlockSpec((SSM_PAIR, pw), lambda q: (0, 0)),
                  pl.BlockSpec((pw, SSM_PAIR), lambda q: (0, 0))],
        out_specs=[pl.BlockSpec((1, SSM_PAIR, SSM_PAIR), lambda q: (q, 0, 0)),
                   pl.BlockSpec((1, st, SSM_PAIR), lambda q: (q, 0, 0)),
                   pl.BlockSpec((1, st, SSM_PAIR), lambda q: (q, 0, 0))],
        out_shape=[jax.ShapeDtypeStruct((n_pairs, SSM_PAIR, SSM_PAIR), BF16),
                   jax.ShapeDtypeStruct((n_pairs, st, SSM_PAIR), BF16),
                   jax.ShapeDtypeStruct((n_pairs, st, SSM_PAIR), BF16)],
        compiler_params=_params(("parallel",)),
        name="s5_expand_weights",
    )(k_blk, co_re, co_im, e_r, e_c)


S5_LANE_PAIRS = LANES // (2 * SSM_GROUP)
S5_SCAN_LANES = 512


def _s5_state_kernel(u0_ref, u1_ref, u2_ref, u3_ref, wre_ref, wim_ref, are_ref, aim_ref,
                     x_ref, hre_ref, him_ref, sre_ref, sim_ref):
    n_chunks = x_ref.shape[0]
    pw = 2 * SSM_GROUP
    u_refs = (u0_ref, u1_ref, u2_ref, u3_ref)
    for t in range(SSM_CHUNK):
        for j, u_ref in enumerate(u_refs):
            ut = u_ref[pl.ds(t, n_chunks, stride=SSM_CHUNK), :]
            for qq in range(S5_LANE_PAIRS):
                q = j * S5_LANE_PAIRS + qq
                x_ref[:, q * SSM_PAIR + t * pw: q * SSM_PAIR + (t + 1) * pw] = (
                    ut[:, qq * pw:(qq + 1) * pw].astype(x_ref.dtype))
    st = 2 * SSM_STATE
    for q in range(SSM_GROUPS // 2):
        xq = x_ref[:, q * SSM_PAIR:(q + 1) * SSM_PAIR]
        sre_ref[:, q * st:(q + 1) * st] = _dot(xq, wre_ref[q])
        sim_ref[:, q * st:(q + 1) * st] = _dot(xq, wim_ref[q])

    row = lax.broadcasted_iota(jnp.int32, (SUBLANES, S5_SCAN_LANES), 0)
    zero = jnp.zeros((SUBLANES, S5_SCAN_LANES), F32)
    for c0 in range(0, sre_ref.shape[1], S5_SCAN_LANES):
        cols = pl.ds(c0, S5_SCAN_LANES)
        ar = are_ref[:, cols]
        ai = aim_ref[:, cols]

        def body(k, carry, cols=cols, ar=ar, ai=ai):
            r0 = pl.multiple_of(k * SUBLANES, SUBLANES)
            sr = sre_ref[pl.ds(r0, SUBLANES), cols]
            si = sim_ref[pl.ds(r0, SUBLANES), cols]
            out_r, out_i = carry
            for i in range(1, SUBLANES + 1):
                tr = ar * out_r - ai * out_i + sr
                ti = ar * out_i + ai * out_r + si
                tr = pltpu.roll(tr, 1, 0)
                ti = pltpu.roll(ti, 1, 0)
                if i < SUBLANES:
                    out_r = jnp.where(row == i, tr, out_r)
                    out_i = jnp.where(row == i, ti, out_i)
            hre_ref[pl.ds(r0, SUBLANES), cols] = out_r
            him_ref[pl.ds(r0, SUBLANES), cols] = out_i
            return tr, ti

        lax.fori_loop(0, n_chunks // SUBLANES, body, (zero, zero))


def _s5_out_kernel(x_ref, wi_ref, hre_ref, him_ref, wore_ref, woim_ref, d_ref, y_ref, yt_ref):
    n_chunks = x_ref.shape[0]
    pw = 2 * SSM_GROUP
    st = 2 * SSM_STATE
    for qq in range(S5_LANE_PAIRS):
        x = x_ref[:, qq * SSM_PAIR:(qq + 1) * SSM_PAIR]
        y = _dot(x, wi_ref[qq])
        y += _dot(hre_ref[:, qq * st:(qq + 1) * st].astype(BF16), wore_ref[qq])
        y += _dot(him_ref[:, qq * st:(qq + 1) * st].astype(BF16), woim_ref[qq])
        y += d_ref[:, qq * SSM_PAIR:(qq + 1) * SSM_PAIR] * x.astype(F32)
        y = jax.nn.gelu(y)
        for t in range(SSM_CHUNK):
            yt_ref[t, :, qq * pw:(qq + 1) * pw] = y[:, t * pw:(t + 1) * pw]
    for t in range(SSM_CHUNK):
        y_ref[pl.ds(t, n_chunks, stride=SSM_CHUNK), :] = yt_ref[t]


def _s5_mixer(u, mats, d, batch, seq):
    t = SSM_CHUNK
    n_chunks = seq // t
    n_pairs = SSM_GROUPS // 2
    cols = n_pairs * SSM_PAIR
    st = 2 * SSM_STATE
    n_state = n_pairs * st
    n_blk = SSM_WIDTH // LANES
    assert n_blk == 4 and n_chunks % SUBLANES == 0
    once = pl.Buffered(1)
    x, h_re, h_im = pl.pallas_call(
        _s5_state_kernel,
        grid=(batch,),
        in_specs=[pl.BlockSpec((seq, LANES), lambda b, j=j: (b, j)) for j in range(n_blk)] + [
            pl.BlockSpec((n_pairs, SSM_PAIR, st), lambda b: (0, 0, 0), pipeline_mode=once),
            pl.BlockSpec((n_pairs, SSM_PAIR, st), lambda b: (0, 0, 0), pipeline_mode=once),
            pl.BlockSpec((1, n_state), lambda b: (0, 0)),
            pl.BlockSpec((1, n_state), lambda b: (0, 0))],
        out_specs=[pl.BlockSpec((n_chunks, cols), lambda b: (b, 0)),
                   pl.BlockSpec((n_chunks, n_state), lambda b: (b, 0)),
                   pl.BlockSpec((n_chunks, n_state), lambda b: (b, 0))],
        out_shape=[jax.ShapeDtypeStruct((batch * n_chunks, cols), BF16),
                   jax.ShapeDtypeStruct((batch * n_chunks, n_state), F32),
                   jax.ShapeDtypeStruct((batch * n_chunks, n_state), F32)],
        scratch_shapes=[pltpu.VMEM((n_chunks, n_state), F32), pltpu.VMEM((n_chunks, n_state), F32)],
        compiler_params=_params(("parallel",)),
        name="s5_state_scan",
    )(u, u, u, u, mats['ws_re'], mats['ws_im'], mats['at_re'], mats['at_im'])
    lp = S5_LANE_PAIRS
    d_cols = jnp.broadcast_to(d.astype(F32).reshape(n_pairs, 1, 2 * SSM_GROUP),
                              (n_pairs, t, 2 * SSM_GROUP)).reshape(1, cols)
    return pl.pallas_call(
        _s5_out_kernel,
        grid=(batch, n_blk),
        in_specs=[pl.BlockSpec((n_chunks, lp * SSM_PAIR), lambda b, j: (b, j)),
                  pl.BlockSpec((lp, SSM_PAIR, SSM_PAIR), lambda b, j: (j, 0, 0)),
                  pl.BlockSpec((n_chunks, lp * st), lambda b, j: (b, j)),
                  pl.BlockSpec((n_chunks, lp * st), lambda b, j: (b, j)),
                  pl.BlockSpec((lp, st, SSM_PAIR), lambda b, j: (j, 0, 0)),
                  pl.BlockSpec((lp, st, SSM_PAIR), lambda b, j: (j, 0, 0)),
                  pl.BlockSpec((1, lp * SSM_PAIR), lambda b, j: (0, j))],
        out_specs=pl.BlockSpec((seq, LANES), lambda b, j: (b, j)),
        out_shape=jax.ShapeDtypeStruct((batch * seq, SSM_WIDTH), F32),
        scratch_shapes=[pltpu.VMEM((t, n_chunks, LANES), F32)],
        compiler_params=_params(("parallel", "parallel")),
        name="s5_out",
    )(x, mats['w_intra'], h_re, h_im, mats['wo_re'], mats['wo_im'], d_cols)


def _even_mix_kernel(ys_ref, bu_ref, bv_ref, h_ref, wglu_ref, lng_ref, lnb_ref, ws_ref, bias_ref,
                     wo_a_ref, wo_b_ref, g1_ref, g2_ref, hout_ref, z_ref, s_scr):
    tm = ys_ref.shape[0]
    ys = ys_ref[...]
    ya = ys * jax.nn.sigmoid(_dot(ys.astype(BF16), wglu_ref[...]))
    u = jax.nn.gelu(bu_ref[...].astype(F32))
    v = _layer_norm(jax.nn.gelu(bv_ref[...].astype(F32)), lng_ref[...], lnb_ref[...])
    lane = lax.broadcasted_iota(jnp.int32, v.shape, 1)
    left = (lane % LANES) < SGU_HEAD_DIM
    v_l = jnp.where(left, v, 0.0).astype(BF16)
    v_r = jnp.where(left, 0.0, v).astype(BF16)
    for c in range(tm // SGU_CHUNK):
        rows = slice(c * SGU_CHUNK, (c + 1) * SGU_CHUNK)
        for p in range(SGU_HEADS // 2):
            cols = slice(p * LANES, (p + 1) * LANES)
            s_scr[rows, cols] = (_dot(ws_ref[2 * p], v_l[rows, cols]) + _dot(ws_ref[2 * p + 1], v_r[rows, cols]))
    bias = jnp.concatenate([bias_ref[...]] * (tm // SGU_CHUNK), axis=0)
    yb = u * (s_scr[...] + bias)
    mix = _dot(ya.astype(BF16), wo_a_ref[...]) + _dot(yb.astype(BF16), wo_b_ref[...])
    h_new = h_ref[...] + _rms(mix, g1_ref[...])
    hout_ref[...] = h_new
    z_ref[...] = _rms(h_new, g2_ref[...]).astype(z_ref.dtype)


def _even_mix(ys, proj, h, wglu, lng, lnb, ws, bias, wo_a, wo_b, g1, g2, tm):
    n, d = h.shape
    w = SGU_WIDTH
    const = lambda *shape: pl.BlockSpec(shape, lambda i: (0,) * len(shape))
    return pl.pallas_call(
        _even_mix_kernel,
        grid=(n // tm,),
        in_specs=[pl.BlockSpec((tm, w), lambda i: (i, 0)),
                  pl.BlockSpec((tm, w), lambda i: (i, 0)),
                  pl.BlockSpec((tm, w), lambda i: (i, 1)),
                  pl.BlockSpec((tm, d), lambda i: (i, 0)),
                  const(w, w), const(1, w), const(1, w),
                  const(SGU_HEADS, SGU_CHUNK, SGU_CHUNK), const(SGU_CHUNK, w),
                  const(w, d), const(w, d), const(1, d), const(1, d)],
        out_specs=[pl.BlockSpec((tm, d), lambda i: (i, 0)),
                   pl.BlockSpec((tm, d), lambda i: (i, 0))],
        out_shape=[jax.ShapeDtypeStruct((n, d), F32), jax.ShapeDtypeStruct((n, d), BF16)],
        scratch_shapes=[pltpu.VMEM((tm, w), F32)],
        compiler_params=_params(("parallel",)),
        name="even_mix",
    )(ys, proj, proj, h, wglu, lng, lnb, ws, bias, wo_a, wo_b, g1, g2)


def _ffn_kernel(z_ref, wg_ref, wu_ref, wd_ref, h_ref, g3_ref, gn_ref, hout_ref, zout_ref, acc_ref):
    j = pl.program_id(1)

    @pl.when(j == 0)
    def _():
        acc_ref[...] = jnp.zeros_like(acc_ref)

    z = z_ref[...]
    a = jax.nn.silu(_dot(z, wg_ref[...])) * _dot(z, wu_ref[...])
    acc_ref[...] += _dot(a.astype(BF16), wd_ref[...])

    @pl.when(j == pl.num_programs(1) - 1)
    def _():
        h_new = h_ref[...] + _rms(acc_ref[...], g3_ref[...])
        hout_ref[...] = h_new
        zout_ref[...] = _rms(h_new, gn_ref[...]).astype(zout_ref.dtype)


def _dense_ffn(z, wg, wu, wd, h, g3, g_next, tm, tf):
    n, d = h.shape
    ff = wg.shape[1]
    return pl.pallas_call(
        _ffn_kernel,
        grid=(n // tm, ff // tf),
        in_specs=[pl.BlockSpec((tm, d), lambda i, j: (i, 0)),
                  pl.BlockSpec((d, tf), lambda i, j: (0, j)),
                  pl.BlockSpec((d, tf), lambda i, j: (0, j)),
                  pl.BlockSpec((tf, d), lambda i, j: (j, 0)),
                  pl.BlockSpec((tm, d), lambda i, j: (i, 0)),
                  pl.BlockSpec((1, d), lambda i, j: (0, 0)),
                  pl.BlockSpec((1, d), lambda i, j: (0, 0))],
        out_specs=[pl.BlockSpec((tm, d), lambda i, j: (i, 0)),
                   pl.BlockSpec((tm, d), lambda i, j: (i, 0))],
        out_shape=[jax.ShapeDtypeStruct((n, d), F32), jax.ShapeDtypeStruct((n, d), BF16)],
        scratch_shapes=[pltpu.VMEM((tm, d), F32)],
        compiler_params=_params(("parallel", "arbitrary")),
        name="dense_ffn",
    )(z, wg, wu, wd, h, g3, g_next)


def _odd_proj_kernel(z_ref, win_ref, gq_ref, gkv_ref, wuq_ref, wuqs_ref, wuk_ref, wuv_ref, vone_ref, cos_ref, sin_ref,
                     zc_ref, q_ref, k_ref, v_ref, *, scale):
    z = z_ref[...]
    proj = _dot(z, win_ref[...])
    c0 = 2 * CONV_CH
    c1 = c0 + MLA_Q_RANK
    c2 = c1 + MLA_KV_RANK
    c3 = c2 + MLA_PAD
    zc_ref[...] = proj[:, :c0].astype(zc_ref.dtype)
    cq = _rms(proj[:, c0:c1], gq_ref[...]).astype(BF16)
    ckv = _rms(proj[:, c1:c2], gkv_ref[...]).astype(BF16)
    cos = cos_ref[...]
    sin = sin_ref[...]
    cos_h = jnp.concatenate([cos] * MLA_HEADS, axis=1)
    sin_h = jnp.concatenate([sin] * MLA_HEADS, axis=1)
    q = _dot(cq, wuq_ref[...]) * cos_h + _dot(cq, wuqs_ref[...]) * sin_h
    q_ref[...] = (q * scale).astype(q_ref.dtype)
    kr = proj[:, c2:c3] * cos + proj[:, c3:] * sin
    k = _dot(ckv, wuk_ref[...]) + jnp.concatenate([kr] * MLA_HEADS, axis=1)
    k_ref[...] = k.astype(k_ref.dtype)
    vt = lax.dot_general(wuv_ref[...], ckv, (((1,), (1,)), ((), ())), preferred_element_type=F32)
    v_ref[0] = (vt + vone_ref[...]).astype(v_ref.dtype)


def _odd_proj(z, win, gq, gkv, wuq, wuqs, wuk, wuv_t, v_one, cos_t, sin_t, seq, tm):
    n, d = z.shape
    hp = MLA_HEADS * MLA_PAD
    vr = MLA_HEADS * MLA_VROWS
    n_l = seq // tm
    const = lambda *shape: pl.BlockSpec(shape, lambda i: (0,) * len(shape))
    out = jax.ShapeDtypeStruct((n, hp), BF16)
    scale = float((MLA_NOPE + MLA_ROPE) ** -0.5 * math.log2(math.e))
    return pl.pallas_call(
        functools.partial(_odd_proj_kernel, scale=scale),
        grid=(n // tm,),
        in_specs=[pl.BlockSpec((tm, d), lambda i: (i, 0)),
                  const(d, win.shape[1]), const(1, MLA_Q_RANK), const(1, MLA_KV_RANK),
                  const(MLA_Q_RANK, hp), const(MLA_Q_RANK, hp), const(MLA_KV_RANK, hp), const(vr, MLA_KV_RANK),
                  const(vr, 1),
                  pl.BlockSpec((tm, MLA_PAD), lambda i: (i % n_l, 0)),
                  pl.BlockSpec((tm, MLA_PAD), lambda i: (i % n_l, 0))],
        out_specs=[pl.BlockSpec((tm, 2 * CONV_CH), lambda i: (i, 0)),
                   pl.BlockSpec((tm, hp), lambda i: (i, 0)),
                   pl.BlockSpec((tm, hp), lambda i: (i, 0)),
                   pl.BlockSpec((1, vr, tm), lambda i: (i, 0, 0))],
        out_shape=[jax.ShapeDtypeStruct((n, 2 * CONV_CH), BF16), out, out,
                   jax.ShapeDtypeStruct((n // tm, vr, tm), BF16)],
        compiler_params=_params(("parallel",)),
        name="odd_in_proj",
    )(z, win, gq, gkv, wuq, wuqs, wuk, wuv_t, v_one, cos_t, sin_t)


def _attn_kernel(q_ref, k_ref, vt_ref, o_ref, acc_ref, *, blk):
    i = pl.program_id(2)
    acc_ref[...] = jnp.zeros_like(acc_ref)

    def step(j, m, masked):
        r0 = pl.multiple_of(j * blk, blk)
        scores = []
        for hh in range(2):
            q = q_ref[0, :, hh * MLA_PAD:(hh + 1) * MLA_PAD]
            k = k_ref[0, pl.ds(r0, blk), hh * MLA_PAD:(hh + 1) * MLA_PAD]
            st = lax.dot_general(k, q, (((1,), (1,)), ((), ())), preferred_element_type=F32)
            if masked:
                key = lax.broadcasted_iota(jnp.int32, st.shape, 0)
                qry = lax.broadcasted_iota(jnp.int32, st.shape, 1)
                st = jnp.where(key <= qry, st, -1e30)
            scores.append(st)
        soft = []
        for hh in range(2):
            m_new = jnp.maximum(m[hh], jnp.max(scores[hh], axis=0, keepdims=True))
            soft.append((m_new, jnp.exp2(m[hh] - m_new), jnp.exp2(scores[hh] - m_new).astype(BF16)))
        for hh in range(2):
            vt = vt_ref[j, hh * MLA_VROWS:(hh + 1) * MLA_VROWS, :]
            acc_ref[hh] = soft[hh][1] * acc_ref[hh] + _dot(vt, soft[hh][2])
        return (soft[0][0], soft[1][0])

    init = jnp.full((1, blk), -1e30, F32)
    m = lax.fori_loop(0, i, lambda j, m: step(j, m, False), (init, init))
    step(i, m, True)
    a0 = acc_ref[0]
    a1 = acc_ref[1]
    ot = jnp.concatenate([a0[:MLA_V] / a0[MLA_V:MLA_V + 1], a1[:MLA_V] / a1[MLA_V:MLA_V + 1]], axis=0)
    o_ref[0] = ot.T.astype(o_ref.dtype)


def _attention(q, k, vt, blk):
    b, seq, _ = q.shape
    n_blk = seq // blk
    return pl.pallas_call(
        functools.partial(_attn_kernel, blk=blk),
        grid=(b, MLA_HEADS // 2, n_blk),
        in_specs=[pl.BlockSpec((1, blk, 2 * MLA_PAD), lambda bi, p, i: (bi, i, p)),
                  pl.BlockSpec((1, seq, 2 * MLA_PAD), lambda bi, p, i: (bi, 0, p)),
                  pl.BlockSpec((n_blk, 2 * MLA_VROWS, blk), lambda bi, p, i: (bi, p, 0))],
        out_specs=pl.BlockSpec((1, blk, 2 * MLA_V), lambda bi, p, i: (bi, i, p)),
        out_shape=jax.ShapeDtypeStruct((b, seq, MLA_HEADS * MLA_V), BF16),
        scratch_shapes=[pltpu.VMEM((2, MLA_VROWS, blk), F32)],
        compiler_params=_params(("parallel", "parallel", "parallel")),
        name="mla_attention",
    )(q, k, vt)


def _conv_kernel(zc_ref, w_ref, b_ref, lng_ref, lnb_ref, y_ref, buf_ref, part_ref):
    tm = zc_ref.shape[1]

    @pl.when(pl.program_id(1) == 0)
    def _():
        buf_ref[pl.ds(0, CONV_HALO), :] = jnp.zeros((CONV_HALO, CONV_CH), F32)
        buf_ref[pl.ds(CONV_HALO + tm, SUBLANES), :] = jnp.zeros((SUBLANES, CONV_CH), F32)

    zc = zc_ref[0].astype(F32)
    hh = zc[:, :CONV_CH] * jax.nn.sigmoid(zc[:, CONV_CH:])
    buf_ref[pl.ds(CONV_HALO, tm), :] = hh
    off = CONV_HALO - (CONV_TAPS - 1)
    acc = jnp.zeros((tm, CONV_CH), F32) + b_ref[...]
    for b in range(SUBLANES):
        taps = [k for k in range(CONV_TAPS) if (off + k) % SUBLANES == b]
        part = None
        for k in taps:
            term = w_ref[pl.ds(k, 1), :] * buf_ref[pl.ds(off + k - b, tm + SUBLANES), :]
            part = term if part is None else part + term
        if b == 0:
            acc = acc + part[:tm]
        else:
            part_ref[...] = part
            acc = acc + part_ref[pl.ds(b, tm), :]
    buf_ref[pl.ds(0, CONV_HALO), :] = buf_ref[pl.ds(tm, CONV_HALO), :]
    y_ref[0] = jax.nn.silu(_layer_norm(acc, lng_ref[...], lnb_ref[...])).astype(y_ref.dtype)


def _conv_mixer(zc, w, b, lng, lnb, tm):
    bsz, seq, _ = zc.shape
    const = lambda *shape: pl.BlockSpec(shape, lambda bi, i: (0,) * len(shape))
    return pl.pallas_call(
        _conv_kernel,
        grid=(bsz, seq // tm),
        in_specs=[pl.BlockSpec((1, tm, 2 * CONV_CH), lambda bi, i: (bi, i, 0)),
                  const(CONV_HALO, CONV_CH), const(1, CONV_CH), const(1, CONV_CH), const(1, CONV_CH)],
        out_specs=pl.BlockSpec((1, tm, CONV_CH), lambda bi, i: (bi, i, 0)),
        out_shape=jax.ShapeDtypeStruct((bsz, seq, CONV_CH), BF16),
        scratch_shapes=[pltpu.VMEM((CONV_HALO + tm + SUBLANES, CONV_CH), F32),
                        pltpu.VMEM((tm + SUBLANES, CONV_CH), F32)],
        compiler_params=_params(("arbitrary", "arbitrary")),
        name="conv_module",
    )(zc, w, b, lng, lnb)


def _odd_mix_kernel(yc_ref, yd_ref, h_ref, wo_a_ref, wo_b_ref, g1_ref, g2_ref, wr_ref, hout_ref, z_ref, route_ref):
    mix = _dot(yc_ref[...], wo_a_ref[...]) + _dot(yd_ref[...], wo_b_ref[...])
    h_new = h_ref[...] + _rms(mix, g1_ref[...])
    hout_ref[...] = h_new
    z = _rms(h_new, g2_ref[...])
    _store_row_tiles(z_ref, z)
    logits = _dot(z.astype(BF16), wr_ref[...])
    lane = lax.broadcasted_iota(jnp.int32, logits.shape, 1)
    neg = -jnp.inf
    logits = jnp.where(lane < N_EXPERTS, logits, neg)
    m1 = jnp.max(logits, axis=-1, keepdims=True)
    i1 = jnp.min(jnp.where(logits == m1, lane, LANES), axis=-1, keepdims=True)
    rest = jnp.where(lane == i1, neg, logits)
    m2 = jnp.max(rest, axis=-1, keepdims=True)
    i2 = jnp.min(jnp.where(rest == m2, lane, LANES), axis=-1, keepdims=True)
    e = jnp.exp(m2 - m1)
    w1 = 1.0 / (1.0 + e)
    w2 = e / (1.0 + e)
    route = jnp.where(lane == 0, i1.astype(F32),
                      jnp.where(lane == 1, i2.astype(F32),
                                jnp.where(lane == 2, w1, jnp.where(lane == 3, w2, 0.0))))
    route_ref[...] = route


def _odd_mix(yc, yd, h, wo_a, wo_b, g1, g2, wr, tm):
    n, d = h.shape
    const = lambda *shape: pl.BlockSpec(shape, lambda i: (0,) * len(shape))
    return pl.pallas_call(
        _odd_mix_kernel,
        grid=(n // tm,),
        in_specs=[pl.BlockSpec((tm, yc.shape[1]), lambda i: (i, 0)),
                  pl.BlockSpec((tm, yd.shape[1]), lambda i: (i, 0)),
                  pl.BlockSpec((tm, d), lambda i: (i, 0)),
                  const(*wo_a.shape), const(*wo_b.shape), const(1, d), const(1, d), const(d, LANES)],
        out_specs=[pl.BlockSpec((tm, d), lambda i: (i, 0)),
                   pl.BlockSpec((tm * ROW_TILE, LANES), lambda i: (i, 0)),
                   pl.BlockSpec((tm, LANES), lambda i: (i, 0))],
        out_shape=[jax.ShapeDtypeStruct((n, d), F32), jax.ShapeDtypeStruct((n * ROW_TILE, LANES), F32),
                   jax.ShapeDtypeStruct((n, LANES), F32)],
        compiler_params=_params(("parallel",)),
        name="odd_mix_router",
    )(yc, yd, h, wo_a, wo_b, g1, g2, wr)


def _store_row_tiles(ref, x):
    rows = x.shape[0]
    for s in range(ROW_TILE):
        ref[pl.ds(s, rows, stride=ROW_TILE), :] = x[:, s * LANES:(s + 1) * LANES]


def _load_row_tiles(ref, rows):
    return [ref[pl.ds(s, rows, stride=ROW_TILE), :] for s in range(ROW_TILE)]


def _gather_rows(idx_ref, base, n_rows, src_hbm, dst_ref, sem):
    def body(r, c):
        src = pl.multiple_of(idx_ref[base + r] * ROW_TILE, ROW_TILE)
        dst = pl.multiple_of(r * ROW_TILE, ROW_TILE)
        pltpu.make_async_copy(src_hbm.at[pl.ds(src, ROW_TILE), :], dst_ref.at[pl.ds(dst, ROW_TILE), :], sem).start()
        return c

    lax.fori_loop(0, n_rows, body, 0, unroll=8)


def _wait_rows(src_hbm, dst_ref, sem):
    pltpu.make_async_copy(src_hbm.at[pl.ds(0, dst_ref.shape[0]), :], dst_ref, sem).wait()


def _moe_ffn_kernel(te_ref, nu_ref, tok_ref, z_hbm, wg_ref, wu_ref, wd_ref, y_ref, xraw_ref, xb_ref, acc_ref, sem,
                    *, rows_per_step):
    i = pl.program_id(0)
    j = pl.program_id(1)
    tm = xb_ref.shape[0]
    n_used = nu_ref[0]
    slot = i % 2
    first = j == 0

    @pl.when(first & (i == 0))
    def _():
        _gather_rows(tok_ref, 0, xraw_ref.shape[1] // ROW_TILE, z_hbm, xraw_ref.at[0], sem.at[0])

    @pl.when(first & (i <= n_used))
    def _():
        _wait_rows(z_hbm, xraw_ref.at[slot], sem.at[slot])

    @pl.when(first & (i < n_used))
    def _():
        for s, blk in enumerate(_load_row_tiles(xraw_ref.at[slot], tm)):
            xb_ref[:, s * LANES:(s + 1) * LANES] = blk.astype(BF16)

    @pl.when(first)
    def _():
        acc_ref[...] = jnp.zeros_like(acc_ref)

    @pl.when(i < n_used)
    def _():
        x = xb_ref[...]
        g = _dot(x, wg_ref[0].astype(BF16))
        nxt = xraw_ref.at[1 - slot]
        for rr in range(rows_per_step):
            r = j * rows_per_step + rr
            src = pl.multiple_of(tok_ref[(i + 1) * tm + r] * ROW_TILE, ROW_TILE)
            dst = pl.multiple_of(r * ROW_TILE, ROW_TILE)
            pltpu.make_async_copy(z_hbm.at[pl.ds(src, ROW_TILE), :], nxt.at[pl.ds(dst, ROW_TILE), :],
                                  sem.at[1 - slot]).start()
        a = jax.nn.silu(g) * _dot(x, wu_ref[0].astype(BF16))
        acc_ref[...] += _dot(a.astype(BF16), wd_ref[0].astype(BF16))

    @pl.when(j == pl.num_programs(1) - 1)
    def _():
        _store_row_tiles(y_ref, acc_ref[...])


def _moe_ffn(tile_expert, n_used, tok_of_slot, z_tiles, wg, wu, wd, tm, tf):
    d, ff = wg.shape[1], wg.shape[2]
    n_f = ff // tf
    rows_per_step = -(-tm // n_f)
    buf_rows = n_f * rows_per_step
    n_tiles = tile_expert.shape[0]
    assert tok_of_slot.shape[0] >= (n_tiles - 1) * tm + buf_rows
    n_slots = n_tiles * tm

    def col(i, j, nu):
        return jnp.where(i < nu[0], j, n_f - 1)

    return pl.pallas_call(
        functools.partial(_moe_ffn_kernel, rows_per_step=rows_per_step),
        grid_spec=pltpu.PrefetchScalarGridSpec(
            num_scalar_prefetch=3,
            grid=(n_tiles, n_f),
            in_specs=[pl.BlockSpec(memory_space=pl.ANY),
                      pl.BlockSpec((1, d, tf), lambda i, j, te, nu, tok: (te[i], 0, col(i, j, nu))),
                      pl.BlockSpec((1, d, tf), lambda i, j, te, nu, tok: (te[i], 0, col(i, j, nu))),
                      pl.BlockSpec((1, tf, d), lambda i, j, te, nu, tok: (te[i], col(i, j, nu), 0))],
            out_specs=pl.BlockSpec((tm * ROW_TILE, LANES), lambda i, j, te, nu, tok: (i, 0)),
            scratch_shapes=[pltpu.VMEM((2, buf_rows * ROW_TILE, LANES), F32), pltpu.VMEM((tm, d), BF16),
                            pltpu.VMEM((tm, d), F32), pltpu.SemaphoreType.DMA((2,))]),
        out_shape=jax.ShapeDtypeStruct((n_slots * ROW_TILE, LANES), F32),
        compiler_params=_params(("arbitrary", "arbitrary")),
        name="moe_grouped_ffn",
    )(tile_expert, n_used, tok_of_slot, z_tiles, wg, wu, wd)


def _combine_kernel(sa_ref, sb_ref, y_hbm, route_ref, h_ref, g_ref, o_ref, a_buf, b_buf, sem):
    i = pl.program_id(0)
    n_steps = pl.num_programs(0)
    tm = h_ref.shape[0]
    slot = i % 2

    def start(step, sl):
        _gather_rows(sa_ref, step * tm, tm, y_hbm, a_buf.at[sl], sem.at[0, sl])
        _gather_rows(sb_ref, step * tm, tm, y_hbm, b_buf.at[sl], sem.at[1, sl])

    @pl.when(i == 0)
    def _():
        start(0, 0)

    @pl.when(i + 1 < n_steps)
    def _():
        start(i + 1, 1 - slot)

    _wait_rows(y_hbm, a_buf.at[slot], sem.at[0, slot])
    _wait_rows(y_hbm, b_buf.at[slot], sem.at[1, slot])
    route = route_ref[...]
    a = jnp.concatenate(_load_row_tiles(a_buf.at[slot], tm), axis=1)
    b = jnp.concatenate(_load_row_tiles(b_buf.at[slot], tm), axis=1)
    f = route[:, 2:3] * a + route[:, 3:4] * b
    o_ref[...] = h_ref[...] + _rms(f, g_ref[...])


def _combine(slot_a, slot_b, y, route, h, g, tm):
    n, d = h.shape
    return pl.pallas_call(
        _combine_kernel,
        grid_spec=pltpu.PrefetchScalarGridSpec(
            num_scalar_prefetch=2,
            grid=(n // tm,),
            in_specs=[pl.BlockSpec(memory_space=pl.ANY),
                      pl.BlockSpec((tm, LANES), lambda i, sa, sb: (i, 0)),
                      pl.BlockSpec((tm, d), lambda i, sa, sb: (i, 0)),
                      pl.BlockSpec((1, d), lambda i, sa, sb: (0, 0))],
            out_specs=pl.BlockSpec((tm, d), lambda i, sa, sb: (i, 0)),
            scratch_shapes=[pltpu.VMEM((2, tm * ROW_TILE, LANES), F32), pltpu.VMEM((2, tm * ROW_TILE, LANES), F32),
                            pltpu.SemaphoreType.DMA((2, 2))]),
        out_shape=jax.ShapeDtypeStruct((n, d), F32),
        compiler_params=_params(("arbitrary",)),
        name="moe_combine",
    )(slot_a, slot_b, y, route, h, g)


def _moe_plan(route, tm):
    n = route.shape[0]
    eids = jnp.concatenate([route[:, 0], route[:, 1]]).astype(jnp.int32)
    onehot = (eids[:, None] == jnp.arange(N_EXPERTS, dtype=jnp.int32)[None, :]).astype(jnp.int32)
    csum = jnp.cumsum(onehot, axis=0)
    rank = jnp.sum(csum * onehot, axis=1) - 1
    counts = csum[-1]
    padded = ((counts + tm - 1) // tm) * tm
    ends = jnp.cumsum(padded)
    starts = ends - padded
    slot = jnp.sum(onehot * starts[None, :], axis=1) + rank
    n_tiles = 2 * n // tm + N_EXPERTS + 1
    tok = jnp.concatenate([jnp.arange(n, dtype=jnp.int32)] * 2)
    tok_of_slot = jnp.zeros((n_tiles * tm + SUBLANES,), jnp.int32).at[slot].set(tok)
    n_used = (ends[-1] // tm).astype(jnp.int32)
    tile_start = jnp.minimum(jnp.arange(n_tiles, dtype=jnp.int32), n_used - 1) * tm
    tile_expert = jnp.sum((tile_start[:, None] >= ends[None, :]).astype(jnp.int32), axis=1)
    return tok_of_slot, slot[:n], slot[n:], tile_expert, n_used.reshape(1)


def _odd_weights(od_w_in, mla_w_uq, mla_w_ukv):
    c2 = 2 * CONV_CH + MLA_Q_RANK + MLA_KV_RANK
    half = MLA_ROPE // 2
    w_kr = od_w_in[:, c2:]
    w_kr_sw = jnp.concatenate([w_kr[:, half:], w_kr[:, :half]], axis=1)
    zl = jnp.zeros((D_MODEL, MLA_NOPE), F32)
    zr = jnp.zeros((D_MODEL, MLA_PAD - MLA_NOPE - MLA_ROPE), F32)
    win = jnp.concatenate([od_w_in[:, :c2], zl, w_kr, zr, zl, w_kr_sw, zr], axis=1)
    dk = MLA_NOPE + MLA_ROPE
    wq = mla_w_uq.reshape(MLA_Q_RANK, MLA_HEADS, dk)
    zq = jnp.zeros((MLA_Q_RANK, MLA_HEADS, MLA_PAD - dk), F32)
    wuq = jnp.concatenate([wq, zq], axis=2).reshape(MLA_Q_RANK, MLA_HEADS * MLA_PAD)
    wq_sw = jnp.concatenate([jnp.zeros_like(wq[:, :, :MLA_NOPE]), wq[:, :, MLA_NOPE + half:],
                             wq[:, :, MLA_NOPE:MLA_NOPE + half], zq], axis=2)
    wuqs = wq_sw.reshape(MLA_Q_RANK, MLA_HEADS * MLA_PAD)
    wkv = mla_w_ukv.reshape(MLA_KV_RANK, MLA_HEADS, MLA_NOPE + MLA_V)
    zk = jnp.zeros((MLA_KV_RANK, MLA_HEADS, MLA_PAD - MLA_NOPE), F32)
    wuk = jnp.concatenate([wkv[:, :, :MLA_NOPE], zk], axis=2).reshape(MLA_KV_RANK, MLA_HEADS * MLA_PAD)
    zv = jnp.zeros((MLA_KV_RANK, MLA_HEADS, MLA_VROWS - MLA_V), F32)
    wuv_t = jnp.concatenate([wkv[:, :, MLA_NOPE:], zv], axis=2).reshape(MLA_KV_RANK, MLA_HEADS * MLA_VROWS).T
    v_one = jnp.zeros((MLA_HEADS, MLA_VROWS), F32).at[:, MLA_V].set(1.0).reshape(MLA_HEADS * MLA_VROWS, 1)
    return win.astype(BF16), wuq.astype(BF16), wuqs.astype(BF16), wuk.astype(BF16), wuv_t.astype(BF16), v_one


def _rope_tables(seq):
    inv = 1.0 / (ROPE_THETA ** (jnp.arange(0, MLA_ROPE, 2, dtype=F32) / MLA_ROPE))
    ang = jnp.arange(seq, dtype=F32)[:, None] * inv[None, :]
    cos, sin = jnp.cos(ang), jnp.sin(ang)
    ones = jnp.ones((seq, MLA_NOPE), F32)
    zl = jnp.zeros((seq, MLA_NOPE), F32)
    zr = jnp.zeros((seq, MLA_PAD - MLA_NOPE - MLA_ROPE), F32)
    return (jnp.concatenate([ones, cos, cos, zr], axis=1), jnp.concatenate([zl, -sin, sin, zr], axis=1))


def kernel(x, norm_g, ev_w_in, ssm_lambda_re, ssm_lambda_im, ssm_log_dt, ssm_b_re, ssm_b_im, ssm_c_re, ssm_c_im, ssm_d, ssm_w_glu, sgu_ln_g, sgu_ln_b, sgu_w, sgu_b, ev_w_out, ffn_w_gate, ffn_w_up, ffn_w_down, od_w_in, conv_w, conv_b, conv_ln_g, conv_ln_b, mla_q_norm_g, mla_w_uq, mla_kv_norm_g, mla_w_ukv, od_w_out, moe_w_router, moe_w_gate, moe_w_up, moe_w_down):
    bsz, seq, d = x.shape
    n = bsz * seq
    assert d == D_MODEL and SUBLANES % bsz == 0 and seq % 512 == 0
    row = lambda v: v.astype(F32).reshape(1, -1)
    h = x.astype(F32).reshape(n, d)
    tm = 512

    g = norm_g[0]
    a_in, proj = _norm_proj(h, row(g[0]), ev_w_in[0].astype(BF16), tm)
    mats = _s5_matrices(ssm_lambda_re[0], ssm_lambda_im[0], ssm_log_dt[0], ssm_b_re[0], ssm_b_im[0],
                        ssm_c_re[0], ssm_c_im[0])
    ys = _s5_mixer(a_in, mats, ssm_d[0], bsz, seq)
    causal = jnp.tril(jnp.ones((SGU_CHUNK, SGU_CHUNK), dtype=bool))
    ws = jnp.where(causal[None], sgu_w[0], 0.0).astype(BF16)
    bias = jnp.repeat(sgu_b[0].astype(F32).T, SGU_HEAD_DIM, axis=1)
    wo = ev_w_out[0].astype(BF16)
    h, z = _even_mix(ys, proj, h, ssm_w_glu[0].astype(BF16), row(sgu_ln_g[0]), row(sgu_ln_b[0]), ws, bias,
                     wo[:SSM_WIDTH], wo[SSM_WIDTH:], row(g[1]), row(g[2]), tm)
    h, z = _dense_ffn(z, ffn_w_gate[0].astype(BF16), ffn_w_up[0].astype(BF16), ffn_w_down[0].astype(BF16),
                      h, row(g[3]), row(norm_g[1][0]), 1024, 512)

    g = norm_g[1]
    win, wuq, wuqs, wuk, wuv_t, v_one = _odd_weights(od_w_in[0], mla_w_uq[0], mla_w_ukv[0])
    cos_t, sin_t = _rope_tables(seq)
    zc, q, k, vt = _odd_proj(z, win, row(mla_q_norm_g[0]), row(mla_kv_norm_g[0]), wuq, wuqs, wuk, wuv_t, v_one,
                             cos_t, sin_t, seq, tm)
    hp = MLA_HEADS * MLA_PAD
    yd = _attention(q.reshape(bsz, seq, hp), k.reshape(bsz, seq, hp), vt, tm)
    conv_w_pad = jnp.concatenate([conv_w[0].astype(F32), jnp.zeros((CONV_HALO - CONV_TAPS, CONV_CH), F32)], axis=0)
    yc = _conv_mixer(zc.reshape(bsz, seq, 2 * CONV_CH), conv_w_pad, row(conv_b[0]), row(conv_ln_g[0]),
                     row(conv_ln_b[0]), tm)
    wo = od_w_out[0].astype(BF16)
    wr = jnp.concatenate([moe_w_router[0].astype(F32), jnp.zeros((d, LANES - N_EXPERTS), F32)], axis=1)
    h, z, route = _odd_mix(yc.reshape(n, CONV_CH), yd.reshape(n, MLA_HEADS * MLA_V), h, wo[:CONV_CH], wo[CONV_CH:],
                           row(g[1]), row(g[2]), wr.astype(BF16), tm)
    tm_moe = 1024
    tok_of_slot, slot_a, slot_b, tile_expert, n_used = _moe_plan(route, tm_moe)
    y = _moe_ffn(tile_expert, n_used, tok_of_slot, z, moe_w_gate[0], moe_w_up[0], moe_w_down[0], tm_moe, 512)
    h = _combine(slot_a, slot_b, y, route, h, row(g[3]), 256)
    return h.reshape(bsz, seq, d).astype(x.dtype)
```

```python
import functools
import math

import jax
import jax.numpy as jnp
from jax import lax
from jax.experimental import pallas as pl
from jax.experimental.pallas import tpu as pltpu

F32 = jnp.float32
BF16 = jnp.bfloat16

D_MODEL = 1024
NORM_EPS = 1e-6
SSM_WIDTH = 512
SSM_GROUP = 16
SSM_GROUPS = 32
SSM_STATE = 64
SSM_CHUNK = 16
SSM_PAIR = 2 * SSM_GROUP * SSM_CHUNK
SGU_WIDTH = 512
SGU_HEADS = 8
SGU_HEAD_DIM = 64
SGU_CHUNK = 128
CONV_CH = 512
CONV_TAPS = 31
CONV_HALO = 32
MLA_HEADS = 8
MLA_Q_RANK = 256
MLA_KV_RANK = 128
MLA_NOPE = 64
MLA_ROPE = 32
MLA_V = 64
MLA_PAD = 128
MLA_VROWS = 80
ROPE_THETA = 10000.0
FF_DENSE = 4096
N_EXPERTS = 8
FF_EXPERT = 3584
LANES = 128
SUBLANES = 8
ROW_TILE = D_MODEL // LANES
VMEM_LIMIT = 56 * 1024 * 1024


def _params(sem, vmem=VMEM_LIMIT):
    return pltpu.CompilerParams(dimension_semantics=sem, vmem_limit_bytes=vmem)


def _rms(x, g):
    return x * lax.rsqrt(jnp.mean(x * x, axis=-1, keepdims=True) + NORM_EPS) * g


def _layer_norm(x, g, b):
    mu = jnp.mean(x, axis=-1, keepdims=True)
    xc = x - mu
    return xc * lax.rsqrt(jnp.mean(xc * xc, axis=-1, keepdims=True) + NORM_EPS) * g + b


def _dot(a, b):
    return jnp.dot(a, b, preferred_element_type=F32)


def _norm_proj_kernel(h_ref, g_ref, w_ref, a_ref, b_ref):
    z = _rms(h_ref[...], g_ref[...])
    proj = _dot(z.astype(BF16), w_ref[...])
    a_ref[...] = proj[:, :SSM_WIDTH]
    b_ref[...] = proj[:, SSM_WIDTH:].astype(b_ref.dtype)


def _norm_proj(h, g, w, tm):
    n, d = h.shape
    cols = w.shape[1]
    return pl.pallas_call(
        _norm_proj_kernel,
        grid=(n // tm,),
        in_specs=[pl.BlockSpec((tm, d), lambda i: (i, 0)),
                  pl.BlockSpec((1, d), lambda i: (0, 0)),
                  pl.BlockSpec((d, cols), lambda i: (0, 0))],
        out_specs=[pl.BlockSpec((tm, SSM_WIDTH), lambda i: (i, 0)),
                   pl.BlockSpec((tm, cols - SSM_WIDTH), lambda i: (i, 0))],
        out_shape=[jax.ShapeDtypeStruct((n, SSM_WIDTH), F32), jax.ShapeDtypeStruct((n, cols - SSM_WIDTH), BF16)],
        compiler_params=_params(("parallel",)),
        name="even_in_proj",
    )(h, g, w)


def _s5_matrices(lam_re, lam_im, log_dt, b_re, b_im, c_re, c_im):
    t = SSM_CHUNK
    lr = jnp.minimum(lam_re.astype(F32), -1e-4)
    li = lam_im.astype(F32)
    dt = jnp.exp(log_dt.astype(F32))[:, None]
    mag = jnp.exp(lr * dt)
    a_re = mag * jnp.cos(li * dt)
    a_im = mag * jnp.sin(li * dt)
    den = lr * lr + li * li
    nr = a_re - 1.0
    coef_re = (nr * lr + a_im * li) / den
    coef_im = (a_im * lr - nr * li) / den
    br = b_re.astype(F32)
    bi = b_im.astype(F32)
    bb_re = coef_re[..., None] * br - coef_im[..., None] * bi
    bb_im = coef_re[..., None] * bi + coef_im[..., None] * br
    cr = c_re.astype(F32)
    ci = c_im.astype(F32)
    pw_re = [jnp.ones_like(a_re)]
    pw_im = [jnp.zeros_like(a_im)]
    for _ in range(t):
        pr, pi = pw_re[-1], pw_im[-1]
        pw_re.append(pr * a_re - pi * a_im)
        pw_im.append(pr * a_im + pi * a_re)
    pw_re = jnp.stack(pw_re)
    pw_im = jnp.stack(pw_im)
    ab_re = pw_re[:t, :, :, None] * bb_re[None] - pw_im[:t, :, :, None] * bb_im[None]
    ab_im = pw_re[:t, :, :, None] * bb_im[None] + pw_im[:t, :, :, None] * bb_re[None]
    hi = lax.Precision.HIGHEST
    k_lag = (jnp.einsum('gnp,tgpm->tgnm', cr, ab_re, precision=hi)
             - jnp.einsum('gnp,tgpm->tgnm', ci, ab_im, precision=hi))
    n_pairs = SSM_GROUPS // 2
    st = 2 * SSM_STATE

    def pair_diag(w):
        w = w.reshape((n_pairs, 2) + w.shape[1:])
        z = jnp.zeros_like(w[:, 0])
        top = jnp.concatenate([w[:, 0], z], axis=-1)
        bot = jnp.concatenate([z, w[:, 1]], axis=-1)
        return jnp.concatenate([top, bot], axis=-2)

    k_blk = pair_diag(k_lag.transpose(1, 0, 3, 2))
    rev_re = pw_re[:t][::-1]
    rev_im = pw_im[:t][::-1]
    ws_re = rev_re[..., None] * bb_re[None] - rev_im[..., None] * bb_im[None]
    ws_im = rev_re[..., None] * bb_im[None] + rev_im[..., None] * bb_re[None]
    ws_re = pair_diag(ws_re.transpose(1, 0, 3, 2)).reshape(n_pairs, SSM_PAIR, st).astype(BF16)
    ws_im = pair_diag(ws_im.transpose(1, 0, 3, 2)).reshape(n_pairs, SSM_PAIR, st).astype(BF16)
    ca_re = cr[None] * pw_re[1:, :, None, :] - ci[None] * pw_im[1:, :, None, :]
    ca_im = cr[None] * pw_im[1:, :, None, :] + ci[None] * pw_re[1:, :, None, :]
    co_re = pair_diag(ca_re.transpose(1, 0, 3, 2))
    co_im = pair_diag((-ca_im).transpose(1, 0, 3, 2))
    w_intra, wo_re, wo_im = _s5_expand(k_blk, co_re, co_im)
    return dict(
        w_intra=w_intra, ws_re=ws_re, ws_im=ws_im, wo_re=wo_re, wo_im=wo_im,
        at_re=pw_re[t].reshape(1, SSM_GROUPS * SSM_STATE), at_im=pw_im[t].reshape(1, SSM_GROUPS * SSM_STATE))


def _s5_expand_kernel(k_ref, cre_ref, cim_ref, wi_ref, wore_ref, woim_ref, kcat_ref):
    pw = 2 * SSM_GROUP
    for tau in range(SSM_CHUNK):
        kcat_ref[:, tau * pw:(tau + 1) * pw] = k_ref[0, tau]
        wore_ref[0, :, tau * pw:(tau + 1) * pw] = cre_ref[0, tau].astype(wore_ref.dtype)
        woim_ref[0, :, tau * pw:(tau + 1) * pw] = cim_ref[0, tau].astype(woim_ref.dtype)
    kcat = kcat_ref[...]
    col = lax.broadcasted_iota(jnp.int32, kcat.shape, 1)
    for s in range(SSM_CHUNK):
        blk = kcat if s == 0 else jnp.where(col >= s * pw, pltpu.roll(kcat, s * pw, 1), 0.0)
        wi_ref[0, s * pw:(s + 1) * pw, :] = blk.astype(wi_ref.dtype)


def _s5_expand(k_blk, co_re, co_im):
    n_pairs = k_blk.shape[0]
    pw = 2 * SSM_GROUP
    st = 2 * SSM_STATE
    return pl.pallas_call(
        _s5_expand_kernel,
        grid=(n_pairs,),
        in_specs=[pl.BlockSpec((1, SSM_CHUNK, pw, pw), lambda q: (q, 0, 0, 0)),
                  pl.BlockSpec((1, SSM_CHUNK, st, pw), lambda q: (q, 0, 0, 0)),
                  pl.BlockSpec((1, SSM_CHUNK, st, pw), lambda q: (q, 0, 0, 0))],
        out_specs=[pl.BlockSpec((1, SSM_PAIR, SSM_PAIR), lambda q: (q, 0, 0)),
                   pl.BlockSpec((1, st, SSM_PAIR), lambda q: (q, 0, 0)),
                   pl.BlockSpec((1, st, SSM_PAIR), lambda q: (q, 0, 0))],
        out_shape=[jax.ShapeDtypeStruct((n_pairs, SSM_PAIR, SSM_PAIR), BF16),
                   jax.ShapeDtypeStruct((n_pairs, st, SSM_PAIR), BF16),
                   jax.ShapeDtypeStruct((n_pairs, st, SSM_PAIR), BF16)],
        scratch_shapes=[pltpu.VMEM((pw, SSM_PAIR), F32)],
        compiler_params=_params(("parallel",)),
        name="s5_expand_weights",
    )(k_blk, co_re, co_im)


S5_LANE_PAIRS = LANES // (2 * SSM_GROUP)
S5_SCAN_LANES = 512


def _s5_state_kernel(u0_ref, u1_ref, u2_ref, u3_ref, wre_ref, wim_ref, are_ref, aim_ref,
                     x_ref, hre_ref, him_ref, sre_ref, sim_ref):
    n_chunks = x_ref.shape[0]
    pw = 2 * SSM_GROUP
    u_refs = (u0_ref, u1_ref, u2_ref, u3_ref)
    for t in range(SSM_CHUNK):
        for j, u_ref in enumerate(u_refs):
            ut = u_ref[pl.ds(t, n_chunks, stride=SSM_CHUNK), :]
            for qq in range(S5_LANE_PAIRS):
                q = j * S5_LANE_PAIRS + qq
                x_ref[:, q * SSM_PAIR + t * pw: q * SSM_PAIR + (t + 1) * pw] = (
                    ut[:, qq * pw:(qq + 1) * pw].astype(x_ref.dtype))
    st = 2 * SSM_STATE
    for q in range(SSM_GROUPS // 2):
        xq = x_ref[:, q * SSM_PAIR:(q + 1) * SSM_PAIR]
        sre_ref[:, q * st:(q + 1) * st] = _dot(xq, wre_ref[q])
        sim_ref[:, q * st:(q + 1) * st] = _dot(xq, wim_ref[q])

    row = lax.broadcasted_iota(jnp.int32, (SUBLANES, S5_SCAN_LANES), 0)
    zero = jnp.zeros((SUBLANES, S5_SCAN_LANES), F32)
    for c0 in range(0, sre_ref.shape[1], S5_SCAN_LANES):
        cols = pl.ds(c0, S5_SCAN_LANES)
        ar = are_ref[:, cols]
        ai = aim_ref[:, cols]

        def body(k, carry, cols=cols, ar=ar, ai=ai):
            r0 = pl.multiple_of(k * SUBLANES, SUBLANES)
            sr = sre_ref[pl.ds(r0, SUBLANES), cols]
            si = sim_ref[pl.ds(r0, SUBLANES), cols]
            out_r, out_i = carry
            for i in range(1, SUBLANES + 1):
                tr = ar * out_r - ai * out_i + sr
                ti = ar * out_i + ai * out_r + si
                tr = pltpu.roll(tr, 1, 0)
                ti = pltpu.roll(ti, 1, 0)
                if i < SUBLANES:
                    out_r = jnp.where(row == i, tr, out_r)
                    out_i = jnp.where(row == i, ti, out_i)
            hre_ref[pl.ds(r0, SUBLANES), cols] = out_r
            him_ref[pl.ds(r0, SUBLANES), cols] = out_i
            return tr, ti

        lax.fori_loop(0, n_chunks // SUBLANES, body, (zero, zero))


def _s5_out_kernel(x_ref, wi_ref, hre_ref, him_ref, wore_ref, woim_ref, d_ref, y_ref, yt_ref):
    n_chunks = x_ref.shape[0]
    pw = 2 * SSM_GROUP
    st = 2 * SSM_STATE
    for qq in range(S5_LANE_PAIRS):
        x = x_ref[:, qq * SSM_PAIR:(qq + 1) * SSM_PAIR]
        y = _dot(x, wi_ref[qq])
        y += _dot(hre_ref[:, qq * st:(qq + 1) * st].astype(BF16), wore_ref[qq])
        y += _dot(him_ref[:, qq * st:(qq + 1) * st].astype(BF16), woim_ref[qq])
        y += d_ref[:, qq * SSM_PAIR:(qq + 1) * SSM_PAIR] * x.astype(F32)
        y = jax.nn.gelu(y)
        for t in range(SSM_CHUNK):
            yt_ref[t, :, qq * pw:(qq + 1) * pw] = y[:, t * pw:(t + 1) * pw]
    for t in range(SSM_CHUNK):
        y_ref[pl.ds(t, n_chunks, stride=SSM_CHUNK), :] = yt_ref[t]


def _s5_mixer(u, mats, d, batch, seq):
    t = SSM_CHUNK
    n_chunks = seq // t
    n_pairs = SSM_GROUPS // 2
    cols = n_pairs * SSM_PAIR
    st = 2 * SSM_STATE
    n_state = n_pairs * st
    n_blk = SSM_WIDTH // LANES
    assert n_blk == 4 and n_chunks % SUBLANES == 0
    once = pl.Buffered(1)
    x, h_re, h_im = pl.pallas_call(
        _s5_state_kernel,
        grid=(batch,),
        in_specs=[pl.BlockSpec((seq, LANES), lambda b, j=j: (b, j)) for j in range(n_blk)] + [
            pl.BlockSpec((n_pairs, SSM_PAIR, st), lambda b: (0, 0, 0), pipeline_mode=once),
            pl.BlockSpec((n_pairs, SSM_PAIR, st), lambda b: (0, 0, 0), pipeline_mode=once),
            pl.BlockSpec((1, n_state), lambda b: (0, 0)),
            pl.BlockSpec((1, n_state), lambda b: (0, 0))],
        out_specs=[pl.BlockSpec((n_chunks, cols), lambda b: (b, 0)),
                   pl.BlockSpec((n_chunks, n_state), lambda b: (b, 0)),
                   pl.BlockSpec((n_chunks, n_state), lambda b: (b, 0))],
        out_shape=[jax.ShapeDtypeStruct((batch * n_chunks, cols), BF16),
                   jax.ShapeDtypeStruct((batch * n_chunks, n_state), F32),
                   jax.ShapeDtypeStruct((batch * n_chunks, n_state), F32)],
        scratch_shapes=[pltpu.VMEM((n_chunks, n_state), F32), pltpu.VMEM((n_chunks, n_state), F32)],
        compiler_params=_params(("parallel",)),
        name="s5_state_scan",
    )(u, u, u, u, mats['ws_re'], mats['ws_im'], mats['at_re'], mats['at_im'])
    lp = S5_LANE_PAIRS
    d_cols = jnp.broadcast_to(d.astype(F32).reshape(n_pairs, 1, 2 * SSM_GROUP),
                              (n_pairs, t, 2 * SSM_GROUP)).reshape(1, cols)
    return pl.pallas_call(
        _s5_out_kernel,
        grid=(batch, n_blk),
        in_specs=[pl.BlockSpec((n_chunks, lp * SSM_PAIR), lambda b, j: (b, j)),
                  pl.BlockSpec((lp, SSM_PAIR, SSM_PAIR), lambda b, j: (j, 0, 0)),
                  pl.BlockSpec((n_chunks, lp * st), lambda b, j: (b, j)),
                  pl.BlockSpec((n_chunks, lp * st), lambda b, j: (b, j)),
                  pl.BlockSpec((lp, st, SSM_PAIR), lambda b, j: (j, 0, 0)),
                  pl.BlockSpec((lp, st, SSM_PAIR), lambda b, j: (j, 0, 0)),
                  pl.BlockSpec((1, lp * SSM_PAIR), lambda b, j: (0, j))],
        out_specs=pl.BlockSpec((seq, LANES), lambda b, j: (b, j)),
        out_shape=jax.ShapeDtypeStruct((batch * seq, SSM_WIDTH), F32),
        scratch_shapes=[pltpu.VMEM((t, n_chunks, LANES), F32)],
        compiler_params=_params(("parallel", "parallel")),
        name="s5_out",
    )(x, mats['w_intra'], h_re, h_im, mats['wo_re'], mats['wo_im'], d_cols)


def _even_mix_kernel(ys_ref, bu_ref, bv_ref, h_ref, wglu_ref, lng_ref, lnb_ref, ws_ref, bias_ref,
                     wo_a_ref, wo_b_ref, g1_ref, g2_ref, hout_ref, z_ref, s_scr):
    tm = ys_ref.shape[0]
    ys = ys_ref[...]
    ya = ys * jax.nn.sigmoid(_dot(ys.astype(BF16), wglu_ref[...]))
    u = jax.nn.gelu(bu_ref[...].astype(F32))
    v = _layer_norm(jax.nn.gelu(bv_ref[...].astype(F32)), lng_ref[...], lnb_ref[...])
    lane = lax.broadcasted_iota(jnp.int32, v.shape, 1)
    left = (lane % LANES) < SGU_HEAD_DIM
    v_l = jnp.where(left, v, 0.0).astype(BF16)
    v_r = jnp.where(left, 0.0, v).astype(BF16)
    for c in range(tm // SGU_CHUNK):
        rows = slice(c * SGU_CHUNK, (c + 1) * SGU_CHUNK)
        for p in range(SGU_HEADS // 2):
            cols = slice(p * LANES, (p + 1) * LANES)
            s_scr[rows, cols] = (_dot(ws_ref[2 * p], v_l[rows, cols]) + _dot(ws_ref[2 * p + 1], v_r[rows, cols]))
    bias = jnp.concatenate([bias_ref[...]] * (tm // SGU_CHUNK), axis=0)
    yb = u * (s_scr[...] + bias)
    mix = _dot(ya.astype(BF16), wo_a_ref[...]) + _dot(yb.astype(BF16), wo_b_ref[...])
    h_new = h_ref[...] + _rms(mix, g1_ref[...])
    hout_ref[...] = h_new
    z_ref[...] = _rms(h_new, g2_ref[...]).astype(z_ref.dtype)


def _even_mix(ys, proj, h, wglu, lng, lnb, ws, bias, wo_a, wo_b, g1, g2, tm):
    n, d = h.shape
    w = SGU_WIDTH
    const = lambda *shape: pl.BlockSpec(shape, lambda i: (0,) * len(shape))
    return pl.pallas_call(
        _even_mix_kernel,
        grid=(n // tm,),
        in_specs=[pl.BlockSpec((tm, w), lambda i: (i, 0)),
                  pl.BlockSpec((tm, w), lambda i: (i, 0)),
                  pl.BlockSpec((tm, w), lambda i: (i, 1)),
                  pl.BlockSpec((tm, d), lambda i: (i, 0)),
                  const(w, w), const(1, w), const(1, w),
                  const(SGU_HEADS, SGU_CHUNK, SGU_CHUNK), const(SGU_CHUNK, w),
                  const(w, d), const(w, d), const(1, d), const(1, d)],
        out_specs=[pl.BlockSpec((tm, d), lambda i: (i, 0)),
                   pl.BlockSpec((tm, d), lambda i: (i, 0))],
        out_shape=[jax.ShapeDtypeStruct((n, d), F32), jax.ShapeDtypeStruct((n, d), BF16)],
        scratch_shapes=[pltpu.VMEM((tm, w), F32)],
        compiler_params=_params(("parallel",)),
        name="even_mix",
    )(ys, proj, proj, h, wglu, lng, lnb, ws, bias, wo_a, wo_b, g1, g2)


def _ffn_kernel(z_ref, wg_ref, wu_ref, wd_ref, h_ref, g3_ref, gn_ref, hout_ref, zout_ref, acc_ref):
    j = pl.program_id(1)

    @pl.when(j == 0)
    def _():
        acc_ref[...] = jnp.zeros_like(acc_ref)

    z = z_ref[...]
    a = jax.nn.silu(_dot(z, wg_ref[...])) * _dot(z, wu_ref[...])
    acc_ref[...] += _dot(a.astype(BF16), wd_ref[...])

    @pl.when(j == pl.num_programs(1) - 1)
    def _():
        h_new = h_ref[...] + _rms(acc_ref[...], g3_ref[...])
        hout_ref[...] = h_new
        zout_ref[...] = _rms(h_new, gn_ref[...]).astype(zout_ref.dtype)


def _dense_ffn(z, wg, wu, wd, h, g3, g_next, tm, tf):
    n, d = h.shape
    ff = wg.shape[1]
    return pl.pallas_call(
        _ffn_kernel,
        grid=(n // tm, ff // tf),
        in_specs=[pl.BlockSpec((tm, d), lambda i, j: (i, 0)),
                  pl.BlockSpec((d, tf), lambda i, j: (0, j)),
                  pl.BlockSpec((d, tf), lambda i, j: (0, j)),
                  pl.BlockSpec((tf, d), lambda i, j: (j, 0)),
                  pl.BlockSpec((tm, d), lambda i, j: (i, 0)),
                  pl.BlockSpec((1, d), lambda i, j: (0, 0)),
                  pl.BlockSpec((1, d), lambda i, j: (0, 0))],
        out_specs=[pl.BlockSpec((tm, d), lambda i, j: (i, 0)),
                   pl.BlockSpec((tm, d), lambda i, j: (i, 0))],
        out_shape=[jax.ShapeDtypeStruct((n, d), F32), jax.ShapeDtypeStruct((n, d), BF16)],
        scratch_shapes=[pltpu.VMEM((tm, d), F32)],
        compiler_params=_params(("parallel", "arbitrary")),
        name="dense_ffn",
    )(z, wg, wu, wd, h, g3, g_next)


def _odd_proj_kernel(z_ref, win_ref, gq_ref, gkv_ref, wuq_ref, wuqs_ref, wuk_ref, wuv_ref, vone_ref, cos_ref, sin_ref,
                     zc_ref, q_ref, k_ref, v_ref, *, scale):
    z = z_ref[...]
    proj = _dot(z, win_ref[...])
    c0 = 2 * CONV_CH
    c1 = c0 + MLA_Q_RANK
    c2 = c1 + MLA_KV_RANK
    c3 = c2 + MLA_PAD
    zc_ref[...] = proj[:, :c0].astype(zc_ref.dtype)
    cq = _rms(proj[:, c0:c1], gq_ref[...]).astype(BF16)
    ckv = _rms(proj[:, c1:c2], gkv_ref[...]).astype(BF16)
    cos = cos_ref[...]
    sin = sin_ref[...]
    cos_h = jnp.concatenate([cos] * MLA_HEADS, axis=1)
    sin_h = jnp.concatenate([sin] * MLA_HEADS, axis=1)
    q = _dot(cq, wuq_ref[...]) * cos_h + _dot(cq, wuqs_ref[...]) * sin_h
    q_ref[...] = (q * scale).astype(q_ref.dtype)
    kr = proj[:, c2:c3] * cos + proj[:, c3:] * sin
    k = _dot(ckv, wuk_ref[...]) + jnp.concatenate([kr] * MLA_HEADS, axis=1)
    k_ref[...] = k.astype(k_ref.dtype)
    vt = lax.dot_general(wuv_ref[...], ckv, (((1,), (1,)), ((), ())), preferred_element_type=F32)
    v_ref[0] = (vt + vone_ref[...]).astype(v_ref.dtype)


def _odd_proj(z, win, gq, gkv, wuq, wuqs, wuk, wuv_t, v_one, cos_t, sin_t, seq, tm):
    n, d = z.shape
    hp = MLA_HEADS * MLA_PAD
    vr = MLA_HEADS * MLA_VROWS
    n_l = seq // tm
    const = lambda *shape: pl.BlockSpec(shape, lambda i: (0,) * len(shape))
    out = jax.ShapeDtypeStruct((n, hp), BF16)
    scale = float((MLA_NOPE + MLA_ROPE) ** -0.5 * math.log2(math.e))
    return pl.pallas_call(
        functools.partial(_odd_proj_kernel, scale=scale),
        grid=(n // tm,),
        in_specs=[pl.BlockSpec((tm, d), lambda i: (i, 0)),
                  const(d, win.shape[1]), const(1, MLA_Q_RANK), const(1, MLA_KV_RANK),
                  const(MLA_Q_RANK, hp), const(MLA_Q_RANK, hp), const(MLA_KV_RANK, hp), const(vr, MLA_KV_RANK),
                  const(vr, 1),
                  pl.BlockSpec((tm, MLA_PAD), lambda i: (i % n_l, 0)),
                  pl.BlockSpec((tm, MLA_PAD), lambda i: (i % n_l, 0))],
        out_specs=[pl.BlockSpec((tm, 2 * CONV_CH), lambda i: (i, 0)),
                   pl.BlockSpec((tm, hp), lambda i: (i, 0)),
                   pl.BlockSpec((tm, hp), lambda i: (i, 0)),
                   pl.BlockSpec((1, vr, tm), lambda i: (i, 0, 0))],
        out_shape=[jax.ShapeDtypeStruct((n, 2 * CONV_CH), BF16), out, out,
                   jax.ShapeDtypeStruct((n // tm, vr, tm), BF16)],
        compiler_params=_params(("parallel",)),
        name="odd_in_proj",
    )(z, win, gq, gkv, wuq, wuqs, wuk, wuv_t, v_one, cos_t, sin_t)


def _attn_kernel(q_ref, k_ref, vt_ref, wg_ref, wu_ref, wd_ref, o_ref, wgb_ref, wub_ref, wdb_ref, acc_ref, *, blk):
    i = pl.program_id(2)
    acc_ref[...] = jnp.zeros_like(acc_ref)
    tf = wgb_ref.shape[3]
    for f in range(wgb_ref.shape[1]):
        wgb_ref[0, f] = wg_ref[0, :, f * tf:(f + 1) * tf].astype(BF16)
        wub_ref[0, f] = wu_ref[0, :, f * tf:(f + 1) * tf].astype(BF16)
    wdb_ref[0] = wd_ref[0].astype(BF16)

    def step(j, m, masked):
        r0 = pl.multiple_of(j * blk, blk)
        scores = []
        for hh in range(2):
            q = q_ref[0, :, hh * MLA_PAD:(hh + 1) * MLA_PAD]
            k = k_ref[0, pl.ds(r0, blk), hh * MLA_PAD:(hh + 1) * MLA_PAD]
            st = lax.dot_general(k, q, (((1,), (1,)), ((), ())), preferred_element_type=F32)
            if masked:
                key = lax.broadcasted_iota(jnp.int32, st.shape, 0)
                qry = lax.broadcasted_iota(jnp.int32, st.shape, 1)
                st = jnp.where(key <= qry, st, -1e30)
            scores.append(st)
        soft = []
        for hh in range(2):
            m_new = jnp.maximum(m[hh], jnp.max(scores[hh], axis=0, keepdims=True))
            soft.append((m_new, jnp.exp2(m[hh] - m_new), jnp.exp2(scores[hh] - m_new).astype(BF16)))
        for hh in range(2):
            vt = vt_ref[j, hh * MLA_VROWS:(hh + 1) * MLA_VROWS, :]
            acc_ref[hh] = soft[hh][1] * acc_ref[hh] + _dot(vt, soft[hh][2])
        return (soft[0][0], soft[1][0])

    init = jnp.full((1, blk), -1e30, F32)
    m = lax.fori_loop(0, i, lambda j, m: step(j, m, False), (init, init))
    step(i, m, True)
    a0 = acc_ref[0]
    a1 = acc_ref[1]
    ot = jnp.concatenate([a0[:MLA_V] / a0[MLA_V:MLA_V + 1], a1[:MLA_V] / a1[MLA_V:MLA_V + 1]], axis=0)
    o_ref[0] = ot.T.astype(o_ref.dtype)


def _attention(q, k, vt, blk, wg, wu, wd, tf):
    b, seq, _ = q.shape
    n_blk = seq // blk
    n_pairs = MLA_HEADS // 2
    n_e, d, ff = wg.shape
    steps = b * n_pairs * n_blk
    per_e = steps // n_e
    assert steps == per_e * n_e and d % per_e == 0 and ff % per_e == 0
    rows_in, rows_down = d // per_e, ff // per_e
    assert rows_in % 16 == 0 and rows_down % 16 == 0 and ff % tf == 0

    def lin(bi, p, i):
        return (bi * n_pairs + p) * n_blk + i

    w_in = pl.BlockSpec((1, rows_in, ff), lambda bi, p, i: (lin(bi, p, i) // per_e, lin(bi, p, i) % per_e, 0))
    w_out = pl.BlockSpec((1, ff // tf, rows_in, tf),
                         lambda bi, p, i: (lin(bi, p, i) // per_e, 0, lin(bi, p, i) % per_e, 0))
    w_down = pl.BlockSpec((1, rows_down, d), lambda bi, p, i: (lin(bi, p, i) // per_e, lin(bi, p, i) % per_e, 0))
    return pl.pallas_call(
        functools.partial(_attn_kernel, blk=blk),
        grid=(b, n_pairs, n_blk),
        in_specs=[pl.BlockSpec((1, blk, 2 * MLA_PAD), lambda bi, p, i: (bi, i, p)),
                  pl.BlockSpec((1, seq, 2 * MLA_PAD), lambda bi, p, i: (bi, 0, p)),
                  pl.BlockSpec((n_blk, 2 * MLA_VROWS, blk), lambda bi, p, i: (bi, p, 0)),
                  w_in, w_in, w_down],
        out_specs=[pl.BlockSpec((1, blk, 2 * MLA_V), lambda bi, p, i: (bi, i, p)), w_out, w_out, w_down],
        out_shape=[jax.ShapeDtypeStruct((b, seq, MLA_HEADS * MLA_V), BF16),
                   jax.ShapeDtypeStruct((n_e, ff // tf, d, tf), BF16),
                   jax.ShapeDtypeStruct((n_e, ff // tf, d, tf), BF16),
                   jax.ShapeDtypeStruct((n_e, ff, d), BF16)],
        scratch_shapes=[pltpu.VMEM((2, MLA_VROWS, blk), F32)],
        compiler_params=_params(("parallel", "parallel", "parallel")),
        name="mla_attention",
    )(q, k, vt, wg, wu, wd)


def _conv_kernel(zc_ref, w_ref, b_ref, lng_ref, lnb_ref, y_ref, buf_ref, part_ref):
    tm = zc_ref.shape[1]

    @pl.when(pl.program_id(1) == 0)
    def _():
        buf_ref[pl.ds(0, CONV_HALO), :] = jnp.zeros((CONV_HALO, CONV_CH), F32)
        buf_ref[pl.ds(CONV_HALO + tm, SUBLANES), :] = jnp.zeros((SUBLANES, CONV_CH), F32)

    zc = zc_ref[0].astype(F32)
    hh = zc[:, :CONV_CH] * jax.nn.sigmoid(zc[:, CONV_CH:])
    buf_ref[pl.ds(CONV_HALO, tm), :] = hh
    off = CONV_HALO - (CONV_TAPS - 1)
    acc = jnp.zeros((tm, CONV_CH), F32) + b_ref[...]
    for b in range(SUBLANES):
        taps = [k for k in range(CONV_TAPS) if (off + k) % SUBLANES == b]
        part = None
        for k in taps:
            term = w_ref[pl.ds(k, 1), :] * buf_ref[pl.ds(off + k - b, tm + SUBLANES), :]
            part = term if part is None else part + term
        if b == 0:
            acc = acc + part[:tm]
        else:
            part_ref[...] = part
            acc = acc + part_ref[pl.ds(b, tm), :]
    buf_ref[pl.ds(0, CONV_HALO), :] = buf_ref[pl.ds(tm, CONV_HALO), :]
    y_ref[0] = jax.nn.silu(_layer_norm(acc, lng_ref[...], lnb_ref[...])).astype(y_ref.dtype)


def _conv_mixer(zc, w, b, lng, lnb, tm):
    bsz, seq, _ = zc.shape
    const = lambda *shape: pl.BlockSpec(shape, lambda bi, i: (0,) * len(shape))
    return pl.pallas_call(
        _conv_kernel,
        grid=(bsz, seq // tm),
        in_specs=[pl.BlockSpec((1, tm, 2 * CONV_CH), lambda bi, i: (bi, i, 0)),
                  const(CONV_HALO, CONV_CH), const(1, CONV_CH), const(1, CONV_CH), const(1, CONV_CH)],
        out_specs=pl.BlockSpec((1, tm, CONV_CH), lambda bi, i: (bi, i, 0)),
        out_shape=jax.ShapeDtypeStruct((bsz, seq, CONV_CH), BF16),
        scratch_shapes=[pltpu.VMEM((CONV_HALO + tm + SUBLANES, CONV_CH), F32),
                        pltpu.VMEM((tm + SUBLANES, CONV_CH), F32)],
        compiler_params=_params(("arbitrary", "arbitrary")),
        name="conv_module",
    )(zc, w, b, lng, lnb)


def _odd_mix_kernel(yc_ref, yd_ref, h_ref, wo_a_ref, wo_b_ref, g1_ref, g2_ref, wr_ref, hout_ref, z_ref, route_ref):
    mix = _dot(yc_ref[...], wo_a_ref[...]) + _dot(yd_ref[...], wo_b_ref[...])
    h_new = h_ref[...] + _rms(mix, g1_ref[...])
    hout_ref[...] = h_new
    z = _rms(h_new, g2_ref[...])
    _store_row_tiles(z_ref, z)
    logits = _dot(z.astype(BF16), wr_ref[...])
    lane = lax.broadcasted_iota(jnp.int32, logits.shape, 1)
    neg = -jnp.inf
    logits = jnp.where(lane < N_EXPERTS, logits, neg)
    m1 = jnp.max(logits, axis=-1, keepdims=True)
    i1 = jnp.min(jnp.where(logits == m1, lane, LANES), axis=-1, keepdims=True)
    rest = jnp.where(lane == i1, neg, logits)
    m2 = jnp.max(rest, axis=-1, keepdims=True)
    i2 = jnp.min(jnp.where(rest == m2, lane, LANES), axis=-1, keepdims=True)
    e = jnp.exp(m2 - m1)
    w1 = 1.0 / (1.0 + e)
    w2 = e / (1.0 + e)
    route = jnp.where(lane == 0, i1.astype(F32),
                      jnp.where(lane == 1, i2.astype(F32),
                                jnp.where(lane == 2, w1, jnp.where(lane == 3, w2, 0.0))))
    route_ref[...] = route


def _odd_mix(yc, yd, h, wo_a, wo_b, g1, g2, wr, tm):
    n, d = h.shape
    const = lambda *shape: pl.BlockSpec(shape, lambda i: (0,) * len(shape))
    return pl.pallas_call(
        _odd_mix_kernel,
        grid=(n // tm,),
        in_specs=[pl.BlockSpec((tm, yc.shape[1]), lambda i: (i, 0)),
                  pl.BlockSpec((tm, yd.shape[1]), lambda i: (i, 0)),
                  pl.BlockSpec((tm, d), lambda i: (i, 0)),
                  const(*wo_a.shape), const(*wo_b.shape), const(1, d), const(1, d), const(d, LANES)],
        out_specs=[pl.BlockSpec((tm, d), lambda i: (i, 0)),
                   pl.BlockSpec((tm * ROW_TILE, LANES), lambda i: (i, 0)),
                   pl.BlockSpec((tm, LANES), lambda i: (i, 0))],
        out_shape=[jax.ShapeDtypeStruct((n, d), F32), jax.ShapeDtypeStruct((n * ROW_TILE, LANES), F32),
                   jax.ShapeDtypeStruct((n, LANES), F32)],
        compiler_params=_params(("parallel",)),
        name="odd_mix_router",
    )(yc, yd, h, wo_a, wo_b, g1, g2, wr)


def _store_row_tiles(ref, x):
    rows = x.shape[0]
    for s in range(ROW_TILE):
        ref[pl.ds(s, rows, stride=ROW_TILE), :] = x[:, s * LANES:(s + 1) * LANES]


def _load_row_tiles(ref, rows):
    return [ref[pl.ds(s, rows, stride=ROW_TILE), :] for s in range(ROW_TILE)]


def _gather_rows(idx_ref, base, n_rows, src_hbm, dst_ref, sem):
    def body(r, c):
        src = pl.multiple_of(idx_ref[base + r] * ROW_TILE, ROW_TILE)
        dst = pl.multiple_of(r * ROW_TILE, ROW_TILE)
        pltpu.make_async_copy(src_hbm.at[pl.ds(src, ROW_TILE), :], dst_ref.at[pl.ds(dst, ROW_TILE), :], sem).start()
        return c

    lax.fori_loop(0, n_rows, body, 0, unroll=8)


def _wait_rows(src_hbm, dst_ref, sem):
    pltpu.make_async_copy(src_hbm.at[pl.ds(0, dst_ref.shape[0]), :], dst_ref, sem).wait()


def _row_copy(src_ref, src_row, dst_ref, dst_row, sem):
    src = pl.multiple_of(src_row * ROW_TILE, ROW_TILE)
    dst = pl.multiple_of(dst_row * ROW_TILE, ROW_TILE)
    return pltpu.make_async_copy(src_ref.at[pl.ds(src, ROW_TILE), :], dst_ref.at[pl.ds(dst, ROW_TILE), :], sem)


def _moe_ffn_kernel(te_ref, nu_ref, tok_ref, dst_ref, z_hbm, wg_ref, wu_ref, wd_ref, y_hbm,
                    xraw_ref, xb_ref, acc_ref, yst_ref, gsem, ssem, *, rows_per_step):
    i = pl.program_id(0)
    j = pl.program_id(1)
    tm = xb_ref.shape[0]
    stride = yst_ref.shape[0] // ROW_TILE
    n_used = nu_ref[0]
    slot = i % 2
    first = j == 0
    last = j == pl.num_programs(1) - 1

    @pl.when(first & (i == 0))
    def _():
        yst_ref[...] = jnp.zeros_like(yst_ref)
        _gather_rows(tok_ref, 0, stride, z_hbm, xraw_ref.at[0], gsem.at[0])

    @pl.when(first & (i <= n_used))
    def _():
        _wait_rows(z_hbm, xraw_ref.at[slot], gsem.at[slot])

    @pl.when(first & (i < n_used))
    def _():
        for s, blk in enumerate(_load_row_tiles(xraw_ref.at[slot], tm)):
            xb_ref[:, s * LANES:(s + 1) * LANES] = blk.astype(BF16)
        acc_ref[...] = jnp.zeros_like(acc_ref)

    @pl.when(first & (i == n_used))
    def _():
        def body(r, c):
            _row_copy(yst_ref, r, y_hbm, dst_ref[i * stride + r], ssem).start()
            return c
        lax.fori_loop(0, stride, body, 0, unroll=8)

    @pl.when(i < n_used)
    def _():
        x = xb_ref[...]
        g = _dot(x, wg_ref[0, 0])
        nxt = xraw_ref.at[1 - slot]
        for rr in range(rows_per_step):
            r = j * rows_per_step + rr
            _row_copy(z_hbm, tok_ref[(i + 1) * stride + r], nxt, r, gsem.at[1 - slot]).start()
        u = _dot(x, wu_ref[0, 0])
        for rr in range(rows_per_step):
            r = j * rows_per_step + rr
            _row_copy(yst_ref, r, y_hbm, dst_ref[i * stride + r], ssem).start()
        a = jax.nn.silu(g) * u
        acc_ref[...] += _dot(a.astype(BF16), wd_ref[0])

    @pl.when(last & (i <= n_used))
    def _():
        _wait_rows(z_hbm, yst_ref, ssem)

    @pl.when(last & (i < n_used))
    def _():
        _store_row_tiles(yst_ref, acc_ref[...])


def _moe_ffn(tile_expert, n_used, tok_tab, dst_tab, z_tiles, wg, wu, wd, n_tok, tm):
    n_f, d, tf = wg.shape[1], wg.shape[2], wg.shape[3]
    rows_per_step = -(-tm // n_f)
    stride = n_f * rows_per_step
    n_tiles = tile_expert.shape[0]
    assert tok_tab.shape[0] == dst_tab.shape[0] == (n_tiles + 1) * stride

    def col(i, j, nu):
        return jnp.where(i < nu[0], j, n_f - 1)

    return pl.pallas_call(
        functools.partial(_moe_ffn_kernel, rows_per_step=rows_per_step),
        grid_spec=pltpu.PrefetchScalarGridSpec(
            num_scalar_prefetch=4,
            grid=(n_tiles, n_f),
            in_specs=[pl.BlockSpec(memory_space=pl.ANY),
                      pl.BlockSpec((1, 1, d, tf), lambda i, j, te, nu, tok, dst: (te[i], col(i, j, nu), 0, 0)),
                      pl.BlockSpec((1, 1, d, tf), lambda i, j, te, nu, tok, dst: (te[i], col(i, j, nu), 0, 0)),
                      pl.BlockSpec((1, tf, d), lambda i, j, te, nu, tok, dst: (te[i], col(i, j, nu), 0))],
            out_specs=pl.BlockSpec(memory_space=pl.ANY),
            scratch_shapes=[pltpu.VMEM((2, stride * ROW_TILE, LANES), F32), pltpu.VMEM((tm, d), BF16),
                            pltpu.VMEM((tm, d), F32), pltpu.VMEM((stride * ROW_TILE, LANES), F32),
                            pltpu.SemaphoreType.DMA((2,)), pltpu.SemaphoreType.DMA(())]),
        out_shape=jax.ShapeDtypeStruct(((2 * n_tok + stride) * ROW_TILE, LANES), F32),
        compiler_params=_params(("arbitrary", "arbitrary")),
        name="moe_grouped_ffn",
    )(tile_expert, n_used, tok_tab, dst_tab, z_tiles, wg, wu, wd)


def _combine_kernel(ya_ref, yb_ref, route_ref, h_ref, g_ref, o_ref):
    tm = h_ref.shape[0]
    route = route_ref[...]
    a = jnp.concatenate(_load_row_tiles(ya_ref, tm), axis=1)
    b = jnp.concatenate(_load_row_tiles(yb_ref, tm), axis=1)
    f = route[:, 2:3] * a + route[:, 3:4] * b
    o_ref[...] = h_ref[...] + _rms(f, g_ref[...])


def _combine(y, route, h, g, tm):
    n, d = h.shape
    n_blk = n // tm
    return pl.pallas_call(
        _combine_kernel,
        grid=(n_blk,),
        in_specs=[pl.BlockSpec((tm * ROW_TILE, LANES), lambda i: (i, 0)),
                  pl.BlockSpec((tm * ROW_TILE, LANES), lambda i: (n_blk + i, 0)),
                  pl.BlockSpec((tm, LANES), lambda i: (i, 0)),
                  pl.BlockSpec((tm, d), lambda i: (i, 0)),
                  pl.BlockSpec((1, d), lambda i: (0, 0))],
        out_specs=pl.BlockSpec((tm, d), lambda i: (i, 0)),
        out_shape=jax.ShapeDtypeStruct((n, d), F32),
        compiler_params=_params(("parallel",)),
        name="moe_combine",
    )(y, y, route, h, g)


def _moe_plan(route, tm, stride):
    n = route.shape[0]
    eids = jnp.concatenate([route[:, 0], route[:, 1]]).astype(jnp.int32)
    onehot = (eids[:, None] == jnp.arange(N_EXPERTS, dtype=jnp.int32)[None, :]).astype(jnp.int32)
    csum = jnp.cumsum(onehot, axis=0)
    rank = jnp.sum(csum * onehot, axis=1) - 1
    counts = csum[-1]
    padded = ((counts + tm - 1) // tm) * tm
    ends = jnp.cumsum(padded)
    starts = ends - padded
    slot = jnp.sum(onehot * starts[None, :], axis=1) + rank
    n_tiles = 2 * n // tm + N_EXPERTS + 1
    copy_of_slot = jnp.full((n_tiles * tm,), -1, jnp.int32).at[slot].set(jnp.arange(2 * n, dtype=jnp.int32))
    copy_tab = jnp.pad(copy_of_slot.reshape(n_tiles, tm), ((0, 1), (0, stride - tm)), constant_values=-1)
    tok_tab = jnp.where(copy_tab >= 0, copy_tab % n, 0)
    dump = 2 * n + jnp.arange(stride, dtype=jnp.int32)[None, :]
    dst_tab = jnp.where(copy_tab >= 0, copy_tab, dump)
    dst_tab = jnp.concatenate([jnp.broadcast_to(dump, (1, stride)), dst_tab[:-1]], axis=0)
    n_used = (ends[-1] // tm).astype(jnp.int32)
    tile_start = jnp.minimum(jnp.arange(n_tiles, dtype=jnp.int32), n_used - 1) * tm
    tile_expert = jnp.sum((tile_start[:, None] >= ends[None, :]).astype(jnp.int32), axis=1)
    return tok_tab.reshape(-1), dst_tab.reshape(-1), tile_expert, n_used.reshape(1)


def _odd_weights(od_w_in, mla_w_uq, mla_w_ukv):
    c2 = 2 * CONV_CH + MLA_Q_RANK + MLA_KV_RANK
    half = MLA_ROPE // 2
    w_kr = od_w_in[:, c2:]
    w_kr_sw = jnp.concatenate([w_kr[:, half:], w_kr[:, :half]], axis=1)
    zl = jnp.zeros((D_MODEL, MLA_NOPE), F32)
    zr = jnp.zeros((D_MODEL, MLA_PAD - MLA_NOPE - MLA_ROPE), F32)
    win = jnp.concatenate([od_w_in[:, :c2], zl, w_kr, zr, zl, w_kr_sw, zr], axis=1)
    dk = MLA_NOPE + MLA_ROPE
    wq = mla_w_uq.reshape(MLA_Q_RANK, MLA_HEADS, dk)
    zq = jnp.zeros((MLA_Q_RANK, MLA_HEADS, MLA_PAD - dk), F32)
    wuq = jnp.concatenate([wq, zq], axis=2).reshape(MLA_Q_RANK, MLA_HEADS * MLA_PAD)
    wq_sw = jnp.concatenate([jnp.zeros_like(wq[:, :, :MLA_NOPE]), wq[:, :, MLA_NOPE + half:],
                             wq[:, :, MLA_NOPE:MLA_NOPE + half], zq], axis=2)
    wuqs = wq_sw.reshape(MLA_Q_RANK, MLA_HEADS * MLA_PAD)
    wkv = mla_w_ukv.reshape(MLA_KV_RANK, MLA_HEADS, MLA_NOPE + MLA_V)
    zk = jnp.zeros((MLA_KV_RANK, MLA_HEADS, MLA_PAD - MLA_NOPE), F32)
    wuk = jnp.concatenate([wkv[:, :, :MLA_NOPE], zk], axis=2).reshape(MLA_KV_RANK, MLA_HEADS * MLA_PAD)
    zv = jnp.zeros((MLA_KV_RANK, MLA_HEADS, MLA_VROWS - MLA_V), F32)
    wuv_t = jnp.concatenate([wkv[:, :, MLA_NOPE:], zv], axis=2).reshape(MLA_KV_RANK, MLA_HEADS * MLA_VROWS).T
    v_one = jnp.zeros((MLA_HEADS, MLA_VROWS), F32).at[:, MLA_V].set(1.0).reshape(MLA_HEADS * MLA_VROWS, 1)
    return win.astype(BF16), wuq.astype(BF16), wuqs.astype(BF16), wuk.astype(BF16), wuv_t.astype(BF16), v_one


def _rope_tables(seq):
    inv = 1.0 / (ROPE_THETA ** (jnp.arange(0, MLA_ROPE, 2, dtype=F32) / MLA_ROPE))
    ang = jnp.arange(seq, dtype=F32)[:, None] * inv[None, :]
    cos, sin = jnp.cos(ang), jnp.sin(ang)
    ones = jnp.ones((seq, MLA_NOPE), F32)
    zl = jnp.zeros((seq, MLA_NOPE), F32)
    zr = jnp.zeros((seq, MLA_PAD - MLA_NOPE - MLA_ROPE), F32)
    return (jnp.concatenate([ones, cos, cos, zr], axis=1), jnp.concatenate([zl, -sin, sin, zr], axis=1))


def kernel(x, norm_g, ev_w_in, ssm_lambda_re, ssm_lambda_im, ssm_log_dt, ssm_b_re, ssm_b_im, ssm_c_re, ssm_c_im, ssm_d, ssm_w_glu, sgu_ln_g, sgu_ln_b, sgu_w, sgu_b, ev_w_out, ffn_w_gate, ffn_w_up, ffn_w_down, od_w_in, conv_w, conv_b, conv_ln_g, conv_ln_b, mla_q_norm_g, mla_w_uq, mla_kv_norm_g, mla_w_ukv, od_w_out, moe_w_router, moe_w_gate, moe_w_up, moe_w_down):
    bsz, seq, d = x.shape
    n = bsz * seq
    assert d == D_MODEL and SUBLANES % bsz == 0 and seq % 512 == 0
    row = lambda v: v.astype(F32).reshape(1, -1)
    h = x.astype(F32).reshape(n, d)
    tm = 512

    g = norm_g[0]
    a_in, proj = _norm_proj(h, row(g[0]), ev_w_in[0].astype(BF16), tm)
    mats = _s5_matrices(ssm_lambda_re[0], ssm_lambda_im[0], ssm_log_dt[0], ssm_b_re[0], ssm_b_im[0],
                        ssm_c_re[0], ssm_c_im[0])
    ys = _s5_mixer(a_in, mats, ssm_d[0], bsz, seq)
    causal = jnp.tril(jnp.ones((SGU_CHUNK, SGU_CHUNK), dtype=bool))
    ws = jnp.where(causal[None], sgu_w[0], 0.0).astype(BF16)
    bias = jnp.repeat(sgu_b[0].astype(F32).T, SGU_HEAD_DIM, axis=1)
    wo = ev_w_out[0].astype(BF16)
    h, z = _even_mix(ys, proj, h, ssm_w_glu[0].astype(BF16), row(sgu_ln_g[0]), row(sgu_ln_b[0]), ws, bias,
                     wo[:SSM_WIDTH], wo[SSM_WIDTH:], row(g[1]), row(g[2]), tm)
    h, z = _dense_ffn(z, ffn_w_gate[0].astype(BF16), ffn_w_up[0].astype(BF16), ffn_w_down[0].astype(BF16),
                      h, row(g[3]), row(norm_g[1][0]), 1024, 512)

    g = norm_g[1]
    win, wuq, wuqs, wuk, wuv_t, v_one = _odd_weights(od_w_in[0], mla_w_uq[0], mla_w_ukv[0])
    cos_t, sin_t = _rope_tables(seq)
    zc, q, k, vt = _odd_proj(z, win, row(mla_q_norm_g[0]), row(mla_kv_norm_g[0]), wuq, wuqs, wuk, wuv_t, v_one,
                             cos_t, sin_t, seq, tm)
    hp = MLA_HEADS * MLA_PAD
    tm_moe, tf_moe = 1024, 512
    yd, wg_b, wu_b, wd_b = _attention(q.reshape(bsz, seq, hp), k.reshape(bsz, seq, hp), vt, tm,
                                      moe_w_gate[0], moe_w_up[0], moe_w_down[0], tf_moe)
    conv_w_pad = jnp.concatenate([conv_w[0].astype(F32), jnp.zeros((CONV_HALO - CONV_TAPS, CONV_CH), F32)], axis=0)
    yc = _conv_mixer(zc.reshape(bsz, seq, 2 * CONV_CH), conv_w_pad, row(conv_b[0]), row(conv_ln_g[0]),
                     row(conv_ln_b[0]), tm)
    wo = od_w_out[0].astype(BF16)
    wr = jnp.concatenate([moe_w_router[0].astype(F32), jnp.zeros((d, LANES - N_EXPERTS), F32)], axis=1)
    h, z, route = _odd_mix(yc.reshape(n, CONV_CH), yd.reshape(n, MLA_HEADS * MLA_V), h, wo[:CONV_CH], wo[CONV_CH:],
                           row(g[1]), row(g[2]), wr.astype(BF16), tm)
    n_f = wg_b.shape[1]
    tok_tab, dst_tab, tile_expert, n_used = _moe_plan(route, tm_moe, n_f * -(-tm_moe // n_f))
    y = _moe_ffn(tile_expert, n_used, tok_tab, dst_tab, z, wg_b, wu_b, wd_b, n, tm_moe)
    h = _combine(y, route, h, row(g[3]), 256)
    return h.reshape(bsz, seq, d).astype(x.dtype)
```

```python
import functools
import math

import jax
import jax.numpy as jnp
from jax import lax
from jax.experimental import pallas as pl
from jax.experimental.pallas import tpu as pltpu

F32 = jnp.float32
BF16 = jnp.bfloat16

D_MODEL = 1024
NORM_EPS = 1e-6
SSM_WIDTH = 512
SSM_GROUP = 16
SSM_GROUPS = 32
SSM_STATE = 64
SSM_CHUNK = 16
SSM_PAIR = 2 * SSM_GROUP * SSM_CHUNK
SGU_WIDTH = 512
SGU_HEADS = 8
SGU_HEAD_DIM = 64
SGU_CHUNK = 128
CONV_CH = 512
CONV_TAPS = 31
CONV_HALO = 32
MLA_HEADS = 8
MLA_Q_RANK = 256
MLA_KV_RANK = 128
MLA_NOPE = 64
MLA_ROPE = 32
MLA_V = 64
MLA_PAD = 128
MLA_VROWS = 80
ROPE_THETA = 10000.0
FF_DENSE = 4096
N_EXPERTS = 8
FF_EXPERT = 3584
LANES = 128
SUBLANES = 8
ROW_TILE = D_MODEL // LANES
VMEM_LIMIT = 56 * 1024 * 1024


def _params(sem, vmem=VMEM_LIMIT):
    return pltpu.CompilerParams(dimension_semantics=sem, vmem_limit_bytes=vmem)


def _rms(x, g):
    return x * lax.rsqrt(jnp.mean(x * x, axis=-1, keepdims=True) + NORM_EPS) * g


def _layer_norm(x, g, b):
    mu = jnp.mean(x, axis=-1, keepdims=True)
    xc = x - mu
    return xc * lax.rsqrt(jnp.mean(xc * xc, axis=-1, keepdims=True) + NORM_EPS) * g + b


def _dot(a, b):
    return jnp.dot(a, b, preferred_element_type=F32)


def _norm_proj_kernel(h_ref, g_ref, w_ref, a_ref, b_ref):
    z = _rms(h_ref[...], g_ref[...])
    proj = _dot(z.astype(BF16), w_ref[...])
    for jb in range(SSM_WIDTH // LANES):
        a_ref[jb] = proj[:, jb * LANES:(jb + 1) * LANES]
    b_ref[...] = proj[:, SSM_WIDTH:].astype(b_ref.dtype)


def _norm_proj(h, g, w, tm):
    n, d = h.shape
    cols = w.shape[1]
    return pl.pallas_call(
        _norm_proj_kernel,
        grid=(n // tm,),
        in_specs=[pl.BlockSpec((tm, d), lambda i: (i, 0)),
                  pl.BlockSpec((1, d), lambda i: (0, 0)),
                  pl.BlockSpec((d, cols), lambda i: (0, 0))],
        out_specs=[pl.BlockSpec((SSM_WIDTH // LANES, tm, LANES), lambda i: (0, i, 0)),
                   pl.BlockSpec((tm, cols - SSM_WIDTH), lambda i: (i, 0))],
        out_shape=[jax.ShapeDtypeStruct((SSM_WIDTH // LANES, n, LANES), F32),
                   jax.ShapeDtypeStruct((n, cols - SSM_WIDTH), BF16)],
        compiler_params=_params(("parallel",)),
        name="even_in_proj",
    )(h, g, w)


def _s5_matrices(lam_re, lam_im, log_dt, b_re, b_im, c_re, c_im):
    t = SSM_CHUNK
    lr = jnp.minimum(lam_re.astype(F32), -1e-4)
    li = lam_im.astype(F32)
    dt = jnp.exp(log_dt.astype(F32))[:, None]
    mag = jnp.exp(lr * dt)
    a_re = mag * jnp.cos(li * dt)
    a_im = mag * jnp.sin(li * dt)
    den = lr * lr + li * li
    nr = a_re - 1.0
    coef_re = (nr * lr + a_im * li) / den
    coef_im = (a_im * lr - nr * li) / den
    br = b_re.astype(F32)
    bi = b_im.astype(F32)
    bb_re = coef_re[..., None] * br - coef_im[..., None] * bi
    bb_im = coef_re[..., None] * bi + coef_im[..., None] * br
    cr = c_re.astype(F32)
    ci = c_im.astype(F32)
    pw_re = [jnp.ones_like(a_re)]
    pw_im = [jnp.zeros_like(a_im)]
    for _ in range(t):
        pr, pi = pw_re[-1], pw_im[-1]
        pw_re.append(pr * a_re - pi * a_im)
        pw_im.append(pr * a_im + pi * a_re)
    pw_re = jnp.stack(pw_re)
    pw_im = jnp.stack(pw_im)
    ab_re = pw_re[:t, :, :, None] * bb_re[None] - pw_im[:t, :, :, None] * bb_im[None]
    ab_im = pw_re[:t, :, :, None] * bb_im[None] + pw_im[:t, :, :, None] * bb_re[None]
    hi = lax.Precision.HIGHEST
    k_lag = (jnp.einsum('gnp,tgpm->tgnm', cr, ab_re, precision=hi)
             - jnp.einsum('gnp,tgpm->tgnm', ci, ab_im, precision=hi))
    n_pairs = SSM_GROUPS // 2
    st = 2 * SSM_STATE

    def pair_diag(w):
        w = w.reshape((n_pairs, 2) + w.shape[1:])
        z = jnp.zeros_like(w[:, 0])
        top = jnp.concatenate([w[:, 0], z], axis=-1)
        bot = jnp.concatenate([z, w[:, 1]], axis=-1)
        return jnp.concatenate([top, bot], axis=-2)

    k_blk = pair_diag(k_lag.transpose(1, 0, 3, 2))
    rev_re = pw_re[:t][::-1]
    rev_im = pw_im[:t][::-1]
    ws_re = rev_re[..., None] * bb_re[None] - rev_im[..., None] * bb_im[None]
    ws_im = rev_re[..., None] * bb_im[None] + rev_im[..., None] * bb_re[None]
    ws_re = pair_diag(ws_re.transpose(1, 0, 3, 2)).reshape(n_pairs, SSM_PAIR, st).astype(BF16)
    ws_im = pair_diag(ws_im.transpose(1, 0, 3, 2)).reshape(n_pairs, SSM_PAIR, st).astype(BF16)
    ca_re = cr[None] * pw_re[1:, :, None, :] - ci[None] * pw_im[1:, :, None, :]
    ca_im = cr[None] * pw_im[1:, :, None, :] + ci[None] * pw_re[1:, :, None, :]
    co_re = pair_diag(ca_re.transpose(1, 0, 3, 2))
    co_im = pair_diag((-ca_im).transpose(1, 0, 3, 2))
    w_intra, wo_re, wo_im = _s5_expand(k_blk, co_re, co_im)
    return dict(
        w_intra=w_intra, ws_re=ws_re, ws_im=ws_im, wo_re=wo_re, wo_im=wo_im,
        at_re=pw_re[t].reshape(1, SSM_GROUPS * SSM_STATE), at_im=pw_im[t].reshape(1, SSM_GROUPS * SSM_STATE))


def _s5_expand_kernel(k_ref, cre_ref, cim_ref, wi_ref, wore_ref, woim_ref, kcat_ref):
    pw = 2 * SSM_GROUP
    for tau in range(SSM_CHUNK):
        kcat_ref[:, tau * pw:(tau + 1) * pw] = k_ref[0, tau]
        wore_ref[0, :, tau * pw:(tau + 1) * pw] = cre_ref[0, tau].astype(wore_ref.dtype)
        woim_ref[0, :, tau * pw:(tau + 1) * pw] = cim_ref[0, tau].astype(woim_ref.dtype)
    kcat = kcat_ref[...]
    col = lax.broadcasted_iota(jnp.int32, kcat.shape, 1)
    for s in range(SSM_CHUNK):
        blk = kcat if s == 0 else jnp.where(col >= s * pw, pltpu.roll(kcat, s * pw, 1), 0.0)
        wi_ref[0, s * pw:(s + 1) * pw, :] = blk.astype(wi_ref.dtype)


def _s5_expand(k_blk, co_re, co_im):
    n_pairs = k_blk.shape[0]
    pw = 2 * SSM_GROUP
    st = 2 * SSM_STATE
    return pl.pallas_call(
        _s5_expand_kernel,
        grid=(n_pairs,),
        in_specs=[pl.BlockSpec((1, SSM_CHUNK, pw, pw), lambda q: (q, 0, 0, 0)),
                  pl.BlockSpec((1, SSM_CHUNK, st, pw), lambda q: (q, 0, 0, 0)),
                  pl.BlockSpec((1, SSM_CHUNK, st, pw), lambda q: (q, 0, 0, 0))],
        out_specs=[pl.BlockSpec((1, SSM_PAIR, SSM_PAIR), lambda q: (q, 0, 0)),
                   pl.BlockSpec((1, st, SSM_PAIR), lambda q: (q, 0, 0)),
                   pl.BlockSpec((1, st, SSM_PAIR), lambda q: (q, 0, 0))],
        out_shape=[jax.ShapeDtypeStruct((n_pairs, SSM_PAIR, SSM_PAIR), BF16),
                   jax.ShapeDtypeStruct((n_pairs, st, SSM_PAIR), BF16),
                   jax.ShapeDtypeStruct((n_pairs, st, SSM_PAIR), BF16)],
        scratch_shapes=[pltpu.VMEM((pw, SSM_PAIR), F32)],
        compiler_params=_params(("parallel",)),
        name="s5_expand_weights",
    )(k_blk, co_re, co_im)


S5_LANE_PAIRS = LANES // (2 * SSM_GROUP)
S5_SCAN_LANES = 512


def _s5_state_kernel(u0_ref, u1_ref, u2_ref, u3_ref, wre_ref, wim_ref, are_ref, aim_ref,
                     x_ref, hre_ref, him_ref, sre_ref, sim_ref):
    n_chunks = x_ref.shape[0]
    pw = 2 * SSM_GROUP
    u_refs = (u0_ref, u1_ref, u2_ref, u3_ref)
    for t in range(SSM_CHUNK):
        for j, u_ref in enumerate(u_refs):
            ut = u_ref[pl.ds(t, n_chunks, stride=SSM_CHUNK), :]
            for qq in range(S5_LANE_PAIRS):
                q = j * S5_LANE_PAIRS + qq
                x_ref[:, q * SSM_PAIR + t * pw: q * SSM_PAIR + (t + 1) * pw] = (
                    ut[:, qq * pw:(qq + 1) * pw].astype(x_ref.dtype))
    st = 2 * SSM_STATE
    for q in range(SSM_GROUPS // 2):
        xq = x_ref[:, q * SSM_PAIR:(q + 1) * SSM_PAIR]
        sre_ref[:, q * st:(q + 1) * st] = _dot(xq, wre_ref[q])
        sim_ref[:, q * st:(q + 1) * st] = _dot(xq, wim_ref[q])

    row = lax.broadcasted_iota(jnp.int32, (SUBLANES, S5_SCAN_LANES), 0)
    zero = jnp.zeros((SUBLANES, S5_SCAN_LANES), F32)
    for c0 in range(0, sre_ref.shape[1], S5_SCAN_LANES):
        cols = pl.ds(c0, S5_SCAN_LANES)
        ar = are_ref[:, cols]
        ai = aim_ref[:, cols]

        def body(k, carry, cols=cols, ar=ar, ai=ai):
            r0 = pl.multiple_of(k * SUBLANES, SUBLANES)
            sr = sre_ref[pl.ds(r0, SUBLANES), cols]
            si = sim_ref[pl.ds(r0, SUBLANES), cols]
            out_r, out_i = carry
            for i in range(1, SUBLANES + 1):
                tr = ar * out_r - ai * out_i + sr
                ti = ar * out_i + ai * out_r + si
                tr = pltpu.roll(tr, 1, 0)
                ti = pltpu.roll(ti, 1, 0)
                if i < SUBLANES:
                    out_r = jnp.where(row == i, tr, out_r)
                    out_i = jnp.where(row == i, ti, out_i)
            hre_ref[pl.ds(r0, SUBLANES), cols] = out_r
            him_ref[pl.ds(r0, SUBLANES), cols] = out_i
            return tr, ti

        lax.fori_loop(0, n_chunks // SUBLANES, body, (zero, zero))


def _s5_out_kernel(x_ref, wi_ref, hre_ref, him_ref, wore_ref, woim_ref, d_ref, y_ref, yt_ref):
    n_chunks = x_ref.shape[0]
    pw = 2 * SSM_GROUP
    st = 2 * SSM_STATE
    for qq in range(S5_LANE_PAIRS):
        x = x_ref[:, qq * SSM_PAIR:(qq + 1) * SSM_PAIR]
        y = _dot(x, wi_ref[qq])
        y += _dot(hre_ref[:, qq * st:(qq + 1) * st].astype(BF16), wore_ref[qq])
        y += _dot(him_ref[:, qq * st:(qq + 1) * st].astype(BF16), woim_ref[qq])
        y += d_ref[:, qq * SSM_PAIR:(qq + 1) * SSM_PAIR] * x.astype(F32)
        y = jax.nn.gelu(y)
        for t in range(SSM_CHUNK):
            yt_ref[t, :, qq * pw:(qq + 1) * pw] = y[:, t * pw:(t + 1) * pw]
    for t in range(SSM_CHUNK):
        y_ref[pl.ds(t, n_chunks, stride=SSM_CHUNK), :] = yt_ref[t]


def _s5_mixer(u, mats, d, batch, seq):
    t = SSM_CHUNK
    n_chunks = seq // t
    n_pairs = SSM_GROUPS // 2
    cols = n_pairs * SSM_PAIR
    st = 2 * SSM_STATE
    n_state = n_pairs * st
    n_blk = SSM_WIDTH // LANES
    assert n_blk == 4 and n_chunks % SUBLANES == 0
    once = pl.Buffered(1)
    x, h_re, h_im = pl.pallas_call(
        _s5_state_kernel,
        grid=(batch,),
        in_specs=[pl.BlockSpec((None, seq, LANES), lambda b, j=j: (j, b, 0)) for j in range(n_blk)] + [
            pl.BlockSpec((n_pairs, SSM_PAIR, st), lambda b: (0, 0, 0), pipeline_mode=once),
            pl.BlockSpec((n_pairs, SSM_PAIR, st), lambda b: (0, 0, 0), pipeline_mode=once),
            pl.BlockSpec((1, n_state), lambda b: (0, 0)),
            pl.BlockSpec((1, n_state), lambda b: (0, 0))],
        out_specs=[pl.BlockSpec((n_chunks, cols), lambda b: (b, 0)),
                   pl.BlockSpec((n_chunks, n_state), lambda b: (b, 0)),
                   pl.BlockSpec((n_chunks, n_state), lambda b: (b, 0))],
        out_shape=[jax.ShapeDtypeStruct((batch * n_chunks, cols), BF16),
                   jax.ShapeDtypeStruct((batch * n_chunks, n_state), F32),
                   jax.ShapeDtypeStruct((batch * n_chunks, n_state), F32)],
        scratch_shapes=[pltpu.VMEM((n_chunks, n_state), F32), pltpu.VMEM((n_chunks, n_state), F32)],
        compiler_params=_params(("parallel",)),
        name="s5_state_scan",
    )(u, u, u, u, mats['ws_re'], mats['ws_im'], mats['at_re'], mats['at_im'])
    lp = S5_LANE_PAIRS
    d_cols = jnp.broadcast_to(d.astype(F32).reshape(n_pairs, 1, 2 * SSM_GROUP),
                              (n_pairs, t, 2 * SSM_GROUP)).reshape(1, cols)
    return pl.pallas_call(
        _s5_out_kernel,
        grid=(batch, n_blk),
        in_specs=[pl.BlockSpec((n_chunks, lp * SSM_PAIR), lambda b, j: (b, j)),
                  pl.BlockSpec((lp, SSM_PAIR, SSM_PAIR), lambda b, j: (j, 0, 0)),
                  pl.BlockSpec((n_chunks, lp * st), lambda b, j: (b, j)),
                  pl.BlockSpec((n_chunks, lp * st), lambda b, j: (b, j)),
                  pl.BlockSpec((lp, st, SSM_PAIR), lambda b, j: (j, 0, 0)),
                  pl.BlockSpec((lp, st, SSM_PAIR), lambda b, j: (j, 0, 0)),
                  pl.BlockSpec((1, lp * SSM_PAIR), lambda b, j: (0, j))],
        out_specs=pl.BlockSpec((None, seq, LANES), lambda b, j: (j, b, 0)),
        out_shape=jax.ShapeDtypeStruct((n_blk, batch * seq, LANES), F32),
        scratch_shapes=[pltpu.VMEM((t, n_chunks, LANES), F32)],
        compiler_params=_params(("parallel", "parallel")),
        name="s5_out",
    )(x, mats['w_intra'], h_re, h_im, mats['wo_re'], mats['wo_im'], d_cols)


def _even_mix_kernel(ys_ref, bu_ref, bv_ref, h_ref, wglu_ref, lng_ref, lnb_ref, ws_ref, bias_ref,
                     wo_a_ref, wo_b_ref, g1_ref, g2_ref, hout_ref, z_ref, s_scr):
    tm = h_ref.shape[0]
    ys = jnp.concatenate([ys_ref[jb] for jb in range(ys_ref.shape[0])], axis=1)
    ya = ys * jax.nn.sigmoid(_dot(ys.astype(BF16), wglu_ref[...]))
    u = jax.nn.gelu(bu_ref[...].astype(F32))
    v = _layer_norm(jax.nn.gelu(bv_ref[...].astype(F32)), lng_ref[...], lnb_ref[...])
    lane = lax.broadcasted_iota(jnp.int32, v.shape, 1)
    left = (lane % LANES) < SGU_HEAD_DIM
    v_l = jnp.where(left, v, 0.0).astype(BF16)
    v_r = jnp.where(left, 0.0, v).astype(BF16)
    for c in range(tm // SGU_CHUNK):
        rows = slice(c * SGU_CHUNK, (c + 1) * SGU_CHUNK)
        for p in range(SGU_HEADS // 2):
            cols = slice(p * LANES, (p + 1) * LANES)
            s_scr[rows, cols] = (_dot(ws_ref[2 * p], v_l[rows, cols]) + _dot(ws_ref[2 * p + 1], v_r[rows, cols]))
    bias = jnp.concatenate([bias_ref[...]] * (tm // SGU_CHUNK), axis=0)
    yb = u * (s_scr[...] + bias)
    mix = _dot(ya.astype(BF16), wo_a_ref[...]) + _dot(yb.astype(BF16), wo_b_ref[...])
    h_new = h_ref[...] + _rms(mix, g1_ref[...])
    hout_ref[...] = h_new
    z_ref[...] = _rms(h_new, g2_ref[...]).astype(z_ref.dtype)


def _even_mix(ys, proj, h, wglu, lng, lnb, ws, bias, wo_a, wo_b, g1, g2, tm):
    n, d = h.shape
    w = SGU_WIDTH
    const = lambda *shape: pl.BlockSpec(shape, lambda i: (0,) * len(shape))
    return pl.pallas_call(
        _even_mix_kernel,
        grid=(n // tm,),
        in_specs=[pl.BlockSpec((w // LANES, tm, LANES), lambda i: (0, i, 0)),
                  pl.BlockSpec((tm, w), lambda i: (i, 0)),
                  pl.BlockSpec((tm, w), lambda i: (i, 1)),
                  pl.BlockSpec((tm, d), lambda i: (i, 0)),
                  const(w, w), const(1, w), const(1, w),
                  const(SGU_HEADS, SGU_CHUNK, SGU_CHUNK), const(SGU_CHUNK, w),
                  const(w, d), const(w, d), const(1, d), const(1, d)],
        out_specs=[pl.BlockSpec((tm, d), lambda i: (i, 0)),
                   pl.BlockSpec((tm, d), lambda i: (i, 0))],
        out_shape=[jax.ShapeDtypeStruct((n, d), F32), jax.ShapeDtypeStruct((n, d), BF16)],
        scratch_shapes=[pltpu.VMEM((tm, w), F32)],
        compiler_params=_params(("parallel",)),
        name="even_mix",
    )(ys, proj, proj, h, wglu, lng, lnb, ws, bias, wo_a, wo_b, g1, g2)


def _ffn_kernel(z_ref, wg_ref, wu_ref, wd_ref, h_ref, g3_ref, gn_ref, hout_ref, zout_ref, acc_ref):
    j = pl.program_id(1)

    @pl.when(j == 0)
    def _():
        acc_ref[...] = jnp.zeros_like(acc_ref)

    z = z_ref[...]
    a = jax.nn.silu(_dot(z, wg_ref[...])) * _dot(z, wu_ref[...])
    acc_ref[...] += _dot(a.astype(BF16), wd_ref[...])

    @pl.when(j == pl.num_programs(1) - 1)
    def _():
        h_new = h_ref[...] + _rms(acc_ref[...], g3_ref[...])
        hout_ref[...] = h_new
        zout_ref[...] = _rms(h_new, gn_ref[...]).astype(zout_ref.dtype)


def _dense_ffn(z, wg, wu, wd, h, g3, g_next, tm, tf):
    n, d = h.shape
    ff = wg.shape[1]
    return pl.pallas_call(
        _ffn_kernel,
        grid=(n // tm, ff // tf),
        in_specs=[pl.BlockSpec((tm, d), lambda i, j: (i, 0)),
                  pl.BlockSpec((d, tf), lambda i, j: (0, j)),
                  pl.BlockSpec((d, tf), lambda i, j: (0, j)),
                  pl.BlockSpec((tf, d), lambda i, j: (j, 0)),
                  pl.BlockSpec((tm, d), lambda i, j: (i, 0)),
                  pl.BlockSpec((1, d), lambda i, j: (0, 0)),
                  pl.BlockSpec((1, d), lambda i, j: (0, 0))],
        out_specs=[pl.BlockSpec((tm, d), lambda i, j: (i, 0)),
                   pl.BlockSpec((tm, d), lambda i, j: (i, 0))],
        out_shape=[jax.ShapeDtypeStruct((n, d), F32), jax.ShapeDtypeStruct((n, d), BF16)],
        scratch_shapes=[pltpu.VMEM((tm, d), F32)],
        compiler_params=_params(("parallel", "arbitrary")),
        name="dense_ffn",
    )(z, wg, wu, wd, h, g3, g_next)


def _odd_proj_kernel(z_ref, win_ref, gq_ref, gkv_ref, wuq_ref, wuqs_ref, wuk_ref, wuv_ref, vone_ref, cos_ref, sin_ref,
                     zc_ref, q_ref, k_ref, v_ref, *, scale):
    z = z_ref[...]
    proj = _dot(z, win_ref[...])
    c0 = 2 * CONV_CH
    c1 = c0 + MLA_Q_RANK
    c2 = c1 + MLA_KV_RANK
    c3 = c2 + MLA_PAD
    zc_ref[...] = proj[:, :c0].astype(zc_ref.dtype)
    cq = _rms(proj[:, c0:c1], gq_ref[...]).astype(BF16)
    ckv = _rms(proj[:, c1:c2], gkv_ref[...]).astype(BF16)
    cos = cos_ref[...]
    sin = sin_ref[...]
    cos_h = jnp.concatenate([cos] * MLA_HEADS, axis=1)
    sin_h = jnp.concatenate([sin] * MLA_HEADS, axis=1)
    q = _dot(cq, wuq_ref[...]) * cos_h + _dot(cq, wuqs_ref[...]) * sin_h
    q_ref[...] = (q * scale).astype(q_ref.dtype)
    kr = proj[:, c2:c3] * cos + proj[:, c3:] * sin
    k = _dot(ckv, wuk_ref[...]) + jnp.concatenate([kr] * MLA_HEADS, axis=1)
    k_ref[...] = k.astype(k_ref.dtype)
    vt = lax.dot_general(wuv_ref[...], ckv, (((1,), (1,)), ((), ())), preferred_element_type=F32)
    v_ref[0] = (vt + vone_ref[...]).astype(v_ref.dtype)


def _odd_proj(z, win, gq, gkv, wuq, wuqs, wuk, wuv_t, v_one, cos_t, sin_t, seq, tm):
    n, d = z.shape
    hp = MLA_HEADS * MLA_PAD
    vr = MLA_HEADS * MLA_VROWS
    n_l = seq // tm
    const = lambda *shape: pl.BlockSpec(shape, lambda i: (0,) * len(shape))
    out = jax.ShapeDtypeStruct((n, hp), BF16)
    scale = float((MLA_NOPE + MLA_ROPE) ** -0.5 * math.log2(math.e))
    return pl.pallas_call(
        functools.partial(_odd_proj_kernel, scale=scale),
        grid=(n // tm,),
        in_specs=[pl.BlockSpec((tm, d), lambda i: (i, 0)),
                  const(d, win.shape[1]), const(1, MLA_Q_RANK), const(1, MLA_KV_RANK),
                  const(MLA_Q_RANK, hp), const(MLA_Q_RANK, hp), const(MLA_KV_RANK, hp), const(vr, MLA_KV_RANK),
                  const(vr, 1),
                  pl.BlockSpec((tm, MLA_PAD), lambda i: (i % n_l, 0)),
                  pl.BlockSpec((tm, MLA_PAD), lambda i: (i % n_l, 0))],
        out_specs=[pl.BlockSpec((tm, 2 * CONV_CH), lambda i: (i, 0)),
                   pl.BlockSpec((tm, hp), lambda i: (i, 0)),
                   pl.BlockSpec((tm, hp), lambda i: (i, 0)),
                   pl.BlockSpec((1, vr, tm), lambda i: (i, 0, 0))],
        out_shape=[jax.ShapeDtypeStruct((n, 2 * CONV_CH), BF16), out, out,
                   jax.ShapeDtypeStruct((n // tm, vr, tm), BF16)],
        compiler_params=_params(("parallel",)),
        name="odd_in_proj",
    )(z, win, gq, gkv, wuq, wuqs, wuk, wuv_t, v_one, cos_t, sin_t)


def _attn_kernel(q_ref, k_ref, vt_ref, wg_ref, wu_ref, wd_ref, o_ref, wgb_ref, wub_ref, wdb_ref, acc_ref, *, blk):
    i = pl.program_id(2)
    acc_ref[...] = jnp.zeros_like(acc_ref)
    tf = wgb_ref.shape[3]
    for f in range(wgb_ref.shape[1]):
        wgb_ref[0, f] = wg_ref[0, :, f * tf:(f + 1) * tf].astype(BF16)
        wub_ref[0, f] = wu_ref[0, :, f * tf:(f + 1) * tf].astype(BF16)
    wdb_ref[0] = wd_ref[0].astype(BF16)

    def step(j, m, masked):
        r0 = pl.multiple_of(j * blk, blk)
        scores = []
        for hh in range(2):
            q = q_ref[0, :, hh * MLA_PAD:(hh + 1) * MLA_PAD]
            k = k_ref[0, pl.ds(r0, blk), hh * MLA_PAD:(hh + 1) * MLA_PAD]
            st = lax.dot_general(k, q, (((1,), (1,)), ((), ())), preferred_element_type=F32)
            if masked:
                key = lax.broadcasted_iota(jnp.int32, st.shape, 0)
                qry = lax.broadcasted_iota(jnp.int32, st.shape, 1)
                st = jnp.where(key <= qry, st, -1e30)
            scores.append(st)
        soft = []
        for hh in range(2):
            m_new = jnp.maximum(m[hh], jnp.max(scores[hh], axis=0, keepdims=True))
            soft.append((m_new, jnp.exp2(m[hh] - m_new), jnp.exp2(scores[hh] - m_new).astype(BF16)))
        for hh in range(2):
            vt = vt_ref[j, hh * MLA_VROWS:(hh + 1) * MLA_VROWS, :]
            acc_ref[hh] = soft[hh][1] * acc_ref[hh] + _dot(vt, soft[hh][2])
        return (soft[0][0], soft[1][0])

    init = jnp.full((1, blk), -1e30, F32)
    m = lax.fori_loop(0, i, lambda j, m: step(j, m, False), (init, init))
    step(i, m, True)
    a0 = acc_ref[0]
    a1 = acc_ref[1]
    ot = jnp.concatenate([a0[:MLA_V] / a0[MLA_V:MLA_V + 1], a1[:MLA_V] / a1[MLA_V:MLA_V + 1]], axis=0)
    o_ref[0] = ot.T.astype(o_ref.dtype)


def _attention(q, k, vt, blk, wg, wu, wd, tf):
    b, seq, _ = q.shape
    n_blk = seq // blk
    n_pairs = MLA_HEADS // 2
    n_e, d, ff = wg.shape
    steps = b * n_pairs * n_blk
    per_e = steps // n_e
    assert steps == per_e * n_e and d % per_e == 0 and ff % per_e == 0
    rows_in, rows_down = d // per_e, ff // per_e
    assert rows_in % 16 == 0 and rows_down % 16 == 0 and ff % tf == 0

    def lin(bi, p, i):
        return (bi * n_pairs + p) * n_blk + i

    w_in = pl.BlockSpec((1, rows_in, ff), lambda bi, p, i: (lin(bi, p, i) // per_e, lin(bi, p, i) % per_e, 0))
    w_out = pl.BlockSpec((1, ff // tf, rows_in, tf),
                         lambda bi, p, i: (lin(bi, p, i) // per_e, 0, lin(bi, p, i) % per_e, 0))
    w_down = pl.BlockSpec((1, rows_down, d), lambda bi, p, i: (lin(bi, p, i) // per_e, lin(bi, p, i) % per_e, 0))
    return pl.pallas_call(
        functools.partial(_attn_kernel, blk=blk),
        grid=(b, n_pairs, n_blk),
        in_specs=[pl.BlockSpec((1, blk, 2 * MLA_PAD), lambda bi, p, i: (bi, i, p)),
                  pl.BlockSpec((1, seq, 2 * MLA_PAD), lambda bi, p, i: (bi, 0, p)),
                  pl.BlockSpec((n_blk, 2 * MLA_VROWS, blk), lambda bi, p, i: (bi, p, 0)),
                  w_in, w_in, w_down],
        out_specs=[pl.BlockSpec((1, blk, 2 * MLA_V), lambda bi, p, i: (bi, i, p)), w_out, w_out, w_down],
        out_shape=[jax.ShapeDtypeStruct((b, seq, MLA_HEADS * MLA_V), BF16),
                   jax.ShapeDtypeStruct((n_e, ff // tf, d, tf), BF16),
                   jax.ShapeDtypeStruct((n_e, ff // tf, d, tf), BF16),
                   jax.ShapeDtypeStruct((n_e, ff, d), BF16)],
        scratch_shapes=[pltpu.VMEM((2, MLA_VROWS, blk), F32)],
        compiler_params=_params(("parallel", "parallel", "parallel")),
        name="mla_attention",
    )(q, k, vt, wg, wu, wd)


def _conv_kernel(zc_ref, w_ref, b_ref, lng_ref, lnb_ref, y_ref, buf_ref, part_ref):
    tm = zc_ref.shape[1]

    @pl.when(pl.program_id(1) == 0)
    def _():
        buf_ref[pl.ds(0, CONV_HALO), :] = jnp.zeros((CONV_HALO, CONV_CH), F32)
        buf_ref[pl.ds(CONV_HALO + tm, SUBLANES), :] = jnp.zeros((SUBLANES, CONV_CH), F32)

    zc = zc_ref[0].astype(F32)
    hh = zc[:, :CONV_CH] * jax.nn.sigmoid(zc[:, CONV_CH:])
    buf_ref[pl.ds(CONV_HALO, tm), :] = hh
    off = CONV_HALO - (CONV_TAPS - 1)
    acc = jnp.zeros((tm, CONV_CH), F32) + b_ref[...]
    for b in range(SUBLANES):
        taps = [k for k in range(CONV_TAPS) if (off + k) % SUBLANES == b]
        part = None
        for k in taps:
            term = w_ref[pl.ds(k, 1), :] * buf_ref[pl.ds(off + k - b, tm + SUBLANES), :]
            part = term if part is None else part + term
        if b == 0:
            acc = acc + part[:tm]
        else:
            part_ref[...] = part
            acc = acc + part_ref[pl.ds(b, tm), :]
    buf_ref[pl.ds(0, CONV_HALO), :] = buf_ref[pl.ds(tm, CONV_HALO), :]
    y_ref[0] = jax.nn.silu(_layer_norm(acc, lng_ref[...], lnb_ref[...])).astype(y_ref.dtype)


def _conv_mixer(zc, w, b, lng, lnb, tm):
    bsz, seq, _ = zc.shape
    const = lambda *shape: pl.BlockSpec(shape, lambda bi, i: (0,) * len(shape))
    return pl.pallas_call(
        _conv_kernel,
        grid=(bsz, seq // tm),
        in_specs=[pl.BlockSpec((1, tm, 2 * CONV_CH), lambda bi, i: (bi, i, 0)),
                  const(CONV_HALO, CONV_CH), const(1, CONV_CH), const(1, CONV_CH), const(1, CONV_CH)],
        out_specs=pl.BlockSpec((1, tm, CONV_CH), lambda bi, i: (bi, i, 0)),
        out_shape=jax.ShapeDtypeStruct((bsz, seq, CONV_CH), BF16),
        scratch_shapes=[pltpu.VMEM((CONV_HALO + tm + SUBLANES, CONV_CH), F32),
                        pltpu.VMEM((tm + SUBLANES, CONV_CH), F32)],
        compiler_params=_params(("arbitrary", "arbitrary")),
        name="conv_module",
    )(zc, w, b, lng, lnb)


def _odd_mix_kernel(yc_ref, yd_ref, h_ref, wo_a_ref, wo_b_ref, g1_ref, g2_ref, wr_ref, hout_ref, z_ref, route_ref):
    tm = h_ref.shape[0]
    halves = [slice(0, tm // 2), slice(tm // 2, tm)]
    mixes = [_dot(yc_ref[r, :], wo_a_ref[...]) + _dot(yd_ref[r, :], wo_b_ref[...]) for r in halves]
    zs = []
    for r, mix in zip(halves, mixes):
        h_new = h_ref[r, :] + _rms(mix, g1_ref[...])
        hout_ref[r, :] = h_new
        zs.append(_rms(h_new, g2_ref[...]))
    all_logits = [_dot(z.astype(BF16), wr_ref[...]) for z in zs]
    _store_row_tiles(z_ref, jnp.concatenate(zs, axis=0))
    neg = -jnp.inf
    for r, logits in zip(halves, all_logits):
        lane = lax.broadcasted_iota(jnp.int32, logits.shape, 1)
        logits = jnp.where(lane < N_EXPERTS, logits, neg)
        m1 = jnp.max(logits, axis=-1, keepdims=True)
        i1 = jnp.min(jnp.where(logits == m1, lane, LANES), axis=-1, keepdims=True)
        rest = jnp.where(lane == i1, neg, logits)
        m2 = jnp.max(rest, axis=-1, keepdims=True)
        i2 = jnp.min(jnp.where(rest == m2, lane, LANES), axis=-1, keepdims=True)
        e = jnp.exp(m2 - m1)
        w1 = 1.0 / (1.0 + e)
        w2 = e / (1.0 + e)
        route_ref[r, :] = jnp.where(lane == 0, i1.astype(F32),
                                    jnp.where(lane == 1, i2.astype(F32),
                                              jnp.where(lane == 2, w1, jnp.where(lane == 3, w2, 0.0))))


def _odd_mix(yc, yd, h, wo_a, wo_b, g1, g2, wr, tm):
    n, d = h.shape
    const = lambda *shape: pl.BlockSpec(shape, lambda i: (0,) * len(shape))
    return pl.pallas_call(
        _odd_mix_kernel,
        grid=(n // tm,),
        in_specs=[pl.BlockSpec((tm, yc.shape[1]), lambda i: (i, 0)),
                  pl.BlockSpec((tm, yd.shape[1]), lambda i: (i, 0)),
                  pl.BlockSpec((tm, d), lambda i: (i, 0)),
                  const(*wo_a.shape), const(*wo_b.shape), const(1, d), const(1, d), const(d, LANES)],
        out_specs=[pl.BlockSpec((tm, d), lambda i: (i, 0)),
                   pl.BlockSpec((tm * ROW_TILE, LANES), lambda i: (i, 0)),
                   pl.BlockSpec((tm, LANES), lambda i: (i, 0))],
        out_shape=[jax.ShapeDtypeStruct((n, d), F32), jax.ShapeDtypeStruct((n * ROW_TILE, LANES), F32),
                   jax.ShapeDtypeStruct((n, LANES), F32)],
        compiler_params=_params(("parallel",)),
        name="odd_mix_router",
    )(yc, yd, h, wo_a, wo_b, g1, g2, wr)


def _store_row_tiles(ref, x):
    rows = x.shape[0]
    for s in range(ROW_TILE):
        ref[pl.ds(s, rows, stride=ROW_TILE), :] = x[:, s * LANES:(s + 1) * LANES]


def _load_row_tiles(ref, rows):
    return [ref[pl.ds(s, rows, stride=ROW_TILE), :] for s in range(ROW_TILE)]


def _gather_rows(idx_ref, base, n_rows, src_hbm, dst_ref, sem):
    def body(r, c):
        src = pl.multiple_of(idx_ref[base + r] * ROW_TILE, ROW_TILE)
        dst = pl.multiple_of(r * ROW_TILE, ROW_TILE)
        pltpu.make_async_copy(src_hbm.at[pl.ds(src, ROW_TILE), :], dst_ref.at[pl.ds(dst, ROW_TILE), :], sem).start()
        return c

    lax.fori_loop(0, n_rows, body, 0, unroll=8)


def _wait_rows(src_hbm, dst_ref, sem):
    pltpu.make_async_copy(src_hbm.at[pl.ds(0, dst_ref.shape[0]), :], dst_ref, sem).wait()


def _row_copy(src_ref, src_row, dst_ref, dst_row, sem):
    src = pl.multiple_of(src_row * ROW_TILE, ROW_TILE)
    dst = pl.multiple_of(dst_row * ROW_TILE, ROW_TILE)
    return pltpu.make_async_copy(src_ref.at[pl.ds(src, ROW_TILE), :], dst_ref.at[pl.ds(dst, ROW_TILE), :], sem)


def _moe_ffn_kernel(te_ref, nu_ref, tok_ref, dst_ref, z_hbm, wg_ref, wu_ref, wd_ref, y_hbm,
                    xraw_ref, xb_ref, acc_ref, yst_ref, gsem, ssem, *, rows_per_step):
    i = pl.program_id(0)
    j = pl.program_id(1)
    tm = xb_ref.shape[0]
    stride = yst_ref.shape[0] // ROW_TILE
    n_used = nu_ref[0]
    slot = i % 2
    first = j == 0
    last = j == pl.num_programs(1) - 1

    @pl.when(first & (i == 0))
    def _():
        yst_ref[...] = jnp.zeros_like(yst_ref)
        _gather_rows(tok_ref, 0, stride, z_hbm, xraw_ref.at[0], gsem.at[0])

    @pl.when(first & (i <= n_used))
    def _():
        _wait_rows(z_hbm, xraw_ref.at[slot], gsem.at[slot])

    @pl.when(first & (i < n_used))
    def _():
        for s, blk in enumerate(_load_row_tiles(xraw_ref.at[slot], tm)):
            xb_ref[:, s * LANES:(s + 1) * LANES] = blk.astype(BF16)
        acc_ref[...] = jnp.zeros_like(acc_ref)

    @pl.when(first & (i == n_used))
    def _():
        def body(r, c):
            _row_copy(yst_ref, r, y_hbm, dst_ref[i * stride + r], ssem).start()
            return c
        lax.fori_loop(0, stride, body, 0, unroll=8)

    @pl.when(i < n_used)
    def _():
        x = xb_ref[...]
        g = _dot(x, wg_ref[0, 0])
        nxt = xraw_ref.at[1 - slot]
        for rr in range(rows_per_step):
            r = j * rows_per_step + rr
            _row_copy(z_hbm, tok_ref[(i + 1) * stride + r], nxt, r, gsem.at[1 - slot]).start(priority=rr % 2)
        u = _dot(x, wu_ref[0, 0])
        for rr in range(rows_per_step):
            r = j * rows_per_step + rr
            _row_copy(yst_ref, r, y_hbm, dst_ref[i * stride + r], ssem).start(priority=rr % 2)
        a = jax.nn.silu(g) * u
        acc_ref[...] += _dot(a.astype(BF16), wd_ref[0])

    @pl.when(last & (i <= n_used))
    def _():
        _wait_rows(z_hbm, yst_ref, ssem)

    @pl.when(last & (i < n_used))
    def _():
        _store_row_tiles(yst_ref, acc_ref[...])


def _moe_ffn(tile_expert, n_used, tok_tab, dst_tab, z_tiles, wg, wu, wd, n_tok, tm):
    n_f, d, tf = wg.shape[1], wg.shape[2], wg.shape[3]
    rows_per_step = -(-tm // n_f)
    stride = n_f * rows_per_step
    n_tiles = tile_expert.shape[0]
    assert tok_tab.shape[0] == dst_tab.shape[0] == (n_tiles + 1) * stride

    def col(i, j, nu):
        return jnp.where(i < nu[0], j, n_f - 1)

    return pl.pallas_call(
        functools.partial(_moe_ffn_kernel, rows_per_step=rows_per_step),
        grid_spec=pltpu.PrefetchScalarGridSpec(
            num_scalar_prefetch=4,
            grid=(n_tiles, n_f),
            in_specs=[pl.BlockSpec(memory_space=pl.ANY),
                      pl.BlockSpec((1, 1, d, tf), lambda i, j, te, nu, tok, dst: (te[i], col(i, j, nu), 0, 0)),
                      pl.BlockSpec((1, 1, d, tf), lambda i, j, te, nu, tok, dst: (te[i], col(i, j, nu), 0, 0)),
                      pl.BlockSpec((1, tf, d), lambda i, j, te, nu, tok, dst: (te[i], col(i, j, nu), 0))],
            out_specs=pl.BlockSpec(memory_space=pl.ANY),
            scratch_shapes=[pltpu.VMEM((2, stride * ROW_TILE, LANES), F32), pltpu.VMEM((tm, d), BF16),
                            pltpu.VMEM((tm, d), F32), pltpu.VMEM((stride * ROW_TILE, LANES), F32),
                            pltpu.SemaphoreType.DMA((2,)), pltpu.SemaphoreType.DMA(())]),
        out_shape=jax.ShapeDtypeStruct(((2 * n_tok + stride) * ROW_TILE, LANES), F32),
        compiler_params=_params(("arbitrary", "arbitrary")),
        name="moe_grouped_ffn",
    )(tile_expert, n_used, tok_tab, dst_tab, z_tiles, wg, wu, wd)


def _combine_kernel(ya_ref, yb_ref, route_ref, h_ref, g_ref, o_ref):
    tm = h_ref.shape[0]
    route = route_ref[...]
    a = jnp.concatenate(_load_row_tiles(ya_ref, tm), axis=1)
    b = jnp.concatenate(_load_row_tiles(yb_ref, tm), axis=1)
    f = route[:, 2:3] * a + route[:, 3:4] * b
    o_ref[...] = h_ref[...] + _rms(f, g_ref[...])


def _combine(y, route, h, g, tm):
    n, d = h.shape
    n_blk = n // tm
    return pl.pallas_call(
        _combine_kernel,
        grid=(n_blk,),
        in_specs=[pl.BlockSpec((tm * ROW_TILE, LANES), lambda i: (i, 0)),
                  pl.BlockSpec((tm * ROW_TILE, LANES), lambda i: (n_blk + i, 0)),
                  pl.BlockSpec((tm, LANES), lambda i: (i, 0)),
                  pl.BlockSpec((tm, d), lambda i: (i, 0)),
                  pl.BlockSpec((1, d), lambda i: (0, 0))],
        out_specs=pl.BlockSpec((tm, d), lambda i: (i, 0)),
        out_shape=jax.ShapeDtypeStruct((n, d), F32),
        compiler_params=_params(("parallel",)),
        name="moe_combine",
    )(y, y, route, h, g)


def _moe_plan(route, tm, stride):
    n = route.shape[0]
    eids = jnp.concatenate([route[:, 0], route[:, 1]]).astype(jnp.int32)
    onehot = (eids[:, None] == jnp.arange(N_EXPERTS, dtype=jnp.int32)[None, :]).astype(jnp.int32)
    csum = jnp.cumsum(onehot, axis=0)
    rank = jnp.sum(csum * onehot, axis=1) - 1
    counts = csum[-1]
    padded = ((counts + tm - 1) // tm) * tm
    ends = jnp.cumsum(padded)
    starts = ends - padded
    slot = jnp.sum(onehot * starts[None, :], axis=1) + rank
    n_tiles = 2 * n // tm + N_EXPERTS + 1
    copy_of_slot = jnp.full((n_tiles * tm,), -1, jnp.int32).at[slot].set(jnp.arange(2 * n, dtype=jnp.int32))
    copy_tab = jnp.pad(copy_of_slot.reshape(n_tiles, tm), ((0, 1), (0, stride - tm)), constant_values=-1)
    tok_tab = jnp.where(copy_tab >= 0, copy_tab % n, 0)
    dump = 2 * n + jnp.arange(stride, dtype=jnp.int32)[None, :]
    dst_tab = jnp.where(copy_tab >= 0, copy_tab, dump)
    dst_tab = jnp.concatenate([jnp.broadcast_to(dump, (1, stride)), dst_tab[:-1]], axis=0)
    n_used = (ends[-1] // tm).astype(jnp.int32)
    tile_start = jnp.minimum(jnp.arange(n_tiles, dtype=jnp.int32), n_used - 1) * tm
    tile_expert = jnp.sum((tile_start[:, None] >= ends[None, :]).astype(jnp.int32), axis=1)
    return tok_tab.reshape(-1), dst_tab.reshape(-1), tile_expert, n_used.reshape(1)


def _odd_weights(od_w_in, mla_w_uq, mla_w_ukv):
    c2 = 2 * CONV_CH + MLA_Q_RANK + MLA_KV_RANK
    half = MLA_ROPE // 2
    w_kr = od_w_in[:, c2:]
    w_kr_sw = jnp.concatenate([w_kr[:, half:], w_kr[:, :half]], axis=1)
    zl = jnp.zeros((D_MODEL, MLA_NOPE), F32)
    zr = jnp.zeros((D_MODEL, MLA_PAD - MLA_NOPE - MLA_ROPE), F32)
    win = jnp.concatenate([od_w_in[:, :c2], zl, w_kr, zr, zl, w_kr_sw, zr], axis=1)
    dk = MLA_NOPE + MLA_ROPE
    wq = mla_w_uq.reshape(MLA_Q_RANK, MLA_HEADS, dk)
    zq = jnp.zeros((MLA_Q_RANK, MLA_HEADS, MLA_PAD - dk), F32)
    wuq = jnp.concatenate([wq, zq], axis=2).reshape(MLA_Q_RANK, MLA_HEADS * MLA_PAD)
    wq_sw = jnp.concatenate([jnp.zeros_like(wq[:, :, :MLA_NOPE]), wq[:, :, MLA_NOPE + half:],
                             wq[:, :, MLA_NOPE:MLA_NOPE + half], zq], axis=2)
    wuqs = wq_sw.reshape(MLA_Q_RANK, MLA_HEADS * MLA_PAD)
    wkv = mla_w_ukv.reshape(MLA_KV_RANK, MLA_HEADS, MLA_NOPE + MLA_V)
    zk = jnp.zeros((MLA_KV_RANK, MLA_HEADS, MLA_PAD - MLA_NOPE), F32)
    wuk = jnp.concatenate([wkv[:, :, :MLA_NOPE], zk], axis=2).reshape(MLA_KV_RANK, MLA_HEADS * MLA_PAD)
    zv = jnp.zeros((MLA_KV_RANK, MLA_HEADS, MLA_VROWS - MLA_V), F32)
    wuv_t = jnp.concatenate([wkv[:, :, MLA_NOPE:], zv], axis=2).reshape(MLA_KV_RANK, MLA_HEADS * MLA_VROWS).T
    v_one = jnp.zeros((MLA_HEADS, MLA_VROWS), F32).at[:, MLA_V].set(1.0).reshape(MLA_HEADS * MLA_VROWS, 1)
    return win.astype(BF16), wuq.astype(BF16), wuqs.astype(BF16), wuk.astype(BF16), wuv_t.astype(BF16), v_one


def _rope_tables(seq):
    inv = 1.0 / (ROPE_THETA ** (jnp.arange(0, MLA_ROPE, 2, dtype=F32) / MLA_ROPE))
    ang = jnp.arange(seq, dtype=F32)[:, None] * inv[None, :]
    cos, sin = jnp.cos(ang), jnp.sin(ang)
    ones = jnp.ones((seq, MLA_NOPE), F32)
    zl = jnp.zeros((seq, MLA_NOPE), F32)
    zr = jnp.zeros((seq, MLA_PAD - MLA_NOPE - MLA_ROPE), F32)
    return (jnp.concatenate([ones, cos, cos, zr], axis=1), jnp.concatenate([zl, -sin, sin, zr], axis=1))


def kernel(x, norm_g, ev_w_in, ssm_lambda_re, ssm_lambda_im, ssm_log_dt, ssm_b_re, ssm_b_im, ssm_c_re, ssm_c_im, ssm_d, ssm_w_glu, sgu_ln_g, sgu_ln_b, sgu_w, sgu_b, ev_w_out, ffn_w_gate, ffn_w_up, ffn_w_down, od_w_in, conv_w, conv_b, conv_ln_g, conv_ln_b, mla_q_norm_g, mla_w_uq, mla_kv_norm_g, mla_w_ukv, od_w_out, moe_w_router, moe_w_gate, moe_w_up, moe_w_down):
    bsz, seq, d = x.shape
    n = bsz * seq
    assert d == D_MODEL and SUBLANES % bsz == 0 and seq % 512 == 0
    row = lambda v: v.astype(F32).reshape(1, -1)
    h = x.astype(F32).reshape(n, d)
    tm = 512

    g = norm_g[0]
    a_in, proj = _norm_proj(h, row(g[0]), ev_w_in[0].astype(BF16), tm)
    mats = _s5_matrices(ssm_lambda_re[0], ssm_lambda_im[0], ssm_log_dt[0], ssm_b_re[0], ssm_b_im[0],
                        ssm_c_re[0], ssm_c_im[0])
    ys = _s5_mixer(a_in, mats, ssm_d[0], bsz, seq)
    causal = jnp.tril(jnp.ones((SGU_CHUNK, SGU_CHUNK), dtype=bool))
    ws = jnp.where(causal[None], sgu_w[0], 0.0).astype(BF16)
    bias = jnp.repeat(sgu_b[0].astype(F32).T, SGU_HEAD_DIM, axis=1)
    wo = ev_w_out[0].astype(BF16)
    h, z = _even_mix(ys, proj, h, ssm_w_glu[0].astype(BF16), row(sgu_ln_g[0]), row(sgu_ln_b[0]), ws, bias,
                     wo[:SSM_WIDTH], wo[SSM_WIDTH:], row(g[1]), row(g[2]), tm)
    h, z = _dense_ffn(z, ffn_w_gate[0].astype(BF16), ffn_w_up[0].astype(BF16), ffn_w_down[0].astype(BF16),
                      h, row(g[3]), row(norm_g[1][0]), 1024, 512)

    g = norm_g[1]
    win, wuq, wuqs, wuk, wuv_t, v_one = _odd_weights(od_w_in[0], mla_w_uq[0], mla_w_ukv[0])
    cos_t, sin_t = _rope_tables(seq)
    zc, q, k, vt = _odd_proj(z, win, row(mla_q_norm_g[0]), row(mla_kv_norm_g[0]), wuq, wuqs, wuk, wuv_t, v_one,
                             cos_t, sin_t, seq, tm)
    hp = MLA_HEADS * MLA_PAD
    tm_moe, tf_moe = 1024, 512
    yd, wg_b, wu_b, wd_b = _attention(q.reshape(bsz, seq, hp), k.reshape(bsz, seq, hp), vt, tm,
                                      moe_w_gate[0], moe_w_up[0], moe_w_down[0], tf_moe)
    conv_w_pad = jnp.concatenate([conv_w[0].astype(F32), jnp.zeros((CONV_HALO - CONV_TAPS, CONV_CH), F32)], axis=0)
    yc = _conv_mixer(zc.reshape(bsz, seq, 2 * CONV_CH), conv_w_pad, row(conv_b[0]), row(conv_ln_g[0]),
                     row(conv_ln_b[0]), tm)
    wo = od_w_out[0].astype(BF16)
    wr = jnp.concatenate([moe_w_router[0].astype(F32), jnp.zeros((d, LANES - N_EXPERTS), F32)], axis=1)
    h, z, route = _odd_mix(yc.reshape(n, CONV_CH), yd.reshape(n, MLA_HEADS * MLA_V), h, wo[:CONV_CH], wo[CONV_CH:],
                           row(g[1]), row(g[2]), wr.astype(BF16), tm)
    n_f = wg_b.shape[1]
    tok_tab, dst_tab, tile_expert, n_used = _moe_plan(route, tm_moe, n_f * -(-tm_moe // n_f))
    y = _moe_ffn(tile_expert, n_used, tok_tab, dst_tab, z, wg_b, wu_b, wd_b, n, tm_moe)
    h = _combine(y, route, h, row(g[3]), 256)
    return h.reshape(bsz, seq, d).astype(x.dtype)
```

```python
import functools
import math

import jax
import jax.numpy as jnp
from jax import lax
from jax.experimental import pallas as pl
from jax.experimental.pallas import tpu as pltpu

F32 = jnp.float32
BF16 = jnp.bfloat16

D_MODEL = 1024
NORM_EPS = 1e-6
SSM_WIDTH = 512
SSM_GROUP = 16
SSM_GROUPS = 32
SSM_STATE = 64
SSM_CHUNK = 16
SSM_PAIR = 2 * SSM_GROUP * SSM_CHUNK
SGU_WIDTH = 512
SGU_HEADS = 8
SGU_HEAD_DIM = 64
SGU_CHUNK = 128
CONV_CH = 512
CONV_TAPS = 31
CONV_HALO = 32
MLA_HEADS = 8
MLA_Q_RANK = 256
MLA_KV_RANK = 128
MLA_NOPE = 64
MLA_ROPE = 32
MLA_V = 64
MLA_PAD = 128
MLA_VROWS = 80
ATTN_HEADS = 4
ROPE_THETA = 10000.0
FF_DENSE = 4096
N_EXPERTS = 8
FF_EXPERT = 3584
LANES = 128
SUBLANES = 8
ROW_TILE = D_MODEL // LANES
VMEM_LIMIT = 56 * 1024 * 1024


def _params(sem, vmem=VMEM_LIMIT):
    return pltpu.CompilerParams(dimension_semantics=sem, vmem_limit_bytes=vmem)


def _rms(x, g):
    return x * lax.rsqrt(jnp.mean(x * x, axis=-1, keepdims=True) + NORM_EPS) * g


def _layer_norm(x, g, b):
    mu = jnp.mean(x, axis=-1, keepdims=True)
    xc = x - mu
    return xc * lax.rsqrt(jnp.mean(xc * xc, axis=-1, keepdims=True) + NORM_EPS) * g + b


def _dot(a, b):
    return jnp.dot(a, b, preferred_element_type=F32)


def _norm_proj_kernel(h_ref, g_ref, w_ref, a_ref, b_ref):
    z = _rms(h_ref[...], g_ref[...])
    proj = _dot(z.astype(BF16), w_ref[...])
    for jb in range(SSM_WIDTH // LANES):
        a_ref[jb] = proj[:, jb * LANES:(jb + 1) * LANES]
    b_ref[...] = proj[:, SSM_WIDTH:].astype(b_ref.dtype)


def _norm_proj(h, g, w, tm):
    n, d = h.shape
    cols = w.shape[1]
    return pl.pallas_call(
        _norm_proj_kernel,
        grid=(n // tm,),
        in_specs=[pl.BlockSpec((tm, d), lambda i: (i, 0)),
                  pl.BlockSpec((1, d), lambda i: (0, 0)),
                  pl.BlockSpec((d, cols), lambda i: (0, 0))],
        out_specs=[pl.BlockSpec((SSM_WIDTH // LANES, tm, LANES), lambda i: (0, i, 0)),
                   pl.BlockSpec((tm, cols - SSM_WIDTH), lambda i: (i, 0))],
        out_shape=[jax.ShapeDtypeStruct((SSM_WIDTH // LANES, n, LANES), F32),
                   jax.ShapeDtypeStruct((n, cols - SSM_WIDTH), BF16)],
        compiler_params=_params(("parallel",)),
        name="even_in_proj",
    )(h, g, w)


def _s5_matrices(lam_re, lam_im, log_dt, b_re, b_im, c_re, c_im):
    t = SSM_CHUNK
    lr = jnp.minimum(lam_re.astype(F32), -1e-4)
    li = lam_im.astype(F32)
    dt = jnp.exp(log_dt.astype(F32))[:, None]
    mag = jnp.exp(lr * dt)
    a_re = mag * jnp.cos(li * dt)
    a_im = mag * jnp.sin(li * dt)
    den = lr * lr + li * li
    nr = a_re - 1.0
    coef_re = (nr * lr + a_im * li) / den
    coef_im = (a_im * lr - nr * li) / den
    br = b_re.astype(F32)
    bi = b_im.astype(F32)
    bb_re = coef_re[..., None] * br - coef_im[..., None] * bi
    bb_im = coef_re[..., None] * bi + coef_im[..., None] * br
    cr = c_re.astype(F32)
    ci = c_im.astype(F32)
    pw_re = [jnp.ones_like(a_re)]
    pw_im = [jnp.zeros_like(a_im)]
    for _ in range(t):
        pr, pi = pw_re[-1], pw_im[-1]
        pw_re.append(pr * a_re - pi * a_im)
        pw_im.append(pr * a_im + pi * a_re)
    pw_re = jnp.stack(pw_re)
    pw_im = jnp.stack(pw_im)
    ab_re = pw_re[:t, :, :, None] * bb_re[None] - pw_im[:t, :, :, None] * bb_im[None]
    ab_im = pw_re[:t, :, :, None] * bb_im[None] + pw_im[:t, :, :, None] * bb_re[None]
    hi = lax.Precision.HIGHEST
    k_lag = (jnp.einsum('gnp,tgpm->tgnm', cr, ab_re, precision=hi)
             - jnp.einsum('gnp,tgpm->tgnm', ci, ab_im, precision=hi))
    n_pairs = SSM_GROUPS // 2
    st = 2 * SSM_STATE

    def pair_diag(w):
        w = w.reshape((n_pairs, 2) + w.shape[1:])
        z = jnp.zeros_like(w[:, 0])
        top = jnp.concatenate([w[:, 0], z], axis=-1)
        bot = jnp.concatenate([z, w[:, 1]], axis=-1)
        return jnp.concatenate([top, bot], axis=-2)

    k_blk = pair_diag(k_lag.transpose(1, 0, 3, 2))
    rev_re = pw_re[:t][::-1]
    rev_im = pw_im[:t][::-1]
    ws_re = rev_re[..., None] * bb_re[None] - rev_im[..., None] * bb_im[None]
    ws_im = rev_re[..., None] * bb_im[None] + rev_im[..., None] * bb_re[None]
    ws_re = pair_diag(ws_re.transpose(1, 0, 3, 2)).reshape(n_pairs, SSM_PAIR, st).astype(BF16)
    ws_im = pair_diag(ws_im.transpose(1, 0, 3, 2)).reshape(n_pairs, SSM_PAIR, st).astype(BF16)
    ca_re = cr[None] * pw_re[1:, :, None, :] - ci[None] * pw_im[1:, :, None, :]
    ca_im = cr[None] * pw_im[1:, :, None, :] + ci[None] * pw_re[1:, :, None, :]
    co_re = pair_diag(ca_re.transpose(1, 0, 3, 2))
    co_im = pair_diag((-ca_im).transpose(1, 0, 3, 2))
    w_intra, wo_re, wo_im = _s5_expand(k_blk, co_re, co_im)
    return dict(
        w_intra=w_intra, ws_re=ws_re, ws_im=ws_im, wo_re=wo_re, wo_im=wo_im,
        at_re=pw_re[t].reshape(1, SSM_GROUPS * SSM_STATE), at_im=pw_im[t].reshape(1, SSM_GROUPS * SSM_STATE))


def _s5_expand_kernel(k_ref, cre_ref, cim_ref, wi_ref, wore_ref, woim_ref, kcat_ref):
    pw = 2 * SSM_GROUP
    for tau in range(SSM_CHUNK):
        kcat_ref[:, tau * pw:(tau + 1) * pw] = k_ref[0, tau]
        wore_ref[0, :, tau * pw:(tau + 1) * pw] = cre_ref[0, tau].astype(wore_ref.dtype)
        woim_ref[0, :, tau * pw:(tau + 1) * pw] = cim_ref[0, tau].astype(woim_ref.dtype)
    kcat = kcat_ref[...]
    col = lax.broadcasted_iota(jnp.int32, kcat.shape, 1)
    for s in range(SSM_CHUNK):
        blk = kcat if s == 0 else jnp.where(col >= s * pw, pltpu.roll(kcat, s * pw, 1), 0.0)
        wi_ref[0, s * pw:(s + 1) * pw, :] = blk.astype(wi_ref.dtype)


def _s5_expand(k_blk, co_re, co_im):
    n_pairs = k_blk.shape[0]
    pw = 2 * SSM_GROUP
    st = 2 * SSM_STATE
    return pl.pallas_call(
        _s5_expand_kernel,
        grid=(n_pairs,),
        in_specs=[pl.BlockSpec((1, SSM_CHUNK, pw, pw), lambda q: (q, 0, 0, 0)),
                  pl.BlockSpec((1, SSM_CHUNK, st, pw), lambda q: (q, 0, 0, 0)),
                  pl.BlockSpec((1, SSM_CHUNK, st, pw), lambda q: (q, 0, 0, 0))],
        out_specs=[pl.BlockSpec((1, SSM_PAIR, SSM_PAIR), lambda q: (q, 0, 0)),
                   pl.BlockSpec((1, st, SSM_PAIR), lambda q: (q, 0, 0)),
                   pl.BlockSpec((1, st, SSM_PAIR), lambda q: (q, 0, 0))],
        out_shape=[jax.ShapeDtypeStruct((n_pairs, SSM_PAIR, SSM_PAIR), BF16),
                   jax.ShapeDtypeStruct((n_pairs, st, SSM_PAIR), BF16),
                   jax.ShapeDtypeStruct((n_pairs, st, SSM_PAIR), BF16)],
        scratch_shapes=[pltpu.VMEM((pw, SSM_PAIR), F32)],
        compiler_params=_params(("parallel",)),
        name="s5_expand_weights",
    )(k_blk, co_re, co_im)


S5_LANE_PAIRS = LANES // (2 * SSM_GROUP)
S5_SCAN_LANES = 512


def _s5_state_kernel(u0_ref, u1_ref, u2_ref, u3_ref, wre_ref, wim_ref, are_ref, aim_ref,
                     x_ref, hre_ref, him_ref, sre_ref, sim_ref):
    n_chunks = x_ref.shape[0]
    pw = 2 * SSM_GROUP
    u_refs = (u0_ref, u1_ref, u2_ref, u3_ref)
    for t in range(SSM_CHUNK):
        for j, u_ref in enumerate(u_refs):
            ut = u_ref[pl.ds(t, n_chunks, stride=SSM_CHUNK), :]
            for qq in range(S5_LANE_PAIRS):
                q = j * S5_LANE_PAIRS + qq
                x_ref[:, q * SSM_PAIR + t * pw: q * SSM_PAIR + (t + 1) * pw] = (
                    ut[:, qq * pw:(qq + 1) * pw].astype(x_ref.dtype))
    st = 2 * SSM_STATE
    for q in range(SSM_GROUPS // 2):
        xq = x_ref[:, q * SSM_PAIR:(q + 1) * SSM_PAIR]
        sre_ref[:, q * st:(q + 1) * st] = _dot(xq, wre_ref[q])
        sim_ref[:, q * st:(q + 1) * st] = _dot(xq, wim_ref[q])

    row = lax.broadcasted_iota(jnp.int32, (SUBLANES, S5_SCAN_LANES), 0)
    zero = jnp.zeros((SUBLANES, S5_SCAN_LANES), F32)
    for c0 in range(0, sre_ref.shape[1], S5_SCAN_LANES):
        cols = pl.ds(c0, S5_SCAN_LANES)
        ar = are_ref[:, cols]
        ai = aim_ref[:, cols]

        def body(k, carry, cols=cols, ar=ar, ai=ai):
            r0 = pl.multiple_of(k * SUBLANES, SUBLANES)
            sr = sre_ref[pl.ds(r0, SUBLANES), cols]
            si = sim_ref[pl.ds(r0, SUBLANES), cols]
            out_r, out_i = carry
            for i in range(1, SUBLANES + 1):
                tr = ar * out_r - ai * out_i + sr
                ti = ar * out_i + ai * out_r + si
                tr = pltpu.roll(tr, 1, 0)
                ti = pltpu.roll(ti, 1, 0)
                if i < SUBLANES:
                    out_r = jnp.where(row == i, tr, out_r)
                    out_i = jnp.where(row == i, ti, out_i)
            hre_ref[pl.ds(r0, SUBLANES), cols] = out_r
            him_ref[pl.ds(r0, SUBLANES), cols] = out_i
            return tr, ti

        lax.fori_loop(0, n_chunks // SUBLANES, body, (zero, zero))


def _s5_out_kernel(x_ref, wi_ref, hre_ref, him_ref, wore_ref, woim_ref, d_ref, y_ref, yt_ref):
    n_chunks = x_ref.shape[0]
    pw = 2 * SSM_GROUP
    st = 2 * SSM_STATE
    for qq in range(S5_LANE_PAIRS):
        x = x_ref[:, qq * SSM_PAIR:(qq + 1) * SSM_PAIR]
        y = _dot(x, wi_ref[qq])
        y += _dot(hre_ref[:, qq * st:(qq + 1) * st].astype(BF16), wore_ref[qq])
        y += _dot(him_ref[:, qq * st:(qq + 1) * st].astype(BF16), woim_ref[qq])
        y += d_ref[:, qq * SSM_PAIR:(qq + 1) * SSM_PAIR] * x.astype(F32)
        y = jax.nn.gelu(y)
        for t in range(SSM_CHUNK):
            yt_ref[t, :, qq * pw:(qq + 1) * pw] = y[:, t * pw:(t + 1) * pw]
    for t in range(SSM_CHUNK):
        y_ref[pl.ds(t, n_chunks, stride=SSM_CHUNK), :] = yt_ref[t]


def _s5_mixer(u, mats, d, batch, seq):
    t = SSM_CHUNK
    n_chunks = seq // t
    n_pairs = SSM_GROUPS // 2
    cols = n_pairs * SSM_PAIR
    st = 2 * SSM_STATE
    n_state = n_pairs * st
    n_blk = SSM_WIDTH // LANES
    assert n_blk == 4 and n_chunks % SUBLANES == 0
    once = pl.Buffered(1)
    x, h_re, h_im = pl.pallas_call(
        _s5_state_kernel,
        grid=(batch,),
        in_specs=[pl.BlockSpec((None, seq, LANES), lambda b, j=j: (j, b, 0)) for j in range(n_blk)] + [
            pl.BlockSpec((n_pairs, SSM_PAIR, st), lambda b: (0, 0, 0), pipeline_mode=once),
            pl.BlockSpec((n_pairs, SSM_PAIR, st), lambda b: (0, 0, 0), pipeline_mode=once),
            pl.BlockSpec((1, n_state), lambda b: (0, 0)),
            pl.BlockSpec((1, n_state), lambda b: (0, 0))],
        out_specs=[pl.BlockSpec((n_chunks, cols), lambda b: (b, 0)),
                   pl.BlockSpec((n_chunks, n_state), lambda b: (b, 0)),
                   pl.BlockSpec((n_chunks, n_state), lambda b: (b, 0))],
        out_shape=[jax.ShapeDtypeStruct((batch * n_chunks, cols), BF16),
                   jax.ShapeDtypeStruct((batch * n_chunks, n_state), F32),
                   jax.ShapeDtypeStruct((batch * n_chunks, n_state), F32)],
        scratch_shapes=[pltpu.VMEM((n_chunks, n_state), F32), pltpu.VMEM((n_chunks, n_state), F32)],
        compiler_params=_params(("parallel",)),
        name="s5_state_scan",
    )(u, u, u, u, mats['ws_re'], mats['ws_im'], mats['at_re'], mats['at_im'])
    lp = S5_LANE_PAIRS
    d_cols = jnp.broadcast_to(d.astype(F32).reshape(n_pairs, 1, 2 * SSM_GROUP),
                              (n_pairs, t, 2 * SSM_GROUP)).reshape(1, cols)
    return pl.pallas_call(
        _s5_out_kernel,
        grid=(batch, n_blk),
        in_specs=[pl.BlockSpec((n_chunks, lp * SSM_PAIR), lambda b, j: (b, j)),
                  pl.BlockSpec((lp, SSM_PAIR, SSM_PAIR), lambda b, j: (j, 0, 0)),
                  pl.BlockSpec((n_chunks, lp * st), lambda b, j: (b, j)),
                  pl.BlockSpec((n_chunks, lp * st), lambda b, j: (b, j)),
                  pl.BlockSpec((lp, st, SSM_PAIR), lambda b, j: (j, 0, 0)),
                  pl.BlockSpec((lp, st, SSM_PAIR), lambda b, j: (j, 0, 0)),
                  pl.BlockSpec((1, lp * SSM_PAIR), lambda b, j: (0, j))],
        out_specs=pl.BlockSpec((None, seq, LANES), lambda b, j: (j, b, 0)),
        out_shape=jax.ShapeDtypeStruct((n_blk, batch * seq, LANES), F32),
        scratch_shapes=[pltpu.VMEM((t, n_chunks, LANES), F32)],
        compiler_params=_params(("parallel", "parallel")),
        name="s5_out",
    )(x, mats['w_intra'], h_re, h_im, mats['wo_re'], mats['wo_im'], d_cols)


def _even_mix_kernel(ys_ref, bu_ref, bv_ref, h_ref, wglu_ref, lng_ref, lnb_ref, ws_ref, bias_ref,
                     wo_a_ref, wo_b_ref, g1_ref, g2_ref, hout_ref, z_ref, s_scr):
    tm = h_ref.shape[0]
    ys = jnp.concatenate([ys_ref[jb] for jb in range(ys_ref.shape[0])], axis=1)
    ya = ys * jax.nn.sigmoid(_dot(ys.astype(BF16), wglu_ref[...]))
    u = jax.nn.gelu(bu_ref[...].astype(F32))
    v = _layer_norm(jax.nn.gelu(bv_ref[...].astype(F32)), lng_ref[...], lnb_ref[...])
    lane = lax.broadcasted_iota(jnp.int32, v.shape, 1)
    left = (lane % LANES) < SGU_HEAD_DIM
    v_l = jnp.where(left, v, 0.0).astype(BF16)
    v_r = jnp.where(left, 0.0, v).astype(BF16)
    for c in range(tm // SGU_CHUNK):
        rows = slice(c * SGU_CHUNK, (c + 1) * SGU_CHUNK)
        for p in range(SGU_HEADS // 2):
            cols = slice(p * LANES, (p + 1) * LANES)
            s_scr[rows, cols] = (_dot(ws_ref[2 * p], v_l[rows, cols]) + _dot(ws_ref[2 * p + 1], v_r[rows, cols]))
    bias = jnp.concatenate([bias_ref[...]] * (tm // SGU_CHUNK), axis=0)
    yb = u * (s_scr[...] + bias)
    mix = _dot(ya.astype(BF16), wo_a_ref[...]) + _dot(yb.astype(BF16), wo_b_ref[...])
    h_new = h_ref[...] + _rms(mix, g1_ref[...])
    hout_ref[...] = h_new
    z_ref[...] = _rms(h_new, g2_ref[...]).astype(z_ref.dtype)


def _even_mix(ys, proj, h, wglu, lng, lnb, ws, bias, wo_a, wo_b, g1, g2, tm):
    n, d = h.shape
    w = SGU_WIDTH
    const = lambda *shape: pl.BlockSpec(shape, lambda i: (0,) * len(shape))
    return pl.pallas_call(
        _even_mix_kernel,
        grid=(n // tm,),
        in_specs=[pl.BlockSpec((w // LANES, tm, LANES), lambda i: (0, i, 0)),
                  pl.BlockSpec((tm, w), lambda i: (i, 0)),
                  pl.BlockSpec((tm, w), lambda i: (i, 1)),
                  pl.BlockSpec((tm, d), lambda i: (i, 0)),
                  const(w, w), const(1, w), const(1, w),
                  const(SGU_HEADS, SGU_CHUNK, SGU_CHUNK), const(SGU_CHUNK, w),
                  const(w, d), const(w, d), const(1, d), const(1, d)],
        out_specs=[pl.BlockSpec((tm, d), lambda i: (i, 0)),
                   pl.BlockSpec((tm, d), lambda i: (i, 0))],
        out_shape=[jax.ShapeDtypeStruct((n, d), F32), jax.ShapeDtypeStruct((n, d), BF16)],
        scratch_shapes=[pltpu.VMEM((tm, w), F32)],
        compiler_params=_params(("parallel",)),
        name="even_mix",
    )(ys, proj, proj, h, wglu, lng, lnb, ws, bias, wo_a, wo_b, g1, g2)


def _ffn_kernel(z_ref, wg_ref, wu_ref, wd_ref, h_ref, g3_ref, gn_ref, hout_ref, zout_ref, acc_ref):
    j = pl.program_id(1)

    @pl.when(j == 0)
    def _():
        acc_ref[...] = jnp.zeros_like(acc_ref)

    z = z_ref[...]
    a = jax.nn.silu(_dot(z, wg_ref[...])) * _dot(z, wu_ref[...])
    acc_ref[...] += _dot(a.astype(BF16), wd_ref[...])

    @pl.when(j == pl.num_programs(1) - 1)
    def _():
        h_new = h_ref[...] + _rms(acc_ref[...], g3_ref[...])
        hout_ref[...] = h_new
        zout_ref[...] = _rms(h_new, gn_ref[...]).astype(zout_ref.dtype)


def _dense_ffn(z, wg, wu, wd, h, g3, g_next, tm, tf):
    n, d = h.shape
    ff = wg.shape[1]
    return pl.pallas_call(
        _ffn_kernel,
        grid=(n // tm, ff // tf),
        in_specs=[pl.BlockSpec((tm, d), lambda i, j: (i, 0)),
                  pl.BlockSpec((d, tf), lambda i, j: (0, j)),
                  pl.BlockSpec((d, tf), lambda i, j: (0, j)),
                  pl.BlockSpec((tf, d), lambda i, j: (j, 0)),
                  pl.BlockSpec((tm, d), lambda i, j: (i, 0)),
                  pl.BlockSpec((1, d), lambda i, j: (0, 0)),
                  pl.BlockSpec((1, d), lambda i, j: (0, 0))],
        out_specs=[pl.BlockSpec((tm, d), lambda i, j: (i, 0)),
                   pl.BlockSpec((tm, d), lambda i, j: (i, 0))],
        out_shape=[jax.ShapeDtypeStruct((n, d), F32), jax.ShapeDtypeStruct((n, d), BF16)],
        scratch_shapes=[pltpu.VMEM((tm, d), F32)],
        compiler_params=_params(("parallel", "arbitrary")),
        name="dense_ffn",
    )(z, wg, wu, wd, h, g3, g_next)


def _odd_proj_kernel(z_ref, win_ref, gq_ref, gkv_ref, wuq_ref, wuqs_ref, wuk_ref, wuv_ref, vone_ref, cos_ref, sin_ref,
                     zc_ref, q_ref, k_ref, v_ref, *, scale):
    z = z_ref[...]
    proj = _dot(z, win_ref[...])
    c0 = 2 * CONV_CH
    c1 = c0 + MLA_Q_RANK
    c2 = c1 + MLA_KV_RANK
    c3 = c2 + MLA_PAD
    zc_ref[...] = proj[:, :c0].astype(zc_ref.dtype)
    cq = _rms(proj[:, c0:c1], gq_ref[...]).astype(BF16)
    ckv = _rms(proj[:, c1:c2], gkv_ref[...]).astype(BF16)
    cos = cos_ref[...]
    sin = sin_ref[...]
    cos_h = jnp.concatenate([cos] * MLA_HEADS, axis=1)
    sin_h = jnp.concatenate([sin] * MLA_HEADS, axis=1)
    q = _dot(cq, wuq_ref[...]) * cos_h + _dot(cq, wuqs_ref[...]) * sin_h
    q_ref[...] = (q * scale).astype(q_ref.dtype)
    kr = proj[:, c2:c3] * cos + proj[:, c3:] * sin
    k = _dot(ckv, wuk_ref[...]) + jnp.concatenate([kr] * MLA_HEADS, axis=1)
    k_ref[...] = k.astype(k_ref.dtype)
    vt = lax.dot_general(wuv_ref[...], ckv, (((1,), (1,)), ((), ())), preferred_element_type=F32)
    v_ref[0] = (vt + vone_ref[...]).astype(v_ref.dtype)


def _odd_proj(z, win, gq, gkv, wuq, wuqs, wuk, wuv_t, v_one, cos_t, sin_t, seq, tm):
    n, d = z.shape
    hp = MLA_HEADS * MLA_PAD
    vr = MLA_HEADS * MLA_VROWS
    n_l = seq // tm
    const = lambda *shape: pl.BlockSpec(shape, lambda i: (0,) * len(shape))
    out = jax.ShapeDtypeStruct((n, hp), BF16)
    scale = float((MLA_NOPE + MLA_ROPE) ** -0.5 * math.log2(math.e))
    return pl.pallas_call(
        functools.partial(_odd_proj_kernel, scale=scale),
        grid=(n // tm,),
        in_specs=[pl.BlockSpec((tm, d), lambda i: (i, 0)),
                  const(d, win.shape[1]), const(1, MLA_Q_RANK), const(1, MLA_KV_RANK),
                  const(MLA_Q_RANK, hp), const(MLA_Q_RANK, hp), const(MLA_KV_RANK, hp), const(vr, MLA_KV_RANK),
                  const(vr, 1),
                  pl.BlockSpec((tm, MLA_PAD), lambda i: (i % n_l, 0)),
                  pl.BlockSpec((tm, MLA_PAD), lambda i: (i % n_l, 0))],
        out_specs=[pl.BlockSpec((tm, 2 * CONV_CH), lambda i: (i, 0)),
                   pl.BlockSpec((tm, hp), lambda i: (i, 0)),
                   pl.BlockSpec((tm, hp), lambda i: (i, 0)),
                   pl.BlockSpec((1, vr, tm), lambda i: (i, 0, 0))],
        out_shape=[jax.ShapeDtypeStruct((n, 2 * CONV_CH), BF16), out, out,
                   jax.ShapeDtypeStruct((n // tm, vr, tm), BF16)],
        compiler_params=_params(("parallel",)),
        name="odd_in_proj",
    )(z, win, gq, gkv, wuq, wuqs, wuk, wuv_t, v_one, cos_t, sin_t)


def _attn_kernel(q_ref, k_ref, vt_ref, wg_ref, wu_ref, wd_ref, o_ref, wgb_ref, wub_ref, wdb_ref, acc_ref, *, blk):
    i = pl.program_id(2)
    acc_ref[...] = jnp.zeros_like(acc_ref)
    tf = wgb_ref.shape[3]
    for f in range(wgb_ref.shape[1]):
        wgb_ref[0, f] = wg_ref[0, :, f * tf:(f + 1) * tf].astype(BF16)
        wub_ref[0, f] = wu_ref[0, :, f * tf:(f + 1) * tf].astype(BF16)
    wdb_ref[0] = wd_ref[0].astype(BF16)

    def step(j, m, masked):
        r0 = pl.multiple_of(j * blk, blk)
        scores = []
        for hh in range(ATTN_HEADS):
            q = q_ref[0, :, hh * MLA_PAD:(hh + 1) * MLA_PAD]
            k = k_ref[0, pl.ds(r0, blk), hh * MLA_PAD:(hh + 1) * MLA_PAD]
            st = lax.dot_general(k, q, (((1,), (1,)), ((), ())), preferred_element_type=F32)
            if masked:
                key = lax.broadcasted_iota(jnp.int32, st.shape, 0)
                qry = lax.broadcasted_iota(jnp.int32, st.shape, 1)
                st = jnp.where(key <= qry, st, -1e30)
            scores.append(st)
        soft = []
        for hh in range(ATTN_HEADS):
            m_new = jnp.maximum(m[hh], jnp.max(scores[hh], axis=0, keepdims=True))
            soft.append((m_new, jnp.exp2(m[hh] - m_new), jnp.exp2(scores[hh] - m_new).astype(BF16)))
        for hh in range(ATTN_HEADS):
            vt = vt_ref[j, hh * MLA_VROWS:(hh + 1) * MLA_VROWS, :]
            acc_ref[hh] = soft[hh][1] * acc_ref[hh] + _dot(vt, soft[hh][2])
        return tuple(s[0] for s in soft)

    init = jnp.full((1, blk), -1e30, F32)
    m = lax.fori_loop(0, i, lambda j, m: step(j, m, False), (init,) * ATTN_HEADS)
    step(i, m, True)
    ot = jnp.concatenate([acc_ref[hh][:MLA_V] / acc_ref[hh][MLA_V:MLA_V + 1] for hh in range(ATTN_HEADS)], axis=0)
    o_ref[0] = ot.T.astype(o_ref.dtype)


def _attention(q, k, vt, blk, wg, wu, wd, tf):
    b, seq, _ = q.shape
    n_blk = seq // blk
    n_pairs = MLA_HEADS // ATTN_HEADS
    n_e, d, ff = wg.shape
    steps = b * n_pairs * n_blk
    per_e = steps // n_e
    assert steps == per_e * n_e and d % per_e == 0 and ff % per_e == 0
    rows_in, rows_down = d // per_e, ff // per_e
    assert rows_in % 16 == 0 and rows_down % 16 == 0 and ff % tf == 0

    def lin(bi, p, i):
        return (bi * n_pairs + p) * n_blk + i

    w_in = pl.BlockSpec((1, rows_in, ff), lambda bi, p, i: (lin(bi, p, i) // per_e, lin(bi, p, i) % per_e, 0))
    w_out = pl.BlockSpec((1, ff // tf, rows_in, tf),
                         lambda bi, p, i: (lin(bi, p, i) // per_e, 0, lin(bi, p, i) % per_e, 0))
    w_down = pl.BlockSpec((1, rows_down, d), lambda bi, p, i: (lin(bi, p, i) // per_e, lin(bi, p, i) % per_e, 0))
    return pl.pallas_call(
        functools.partial(_attn_kernel, blk=blk),
        grid=(b, n_pairs, n_blk),
        in_specs=[pl.BlockSpec((1, blk, ATTN_HEADS * MLA_PAD), lambda bi, p, i: (bi, i, p)),
                  pl.BlockSpec((1, seq, ATTN_HEADS * MLA_PAD), lambda bi, p, i: (bi, 0, p)),
                  pl.BlockSpec((n_blk, ATTN_HEADS * MLA_VROWS, blk), lambda bi, p, i: (bi, p, 0)),
                  w_in, w_in, w_down],
        out_specs=[pl.BlockSpec((1, blk, ATTN_HEADS * MLA_V), lambda bi, p, i: (bi, i, p)), w_out, w_out, w_down],
        out_shape=[jax.ShapeDtypeStruct((b, seq, MLA_HEADS * MLA_V), BF16),
                   jax.ShapeDtypeStruct((n_e, ff // tf, d, tf), BF16),
                   jax.ShapeDtypeStruct((n_e, ff // tf, d, tf), BF16),
                   jax.ShapeDtypeStruct((n_e, ff, d), BF16)],
        scratch_shapes=[pltpu.VMEM((ATTN_HEADS, MLA_VROWS, blk), F32)],
        compiler_params=_params(("parallel", "parallel", "parallel")),
        name="mla_attention",
    )(q, k, vt, wg, wu, wd)


def _conv_kernel(zc_ref, w_ref, b_ref, lng_ref, lnb_ref, y_ref, buf_ref, part_ref):
    tm = zc_ref.shape[1]

    @pl.when(pl.program_id(1) == 0)
    def _():
        buf_ref[pl.ds(0, CONV_HALO), :] = jnp.zeros((CONV_HALO, CONV_CH), F32)
        buf_ref[pl.ds(CONV_HALO + tm, SUBLANES), :] = jnp.zeros((SUBLANES, CONV_CH), F32)

    zc = zc_ref[0].astype(F32)
    hh = zc[:, :CONV_CH] * jax.nn.sigmoid(zc[:, CONV_CH:])
    buf_ref[pl.ds(CONV_HALO, tm), :] = hh
    off = CONV_HALO - (CONV_TAPS - 1)
    acc = jnp.zeros((tm, CONV_CH), F32) + b_ref[...]
    for b in range(SUBLANES):
        taps = [k for k in range(CONV_TAPS) if (off + k) % SUBLANES == b]
        part = None
        for k in taps:
            term = w_ref[pl.ds(k, 1), :] * buf_ref[pl.ds(off + k - b, tm + SUBLANES), :]
            part = term if part is None else part + term
        if b == 0:
            acc = acc + part[:tm]
        else:
            part_ref[...] = part
            acc = acc + part_ref[pl.ds(b, tm), :]
    buf_ref[pl.ds(0, CONV_HALO), :] = buf_ref[pl.ds(tm, CONV_HALO), :]
    y_ref[0] = jax.nn.silu(_layer_norm(acc, lng_ref[...], lnb_ref[...])).astype(y_ref.dtype)


def _conv_mixer(zc, w, b, lng, lnb, tm):
    bsz, seq, _ = zc.shape
    const = lambda *shape: pl.BlockSpec(shape, lambda bi, i: (0,) * len(shape))
    return pl.pallas_call(
        _conv_kernel,
        grid=(bsz, seq // tm),
        in_specs=[pl.BlockSpec((1, tm, 2 * CONV_CH), lambda bi, i: (bi, i, 0)),
                  const(CONV_HALO, CONV_CH), const(1, CONV_CH), const(1, CONV_CH), const(1, CONV_CH)],
        out_specs=pl.BlockSpec((1, tm, CONV_CH), lambda bi, i: (bi, i, 0)),
        out_shape=jax.ShapeDtypeStruct((bsz, seq, CONV_CH), BF16),
        scratch_shapes=[pltpu.VMEM((CONV_HALO + tm + SUBLANES, CONV_CH), F32),
                        pltpu.VMEM((tm + SUBLANES, CONV_CH), F32)],
        compiler_params=_params(("arbitrary", "arbitrary")),
        name="conv_module",
    )(zc, w, b, lng, lnb)


def _odd_mix_kernel(yc_ref, yd_ref, h_ref, wo_a_ref, wo_b_ref, g1_ref, g2_ref, wr_ref, hout_ref, z_ref, route_ref):
    tm = h_ref.shape[0]
    halves = [slice(0, tm // 2), slice(tm // 2, tm)]
    mixes = [_dot(yc_ref[r, :], wo_a_ref[...]) + _dot(yd_ref[r, :], wo_b_ref[...]) for r in halves]
    zs = []
    for r, mix in zip(halves, mixes):
        h_new = h_ref[r, :] + _rms(mix, g1_ref[...])
        hout_ref[r, :] = h_new
        zs.append(_rms(h_new, g2_ref[...]))
    all_logits = [_dot(z.astype(BF16), wr_ref[...]) for z in zs]
    _store_row_tiles(z_ref, jnp.concatenate(zs, axis=0))
    neg = -jnp.inf
    for r, logits in zip(halves, all_logits):
        lane = lax.broadcasted_iota(jnp.int32, logits.shape, 1)
        logits = jnp.where(lane < N_EXPERTS, logits, neg)
        m1 = jnp.max(logits, axis=-1, keepdims=True)
        i1 = jnp.min(jnp.where(logits == m1, lane, LANES), axis=-1, keepdims=True)
        rest = jnp.where(lane == i1, neg, logits)
        m2 = jnp.max(rest, axis=-1, keepdims=True)
        i2 = jnp.min(jnp.where(rest == m2, lane, LANES), axis=-1, keepdims=True)
        e = jnp.exp(m2 - m1)
        w1 = 1.0 / (1.0 + e)
        w2 = e / (1.0 + e)
        route_ref[r, :] = jnp.where(lane == 0, i1.astype(F32),
                                    jnp.where(lane == 1, i2.astype(F32),
                                              jnp.where(lane == 2, w1, jnp.where(lane == 3, w2, 0.0))))


def _odd_mix(yc, yd, h, wo_a, wo_b, g1, g2, wr, tm):
    n, d = h.shape
    const = lambda *shape: pl.BlockSpec(shape, lambda i: (0,) * len(shape))
    return pl.pallas_call(
        _odd_mix_kernel,
        grid=(n // tm,),
        in_specs=[pl.BlockSpec((tm, yc.shape[1]), lambda i: (i, 0)),
                  pl.BlockSpec((tm, yd.shape[1]), lambda i: (i, 0)),
                  pl.BlockSpec((tm, d), lambda i: (i, 0)),
                  const(*wo_a.shape), const(*wo_b.shape), const(1, d), const(1, d), const(d, LANES)],
        out_specs=[pl.BlockSpec((tm, d), lambda i: (i, 0)),
                   pl.BlockSpec((tm * ROW_TILE, LANES), lambda i: (i, 0)),
                   pl.BlockSpec((tm, LANES), lambda i: (i, 0))],
        out_shape=[jax.ShapeDtypeStruct((n, d), F32), jax.ShapeDtypeStruct((n * ROW_TILE, LANES), F32),
                   jax.ShapeDtypeStruct((n, LANES), F32)],
        compiler_params=_params(("parallel",)),
        name="odd_mix_router",
    )(yc, yd, h, wo_a, wo_b, g1, g2, wr)


def _store_row_tiles(ref, x):
    rows = x.shape[0]
    for s in range(ROW_TILE):
        ref[pl.ds(s, rows, stride=ROW_TILE), :] = x[:, s * LANES:(s + 1) * LANES]


def _load_row_tiles(ref, rows):
    return [ref[pl.ds(s, rows, stride=ROW_TILE), :] for s in range(ROW_TILE)]


def _gather_rows(idx_ref, base, n_rows, src_hbm, dst_ref, sem):
    def body(r, c):
        src = pl.multiple_of(idx_ref[base + r] * ROW_TILE, ROW_TILE)
        dst = pl.multiple_of(r * ROW_TILE, ROW_TILE)
        pltpu.make_async_copy(src_hbm.at[pl.ds(src, ROW_TILE), :], dst_ref.at[pl.ds(dst, ROW_TILE), :], sem).start()
        return c

    lax.fori_loop(0, n_rows, body, 0, unroll=8)


def _wait_rows(src_hbm, dst_ref, sem):
    pltpu.make_async_copy(src_hbm.at[pl.ds(0, dst_ref.shape[0]), :], dst_ref, sem).wait()


def _row_copy(src_ref, src_row, dst_ref, dst_row, sem):
    src = pl.multiple_of(src_row * ROW_TILE, ROW_TILE)
    dst = pl.multiple_of(dst_row * ROW_TILE, ROW_TILE)
    return pltpu.make_async_copy(src_ref.at[pl.ds(src, ROW_TILE), :], dst_ref.at[pl.ds(dst, ROW_TILE), :], sem)


def _moe_ffn_kernel(te_ref, nu_ref, tok_ref, dst_ref, z_hbm, wg_ref, wu_ref, wd_ref, y_hbm,
                    xraw_ref, xb_ref, acc_ref, yst_ref, gsem, ssem, *, rows_per_step):
    i = pl.program_id(0)
    j = pl.program_id(1)
    tm = xb_ref.shape[0]
    stride = yst_ref.shape[0] // ROW_TILE
    n_used = nu_ref[0]
    slot = i % 2
    first = j == 0
    last = j == pl.num_programs(1) - 1

    @pl.when(first & (i == 0))
    def _():
        yst_ref[...] = jnp.zeros_like(yst_ref)
        _gather_rows(tok_ref, 0, stride, z_hbm, xraw_ref.at[0], gsem.at[0])

    @pl.when(first & (i <= n_used))
    def _():
        _wait_rows(z_hbm, xraw_ref.at[slot], gsem.at[slot])

    @pl.when(first & (i < n_used))
    def _():
        for s, blk in enumerate(_load_row_tiles(xraw_ref.at[slot], tm)):
            xb_ref[:, s * LANES:(s + 1) * LANES] = blk.astype(BF16)
        acc_ref[...] = jnp.zeros_like(acc_ref)

    @pl.when(first & (i == n_used))
    def _():
        def body(r, c):
            _row_copy(yst_ref, r, y_hbm, dst_ref[i * stride + r], ssem).start()
            return c
        lax.fori_loop(0, stride, body, 0, unroll=8)

    def multiply(with_copies):
        x = xb_ref[...]
        g = _dot(x, wg_ref[0, 0])
        u = _dot(x, wu_ref[0, 0])
        if with_copies:
            nxt = xraw_ref.at[1 - slot]
            for rr in range(rows_per_step):
                r = j * rows_per_step + rr
                _row_copy(z_hbm, tok_ref[(i + 1) * stride + r], nxt, r, gsem.at[1 - slot]).start(priority=rr % 2)
                _row_copy(yst_ref, r, y_hbm, dst_ref[i * stride + r], ssem).start(priority=rr % 2)
        a = jax.nn.silu(g) * u
        acc_ref[...] += _dot(a.astype(BF16), wd_ref[0])

    @pl.when((i < n_used) & jnp.logical_not(last))
    def _():
        multiply(True)

    @pl.when((i < n_used) & last)
    def _():
        multiply(False)

    @pl.when(last & (i <= n_used))
    def _():
        _wait_rows(z_hbm, yst_ref, ssem)

    @pl.when(last & (i < n_used))
    def _():
        _store_row_tiles(yst_ref, acc_ref[...])


def _moe_ffn(tile_expert, n_used, tok_tab, dst_tab, z_tiles, wg, wu, wd, n_tok, tm):
    n_f, d, tf = wg.shape[1], wg.shape[2], wg.shape[3]
    copy_steps = n_f - 1
    rows_per_step = -(-tm // copy_steps)
    stride = copy_steps * rows_per_step
    n_tiles = tile_expert.shape[0]
    assert tok_tab.shape[0] == dst_tab.shape[0] == (n_tiles + 1) * stride

    def col(i, j, nu):
        return jnp.where(i < nu[0], j, n_f - 1)

    return pl.pallas_call(
        functools.partial(_moe_ffn_kernel, rows_per_step=rows_per_step),
        grid_spec=pltpu.PrefetchScalarGridSpec(
            num_scalar_prefetch=4,
            grid=(n_tiles, n_f),
            in_specs=[pl.BlockSpec(memory_space=pl.ANY),
                      pl.BlockSpec((1, 1, d, tf), lambda i, j, te, nu, tok, dst: (te[i], col(i, j, nu), 0, 0)),
                      pl.BlockSpec((1, 1, d, tf), lambda i, j, te, nu, tok, dst: (te[i], col(i, j, nu), 0, 0)),
                      pl.BlockSpec((1, tf, d), lambda i, j, te, nu, tok, dst: (te[i], col(i, j, nu), 0))],
            out_specs=pl.BlockSpec(memory_space=pl.ANY),
            scratch_shapes=[pltpu.VMEM((2, stride * ROW_TILE, LANES), F32), pltpu.VMEM((tm, d), BF16),
                            pltpu.VMEM((tm, d), F32), pltpu.VMEM((stride * ROW_TILE, LANES), F32),
                            pltpu.SemaphoreType.DMA((2,)), pltpu.SemaphoreType.DMA(())]),
        out_shape=jax.ShapeDtypeStruct(((2 * n_tok + stride) * ROW_TILE, LANES), F32),
        compiler_params=_params(("arbitrary", "arbitrary")),
        name="moe_grouped_ffn",
    )(tile_expert, n_used, tok_tab, dst_tab, z_tiles, wg, wu, wd)


def _combine_kernel(ya_ref, yb_ref, route_ref, h_ref, g_ref, o_ref):
    tm = h_ref.shape[0]
    route = route_ref[...]
    a = jnp.concatenate(_load_row_tiles(ya_ref, tm), axis=1)
    b = jnp.concatenate(_load_row_tiles(yb_ref, tm), axis=1)
    f = route[:, 2:3] * a + route[:, 3:4] * b
    o_ref[...] = h_ref[...] + _rms(f, g_ref[...])


def _combine(y, route, h, g, tm):
    n, d = h.shape
    n_blk = n // tm
    return pl.pallas_call(
        _combine_kernel,
        grid=(n_blk,),
        in_specs=[pl.BlockSpec((tm * ROW_TILE, LANES), lambda i: (i, 0)),
                  pl.BlockSpec((tm * ROW_TILE, LANES), lambda i: (n_blk + i, 0)),
                  pl.BlockSpec((tm, LANES), lambda i: (i, 0)),
                  pl.BlockSpec((tm, d), lambda i: (i, 0)),
                  pl.BlockSpec((1, d), lambda i: (0, 0))],
        out_specs=pl.BlockSpec((tm, d), lambda i: (i, 0)),
        out_shape=jax.ShapeDtypeStruct((n, d), F32),
        compiler_params=_params(("parallel",)),
        name="moe_combine",
    )(y, y, route, h, g)


def _moe_plan(route, tm, stride):
    n = route.shape[0]
    eids = jnp.concatenate([route[:, 0], route[:, 1]]).astype(jnp.int32)
    onehot = (eids[:, None] == jnp.arange(N_EXPERTS, dtype=jnp.int32)[None, :]).astype(jnp.int32)
    csum = jnp.cumsum(onehot, axis=0)
    rank = jnp.sum(csum * onehot, axis=1) - 1
    counts = csum[-1]
    padded = ((counts + tm - 1) // tm) * tm
    ends = jnp.cumsum(padded)
    starts = ends - padded
    slot = jnp.sum(onehot * starts[None, :], axis=1) + rank
    n_tiles = 2 * n // tm + N_EXPERTS + 1
    copy_of_slot = jnp.full((n_tiles * tm,), -1, jnp.int32).at[slot].set(jnp.arange(2 * n, dtype=jnp.int32))
    copy_tab = jnp.pad(copy_of_slot.reshape(n_tiles, tm), ((0, 1), (0, stride - tm)), constant_values=-1)
    tok_tab = jnp.where(copy_tab >= 0, copy_tab % n, 0)
    dump = 2 * n + jnp.arange(stride, dtype=jnp.int32)[None, :]
    dst_tab = jnp.where(copy_tab >= 0, copy_tab, dump)
    dst_tab = jnp.concatenate([jnp.broadcast_to(dump, (1, stride)), dst_tab[:-1]], axis=0)
    n_used = (ends[-1] // tm).astype(jnp.int32)
    tile_start = jnp.minimum(jnp.arange(n_tiles, dtype=jnp.int32), n_used - 1) * tm
    tile_expert = jnp.sum((tile_start[:, None] >= ends[None, :]).astype(jnp.int32), axis=1)
    return tok_tab.reshape(-1), dst_tab.reshape(-1), tile_expert, n_used.reshape(1)


def _odd_weights(od_w_in, mla_w_uq, mla_w_ukv):
    c2 = 2 * CONV_CH + MLA_Q_RANK + MLA_KV_RANK
    half = MLA_ROPE // 2
    w_kr = od_w_in[:, c2:]
    w_kr_sw = jnp.concatenate([w_kr[:, half:], w_kr[:, :half]], axis=1)
    zl = jnp.zeros((D_MODEL, MLA_NOPE), F32)
    zr = jnp.zeros((D_MODEL, MLA_PAD - MLA_NOPE - MLA_ROPE), F32)
    win = jnp.concatenate([od_w_in[:, :c2], zl, w_kr, zr, zl, w_kr_sw, zr], axis=1)
    dk = MLA_NOPE + MLA_ROPE
    wq = mla_w_uq.reshape(MLA_Q_RANK, MLA_HEADS, dk)
    zq = jnp.zeros((MLA_Q_RANK, MLA_HEADS, MLA_PAD - dk), F32)
    wuq = jnp.concatenate([wq, zq], axis=2).reshape(MLA_Q_RANK, MLA_HEADS * MLA_PAD)
    wq_sw = jnp.concatenate([jnp.zeros_like(wq[:, :, :MLA_NOPE]), wq[:, :, MLA_NOPE + half:],
                             wq[:, :, MLA_NOPE:MLA_NOPE + half], zq], axis=2)
    wuqs = wq_sw.reshape(MLA_Q_RANK, MLA_HEADS * MLA_PAD)
    wkv = mla_w_ukv.reshape(MLA_KV_RANK, MLA_HEADS, MLA_NOPE + MLA_V)
    zk = jnp.zeros((MLA_KV_RANK, MLA_HEADS, MLA_PAD - MLA_NOPE), F32)
    wuk = jnp.concatenate([wkv[:, :, :MLA_NOPE], zk], axis=2).reshape(MLA_KV_RANK, MLA_HEADS * MLA_PAD)
    zv = jnp.zeros((MLA_KV_RANK, MLA_HEADS, MLA_VROWS - MLA_V), F32)
    wuv_t = jnp.concatenate([wkv[:, :, MLA_NOPE:], zv], axis=2).reshape(MLA_KV_RANK, MLA_HEADS * MLA_VROWS).T
    v_one = jnp.zeros((MLA_HEADS, MLA_VROWS), F32).at[:, MLA_V].set(1.0).reshape(MLA_HEADS * MLA_VROWS, 1)
    return win.astype(BF16), wuq.astype(BF16), wuqs.astype(BF16), wuk.astype(BF16), wuv_t.astype(BF16), v_one


def _rope_tables(seq):
    inv = 1.0 / (ROPE_THETA ** (jnp.arange(0, MLA_ROPE, 2, dtype=F32) / MLA_ROPE))
    ang = jnp.arange(seq, dtype=F32)[:, None] * inv[None, :]
    cos, sin = jnp.cos(ang), jnp.sin(ang)
    ones = jnp.ones((seq, MLA_NOPE), F32)
    zl = jnp.zeros((seq, MLA_NOPE), F32)
    zr = jnp.zeros((seq, MLA_PAD - MLA_NOPE - MLA_ROPE), F32)
    return (jnp.concatenate([ones, cos, cos, zr], axis=1), jnp.concatenate([zl, -sin, sin, zr], axis=1))


def kernel(x, norm_g, ev_w_in, ssm_lambda_re, ssm_lambda_im, ssm_log_dt, ssm_b_re, ssm_b_im, ssm_c_re, ssm_c_im, ssm_d, ssm_w_glu, sgu_ln_g, sgu_ln_b, sgu_w, sgu_b, ev_w_out, ffn_w_gate, ffn_w_up, ffn_w_down, od_w_in, conv_w, conv_b, conv_ln_g, conv_ln_b, mla_q_norm_g, mla_w_uq, mla_kv_norm_g, mla_w_ukv, od_w_out, moe_w_router, moe_w_gate, moe_w_up, moe_w_down):
    bsz, seq, d = x.shape
    n = bsz * seq
    assert d == D_MODEL and SUBLANES % bsz == 0 and seq % 512 == 0
    row = lambda v: v.astype(F32).reshape(1, -1)
    h = x.astype(F32).reshape(n, d)
    tm = 512

    g = norm_g[0]
    a_in, proj = _norm_proj(h, row(g[0]), ev_w_in[0].astype(BF16), tm)
    mats = _s5_matrices(ssm_lambda_re[0], ssm_lambda_im[0], ssm_log_dt[0], ssm_b_re[0], ssm_b_im[0],
                        ssm_c_re[0], ssm_c_im[0])
    ys = _s5_mixer(a_in, mats, ssm_d[0], bsz, seq)
    causal = jnp.tril(jnp.ones((SGU_CHUNK, SGU_CHUNK), dtype=bool))
    ws = jnp.where(causal[None], sgu_w[0], 0.0).astype(BF16)
    bias = jnp.repeat(sgu_b[0].astype(F32).T, SGU_HEAD_DIM, axis=1)
    wo = ev_w_out[0].astype(BF16)
    h, z = _even_mix(ys, proj, h, ssm_w_glu[0].astype(BF16), row(sgu_ln_g[0]), row(sgu_ln_b[0]), ws, bias,
                     wo[:SSM_WIDTH], wo[SSM_WIDTH:], row(g[1]), row(g[2]), tm)
    h, z = _dense_ffn(z, ffn_w_gate[0].astype(BF16), ffn_w_up[0].astype(BF16), ffn_w_down[0].astype(BF16),
                      h, row(g[3]), row(norm_g[1][0]), 1024, 512)

    g = norm_g[1]
    win, wuq, wuqs, wuk, wuv_t, v_one = _odd_weights(od_w_in[0], mla_w_uq[0], mla_w_ukv[0])
    cos_t, sin_t = _rope_tables(seq)
    zc, q, k, vt = _odd_proj(z, win, row(mla_q_norm_g[0]), row(mla_kv_norm_g[0]), wuq, wuqs, wuk, wuv_t, v_one,
                             cos_t, sin_t, seq, tm)
    hp = MLA_HEADS * MLA_PAD
    tm_moe, tf_moe = 1024, 512
    yd, wg_b, wu_b, wd_b = _attention(q.reshape(bsz, seq, hp), k.reshape(bsz, seq, hp), vt, tm,
                                      moe_w_gate[0], moe_w_up[0], moe_w_down[0], tf_moe)
    conv_w_pad = jnp.concatenate([conv_w[0].astype(F32), jnp.zeros((CONV_HALO - CONV_TAPS, CONV_CH), F32)], axis=0)
    yc = _conv_mixer(zc.reshape(bsz, seq, 2 * CONV_CH), conv_w_pad, row(conv_b[0]), row(conv_ln_g[0]),
                     row(conv_ln_b[0]), tm)
    wo = od_w_out[0].astype(BF16)
    wr = jnp.concatenate([moe_w_router[0].astype(F32), jnp.zeros((d, LANES - N_EXPERTS), F32)], axis=1)
    h, z, route = _odd_mix(yc.reshape(n, CONV_CH), yd.reshape(n, MLA_HEADS * MLA_V), h, wo[:CONV_CH], wo[CONV_CH:],
                           row(g[1]), row(g[2]), wr.astype(BF16), tm)
    n_f = wg_b.shape[1]
    tok_tab, dst_tab, tile_expert, n_used = _moe_plan(route, tm_moe, (n_f - 1) * -(-tm_moe // (n_f - 1)))
    y = _moe_ffn(tile_expert, n_used, tok_tab, dst_tab, z, wg_b, wu_b, wd_b, n, tm_moe)
    h = _combine(y, route, h, row(g[3]), 256)
    return h.reshape(bsz, seq, d).astype(x.dtype)
```

```python
import functools
import math

import jax
import jax.numpy as jnp
from jax import lax
from jax.experimental import pallas as pl
from jax.experimental.pallas import tpu as pltpu

F32 = jnp.float32
BF16 = jnp.bfloat16

D_MODEL = 1024
NORM_EPS = 1e-6
SSM_WIDTH = 512
SSM_GROUP = 16
SSM_GROUPS = 32
SSM_STATE = 64
SSM_CHUNK = 16
SSM_PAIR = 2 * SSM_GROUP * SSM_CHUNK
SGU_WIDTH = 512
SGU_HEADS = 8
SGU_HEAD_DIM = 64
SGU_CHUNK = 128
CONV_CH = 512
CONV_TAPS = 31
CONV_HALO = 32
MLA_HEADS = 8
MLA_Q_RANK = 256
MLA_KV_RANK = 128
MLA_NOPE = 64
MLA_ROPE = 32
MLA_V = 64
MLA_PAD = 128
MLA_VROWS = 80
ATTN_HEADS = 4
ROPE_THETA = 10000.0
FF_DENSE = 4096
N_EXPERTS = 8
FF_EXPERT = 3584
LANES = 128
SUBLANES = 8
ROW_TILE = D_MODEL // LANES
VMEM_LIMIT = 56 * 1024 * 1024


def _params(sem, vmem=VMEM_LIMIT):
    return pltpu.CompilerParams(dimension_semantics=sem, vmem_limit_bytes=vmem)


def _rms(x, g):
    return x * lax.rsqrt(jnp.mean(x * x, axis=-1, keepdims=True) + NORM_EPS) * g


def _layer_norm(x, g, b):
    mu = jnp.mean(x, axis=-1, keepdims=True)
    xc = x - mu
    return xc * lax.rsqrt(jnp.mean(xc * xc, axis=-1, keepdims=True) + NORM_EPS) * g + b


def _dot(a, b):
    return jnp.dot(a, b, preferred_element_type=F32)


def _norm_proj_kernel(h_ref, g_ref, w_ref, a_ref, b_ref):
    z = _rms(h_ref[...], g_ref[...])
    proj = _dot(z.astype(BF16), w_ref[...])
    for jb in range(SSM_WIDTH // LANES):
        a_ref[jb] = proj[:, jb * LANES:(jb + 1) * LANES]
    b_ref[...] = proj[:, SSM_WIDTH:].astype(b_ref.dtype)


def _norm_proj(h, g, w, tm):
    n, d = h.shape
    cols = w.shape[1]
    return pl.pallas_call(
        _norm_proj_kernel,
        grid=(n // tm,),
        in_specs=[pl.BlockSpec((tm, d), lambda i: (i, 0)),
                  pl.BlockSpec((1, d), lambda i: (0, 0)),
                  pl.BlockSpec((d, cols), lambda i: (0, 0))],
        out_specs=[pl.BlockSpec((SSM_WIDTH // LANES, tm, LANES), lambda i: (0, i, 0)),
                   pl.BlockSpec((tm, cols - SSM_WIDTH), lambda i: (i, 0))],
        out_shape=[jax.ShapeDtypeStruct((SSM_WIDTH // LANES, n, LANES), F32),
                   jax.ShapeDtypeStruct((n, cols - SSM_WIDTH), BF16)],
        compiler_params=_params(("parallel",)),
        name="even_in_proj",
    )(h, g, w)


def _s5_matrices(lam_re, lam_im, log_dt, b_re, b_im, c_re, c_im):
    t = SSM_CHUNK
    lr = jnp.minimum(lam_re.astype(F32), -1e-4)
    li = lam_im.astype(F32)
    dt = jnp.exp(log_dt.astype(F32))[:, None]
    mag = jnp.exp(lr * dt)
    a_re = mag * jnp.cos(li * dt)
    a_im = mag * jnp.sin(li * dt)
    den = lr * lr + li * li
    nr = a_re - 1.0
    coef_re = (nr * lr + a_im * li) / den
    coef_im = (a_im * lr - nr * li) / den
    br = b_re.astype(F32)
    bi = b_im.astype(F32)
    bb_re = coef_re[..., None] * br - coef_im[..., None] * bi
    bb_im = coef_re[..., None] * bi + coef_im[..., None] * br
    cr = c_re.astype(F32)
    ci = c_im.astype(F32)
    pw_re = [jnp.ones_like(a_re)]
    pw_im = [jnp.zeros_like(a_im)]
    for _ in range(t):
        pr, pi = pw_re[-1], pw_im[-1]
        pw_re.append(pr * a_re - pi * a_im)
        pw_im.append(pr * a_im + pi * a_re)
    pw_re = jnp.stack(pw_re)
    pw_im = jnp.stack(pw_im)
    ab_re = pw_re[:t, :, :, None] * bb_re[None] - pw_im[:t, :, :, None] * bb_im[None]
    ab_im = pw_re[:t, :, :, None] * bb_im[None] + pw_im[:t, :, :, None] * bb_re[None]
    hi = lax.Precision.HIGHEST
    k_lag = (jnp.einsum('gnp,tgpm->tgnm', cr, ab_re, precision=hi)
             - jnp.einsum('gnp,tgpm->tgnm', ci, ab_im, precision=hi))
    n_pairs = SSM_GROUPS // 2
    st = 2 * SSM_STATE

    def pair_diag(w):
        w = w.reshape((n_pairs, 2) + w.shape[1:])
        z = jnp.zeros_like(w[:, 0])
        top = jnp.concatenate([w[:, 0], z], axis=-1)
        bot = jnp.concatenate([z, w[:, 1]], axis=-1)
        return jnp.concatenate([top, bot], axis=-2)

    k_blk = pair_diag(k_lag.transpose(1, 0, 3, 2))
    rev_re = pw_re[:t][::-1]
    rev_im = pw_im[:t][::-1]
    ws_re = rev_re[..., None] * bb_re[None] - rev_im[..., None] * bb_im[None]
    ws_im = rev_re[..., None] * bb_im[None] + rev_im[..., None] * bb_re[None]
    ws_re = pair_diag(ws_re.transpose(1, 0, 3, 2)).reshape(n_pairs, SSM_PAIR, st).astype(BF16)
    ws_im = pair_diag(ws_im.transpose(1, 0, 3, 2)).reshape(n_pairs, SSM_PAIR, st).astype(BF16)
    ca_re = cr[None] * pw_re[1:, :, None, :] - ci[None] * pw_im[1:, :, None, :]
    ca_im = cr[None] * pw_im[1:, :, None, :] + ci[None] * pw_re[1:, :, None, :]
    co_re = pair_diag(ca_re.transpose(1, 0, 3, 2))
    co_im = pair_diag((-ca_im).transpose(1, 0, 3, 2))
    w_intra, wo_re, wo_im = _s5_expand(k_blk, co_re, co_im)
    return dict(
        w_intra=w_intra, ws_re=ws_re, ws_im=ws_im, wo_re=wo_re, wo_im=wo_im,
        at_re=pw_re[t].reshape(1, SSM_GROUPS * SSM_STATE), at_im=pw_im[t].reshape(1, SSM_GROUPS * SSM_STATE))


def _s5_expand_kernel(k_ref, cre_ref, cim_ref, wi_ref, wore_ref, woim_ref, kcat_ref):
    pw = 2 * SSM_GROUP
    for tau in range(SSM_CHUNK):
        kcat_ref[:, tau * pw:(tau + 1) * pw] = k_ref[0, tau]
        wore_ref[0, :, tau * pw:(tau + 1) * pw] = cre_ref[0, tau].astype(wore_ref.dtype)
        woim_ref[0, :, tau * pw:(tau + 1) * pw] = cim_ref[0, tau].astype(woim_ref.dtype)
    kcat = kcat_ref[...]
    col = lax.broadcasted_iota(jnp.int32, kcat.shape, 1)
    for s in range(SSM_CHUNK):
        blk = kcat if s == 0 else jnp.where(col >= s * pw, pltpu.roll(kcat, s * pw, 1), 0.0)
        wi_ref[0, s * pw:(s + 1) * pw, :] = blk.astype(wi_ref.dtype)


def _s5_expand(k_blk, co_re, co_im):
    n_pairs = k_blk.shape[0]
    pw = 2 * SSM_GROUP
    st = 2 * SSM_STATE
    return pl.pallas_call(
        _s5_expand_kernel,
        grid=(n_pairs,),
        in_specs=[pl.BlockSpec((1, SSM_CHUNK, pw, pw), lambda q: (q, 0, 0, 0)),
                  pl.BlockSpec((1, SSM_CHUNK, st, pw), lambda q: (q, 0, 0, 0)),
                  pl.BlockSpec((1, SSM_CHUNK, st, pw), lambda q: (q, 0, 0, 0))],
        out_specs=[pl.BlockSpec((1, SSM_PAIR, SSM_PAIR), lambda q: (q, 0, 0)),
                   pl.BlockSpec((1, st, SSM_PAIR), lambda q: (q, 0, 0)),
                   pl.BlockSpec((1, st, SSM_PAIR), lambda q: (q, 0, 0))],
        out_shape=[jax.ShapeDtypeStruct((n_pairs, SSM_PAIR, SSM_PAIR), BF16),
                   jax.ShapeDtypeStruct((n_pairs, st, SSM_PAIR), BF16),
                   jax.ShapeDtypeStruct((n_pairs, st, SSM_PAIR), BF16)],
        scratch_shapes=[pltpu.VMEM((pw, SSM_PAIR), F32)],
        compiler_params=_params(("parallel",)),
        name="s5_expand_weights",
    )(k_blk, co_re, co_im)


S5_LANE_PAIRS = LANES // (2 * SSM_GROUP)
S5_SCAN_LANES = 512


def _s5_state_kernel(u0_ref, u1_ref, u2_ref, u3_ref, wre_ref, wim_ref, are_ref, aim_ref,
                     x_ref, hre_ref, him_ref, sre_ref, sim_ref):
    n_chunks = x_ref.shape[0]
    pw = 2 * SSM_GROUP
    u_refs = (u0_ref, u1_ref, u2_ref, u3_ref)
    for t in range(SSM_CHUNK):
        for j, u_ref in enumerate(u_refs):
            ut = u_ref[pl.ds(t, n_chunks, stride=SSM_CHUNK), :]
            for qq in range(S5_LANE_PAIRS):
                q = j * S5_LANE_PAIRS + qq
                x_ref[:, q * SSM_PAIR + t * pw: q * SSM_PAIR + (t + 1) * pw] = (
                    ut[:, qq * pw:(qq + 1) * pw].astype(x_ref.dtype))
    st = 2 * SSM_STATE
    for q in range(SSM_GROUPS // 2):
        xq = x_ref[:, q * SSM_PAIR:(q + 1) * SSM_PAIR]
        sre_ref[:, q * st:(q + 1) * st] = _dot(xq, wre_ref[q])
        sim_ref[:, q * st:(q + 1) * st] = _dot(xq, wim_ref[q])

    row = lax.broadcasted_iota(jnp.int32, (SUBLANES, S5_SCAN_LANES), 0)
    zero = jnp.zeros((SUBLANES, S5_SCAN_LANES), F32)
    for c0 in range(0, sre_ref.shape[1], S5_SCAN_LANES):
        cols = pl.ds(c0, S5_SCAN_LANES)
        ar = are_ref[:, cols]
        ai = aim_ref[:, cols]

        def body(k, carry, cols=cols, ar=ar, ai=ai):
            r0 = pl.multiple_of(k * SUBLANES, SUBLANES)
            sr = sre_ref[pl.ds(r0, SUBLANES), cols]
            si = sim_ref[pl.ds(r0, SUBLANES), cols]
            out_r, out_i = carry
            for i in range(1, SUBLANES + 1):
                tr = ar * out_r - ai * out_i + sr
                ti = ar * out_i + ai * out_r + si
                tr = pltpu.roll(tr, 1, 0)
                ti = pltpu.roll(ti, 1, 0)
                if i < SUBLANES:
                    out_r = jnp.where(row == i, tr, out_r)
                    out_i = jnp.where(row == i, ti, out_i)
            hre_ref[pl.ds(r0, SUBLANES), cols] = out_r
            him_ref[pl.ds(r0, SUBLANES), cols] = out_i
            return tr, ti

        lax.fori_loop(0, n_chunks // SUBLANES, body, (zero, zero))


def _s5_out_kernel(x_ref, wi_ref, hre_ref, him_ref, wore_ref, woim_ref, d_ref, y_ref, yt_ref):
    n_chunks = x_ref.shape[0]
    pw = 2 * SSM_GROUP
    st = 2 * SSM_STATE
    for qq in range(S5_LANE_PAIRS):
        x = x_ref[:, qq * SSM_PAIR:(qq + 1) * SSM_PAIR]
        y = _dot(x, wi_ref[qq])
        y += _dot(hre_ref[:, qq * st:(qq + 1) * st].astype(BF16), wore_ref[qq])
        y += _dot(him_ref[:, qq * st:(qq + 1) * st].astype(BF16), woim_ref[qq])
        y += d_ref[:, qq * SSM_PAIR:(qq + 1) * SSM_PAIR] * x.astype(F32)
        y = jax.nn.gelu(y)
        for t in range(SSM_CHUNK):
            yt_ref[t, :, qq * pw:(qq + 1) * pw] = y[:, t * pw:(t + 1) * pw]
    for t in range(SSM_CHUNK):
        y_ref[pl.ds(t, n_chunks, stride=SSM_CHUNK), :] = yt_ref[t]


def _s5_mixer(u, mats, d, batch, seq):
    t = SSM_CHUNK
    n_chunks = seq // t
    n_pairs = SSM_GROUPS // 2
    cols = n_pairs * SSM_PAIR
    st = 2 * SSM_STATE
    n_state = n_pairs * st
    n_blk = SSM_WIDTH // LANES
    assert n_blk == 4 and n_chunks % SUBLANES == 0
    once = pl.Buffered(1)
    x, h_re, h_im = pl.pallas_call(
        _s5_state_kernel,
        grid=(batch,),
        in_specs=[pl.BlockSpec((None, seq, LANES), lambda b, j=j: (j, b, 0)) for j in range(n_blk)] + [
            pl.BlockSpec((n_pairs, SSM_PAIR, st), lambda b: (0, 0, 0), pipeline_mode=once),
            pl.BlockSpec((n_pairs, SSM_PAIR, st), lambda b: (0, 0, 0), pipeline_mode=once),
            pl.BlockSpec((1, n_state), lambda b: (0, 0)),
            pl.BlockSpec((1, n_state), lambda b: (0, 0))],
        out_specs=[pl.BlockSpec((n_chunks, cols), lambda b: (b, 0)),
                   pl.BlockSpec((n_chunks, n_state), lambda b: (b, 0)),
                   pl.BlockSpec((n_chunks, n_state), lambda b: (b, 0))],
        out_shape=[jax.ShapeDtypeStruct((batch * n_chunks, cols), BF16),
                   jax.ShapeDtypeStruct((batch * n_chunks, n_state), F32),
                   jax.ShapeDtypeStruct((batch * n_chunks, n_state), F32)],
        scratch_shapes=[pltpu.VMEM((n_chunks, n_state), F32), pltpu.VMEM((n_chunks, n_state), F32)],
        compiler_params=_params(("parallel",)),
        name="s5_state_scan",
    )(u, u, u, u, mats['ws_re'], mats['ws_im'], mats['at_re'], mats['at_im'])
    lp = S5_LANE_PAIRS
    d_cols = jnp.broadcast_to(d.astype(F32).reshape(n_pairs, 1, 2 * SSM_GROUP),
                              (n_pairs, t, 2 * SSM_GROUP)).reshape(1, cols)
    return pl.pallas_call(
        _s5_out_kernel,
        grid=(batch, n_blk),
        in_specs=[pl.BlockSpec((n_chunks, lp * SSM_PAIR), lambda b, j: (b, j)),
                  pl.BlockSpec((lp, SSM_PAIR, SSM_PAIR), lambda b, j: (j, 0, 0)),
                  pl.BlockSpec((n_chunks, lp * st), lambda b, j: (b, j)),
                  pl.BlockSpec((n_chunks, lp * st), lambda b, j: (b, j)),
                  pl.BlockSpec((lp, st, SSM_PAIR), lambda b, j: (j, 0, 0)),
                  pl.BlockSpec((lp, st, SSM_PAIR), lambda b, j: (j, 0, 0)),
                  pl.BlockSpec((1, lp * SSM_PAIR), lambda b, j: (0, j))],
        out_specs=pl.BlockSpec((None, seq, LANES), lambda b, j: (j, b, 0)),
        out_shape=jax.ShapeDtypeStruct((n_blk, batch * seq, LANES), F32),
        scratch_shapes=[pltpu.VMEM((t, n_chunks, LANES), F32)],
        compiler_params=_params(("parallel", "parallel")),
        name="s5_out",
    )(x, mats['w_intra'], h_re, h_im, mats['wo_re'], mats['wo_im'], d_cols)


def _even_mix_kernel(ys_ref, bu_ref, bv_ref, h_ref, wglu_ref, lng_ref, lnb_ref, ws_ref, bias_ref,
                     wo_a_ref, wo_b_ref, g1_ref, g2_ref, hout_ref, z_ref, s_scr):
    tm = h_ref.shape[0]
    ys = jnp.concatenate([ys_ref[jb] for jb in range(ys_ref.shape[0])], axis=1)
    ya = ys * jax.nn.sigmoid(_dot(ys.astype(BF16), wglu_ref[...]))
    u = jax.nn.gelu(bu_ref[...].astype(F32))
    v = _layer_norm(jax.nn.gelu(bv_ref[...].astype(F32)), lng_ref[...], lnb_ref[...])
    lane = lax.broadcasted_iota(jnp.int32, v.shape, 1)
    left = (lane % LANES) < SGU_HEAD_DIM
    v_l = jnp.where(left, v, 0.0).astype(BF16)
    v_r = jnp.where(left, 0.0, v).astype(BF16)
    for c in range(tm // SGU_CHUNK):
        rows = slice(c * SGU_CHUNK, (c + 1) * SGU_CHUNK)
        for p in range(SGU_HEADS // 2):
            cols = slice(p * LANES, (p + 1) * LANES)
            s_scr[rows, cols] = (_dot(ws_ref[2 * p], v_l[rows, cols]) + _dot(ws_ref[2 * p + 1], v_r[rows, cols]))
    bias = jnp.concatenate([bias_ref[...]] * (tm // SGU_CHUNK), axis=0)
    yb = u * (s_scr[...] + bias)
    mix = _dot(ya.astype(BF16), wo_a_ref[...]) + _dot(yb.astype(BF16), wo_b_ref[...])
    h_new = h_ref[...] + _rms(mix, g1_ref[...])
    hout_ref[...] = h_new
    z_ref[...] = _rms(h_new, g2_ref[...]).astype(z_ref.dtype)


def _even_mix(ys, proj, h, wglu, lng, lnb, ws, bias, wo_a, wo_b, g1, g2, tm):
    n, d = h.shape
    w = SGU_WIDTH
    const = lambda *shape: pl.BlockSpec(shape, lambda i: (0,) * len(shape))
    return pl.pallas_call(
        _even_mix_kernel,
        grid=(n // tm,),
        in_specs=[pl.BlockSpec((w // LANES, tm, LANES), lambda i: (0, i, 0)),
                  pl.BlockSpec((tm, w), lambda i: (i, 0)),
                  pl.BlockSpec((tm, w), lambda i: (i, 1)),
                  pl.BlockSpec((tm, d), lambda i: (i, 0)),
                  const(w, w), const(1, w), const(1, w),
                  const(SGU_HEADS, SGU_CHUNK, SGU_CHUNK), const(SGU_CHUNK, w),
                  const(w, d), const(w, d), const(1, d), const(1, d)],
        out_specs=[pl.BlockSpec((tm, d), lambda i: (i, 0)),
                   pl.BlockSpec((tm, d), lambda i: (i, 0))],
        out_shape=[jax.ShapeDtypeStruct((n, d), F32), jax.ShapeDtypeStruct((n, d), BF16)],
        scratch_shapes=[pltpu.VMEM((tm, w), F32)],
        compiler_params=_params(("parallel",)),
        name="even_mix",
    )(ys, proj, proj, h, wglu, lng, lnb, ws, bias, wo_a, wo_b, g1, g2)


def _ffn_kernel(z_ref, wg_ref, wu_ref, wd_ref, h_ref, g3_ref, gn_ref, hout_ref, zout_ref, acc_ref):
    j = pl.program_id(1)

    @pl.when(j == 0)
    def _():
        acc_ref[...] = jnp.zeros_like(acc_ref)

    z = z_ref[...]
    a = jax.nn.silu(_dot(z, wg_ref[...])) * _dot(z, wu_ref[...])
    acc_ref[...] += _dot(a.astype(BF16), wd_ref[...])

    @pl.when(j == pl.num_programs(1) - 1)
    def _():
        h_new = h_ref[...] + _rms(acc_ref[...], g3_ref[...])
        hout_ref[...] = h_new
        zout_ref[...] = _rms(h_new, gn_ref[...]).astype(zout_ref.dtype)


def _dense_ffn(z, wg, wu, wd, h, g3, g_next, tm, tf):
    n, d = h.shape
    ff = wg.shape[1]
    return pl.pallas_call(
        _ffn_kernel,
        grid=(n // tm, ff // tf),
        in_specs=[pl.BlockSpec((tm, d), lambda i, j: (i, 0)),
                  pl.BlockSpec((d, tf), lambda i, j: (0, j)),
                  pl.BlockSpec((d, tf), lambda i, j: (0, j)),
                  pl.BlockSpec((tf, d), lambda i, j: (j, 0)),
                  pl.BlockSpec((tm, d), lambda i, j: (i, 0)),
                  pl.BlockSpec((1, d), lambda i, j: (0, 0)),
                  pl.BlockSpec((1, d), lambda i, j: (0, 0))],
        out_specs=[pl.BlockSpec((tm, d), lambda i, j: (i, 0)),
                   pl.BlockSpec((tm, d), lambda i, j: (i, 0))],
        out_shape=[jax.ShapeDtypeStruct((n, d), F32), jax.ShapeDtypeStruct((n, d), BF16)],
        scratch_shapes=[pltpu.VMEM((tm, d), F32)],
        compiler_params=_params(("parallel", "arbitrary")),
        name="dense_ffn",
    )(z, wg, wu, wd, h, g3, g_next)


def _odd_proj_kernel(z_ref, win_ref, gq_ref, gkv_ref, wuq_ref, wuqs_ref, wuk_ref, wuv_ref, vone_ref, cos_ref, sin_ref,
                     zc_ref, q_ref, k_ref, v_ref, *, scale):
    z = z_ref[...]
    proj = _dot(z, win_ref[...])
    c0 = 2 * CONV_CH
    c1 = c0 + MLA_Q_RANK
    c2 = c1 + MLA_KV_RANK
    c3 = c2 + MLA_PAD
    zc_ref[...] = proj[:, :c0].astype(zc_ref.dtype)
    cq = _rms(proj[:, c0:c1], gq_ref[...]).astype(BF16)
    ckv = _rms(proj[:, c1:c2], gkv_ref[...]).astype(BF16)
    cos = cos_ref[...]
    sin = sin_ref[...]
    cos_h = jnp.concatenate([cos] * MLA_HEADS, axis=1)
    sin_h = jnp.concatenate([sin] * MLA_HEADS, axis=1)
    q = _dot(cq, wuq_ref[...]) * cos_h + _dot(cq, wuqs_ref[...]) * sin_h
    q_ref[...] = (q * scale).astype(q_ref.dtype)
    kr = proj[:, c2:c3] * cos + proj[:, c3:] * sin
    k = _dot(ckv, wuk_ref[...]) + jnp.concatenate([kr] * MLA_HEADS, axis=1)
    k_ref[...] = k.astype(k_ref.dtype)
    vt = lax.dot_general(wuv_ref[...], ckv, (((1,), (1,)), ((), ())), preferred_element_type=F32)
    v_ref[0] = (vt + vone_ref[...]).astype(v_ref.dtype)


def _odd_proj(z, win, gq, gkv, wuq, wuqs, wuk, wuv_t, v_one, cos_t, sin_t, seq, tm):
    n, d = z.shape
    hp = MLA_HEADS * MLA_PAD
    vr = MLA_HEADS * MLA_VROWS
    n_l = seq // tm
    const = lambda *shape: pl.BlockSpec(shape, lambda i: (0,) * len(shape))
    out = jax.ShapeDtypeStruct((n, hp), BF16)
    scale = float((MLA_NOPE + MLA_ROPE) ** -0.5 * math.log2(math.e))
    return pl.pallas_call(
        functools.partial(_odd_proj_kernel, scale=scale),
        grid=(n // tm,),
        in_specs=[pl.BlockSpec((tm, d), lambda i: (i, 0)),
                  const(d, win.shape[1]), const(1, MLA_Q_RANK), const(1, MLA_KV_RANK),
                  const(MLA_Q_RANK, hp), const(MLA_Q_RANK, hp), const(MLA_KV_RANK, hp), const(vr, MLA_KV_RANK),
                  const(vr, 1),
                  pl.BlockSpec((tm, MLA_PAD), lambda i: (i % n_l, 0)),
                  pl.BlockSpec((tm, MLA_PAD), lambda i: (i % n_l, 0))],
        out_specs=[pl.BlockSpec((tm, 2 * CONV_CH), lambda i: (i, 0)),
                   pl.BlockSpec((tm, hp), lambda i: (i, 0)),
                   pl.BlockSpec((tm, hp), lambda i: (i, 0)),
                   pl.BlockSpec((1, vr, tm), lambda i: (i, 0, 0))],
        out_shape=[jax.ShapeDtypeStruct((n, 2 * CONV_CH), BF16), out, out,
                   jax.ShapeDtypeStruct((n // tm, vr, tm), BF16)],
        compiler_params=_params(("parallel",)),
        name="odd_in_proj",
    )(z, win, gq, gkv, wuq, wuqs, wuk, wuv_t, v_one, cos_t, sin_t)


def _attn_kernel(q_ref, k_ref, vt_ref, wg_ref, wu_ref, wd_ref, o_ref, wgb_ref, wub_ref, wdb_ref, acc_ref, *, blk):
    i = pl.program_id(2)
    acc_ref[...] = jnp.zeros_like(acc_ref)
    tf = wgb_ref.shape[3]
    for f in range(wgb_ref.shape[1]):
        wgb_ref[0, f] = wg_ref[0, :, f * tf:(f + 1) * tf].astype(BF16)
        wub_ref[0, f] = wu_ref[0, :, f * tf:(f + 1) * tf].astype(BF16)
    wdb_ref[0] = wd_ref[0].astype(BF16)

    def step(j, m, masked):
        r0 = pl.multiple_of(j * blk, blk)
        scores = []
        for hh in range(ATTN_HEADS):
            q = q_ref[0, :, hh * MLA_PAD:(hh + 1) * MLA_PAD]
            k = k_ref[0, pl.ds(r0, blk), hh * MLA_PAD:(hh + 1) * MLA_PAD]
            st = lax.dot_general(k, q, (((1,), (1,)), ((), ())), preferred_element_type=F32)
            if masked:
                key = lax.broadcasted_iota(jnp.int32, st.shape, 0)
                qry = lax.broadcasted_iota(jnp.int32, st.shape, 1)
                st = jnp.where(key <= qry, st, -1e30)
            scores.append(st)
        soft = []
        for hh in range(ATTN_HEADS):
            m_new = jnp.maximum(m[hh], jnp.max(scores[hh], axis=0, keepdims=True))
            soft.append((m_new, jnp.exp2(m[hh] - m_new), jnp.exp2(scores[hh] - m_new).astype(BF16)))
        for hh in range(ATTN_HEADS):
            vt = vt_ref[j, hh * MLA_VROWS:(hh + 1) * MLA_VROWS, :]
            acc_ref[hh] = soft[hh][1] * acc_ref[hh] + _dot(vt, soft[hh][2])
        return tuple(s[0] for s in soft)

    init = jnp.full((1, blk), -1e30, F32)
    m = lax.fori_loop(0, i, lambda j, m: step(j, m, False), (init,) * ATTN_HEADS)
    step(i, m, True)
    ot = jnp.concatenate([acc_ref[hh][:MLA_V] / acc_ref[hh][MLA_V:MLA_V + 1] for hh in range(ATTN_HEADS)], axis=0)
    o_ref[0] = ot.T.astype(o_ref.dtype)


def _attention(q, k, vt, blk, wg, wu, wd, tf):
    b, seq, _ = q.shape
    n_blk = seq // blk
    n_pairs = MLA_HEADS // ATTN_HEADS
    n_e, d, ff = wg.shape
    steps = b * n_pairs * n_blk
    per_e = steps // n_e
    assert steps == per_e * n_e and d % per_e == 0 and ff % per_e == 0
    rows_in, rows_down = d // per_e, ff // per_e
    assert rows_in % 16 == 0 and rows_down % 16 == 0 and ff % tf == 0

    def lin(bi, p, i):
        return (bi * n_pairs + p) * n_blk + i

    w_in = pl.BlockSpec((1, rows_in, ff), lambda bi, p, i: (lin(bi, p, i) // per_e, lin(bi, p, i) % per_e, 0))
    w_out = pl.BlockSpec((1, ff // tf, rows_in, tf),
                         lambda bi, p, i: (lin(bi, p, i) // per_e, 0, lin(bi, p, i) % per_e, 0))
    w_down = pl.BlockSpec((1, rows_down, d), lambda bi, p, i: (lin(bi, p, i) // per_e, lin(bi, p, i) % per_e, 0))
    return pl.pallas_call(
        functools.partial(_attn_kernel, blk=blk),
        grid=(b, n_pairs, n_blk),
        in_specs=[pl.BlockSpec((1, blk, ATTN_HEADS * MLA_PAD), lambda bi, p, i: (bi, i, p)),
                  pl.BlockSpec((1, seq, ATTN_HEADS * MLA_PAD), lambda bi, p, i: (bi, 0, p)),
                  pl.BlockSpec((n_blk, ATTN_HEADS * MLA_VROWS, blk), lambda bi, p, i: (bi, p, 0)),
                  w_in, w_in, w_down],
        out_specs=[pl.BlockSpec((1, blk, ATTN_HEADS * MLA_V), lambda bi, p, i: (bi, i, p)), w_out, w_out, w_down],
        out_shape=[jax.ShapeDtypeStruct((b, seq, MLA_HEADS * MLA_V), BF16),
                   jax.ShapeDtypeStruct((n_e, ff // tf, d, tf), BF16),
                   jax.ShapeDtypeStruct((n_e, ff // tf, d, tf), BF16),
                   jax.ShapeDtypeStruct((n_e, ff, d), BF16)],
        scratch_shapes=[pltpu.VMEM((ATTN_HEADS, MLA_VROWS, blk), F32)],
        compiler_params=_params(("parallel", "parallel", "parallel")),
        name="mla_attention",
    )(q, k, vt, wg, wu, wd)


def _conv_kernel(zc_ref, w_ref, b_ref, lng_ref, lnb_ref, y_ref, buf_ref, part_ref):
    tm = zc_ref.shape[1]

    @pl.when(pl.program_id(1) == 0)
    def _():
        buf_ref[pl.ds(0, CONV_HALO), :] = jnp.zeros((CONV_HALO, CONV_CH), F32)
        buf_ref[pl.ds(CONV_HALO + tm, SUBLANES), :] = jnp.zeros((SUBLANES, CONV_CH), F32)

    zc = zc_ref[0].astype(F32)
    hh = zc[:, :CONV_CH] * jax.nn.sigmoid(zc[:, CONV_CH:])
    buf_ref[pl.ds(CONV_HALO, tm), :] = hh
    off = CONV_HALO - (CONV_TAPS - 1)
    acc = jnp.zeros((tm, CONV_CH), F32) + b_ref[...]
    for b in range(SUBLANES):
        taps = [k for k in range(CONV_TAPS) if (off + k) % SUBLANES == b]
        part = None
        for k in taps:
            term = w_ref[pl.ds(k, 1), :] * buf_ref[pl.ds(off + k - b, tm + SUBLANES), :]
            part = term if part is None else part + term
        if b == 0:
            acc = acc + part[:tm]
        else:
            part_ref[...] = part
            acc = acc + part_ref[pl.ds(b, tm), :]
    buf_ref[pl.ds(0, CONV_HALO), :] = buf_ref[pl.ds(tm, CONV_HALO), :]
    y_ref[0] = jax.nn.silu(_layer_norm(acc, lng_ref[...], lnb_ref[...])).astype(y_ref.dtype)


def _conv_mixer(zc, w, b, lng, lnb, tm):
    bsz, seq, _ = zc.shape
    const = lambda *shape: pl.BlockSpec(shape, lambda bi, i: (0,) * len(shape))
    return pl.pallas_call(
        _conv_kernel,
        grid=(bsz, seq // tm),
        in_specs=[pl.BlockSpec((1, tm, 2 * CONV_CH), lambda bi, i: (bi, i, 0)),
                  const(CONV_HALO, CONV_CH), const(1, CONV_CH), const(1, CONV_CH), const(1, CONV_CH)],
        out_specs=pl.BlockSpec((1, tm, CONV_CH), lambda bi, i: (bi, i, 0)),
        out_shape=jax.ShapeDtypeStruct((bsz, seq, CONV_CH), BF16),
        scratch_shapes=[pltpu.VMEM((CONV_HALO + tm + SUBLANES, CONV_CH), F32),
                        pltpu.VMEM((tm + SUBLANES, CONV_CH), F32)],
        compiler_params=_params(("arbitrary", "arbitrary")),
        name="conv_module",
    )(zc, w, b, lng, lnb)


def _odd_mix_kernel(yc_ref, yd_ref, h_ref, wo_a_ref, wo_b_ref, g1_ref, g2_ref, wr_ref, hout_ref, z_ref, route_ref):
    tm = h_ref.shape[0]
    halves = [slice(0, tm // 2), slice(tm // 2, tm)]
    mixes = [_dot(yc_ref[r, :], wo_a_ref[...]) + _dot(yd_ref[r, :], wo_b_ref[...]) for r in halves]
    zs = []
    for r, mix in zip(halves, mixes):
        h_new = h_ref[r, :] + _rms(mix, g1_ref[...])
        hout_ref[r, :] = h_new
        zs.append(_rms(h_new, g2_ref[...]))
    all_logits = [_dot(z.astype(BF16), wr_ref[...]) for z in zs]
    _store_row_tiles(z_ref, jnp.concatenate(zs, axis=0))
    neg = -jnp.inf
    for r, logits in zip(halves, all_logits):
        lane = lax.broadcasted_iota(jnp.int32, logits.shape, 1)
        logits = jnp.where(lane < N_EXPERTS, logits, neg)
        m1 = jnp.max(logits, axis=-1, keepdims=True)
        i1 = jnp.min(jnp.where(logits == m1, lane, LANES), axis=-1, keepdims=True)
        rest = jnp.where(lane == i1, neg, logits)
        m2 = jnp.max(rest, axis=-1, keepdims=True)
        i2 = jnp.min(jnp.where(rest == m2, lane, LANES), axis=-1, keepdims=True)
        e = jnp.exp(m2 - m1)
        w1 = 1.0 / (1.0 + e)
        w2 = e / (1.0 + e)
        route_ref[r, :] = jnp.where(lane == 0, i1.astype(F32),
                                    jnp.where(lane == 1, i2.astype(F32),
                                              jnp.where(lane == 2, w1, jnp.where(lane == 3, w2, 0.0))))


def _odd_mix(yc, yd, h, wo_a, wo_b, g1, g2, wr, tm):
    n, d = h.shape
    const = lambda *shape: pl.BlockSpec(shape, lambda i: (0,) * len(shape))
    return pl.pallas_call(
        _odd_mix_kernel,
        grid=(n // tm,),
        in_specs=[pl.BlockSpec((tm, yc.shape[1]), lambda i: (i, 0)),
                  pl.BlockSpec((tm, yd.shape[1]), lambda i: (i, 0)),
                  pl.BlockSpec((tm, d), lambda i: (i, 0)),
                  const(*wo_a.shape), const(*wo_b.shape), const(1, d), const(1, d), const(d, LANES)],
        out_specs=[pl.BlockSpec((tm, d), lambda i: (i, 0)),
                   pl.BlockSpec((tm * ROW_TILE, LANES), lambda i: (i, 0)),
                   pl.BlockSpec((tm, LANES), lambda i: (i, 0))],
        out_shape=[jax.ShapeDtypeStruct((n, d), F32), jax.ShapeDtypeStruct((n * ROW_TILE, LANES), F32),
                   jax.ShapeDtypeStruct((n, LANES), F32)],
        compiler_params=_params(("parallel",)),
        name="odd_mix_router",
    )(yc, yd, h, wo_a, wo_b, g1, g2, wr)


def _store_row_tiles(ref, x):
    rows = x.shape[0]
    for s in range(ROW_TILE):
        ref[pl.ds(s, rows, stride=ROW_TILE), :] = x[:, s * LANES:(s + 1) * LANES]


def _load_row_tiles(ref, rows):
    return [ref[pl.ds(s, rows, stride=ROW_TILE), :] for s in range(ROW_TILE)]


def _gather_rows(idx_ref, base, n_rows, src_hbm, dst_ref, sem):
    def body(r, c):
        src = pl.multiple_of(idx_ref[base + r] * ROW_TILE, ROW_TILE)
        dst = pl.multiple_of(r * ROW_TILE, ROW_TILE)
        pltpu.make_async_copy(src_hbm.at[pl.ds(src, ROW_TILE), :], dst_ref.at[pl.ds(dst, ROW_TILE), :], sem).start()
        return c

    lax.fori_loop(0, n_rows, body, 0, unroll=8)


def _wait_rows(src_hbm, dst_ref, sem):
    pltpu.make_async_copy(src_hbm.at[pl.ds(0, dst_ref.shape[0]), :], dst_ref, sem).wait()


def _row_copy(src_ref, src_row, dst_ref, dst_row, sem):
    src = pl.multiple_of(src_row * ROW_TILE, ROW_TILE)
    dst = pl.multiple_of(dst_row * ROW_TILE, ROW_TILE)
    return pltpu.make_async_copy(src_ref.at[pl.ds(src, ROW_TILE), :], dst_ref.at[pl.ds(dst, ROW_TILE), :], sem)


def _moe_ffn_kernel(te_ref, nu_ref, tok_ref, dst_ref, z_hbm, wg_ref, wu_ref, wd_ref, y_hbm,
                    xraw_ref, xb_ref, acc_ref, yst_ref, gsem, ssem, *, rows_per_step):
    i = pl.program_id(0)
    j = pl.program_id(1)
    tm = xb_ref.shape[0]
    stride = yst_ref.shape[0] // ROW_TILE
    n_used = nu_ref[0]
    slot = i % 2
    first = j == 0
    last = j == pl.num_programs(1) - 1

    @pl.when(first & (i == 0))
    def _():
        yst_ref[...] = jnp.zeros_like(yst_ref)
        _gather_rows(tok_ref, 0, stride, z_hbm, xraw_ref.at[0], gsem.at[0])

    @pl.when(first & (i <= n_used))
    def _():
        _wait_rows(z_hbm, xraw_ref.at[slot], gsem.at[slot])

    @pl.when(first & (i < n_used))
    def _():
        for s, blk in enumerate(_load_row_tiles(xraw_ref.at[slot], tm)):
            xb_ref[:, s * LANES:(s + 1) * LANES] = blk.astype(BF16)
        acc_ref[...] = jnp.zeros_like(acc_ref)

    @pl.when(first & (i == n_used))
    def _():
        def body(r, c):
            _row_copy(yst_ref, r, y_hbm, dst_ref[i * stride + r], ssem).start()
            return c
        lax.fori_loop(0, stride, body, 0, unroll=8)

    def multiply(rows, with_copies):
        x = xb_ref[:rows, :]
        g = _dot(x, wg_ref[0, 0])
        u = _dot(x, wu_ref[0, 0])
        if with_copies:
            nxt = xraw_ref.at[1 - slot]
            for rr in range(rows_per_step):
                r = j * rows_per_step + rr
                _row_copy(z_hbm, tok_ref[(i + 1) * stride + r], nxt, r, gsem.at[1 - slot]).start(priority=rr % 2)
                _row_copy(yst_ref, r, y_hbm, dst_ref[i * stride + r], ssem).start(priority=rr % 2)
        a = jax.nn.silu(g) * u
        acc_ref[:rows, :] += _dot(a.astype(BF16), wd_ref[0])

    used = i < n_used
    half = nu_ref[1 + i] <= tm // 2
    for rows, fits in ((tm, jnp.logical_not(half)), (tm // 2, half)):
        @pl.when(used & fits & jnp.logical_not(last))
        def _(rows=rows):
            multiply(rows, True)

        @pl.when(used & fits & last)
        def _(rows=rows):
            multiply(rows, False)

    @pl.when(last & (i <= n_used))
    def _():
        _wait_rows(z_hbm, yst_ref, ssem)

    @pl.when(last & (i < n_used))
    def _():
        _store_row_tiles(yst_ref, acc_ref[...])


def _moe_ffn(tile_expert, n_used, tok_tab, dst_tab, z_tiles, wg, wu, wd, n_tok, tm):
    n_f, d, tf = wg.shape[1], wg.shape[2], wg.shape[3]
    copy_steps = n_f - 1
    rows_per_step = -(-tm // copy_steps)
    stride = copy_steps * rows_per_step
    n_tiles = tile_expert.shape[0]
    assert tok_tab.shape[0] == dst_tab.shape[0] == (n_tiles + 1) * stride

    def col(i, j, nu):
        return jnp.where(i < nu[0], j, n_f - 1)

    return pl.pallas_call(
        functools.partial(_moe_ffn_kernel, rows_per_step=rows_per_step),
        grid_spec=pltpu.PrefetchScalarGridSpec(
            num_scalar_prefetch=4,
            grid=(n_tiles, n_f),
            in_specs=[pl.BlockSpec(memory_space=pl.ANY),
                      pl.BlockSpec((1, 1, d, tf), lambda i, j, te, nu, tok, dst: (te[i], col(i, j, nu), 0, 0)),
                      pl.BlockSpec((1, 1, d, tf), lambda i, j, te, nu, tok, dst: (te[i], col(i, j, nu), 0, 0)),
                      pl.BlockSpec((1, tf, d), lambda i, j, te, nu, tok, dst: (te[i], col(i, j, nu), 0))],
            out_specs=pl.BlockSpec(memory_space=pl.ANY),
            scratch_shapes=[pltpu.VMEM((2, stride * ROW_TILE, LANES), F32), pltpu.VMEM((tm, d), BF16),
                            pltpu.VMEM((tm, d), F32), pltpu.VMEM((stride * ROW_TILE, LANES), F32),
                            pltpu.SemaphoreType.DMA((2,)), pltpu.SemaphoreType.DMA(())]),
        out_shape=jax.ShapeDtypeStruct(((2 * n_tok + stride) * ROW_TILE, LANES), F32),
        compiler_params=_params(("arbitrary", "arbitrary")),
        name="moe_grouped_ffn",
    )(tile_expert, n_used, tok_tab, dst_tab, z_tiles, wg, wu, wd)


def _combine_kernel(ya_ref, yb_ref, route_ref, h_ref, g_ref, o_ref):
    tm = h_ref.shape[0]
    route = route_ref[...]
    a = jnp.concatenate(_load_row_tiles(ya_ref, tm), axis=1)
    b = jnp.concatenate(_load_row_tiles(yb_ref, tm), axis=1)
    f = route[:, 2:3] * a + route[:, 3:4] * b
    o_ref[...] = h_ref[...] + _rms(f, g_ref[...])


def _combine(y, route, h, g, tm):
    n, d = h.shape
    n_blk = n // tm
    return pl.pallas_call(
        _combine_kernel,
        grid=(n_blk,),
        in_specs=[pl.BlockSpec((tm * ROW_TILE, LANES), lambda i: (i, 0)),
                  pl.BlockSpec((tm * ROW_TILE, LANES), lambda i: (n_blk + i, 0)),
                  pl.BlockSpec((tm, LANES), lambda i: (i, 0)),
                  pl.BlockSpec((tm, d), lambda i: (i, 0)),
                  pl.BlockSpec((1, d), lambda i: (0, 0))],
        out_specs=pl.BlockSpec((tm, d), lambda i: (i, 0)),
        out_shape=jax.ShapeDtypeStruct((n, d), F32),
        compiler_params=_params(("parallel",)),
        name="moe_combine",
    )(y, y, route, h, g)


def _moe_plan(route, tm, stride):
    n = route.shape[0]
    eids = jnp.concatenate([route[:, 0], route[:, 1]]).astype(jnp.int32)
    onehot = (eids[:, None] == jnp.arange(N_EXPERTS, dtype=jnp.int32)[None, :]).astype(jnp.int32)
    csum = jnp.cumsum(onehot, axis=0)
    rank = jnp.sum(csum * onehot, axis=1) - 1
    counts = csum[-1]
    padded = ((counts + tm - 1) // tm) * tm
    ends = jnp.cumsum(padded)
    starts = ends - padded
    slot = jnp.sum(onehot * starts[None, :], axis=1) + rank
    n_tiles = 2 * n // tm + N_EXPERTS + 1
    copy_of_slot = jnp.full((n_tiles * tm,), -1, jnp.int32).at[slot].set(jnp.arange(2 * n, dtype=jnp.int32))
    copy_tab = jnp.pad(copy_of_slot.reshape(n_tiles, tm), ((0, 1), (0, stride - tm)), constant_values=-1)
    tok_tab = jnp.where(copy_tab >= 0, copy_tab % n, 0)
    dump = 2 * n + jnp.arange(stride, dtype=jnp.int32)[None, :]
    dst_tab = jnp.where(copy_tab >= 0, copy_tab, dump)
    dst_tab = jnp.concatenate([jnp.broadcast_to(dump, (1, stride)), dst_tab[:-1]], axis=0)
    n_used = (ends[-1] // tm).astype(jnp.int32)
    tile_start = jnp.minimum(jnp.arange(n_tiles, dtype=jnp.int32), n_used - 1) * tm
    tile_expert = jnp.sum((tile_start[:, None] >= ends[None, :]).astype(jnp.int32), axis=1)
    onehot_e = (tile_expert[:, None] == jnp.arange(N_EXPERTS, dtype=jnp.int32)[None, :]).astype(jnp.int32)
    run_end = jnp.sum(onehot_e * (starts + counts)[None, :], axis=1)
    tile_rows = jnp.clip(run_end - jnp.arange(n_tiles, dtype=jnp.int32) * tm, 0, tm)
    return tok_tab.reshape(-1), dst_tab.reshape(-1), tile_expert, jnp.concatenate([n_used.reshape(1), tile_rows])


def _odd_weights(od_w_in, mla_w_uq, mla_w_ukv):
    c2 = 2 * CONV_CH + MLA_Q_RANK + MLA_KV_RANK
    half = MLA_ROPE // 2
    w_kr = od_w_in[:, c2:]
    w_kr_sw = jnp.concatenate([w_kr[:, half:], w_kr[:, :half]], axis=1)
    zl = jnp.zeros((D_MODEL, MLA_NOPE), F32)
    zr = jnp.zeros((D_MODEL, MLA_PAD - MLA_NOPE - MLA_ROPE), F32)
    win = jnp.concatenate([od_w_in[:, :c2], zl, w_kr, zr, zl, w_kr_sw, zr], axis=1)
    dk = MLA_NOPE + MLA_ROPE
    wq = mla_w_uq.reshape(MLA_Q_RANK, MLA_HEADS, dk)
    zq = jnp.zeros((MLA_Q_RANK, MLA_HEADS, MLA_PAD - dk), F32)
    wuq = jnp.concatenate([wq, zq], axis=2).reshape(MLA_Q_RANK, MLA_HEADS * MLA_PAD)
    wq_sw = jnp.concatenate([jnp.zeros_like(wq[:, :, :MLA_NOPE]), wq[:, :, MLA_NOPE + half:],
                             wq[:, :, MLA_NOPE:MLA_NOPE + half], zq], axis=2)
    wuqs = wq_sw.reshape(MLA_Q_RANK, MLA_HEADS * MLA_PAD)
    wkv = mla_w_ukv.reshape(MLA_KV_RANK, MLA_HEADS, MLA_NOPE + MLA_V)
    zk = jnp.zeros((MLA_KV_RANK, MLA_HEADS, MLA_PAD - MLA_NOPE), F32)
    wuk = jnp.concatenate([wkv[:, :, :MLA_NOPE], zk], axis=2).reshape(MLA_KV_RANK, MLA_HEADS * MLA_PAD)
    zv = jnp.zeros((MLA_KV_RANK, MLA_HEADS, MLA_VROWS - MLA_V), F32)
    wuv_t = jnp.concatenate([wkv[:, :, MLA_NOPE:], zv], axis=2).reshape(MLA_KV_RANK, MLA_HEADS * MLA_VROWS).T
    v_one = jnp.zeros((MLA_HEADS, MLA_VROWS), F32).at[:, MLA_V].set(1.0).reshape(MLA_HEADS * MLA_VROWS, 1)
    return win.astype(BF16), wuq.astype(BF16), wuqs.astype(BF16), wuk.astype(BF16), wuv_t.astype(BF16), v_one


def _rope_tables(seq):
    inv = 1.0 / (ROPE_THETA ** (jnp.arange(0, MLA_ROPE, 2, dtype=F32) / MLA_ROPE))
    ang = jnp.arange(seq, dtype=F32)[:, None] * inv[None, :]
    cos, sin = jnp.cos(ang), jnp.sin(ang)
    ones = jnp.ones((seq, MLA_NOPE), F32)
    zl = jnp.zeros((seq, MLA_NOPE), F32)
    zr = jnp.zeros((seq, MLA_PAD - MLA_NOPE - MLA_ROPE), F32)
    return (jnp.concatenate([ones, cos, cos, zr], axis=1), jnp.concatenate([zl, -sin, sin, zr], axis=1))


def kernel(x, norm_g, ev_w_in, ssm_lambda_re, ssm_lambda_im, ssm_log_dt, ssm_b_re, ssm_b_im, ssm_c_re, ssm_c_im, ssm_d, ssm_w_glu, sgu_ln_g, sgu_ln_b, sgu_w, sgu_b, ev_w_out, ffn_w_gate, ffn_w_up, ffn_w_down, od_w_in, conv_w, conv_b, conv_ln_g, conv_ln_b, mla_q_norm_g, mla_w_uq, mla_kv_norm_g, mla_w_ukv, od_w_out, moe_w_router, moe_w_gate, moe_w_up, moe_w_down):
    bsz, seq, d = x.shape
    n = bsz * seq
    assert d == D_MODEL and SUBLANES % bsz == 0 and seq % 512 == 0
    row = lambda v: v.astype(F32).reshape(1, -1)
    h = x.astype(F32).reshape(n, d)
    tm = 512

    g = norm_g[0]
    a_in, proj = _norm_proj(h, row(g[0]), ev_w_in[0].astype(BF16), tm)
    mats = _s5_matrices(ssm_lambda_re[0], ssm_lambda_im[0], ssm_log_dt[0], ssm_b_re[0], ssm_b_im[0],
                        ssm_c_re[0], ssm_c_im[0])
    ys = _s5_mixer(a_in, mats, ssm_d[0], bsz, seq)
    causal = jnp.tril(jnp.ones((SGU_CHUNK, SGU_CHUNK), dtype=bool))
    ws = jnp.where(causal[None], sgu_w[0], 0.0).astype(BF16)
    bias = jnp.repeat(sgu_b[0].astype(F32).T, SGU_HEAD_DIM, axis=1)
    wo = ev_w_out[0].astype(BF16)
    h, z = _even_mix(ys, proj, h, ssm_w_glu[0].astype(BF16), row(sgu_ln_g[0]), row(sgu_ln_b[0]), ws, bias,
                     wo[:SSM_WIDTH], wo[SSM_WIDTH:], row(g[1]), row(g[2]), tm)
    h, z = _dense_ffn(z, ffn_w_gate[0].astype(BF16), ffn_w_up[0].astype(BF16), ffn_w_down[0].astype(BF16),
                      h, row(g[3]), row(norm_g[1][0]), 1024, 1024)

    g = norm_g[1]
    win, wuq, wuqs, wuk, wuv_t, v_one = _odd_weights(od_w_in[0], mla_w_uq[0], mla_w_ukv[0])
    cos_t, sin_t = _rope_tables(seq)
    zc, q, k, vt = _odd_proj(z, win, row(mla_q_norm_g[0]), row(mla_kv_norm_g[0]), wuq, wuqs, wuk, wuv_t, v_one,
                             cos_t, sin_t, seq, tm)
    hp = MLA_HEADS * MLA_PAD
    tm_moe, tf_moe = 1024, 512
    yd, wg_b, wu_b, wd_b = _attention(q.reshape(bsz, seq, hp), k.reshape(bsz, seq, hp), vt, tm,
                                      moe_w_gate[0], moe_w_up[0], moe_w_down[0], tf_moe)
    conv_w_pad = jnp.concatenate([conv_w[0].astype(F32), jnp.zeros((CONV_HALO - CONV_TAPS, CONV_CH), F32)], axis=0)
    yc = _conv_mixer(zc.reshape(bsz, seq, 2 * CONV_CH), conv_w_pad, row(conv_b[0]), row(conv_ln_g[0]),
                     row(conv_ln_b[0]), tm)
    wo = od_w_out[0].astype(BF16)
    wr = jnp.concatenate([moe_w_router[0].astype(F32), jnp.zeros((d, LANES - N_EXPERTS), F32)], axis=1)
    h, z, route = _odd_mix(yc.reshape(n, CONV_CH), yd.reshape(n, MLA_HEADS * MLA_V), h, wo[:CONV_CH], wo[CONV_CH:],
                           row(g[1]), row(g[2]), wr.astype(BF16), tm)
    n_f = wg_b.shape[1]
    tok_tab, dst_tab, tile_expert, n_used = _moe_plan(route, tm_moe, (n_f - 1) * -(-tm_moe // (n_f - 1)))
    y = _moe_ffn(tile_expert, n_used, tok_tab, dst_tab, z, wg_b, wu_b, wd_b, n, tm_moe)
    h = _combine(y, route, h, row(g[3]), 256)
    return h.reshape(bsz, seq, d).astype(x.dtype)
```

```python
import functools
import math

import jax
import jax.numpy as jnp
from jax import lax
from jax.experimental import pallas as pl
from jax.experimental.pallas import tpu as pltpu

F32 = jnp.float32
BF16 = jnp.bfloat16

D_MODEL = 1024
NORM_EPS = 1e-6
SSM_WIDTH = 512
SSM_GROUP = 16
SSM_GROUPS = 32
SSM_STATE = 64
SSM_CHUNK = 16
SSM_PAIR = 2 * SSM_GROUP * SSM_CHUNK
SGU_WIDTH = 512
SGU_HEADS = 8
SGU_HEAD_DIM = 64
SGU_CHUNK = 128
CONV_CH = 512
CONV_TAPS = 31
CONV_HALO = 32
MLA_HEADS = 8
MLA_Q_RANK = 256
MLA_KV_RANK = 128
MLA_NOPE = 64
MLA_ROPE = 32
MLA_V = 64
MLA_PAD = 128
MLA_VROWS = 80
ATTN_HEADS = 4
ROPE_THETA = 10000.0
FF_DENSE = 4096
N_EXPERTS = 8
FF_EXPERT = 3584
LANES = 128
SUBLANES = 8
ROW_TILE = D_MODEL // LANES
VMEM_LIMIT = 56 * 1024 * 1024


def _params(sem, vmem=VMEM_LIMIT):
    return pltpu.CompilerParams(dimension_semantics=sem, vmem_limit_bytes=vmem)


def _rms(x, g):
    return x * lax.rsqrt(jnp.mean(x * x, axis=-1, keepdims=True) + NORM_EPS) * g


def _layer_norm(x, g, b):
    mu = jnp.mean(x, axis=-1, keepdims=True)
    xc = x - mu
    return xc * lax.rsqrt(jnp.mean(xc * xc, axis=-1, keepdims=True) + NORM_EPS) * g + b


def _dot(a, b):
    return jnp.dot(a, b, preferred_element_type=F32)


def _norm_proj_kernel(h_ref, g_ref, w_ref, a_ref, b_ref):
    z = _rms(h_ref[...], g_ref[...])
    proj = _dot(z.astype(BF16), w_ref[...])
    for jb in range(SSM_WIDTH // LANES):
        a_ref[jb] = proj[:, jb * LANES:(jb + 1) * LANES]
    b_ref[...] = proj[:, SSM_WIDTH:].astype(b_ref.dtype)


def _norm_proj(h, g, w, tm):
    n, d = h.shape
    cols = w.shape[1]
    return pl.pallas_call(
        _norm_proj_kernel,
        grid=(n // tm,),
        in_specs=[pl.BlockSpec((tm, d), lambda i: (i, 0)),
                  pl.BlockSpec((1, d), lambda i: (0, 0)),
                  pl.BlockSpec((d, cols), lambda i: (0, 0))],
        out_specs=[pl.BlockSpec((SSM_WIDTH // LANES, tm, LANES), lambda i: (0, i, 0)),
                   pl.BlockSpec((tm, cols - SSM_WIDTH), lambda i: (i, 0))],
        out_shape=[jax.ShapeDtypeStruct((SSM_WIDTH // LANES, n, LANES), F32),
                   jax.ShapeDtypeStruct((n, cols - SSM_WIDTH), BF16)],
        compiler_params=_params(("parallel",)),
        name="even_in_proj",
    )(h, g, w)


def _s5_matrices(lam_re, lam_im, log_dt, b_re, b_im, c_re, c_im):
    t = SSM_CHUNK
    lr = jnp.minimum(lam_re.astype(F32), -1e-4)
    li = lam_im.astype(F32)
    dt = jnp.exp(log_dt.astype(F32))[:, None]
    mag = jnp.exp(lr * dt)
    a_re = mag * jnp.cos(li * dt)
    a_im = mag * jnp.sin(li * dt)
    den = lr * lr + li * li
    nr = a_re - 1.0
    coef_re = (nr * lr + a_im * li) / den
    coef_im = (a_im * lr - nr * li) / den
    br = b_re.astype(F32)
    bi = b_im.astype(F32)
    bb_re = coef_re[..., None] * br - coef_im[..., None] * bi
    bb_im = coef_re[..., None] * bi + coef_im[..., None] * br
    cr = c_re.astype(F32)
    ci = c_im.astype(F32)
    pw_re = [jnp.ones_like(a_re)]
    pw_im = [jnp.zeros_like(a_im)]
    for _ in range(t):
        pr, pi = pw_re[-1], pw_im[-1]
        pw_re.append(pr * a_re - pi * a_im)
        pw_im.append(pr * a_im + pi * a_re)
    pw_re = jnp.stack(pw_re)
    pw_im = jnp.stack(pw_im)
    ab_re = pw_re[:t, :, :, None] * bb_re[None] - pw_im[:t, :, :, None] * bb_im[None]
    ab_im = pw_re[:t, :, :, None] * bb_im[None] + pw_im[:t, :, :, None] * bb_re[None]
    hi = lax.Precision.HIGHEST
    k_lag = (jnp.einsum('gnp,tgpm->tgnm', cr, ab_re, precision=hi)
             - jnp.einsum('gnp,tgpm->tgnm', ci, ab_im, precision=hi))
    n_pairs = SSM_GROUPS // 2
    st = 2 * SSM_STATE

    def pair_diag(w):
        w = w.reshape((n_pairs, 2) + w.shape[1:])
        z = jnp.zeros_like(w[:, 0])
        top = jnp.concatenate([w[:, 0], z], axis=-1)
        bot = jnp.concatenate([z, w[:, 1]], axis=-1)
        return jnp.concatenate([top, bot], axis=-2)

    k_blk = pair_diag(k_lag.transpose(1, 0, 3, 2))
    rev_re = pw_re[:t][::-1]
    rev_im = pw_im[:t][::-1]
    ws_re = rev_re[..., None] * bb_re[None] - rev_im[..., None] * bb_im[None]
    ws_im = rev_re[..., None] * bb_im[None] + rev_im[..., None] * bb_re[None]
    ws_re = pair_diag(ws_re.transpose(1, 0, 3, 2)).reshape(n_pairs, SSM_PAIR, st).astype(BF16)
    ws_im = pair_diag(ws_im.transpose(1, 0, 3, 2)).reshape(n_pairs, SSM_PAIR, st).astype(BF16)
    ca_re = cr[None] * pw_re[1:, :, None, :] - ci[None] * pw_im[1:, :, None, :]
    ca_im = cr[None] * pw_im[1:, :, None, :] + ci[None] * pw_re[1:, :, None, :]
    co_re = pair_diag(ca_re.transpose(1, 0, 3, 2))
    co_im = pair_diag((-ca_im).transpose(1, 0, 3, 2))
    w_intra, wo_re, wo_im = _s5_expand(k_blk, co_re, co_im)
    ch_re, ch_im = pw_re[t].reshape(1, -1), pw_im[t].reshape(1, -1)
    tab_re, tab_im = [jnp.ones_like(ch_re)], [jnp.zeros_like(ch_im)]
    for _ in range(SUBLANES):
        pr, pi = tab_re[-1], tab_im[-1]
        tab_re.append(pr * ch_re - pi * ch_im)
        tab_im.append(pr * ch_im + pi * ch_re)
    pad = [jnp.zeros_like(ch_re)] * (2 * SUBLANES - len(tab_re))
    return dict(
        w_intra=w_intra, ws_re=ws_re, ws_im=ws_im, wo_re=wo_re, wo_im=wo_im,
        at_re=jnp.concatenate(tab_re + pad, axis=0), at_im=jnp.concatenate(tab_im + pad, axis=0))


def _s5_expand_kernel(k_ref, cre_ref, cim_ref, wi_ref, wore_ref, woim_ref, kcat_ref):
    pw = 2 * SSM_GROUP
    for tau in range(SSM_CHUNK):
        kcat_ref[:, tau * pw:(tau + 1) * pw] = k_ref[0, tau]
        wore_ref[0, :, tau * pw:(tau + 1) * pw] = cre_ref[0, tau].astype(wore_ref.dtype)
        woim_ref[0, :, tau * pw:(tau + 1) * pw] = cim_ref[0, tau].astype(woim_ref.dtype)
    kcat = kcat_ref[...]
    col = lax.broadcasted_iota(jnp.int32, kcat.shape, 1)
    for s in range(SSM_CHUNK):
        blk = kcat if s == 0 else jnp.where(col >= s * pw, pltpu.roll(kcat, s * pw, 1), 0.0)
        wi_ref[0, s * pw:(s + 1) * pw, :] = blk.astype(wi_ref.dtype)


def _s5_expand(k_blk, co_re, co_im):
    n_pairs = k_blk.shape[0]
    pw = 2 * SSM_GROUP
    st = 2 * SSM_STATE
    return pl.pallas_call(
        _s5_expand_kernel,
        grid=(n_pairs,),
        in_specs=[pl.BlockSpec((1, SSM_CHUNK, pw, pw), lambda q: (q, 0, 0, 0)),
                  pl.BlockSpec((1, SSM_CHUNK, st, pw), lambda q: (q, 0, 0, 0)),
                  pl.BlockSpec((1, SSM_CHUNK, st, pw), lambda q: (q, 0, 0, 0))],
        out_specs=[pl.BlockSpec((1, SSM_PAIR, SSM_PAIR), lambda q: (q, 0, 0)),
                   pl.BlockSpec((1, st, SSM_PAIR), lambda q: (q, 0, 0)),
                   pl.BlockSpec((1, st, SSM_PAIR), lambda q: (q, 0, 0))],
        out_shape=[jax.ShapeDtypeStruct((n_pairs, SSM_PAIR, SSM_PAIR), BF16),
                   jax.ShapeDtypeStruct((n_pairs, st, SSM_PAIR), BF16),
                   jax.ShapeDtypeStruct((n_pairs, st, SSM_PAIR), BF16)],
        scratch_shapes=[pltpu.VMEM((pw, SSM_PAIR), F32)],
        compiler_params=_params(("parallel",)),
        name="s5_expand_weights",
    )(k_blk, co_re, co_im)


S5_LANE_PAIRS = LANES // (2 * SSM_GROUP)
S5_SCAN_LANES = 512


def _s5_state_kernel(u0_ref, u1_ref, u2_ref, u3_ref, wre_ref, wim_ref, are_ref, aim_ref,
                     x_ref, hre_ref, him_ref, sre_ref, sim_ref):
    n_chunks = x_ref.shape[0]
    pw = 2 * SSM_GROUP
    u_refs = (u0_ref, u1_ref, u2_ref, u3_ref)
    for t in range(SSM_CHUNK):
        for j, u_ref in enumerate(u_refs):
            ut = u_ref[pl.ds(t, n_chunks, stride=SSM_CHUNK), :]
            for qq in range(S5_LANE_PAIRS):
                q = j * S5_LANE_PAIRS + qq
                x_ref[:, q * SSM_PAIR + t * pw: q * SSM_PAIR + (t + 1) * pw] = (
                    ut[:, qq * pw:(qq + 1) * pw].astype(x_ref.dtype))
    st = 2 * SSM_STATE
    for q in range(SSM_GROUPS // 2):
        xq = x_ref[:, q * SSM_PAIR:(q + 1) * SSM_PAIR]
        sre_ref[:, q * st:(q + 1) * st] = _dot(xq, wre_ref[q])
        sim_ref[:, q * st:(q + 1) * st] = _dot(xq, wim_ref[q])

    row = lax.broadcasted_iota(jnp.int32, (SUBLANES, S5_SCAN_LANES), 0)
    zero = jnp.zeros((SUBLANES, S5_SCAN_LANES), F32)

    def cmul(ar, ai, xr, xi):
        return ar * xr - ai * xi, ar * xi + ai * xr

    def shift(x, k):
        return jnp.where(row >= k, pltpu.roll(x, k, 0), 0.0)

    for c0 in range(0, sre_ref.shape[1], S5_SCAN_LANES):
        cols = pl.ds(c0, S5_SCAN_LANES)

        def body(k, carry, cols=cols):
            r0 = pl.multiple_of(k * SUBLANES, SUBLANES)
            ir = sre_ref[pl.ds(r0, SUBLANES), cols]
            ii = sim_ref[pl.ds(r0, SUBLANES), cols]
            for step in (1, 2, 4):
                tr, ti = cmul(are_ref[pl.ds(step, 1), cols], aim_ref[pl.ds(step, 1), cols],
                              shift(ir, step), shift(ii, step))
                ir, ii = ir + tr, ii + ti
            cr, ci = carry
            pr, pi = cmul(are_ref[pl.ds(0, SUBLANES), cols], aim_ref[pl.ds(0, SUBLANES), cols], cr, ci)
            hre_ref[pl.ds(r0, SUBLANES), cols] = pr + shift(ir, 1)
            him_ref[pl.ds(r0, SUBLANES), cols] = pi + shift(ii, 1)
            nr, ni = cmul(are_ref[pl.ds(SUBLANES, 1), cols], aim_ref[pl.ds(SUBLANES, 1), cols], cr, ci)
            last_r = jnp.broadcast_to(ir[SUBLANES - 1:SUBLANES], ir.shape)
            last_i = jnp.broadcast_to(ii[SUBLANES - 1:SUBLANES], ii.shape)
            return nr + last_r, ni + last_i

        lax.fori_loop(0, n_chunks // SUBLANES, body, (zero, zero))


def _s5_out_kernel(x_ref, wi_ref, hre_ref, him_ref, wore_ref, woim_ref, d_ref, y_ref, yt_ref):
    n_chunks = x_ref.shape[0]
    pw = 2 * SSM_GROUP
    st = 2 * SSM_STATE
    for qq in range(S5_LANE_PAIRS):
        x = x_ref[:, qq * SSM_PAIR:(qq + 1) * SSM_PAIR]
        y = _dot(x, wi_ref[qq])
        y += _dot(hre_ref[:, qq * st:(qq + 1) * st].astype(BF16), wore_ref[qq])
        y += _dot(him_ref[:, qq * st:(qq + 1) * st].astype(BF16), woim_ref[qq])
        y += d_ref[:, qq * SSM_PAIR:(qq + 1) * SSM_PAIR] * x.astype(F32)
        y = jax.nn.gelu(y)
        for t in range(SSM_CHUNK):
            yt_ref[t, :, qq * pw:(qq + 1) * pw] = y[:, t * pw:(t + 1) * pw]
    for t in range(SSM_CHUNK):
        y_ref[pl.ds(t, n_chunks, stride=SSM_CHUNK), :] = yt_ref[t]


def _s5_mixer(u, mats, d, batch, seq):
    t = SSM_CHUNK
    n_chunks = seq // t
    n_pairs = SSM_GROUPS // 2
    cols = n_pairs * SSM_PAIR
    st = 2 * SSM_STATE
    n_state = n_pairs * st
    n_blk = SSM_WIDTH // LANES
    assert n_blk == 4 and n_chunks % SUBLANES == 0
    once = pl.Buffered(1)
    x, h_re, h_im = pl.pallas_call(
        _s5_state_kernel,
        grid=(batch,),
        in_specs=[pl.BlockSpec((None, seq, LANES), lambda b, j=j: (j, b, 0)) for j in range(n_blk)] + [
            pl.BlockSpec((n_pairs, SSM_PAIR, st), lambda b: (0, 0, 0), pipeline_mode=once),
            pl.BlockSpec((n_pairs, SSM_PAIR, st), lambda b: (0, 0, 0), pipeline_mode=once),
            pl.BlockSpec((2 * SUBLANES, n_state), lambda b: (0, 0)),
            pl.BlockSpec((2 * SUBLANES, n_state), lambda b: (0, 0))],
        out_specs=[pl.BlockSpec((n_chunks, cols), lambda b: (b, 0)),
                   pl.BlockSpec((n_chunks, n_state), lambda b: (b, 0)),
                   pl.BlockSpec((n_chunks, n_state), lambda b: (b, 0))],
        out_shape=[jax.ShapeDtypeStruct((batch * n_chunks, cols), BF16),
                   jax.ShapeDtypeStruct((batch * n_chunks, n_state), F32),
                   jax.ShapeDtypeStruct((batch * n_chunks, n_state), F32)],
        scratch_shapes=[pltpu.VMEM((n_chunks, n_state), F32), pltpu.VMEM((n_chunks, n_state), F32)],
        compiler_params=_params(("parallel",)),
        name="s5_state_scan",
    )(u, u, u, u, mats['ws_re'], mats['ws_im'], mats['at_re'], mats['at_im'])
    lp = S5_LANE_PAIRS
    d_cols = jnp.broadcast_to(d.astype(F32).reshape(n_pairs, 1, 2 * SSM_GROUP),
                              (n_pairs, t, 2 * SSM_GROUP)).reshape(1, cols)
    return pl.pallas_call(
        _s5_out_kernel,
        grid=(batch, n_blk),
        in_specs=[pl.BlockSpec((n_chunks, lp * SSM_PAIR), lambda b, j: (b, j)),
                  pl.BlockSpec((lp, SSM_PAIR, SSM_PAIR), lambda b, j: (j, 0, 0)),
                  pl.BlockSpec((n_chunks, lp * st), lambda b, j: (b, j)),
                  pl.BlockSpec((n_chunks, lp * st), lambda b, j: (b, j)),
                  pl.BlockSpec((lp, st, SSM_PAIR), lambda b, j: (j, 0, 0)),
                  pl.BlockSpec((lp, st, SSM_PAIR), lambda b, j: (j, 0, 0)),
                  pl.BlockSpec((1, lp * SSM_PAIR), lambda b, j: (0, j))],
        out_specs=pl.BlockSpec((None, seq, LANES), lambda b, j: (j, b, 0)),
        out_shape=jax.ShapeDtypeStruct((n_blk, batch * seq, LANES), F32),
        scratch_shapes=[pltpu.VMEM((t, n_chunks, LANES), F32)],
        compiler_params=_params(("parallel", "parallel")),
        name="s5_out",
    )(x, mats['w_intra'], h_re, h_im, mats['wo_re'], mats['wo_im'], d_cols)


def _even_mix_kernel(ys_ref, bu_ref, bv_ref, h_ref, wglu_ref, lng_ref, lnb_ref, ws_ref, bias_ref,
                     wo_a_ref, wo_b_ref, g1_ref, g2_ref, hout_ref, z_ref, s_scr):
    tm = h_ref.shape[0]
    ys = jnp.concatenate([ys_ref[jb] for jb in range(ys_ref.shape[0])], axis=1)
    ya = ys * jax.nn.sigmoid(_dot(ys.astype(BF16), wglu_ref[...]))
    u = jax.nn.gelu(bu_ref[...].astype(F32))
    v = _layer_norm(jax.nn.gelu(bv_ref[...].astype(F32)), lng_ref[...], lnb_ref[...])
    lane = lax.broadcasted_iota(jnp.int32, v.shape, 1)
    left = (lane % LANES) < SGU_HEAD_DIM
    v_l = jnp.where(left, v, 0.0).astype(BF16)
    v_r = jnp.where(left, 0.0, v).astype(BF16)
    for c in range(tm // SGU_CHUNK):
        rows = slice(c * SGU_CHUNK, (c + 1) * SGU_CHUNK)
        for p in range(SGU_HEADS // 2):
            cols = slice(p * LANES, (p + 1) * LANES)
            s_scr[rows, cols] = (_dot(ws_ref[2 * p], v_l[rows, cols]) + _dot(ws_ref[2 * p + 1], v_r[rows, cols]))
    bias = jnp.concatenate([bias_ref[...]] * (tm // SGU_CHUNK), axis=0)
    yb = u * (s_scr[...] + bias)
    mix = _dot(ya.astype(BF16), wo_a_ref[...]) + _dot(yb.astype(BF16), wo_b_ref[...])
    h_new = h_ref[...] + _rms(mix, g1_ref[...])
    hout_ref[...] = h_new
    z_ref[...] = _rms(h_new, g2_ref[...]).astype(z_ref.dtype)


def _even_mix(ys, proj, h, wglu, lng, lnb, ws, bias, wo_a, wo_b, g1, g2, tm):
    n, d = h.shape
    w = SGU_WIDTH
    const = lambda *shape: pl.BlockSpec(shape, lambda i: (0,) * len(shape))
    return pl.pallas_call(
        _even_mix_kernel,
        grid=(n // tm,),
        in_specs=[pl.BlockSpec((w // LANES, tm, LANES), lambda i: (0, i, 0)),
                  pl.BlockSpec((tm, w), lambda i: (i, 0)),
                  pl.BlockSpec((tm, w), lambda i: (i, 1)),
                  pl.BlockSpec((tm, d), lambda i: (i, 0)),
                  const(w, w), const(1, w), const(1, w),
                  const(SGU_HEADS, SGU_CHUNK, SGU_CHUNK), const(SGU_CHUNK, w),
                  const(w, d), const(w, d), const(1, d), const(1, d)],
        out_specs=[pl.BlockSpec((tm, d), lambda i: (i, 0)),
                   pl.BlockSpec((tm, d), lambda i: (i, 0))],
        out_shape=[jax.ShapeDtypeStruct((n, d), F32), jax.ShapeDtypeStruct((n, d), BF16)],
        scratch_shapes=[pltpu.VMEM((tm, w), F32)],
        compiler_params=_params(("parallel",)),
        name="even_mix",
    )(ys, proj, proj, h, wglu, lng, lnb, ws, bias, wo_a, wo_b, g1, g2)


def _ffn_kernel(z_ref, wg_ref, wu_ref, wd_ref, h_ref, g3_ref, gn_ref, hout_ref, zout_ref, acc_ref):
    j = pl.program_id(1)

    @pl.when(j == 0)
    def _():
        acc_ref[...] = jnp.zeros_like(acc_ref)

    z = z_ref[...]
    a = jax.nn.silu(_dot(z, wg_ref[...])) * _dot(z, wu_ref[...])
    acc_ref[...] += _dot(a.astype(BF16), wd_ref[...])

    @pl.when(j == pl.num_programs(1) - 1)
    def _():
        h_new = h_ref[...] + _rms(acc_ref[...], g3_ref[...])
        hout_ref[...] = h_new
        zout_ref[...] = _rms(h_new, gn_ref[...]).astype(zout_ref.dtype)


def _dense_ffn(z, wg, wu, wd, h, g3, g_next, tm, tf):
    n, d = h.shape
    ff = wg.shape[1]
    return pl.pallas_call(
        _ffn_kernel,
        grid=(n // tm, ff // tf),
        in_specs=[pl.BlockSpec((tm, d), lambda i, j: (i, 0)),
                  pl.BlockSpec((d, tf), lambda i, j: (0, j)),
                  pl.BlockSpec((d, tf), lambda i, j: (0, j)),
                  pl.BlockSpec((tf, d), lambda i, j: (j, 0)),
                  pl.BlockSpec((tm, d), lambda i, j: (i, 0)),
                  pl.BlockSpec((1, d), lambda i, j: (0, 0)),
                  pl.BlockSpec((1, d), lambda i, j: (0, 0))],
        out_specs=[pl.BlockSpec((tm, d), lambda i, j: (i, 0)),
                   pl.BlockSpec((tm, d), lambda i, j: (i, 0))],
        out_shape=[jax.ShapeDtypeStruct((n, d), F32), jax.ShapeDtypeStruct((n, d), BF16)],
        scratch_shapes=[pltpu.VMEM((tm, d), F32)],
        compiler_params=_params(("parallel", "arbitrary")),
        name="dense_ffn",
    )(z, wg, wu, wd, h, g3, g_next)


def _odd_proj_kernel(z_ref, win_ref, gq_ref, gkv_ref, wuq_ref, wuqs_ref, wuk_ref, wuv_ref, vone_ref, cos_ref, sin_ref,
                     zc_ref, q_ref, k_ref, v_ref, *, scale):
    z = z_ref[...]
    proj = _dot(z, win_ref[...])
    c0 = 2 * CONV_CH
    c1 = c0 + MLA_Q_RANK
    c2 = c1 + MLA_KV_RANK
    c3 = c2 + MLA_PAD
    zc_ref[...] = proj[:, :c0].astype(zc_ref.dtype)
    cq = _rms(proj[:, c0:c1], gq_ref[...]).astype(BF16)
    ckv = _rms(proj[:, c1:c2], gkv_ref[...]).astype(BF16)
    cos = cos_ref[...]
    sin = sin_ref[...]
    cos_h = jnp.concatenate([cos] * MLA_HEADS, axis=1)
    sin_h = jnp.concatenate([sin] * MLA_HEADS, axis=1)
    q = _dot(cq, wuq_ref[...]) * cos_h + _dot(cq, wuqs_ref[...]) * sin_h
    q_ref[...] = (q * scale).astype(q_ref.dtype)
    kr = proj[:, c2:c3] * cos + proj[:, c3:] * sin
    k = _dot(ckv, wuk_ref[...]) + jnp.concatenate([kr] * MLA_HEADS, axis=1)
    k_ref[...] = k.astype(k_ref.dtype)
    vt = lax.dot_general(wuv_ref[...], ckv, (((1,), (1,)), ((), ())), preferred_element_type=F32)
    v_ref[0] = (vt + vone_ref[...]).astype(v_ref.dtype)


def _odd_proj(z, win, gq, gkv, wuq, wuqs, wuk, wuv_t, v_one, cos_t, sin_t, seq, tm):
    n, d = z.shape
    hp = MLA_HEADS * MLA_PAD
    vr = MLA_HEADS * MLA_VROWS
    n_l = seq // tm
    const = lambda *shape: pl.BlockSpec(shape, lambda i: (0,) * len(shape))
    out = jax.ShapeDtypeStruct((n, hp), BF16)
    scale = float((MLA_NOPE + MLA_ROPE) ** -0.5 * math.log2(math.e))
    return pl.pallas_call(
        functools.partial(_odd_proj_kernel, scale=scale),
        grid=(n // tm,),
        in_specs=[pl.BlockSpec((tm, d), lambda i: (i, 0)),
                  const(d, win.shape[1]), const(1, MLA_Q_RANK), const(1, MLA_KV_RANK),
                  const(MLA_Q_RANK, hp), const(MLA_Q_RANK, hp), const(MLA_KV_RANK, hp), const(vr, MLA_KV_RANK),
                  const(vr, 1),
                  pl.BlockSpec((tm, MLA_PAD), lambda i: (i % n_l, 0)),
                  pl.BlockSpec((tm, MLA_PAD), lambda i: (i % n_l, 0))],
        out_specs=[pl.BlockSpec((tm, 2 * CONV_CH), lambda i: (i, 0)),
                   pl.BlockSpec((tm, hp), lambda i: (i, 0)),
                   pl.BlockSpec((tm, hp), lambda i: (i, 0)),
                   pl.BlockSpec((1, vr, tm), lambda i: (i, 0, 0))],
        out_shape=[jax.ShapeDtypeStruct((n, 2 * CONV_CH), BF16), out, out,
                   jax.ShapeDtypeStruct((n // tm, vr, tm), BF16)],
        compiler_params=_params(("parallel",)),
        name="odd_in_proj",
    )(z, win, gq, gkv, wuq, wuqs, wuk, wuv_t, v_one, cos_t, sin_t)


def _attn_kernel(q_ref, k_ref, vt_ref, wg_ref, wu_ref, wd_ref, o_ref, wgb_ref, wub_ref, wdb_ref, acc_ref, *, blk):
    i = pl.program_id(2)
    acc_ref[...] = jnp.zeros_like(acc_ref)

    def step(j, m, masked):
        r0 = pl.multiple_of(j * blk, blk)
        scores = []
        for hh in range(ATTN_HEADS):
            q = q_ref[0, :, hh * MLA_PAD:(hh + 1) * MLA_PAD]
            k = k_ref[0, pl.ds(r0, blk), hh * MLA_PAD:(hh + 1) * MLA_PAD]
            st = lax.dot_general(k, q, (((1,), (1,)), ((), ())), preferred_element_type=F32)
            if masked:
                key = lax.broadcasted_iota(jnp.int32, st.shape, 0)
                qry = lax.broadcasted_iota(jnp.int32, st.shape, 1)
                st = jnp.where(key <= qry, st, -1e30)
            scores.append(st)
        soft = []
        for hh in range(ATTN_HEADS):
            m_new = jnp.maximum(m[hh], jnp.max(scores[hh], axis=0, keepdims=True))
            soft.append((m_new, jnp.exp2(m[hh] - m_new), jnp.exp2(scores[hh] - m_new).astype(BF16)))
        for hh in range(ATTN_HEADS):
            vt = vt_ref[j, hh * MLA_VROWS:(hh + 1) * MLA_VROWS, :]
            acc_ref[hh] = soft[hh][1] * acc_ref[hh] + _dot(vt, soft[hh][2])
        return tuple(s[0] for s in soft)

    init = jnp.full((1, blk), -1e30, F32)
    m = lax.fori_loop(0, i, lambda j, m: step(j, m, False), (init,) * ATTN_HEADS)
    tf = wgb_ref.shape[3]
    for f in range(wgb_ref.shape[1]):
        wgb_ref[0, f] = wg_ref[0, :, f * tf:(f + 1) * tf].astype(BF16)
        wub_ref[0, f] = wu_ref[0, :, f * tf:(f + 1) * tf].astype(BF16)
    wdb_ref[0] = wd_ref[0].astype(BF16)
    step(i, m, True)
    ot = jnp.concatenate([acc_ref[hh][:MLA_V] / acc_ref[hh][MLA_V:MLA_V + 1] for hh in range(ATTN_HEADS)], axis=0)
    o_ref[0] = ot.T.astype(o_ref.dtype)


def _attention(q, k, vt, blk, wg, wu, wd, tf):
    b, seq, _ = q.shape
    n_blk = seq // blk
    n_pairs = MLA_HEADS // ATTN_HEADS
    n_e, d, ff = wg.shape
    steps = b * n_pairs * n_blk
    per_e = steps // n_e
    assert steps == per_e * n_e and d % per_e == 0 and ff % per_e == 0
    rows_in, rows_down = d // per_e, ff // per_e
    assert rows_in % 16 == 0 and rows_down % 16 == 0 and ff % tf == 0

    def lin(bi, p, i):
        return (bi * n_pairs + p) * n_blk + i

    w_in = pl.BlockSpec((1, rows_in, ff), lambda bi, p, i: (lin(bi, p, i) // per_e, lin(bi, p, i) % per_e, 0))
    w_out = pl.BlockSpec((1, ff // tf, rows_in, tf),
                         lambda bi, p, i: (lin(bi, p, i) // per_e, 0, lin(bi, p, i) % per_e, 0))
    w_down = pl.BlockSpec((1, rows_down, d), lambda bi, p, i: (lin(bi, p, i) // per_e, lin(bi, p, i) % per_e, 0))
    return pl.pallas_call(
        functools.partial(_attn_kernel, blk=blk),
        grid=(b, n_pairs, n_blk),
        in_specs=[pl.BlockSpec((1, blk, ATTN_HEADS * MLA_PAD), lambda bi, p, i: (bi, i, p)),
                  pl.BlockSpec((1, seq, ATTN_HEADS * MLA_PAD), lambda bi, p, i: (bi, 0, p)),
                  pl.BlockSpec((n_blk, ATTN_HEADS * MLA_VROWS, blk), lambda bi, p, i: (bi, p, 0)),
                  w_in, w_in, w_down],
        out_specs=[pl.BlockSpec((1, blk, ATTN_HEADS * MLA_V), lambda bi, p, i: (bi, i, p)), w_out, w_out, w_down],
        out_shape=[jax.ShapeDtypeStruct((b, seq, MLA_HEADS * MLA_V), BF16),
                   jax.ShapeDtypeStruct((n_e, ff // tf, d, tf), BF16),
                   jax.ShapeDtypeStruct((n_e, ff // tf, d, tf), BF16),
                   jax.ShapeDtypeStruct((n_e, ff, d), BF16)],
        scratch_shapes=[pltpu.VMEM((ATTN_HEADS, MLA_VROWS, blk), F32)],
        compiler_params=_params(("parallel", "parallel", "parallel")),
        name="mla_attention",
    )(q, k, vt, wg, wu, wd)


def _conv_kernel(zc_ref, w_ref, b_ref, lng_ref, lnb_ref, y_ref, buf_ref, part_ref):
    tm = zc_ref.shape[1]

    @pl.when(pl.program_id(1) == 0)
    def _():
        buf_ref[pl.ds(0, CONV_HALO), :] = jnp.zeros((CONV_HALO, CONV_CH), F32)
        buf_ref[pl.ds(CONV_HALO + tm, SUBLANES), :] = jnp.zeros((SUBLANES, CONV_CH), F32)

    zc = zc_ref[0].astype(F32)
    hh = zc[:, :CONV_CH] * jax.nn.sigmoid(zc[:, CONV_CH:])
    buf_ref[pl.ds(CONV_HALO, tm), :] = hh
    off = CONV_HALO - (CONV_TAPS - 1)
    acc = jnp.zeros((tm, CONV_CH), F32) + b_ref[...]
    for b in range(SUBLANES):
        taps = [k for k in range(CONV_TAPS) if (off + k) % SUBLANES == b]
        part = None
        for k in taps:
            term = w_ref[pl.ds(k, 1), :] * buf_ref[pl.ds(off + k - b, tm + SUBLANES), :]
            part = term if part is None else part + term
        if b == 0:
            acc = acc + part[:tm]
        else:
            part_ref[...] = part
            acc = acc + part_ref[pl.ds(b, tm), :]
    buf_ref[pl.ds(0, CONV_HALO), :] = buf_ref[pl.ds(tm, CONV_HALO), :]
    y_ref[0] = jax.nn.silu(_layer_norm(acc, lng_ref[...], lnb_ref[...])).astype(y_ref.dtype)


def _conv_mixer(zc, w, b, lng, lnb, tm):
    bsz, seq, _ = zc.shape
    const = lambda *shape: pl.BlockSpec(shape, lambda bi, i: (0,) * len(shape))
    return pl.pallas_call(
        _conv_kernel,
        grid=(bsz, seq // tm),
        in_specs=[pl.BlockSpec((1, tm, 2 * CONV_CH), lambda bi, i: (bi, i, 0)),
                  const(CONV_HALO, CONV_CH), const(1, CONV_CH), const(1, CONV_CH), const(1, CONV_CH)],
        out_specs=pl.BlockSpec((1, tm, CONV_CH), lambda bi, i: (bi, i, 0)),
        out_shape=jax.ShapeDtypeStruct((bsz, seq, CONV_CH), BF16),
        scratch_shapes=[pltpu.VMEM((CONV_HALO + tm + SUBLANES, CONV_CH), F32),
                        pltpu.VMEM((tm + SUBLANES, CONV_CH), F32)],
        compiler_params=_params(("arbitrary", "arbitrary")),
        name="conv_module",
    )(zc, w, b, lng, lnb)


def _odd_mix_kernel(yc_ref, yd_ref, h_ref, wo_a_ref, wo_b_ref, g1_ref, g2_ref, wr_ref, hout_ref, z_ref, route_ref):
    tm = h_ref.shape[0]
    halves = [slice(0, tm // 2), slice(tm // 2, tm)]
    mixes = [_dot(yc_ref[r, :], wo_a_ref[...]) + _dot(yd_ref[r, :], wo_b_ref[...]) for r in halves]
    zs = []
    for r, mix in zip(halves, mixes):
        h_new = h_ref[r, :] + _rms(mix, g1_ref[...])
        hout_ref[r, :] = h_new
        zs.append(_rms(h_new, g2_ref[...]))
    all_logits = [_dot(z.astype(BF16), wr_ref[...]) for z in zs]
    _store_row_tiles(z_ref, jnp.concatenate(zs, axis=0))
    neg = -jnp.inf
    for r, logits in zip(halves, all_logits):
        lane = lax.broadcasted_iota(jnp.int32, logits.shape, 1)
        logits = jnp.where(lane < N_EXPERTS, logits, neg)
        m1 = jnp.max(logits, axis=-1, keepdims=True)
        i1 = jnp.min(jnp.where(logits == m1, lane, LANES), axis=-1, keepdims=True)
        rest = jnp.where(lane == i1, neg, logits)
        m2 = jnp.max(rest, axis=-1, keepdims=True)
        i2 = jnp.min(jnp.where(rest == m2, lane, LANES), axis=-1, keepdims=True)
        e = jnp.exp(m2 - m1)
        w1 = 1.0 / (1.0 + e)
        w2 = e / (1.0 + e)
        route_ref[r, :] = jnp.where(lane == 0, i1.astype(F32),
                                    jnp.where(lane == 1, i2.astype(F32),
                                              jnp.where(lane == 2, w1, jnp.where(lane == 3, w2, 0.0))))


def _odd_mix(yc, yd, h, wo_a, wo_b, g1, g2, wr, tm):
    n, d = h.shape
    const = lambda *shape: pl.BlockSpec(shape, lambda i: (0,) * len(shape))
    return pl.pallas_call(
        _odd_mix_kernel,
        grid=(n // tm,),
        in_specs=[pl.BlockSpec((tm, yc.shape[1]), lambda i: (i, 0)),
                  pl.BlockSpec((tm, yd.shape[1]), lambda i: (i, 0)),
                  pl.BlockSpec((tm, d), lambda i: (i, 0)),
                  const(*wo_a.shape), const(*wo_b.shape), const(1, d), const(1, d), const(d, LANES)],
        out_specs=[pl.BlockSpec((tm, d), lambda i: (i, 0)),
                   pl.BlockSpec((tm * ROW_TILE, LANES), lambda i: (i, 0)),
                   pl.BlockSpec((tm, LANES), lambda i: (i, 0))],
        out_shape=[jax.ShapeDtypeStruct((n, d), F32), jax.ShapeDtypeStruct((n * ROW_TILE, LANES), F32),
                   jax.ShapeDtypeStruct((n, LANES), F32)],
        compiler_params=_params(("parallel",)),
        name="odd_mix_router",
    )(yc, yd, h, wo_a, wo_b, g1, g2, wr)


def _store_row_tiles(ref, x):
    rows = x.shape[0]
    for s in range(ROW_TILE):
        ref[pl.ds(s, rows, stride=ROW_TILE), :] = x[:, s * LANES:(s + 1) * LANES]


def _load_row_tiles(ref, rows):
    return [ref[pl.ds(s, rows, stride=ROW_TILE), :] for s in range(ROW_TILE)]


def _gather_rows(idx_ref, base, n_rows, src_hbm, dst_ref, sem):
    def body(r, c):
        src = pl.multiple_of(idx_ref[base + r] * ROW_TILE, ROW_TILE)
        dst = pl.multiple_of(r * ROW_TILE, ROW_TILE)
        pltpu.make_async_copy(src_hbm.at[pl.ds(src, ROW_TILE), :], dst_ref.at[pl.ds(dst, ROW_TILE), :], sem).start()
        return c

    lax.fori_loop(0, n_rows, body, 0, unroll=8)


def _wait_rows(src_hbm, dst_ref, sem):
    pltpu.make_async_copy(src_hbm.at[pl.ds(0, dst_ref.shape[0]), :], dst_ref, sem).wait()


def _row_copy(src_ref, src_row, dst_ref, dst_row, sem):
    src = pl.multiple_of(src_row * ROW_TILE, ROW_TILE)
    dst = pl.multiple_of(dst_row * ROW_TILE, ROW_TILE)
    return pltpu.make_async_copy(src_ref.at[pl.ds(src, ROW_TILE), :], dst_ref.at[pl.ds(dst, ROW_TILE), :], sem)


def _moe_ffn_kernel(te_ref, nu_ref, tok_ref, dst_ref, z_hbm, wg_ref, wu_ref, wd_ref, y_hbm,
                    xraw_ref, xb_ref, acc_ref, yst_ref, gsem, ssem, *, rows_per_step):
    i = pl.program_id(0)
    j = pl.program_id(1)
    tm = xb_ref.shape[1]
    stride = yst_ref.shape[0] // ROW_TILE
    n_used = nu_ref[0]
    n_f = pl.num_programs(1)
    slot = i % 2
    first = j == 0
    last = j == n_f - 1

    def convert(dst_slot):
        for s, blk in enumerate(_load_row_tiles(xraw_ref, tm)):
            xb_ref[dst_slot, :, s * LANES:(s + 1) * LANES] = blk.astype(BF16)

    @pl.when(first & (i == 0))
    def _():
        yst_ref[...] = jnp.zeros_like(yst_ref)
        _gather_rows(tok_ref, 0, stride, z_hbm, xraw_ref, gsem)
        _wait_rows(z_hbm, xraw_ref, gsem)
        convert(0)

    @pl.when(last & (i < n_used))
    def _():
        _wait_rows(z_hbm, xraw_ref, gsem)

    @pl.when(first & (i < n_used))
    def _():
        acc_ref[...] = jnp.zeros_like(acc_ref)

    @pl.when(first & (i == n_used))
    def _():
        def body(r, c):
            _row_copy(yst_ref, r, y_hbm, dst_ref[i * stride + r], ssem).start()
            return c
        lax.fori_loop(0, stride, body, 0, unroll=8)

    def multiply(rows, phase):
        x = xb_ref[slot, :rows, :]
        g = _dot(x, wg_ref[0, 0])
        u = _dot(x, wu_ref[0, 0])
        if phase == "copies":
            for rr in range(rows_per_step):
                r = j * rows_per_step + rr
                _row_copy(z_hbm, tok_ref[(i + 1) * stride + r], xraw_ref, r, gsem).start(priority=rr % 2)
                _row_copy(yst_ref, r, y_hbm, dst_ref[i * stride + r], ssem).start(priority=rr % 2)
        if phase == "convert":
            convert(1 - slot)
        a = jax.nn.silu(g) * u
        acc_ref[:rows, :] += _dot(a.astype(BF16), wd_ref[0])

    used = i < n_used
    half = nu_ref[1 + i] <= tm // 2
    phases = (("copies", j < n_f - 2), ("plain", j == n_f - 2), ("convert", last))
    for rows, fits in ((tm, jnp.logical_not(half)), (tm // 2, half)):
        for phase, now in phases:
            @pl.when(used & fits & now)
            def _(rows=rows, phase=phase):
                multiply(rows, phase)

    @pl.when(last & (i <= n_used))
    def _():
        _wait_rows(z_hbm, yst_ref, ssem)

    @pl.when(last & (i < n_used))
    def _():
        _store_row_tiles(yst_ref, acc_ref[...])


def _moe_ffn(tile_expert, n_used, tok_tab, dst_tab, z_tiles, wg, wu, wd, n_tok, tm):
    n_f, d, tf = wg.shape[1], wg.shape[2], wg.shape[3]
    copy_steps = n_f - 2
    rows_per_step = -(-tm // copy_steps)
    stride = copy_steps * rows_per_step
    n_tiles = tile_expert.shape[0]
    assert tok_tab.shape[0] == dst_tab.shape[0] == (n_tiles + 1) * stride

    def col(i, j, nu):
        return jnp.where(i < nu[0], j, n_f - 1)

    return pl.pallas_call(
        functools.partial(_moe_ffn_kernel, rows_per_step=rows_per_step),
        grid_spec=pltpu.PrefetchScalarGridSpec(
            num_scalar_prefetch=4,
            grid=(n_tiles, n_f),
            in_specs=[pl.BlockSpec(memory_space=pl.ANY),
                      pl.BlockSpec((1, 1, d, tf), lambda i, j, te, nu, tok, dst: (te[i], col(i, j, nu), 0, 0)),
                      pl.BlockSpec((1, 1, d, tf), lambda i, j, te, nu, tok, dst: (te[i], col(i, j, nu), 0, 0)),
                      pl.BlockSpec((1, tf, d), lambda i, j, te, nu, tok, dst: (te[i], col(i, j, nu), 0))],
            out_specs=pl.BlockSpec(memory_space=pl.ANY),
            scratch_shapes=[pltpu.VMEM((stride * ROW_TILE, LANES), F32), pltpu.VMEM((2, tm, d), BF16),
                            pltpu.VMEM((tm, d), F32), pltpu.VMEM((stride * ROW_TILE, LANES), F32),
                            pltpu.SemaphoreType.DMA(()), pltpu.SemaphoreType.DMA(())]),
        out_shape=jax.ShapeDtypeStruct(((2 * n_tok + stride) * ROW_TILE, LANES), F32),
        compiler_params=_params(("arbitrary", "arbitrary")),
        name="moe_grouped_ffn",
    )(tile_expert, n_used, tok_tab, dst_tab, z_tiles, wg, wu, wd)


def _combine_kernel(ya_ref, yb_ref, route_ref, h_ref, g_ref, o_ref):
    tm = h_ref.shape[0]
    route = route_ref[...]
    a = jnp.concatenate(_load_row_tiles(ya_ref, tm), axis=1)
    b = jnp.concatenate(_load_row_tiles(yb_ref, tm), axis=1)
    f = route[:, 2:3] * a + route[:, 3:4] * b
    o_ref[...] = h_ref[...] + _rms(f, g_ref[...])


def _combine(y, route, h, g, tm):
    n, d = h.shape
    n_blk = n // tm
    return pl.pallas_call(
        _combine_kernel,
        grid=(n_blk,),
        in_specs=[pl.BlockSpec((tm * ROW_TILE, LANES), lambda i: (i, 0)),
                  pl.BlockSpec((tm * ROW_TILE, LANES), lambda i: (n_blk + i, 0)),
                  pl.BlockSpec((tm, LANES), lambda i: (i, 0)),
                  pl.BlockSpec((tm, d), lambda i: (i, 0)),
                  pl.BlockSpec((1, d), lambda i: (0, 0))],
        out_specs=pl.BlockSpec((tm, d), lambda i: (i, 0)),
        out_shape=jax.ShapeDtypeStruct((n, d), F32),
        compiler_params=_params(("parallel",)),
        name="moe_combine",
    )(y, y, route, h, g)


def _moe_plan(route, tm, stride):
    n = route.shape[0]
    eids = jnp.concatenate([route[:, 0], route[:, 1]]).astype(jnp.int32)
    onehot = (eids[:, None] == jnp.arange(N_EXPERTS, dtype=jnp.int32)[None, :]).astype(jnp.int32)
    csum = jnp.cumsum(onehot, axis=0)
    rank = jnp.sum(csum * onehot, axis=1) - 1
    counts = csum[-1]
    padded = ((counts + tm - 1) // tm) * tm
    ends = jnp.cumsum(padded)
    starts = ends - padded
    slot = jnp.sum(onehot * starts[None, :], axis=1) + rank
    n_tiles = 2 * n // tm + N_EXPERTS + 1
    copy_of_slot = jnp.full((n_tiles * tm,), -1, jnp.int32).at[slot].set(jnp.arange(2 * n, dtype=jnp.int32))
    copy_tab = jnp.pad(copy_of_slot.reshape(n_tiles, tm), ((0, 1), (0, stride - tm)), constant_values=-1)
    tok_tab = jnp.where(copy_tab >= 0, copy_tab % n, 0)
    dump = 2 * n + jnp.arange(stride, dtype=jnp.int32)[None, :]
    dst_tab = jnp.where(copy_tab >= 0, copy_tab, dump)
    dst_tab = jnp.concatenate([jnp.broadcast_to(dump, (1, stride)), dst_tab[:-1]], axis=0)
    n_used = (ends[-1] // tm).astype(jnp.int32)
    tile_start = jnp.minimum(jnp.arange(n_tiles, dtype=jnp.int32), n_used - 1) * tm
    tile_expert = jnp.sum((tile_start[:, None] >= ends[None, :]).astype(jnp.int32), axis=1)
    onehot_e = (tile_expert[:, None] == jnp.arange(N_EXPERTS, dtype=jnp.int32)[None, :]).astype(jnp.int32)
    run_end = jnp.sum(onehot_e * (starts + counts)[None, :], axis=1)
    tile_rows = jnp.clip(run_end - jnp.arange(n_tiles, dtype=jnp.int32) * tm, 0, tm)
    return tok_tab.reshape(-1), dst_tab.reshape(-1), tile_expert, jnp.concatenate([n_used.reshape(1), tile_rows])


def _odd_weights(od_w_in, mla_w_uq, mla_w_ukv):
    c2 = 2 * CONV_CH + MLA_Q_RANK + MLA_KV_RANK
    half = MLA_ROPE // 2
    w_kr = od_w_in[:, c2:]
    w_kr_sw = jnp.concatenate([w_kr[:, half:], w_kr[:, :half]], axis=1)
    zl = jnp.zeros((D_MODEL, MLA_NOPE), F32)
    zr = jnp.zeros((D_MODEL, MLA_PAD - MLA_NOPE - MLA_ROPE), F32)
    win = jnp.concatenate([od_w_in[:, :c2], zl, w_kr, zr, zl, w_kr_sw, zr], axis=1)
    dk = MLA_NOPE + MLA_ROPE
    wq = mla_w_uq.reshape(MLA_Q_RANK, MLA_HEADS, dk)
    zq = jnp.zeros((MLA_Q_RANK, MLA_HEADS, MLA_PAD - dk), F32)
    wuq = jnp.concatenate([wq, zq], axis=2).reshape(MLA_Q_RANK, MLA_HEADS * MLA_PAD)
    wq_sw = jnp.concatenate([jnp.zeros_like(wq[:, :, :MLA_NOPE]), wq[:, :, MLA_NOPE + half:],
                             wq[:, :, MLA_NOPE:MLA_NOPE + half], zq], axis=2)
    wuqs = wq_sw.reshape(MLA_Q_RANK, MLA_HEADS * MLA_PAD)
    wkv = mla_w_ukv.reshape(MLA_KV_RANK, MLA_HEADS, MLA_NOPE + MLA_V)
    zk = jnp.zeros((MLA_KV_RANK, MLA_HEADS, MLA_PAD - MLA_NOPE), F32)
    wuk = jnp.concatenate([wkv[:, :, :MLA_NOPE], zk], axis=2).reshape(MLA_KV_RANK, MLA_HEADS * MLA_PAD)
    zv = jnp.zeros((MLA_KV_RANK, MLA_HEADS, MLA_VROWS - MLA_V), F32)
    wuv_t = jnp.concatenate([wkv[:, :, MLA_NOPE:], zv], axis=2).reshape(MLA_KV_RANK, MLA_HEADS * MLA_VROWS).T
    v_one = jnp.zeros((MLA_HEADS, MLA_VROWS), F32).at[:, MLA_V].set(1.0).reshape(MLA_HEADS * MLA_VROWS, 1)
    return win.astype(BF16), wuq.astype(BF16), wuqs.astype(BF16), wuk.astype(BF16), wuv_t.astype(BF16), v_one


def _rope_tables(seq):
    inv = 1.0 / (ROPE_THETA ** (jnp.arange(0, MLA_ROPE, 2, dtype=F32) / MLA_ROPE))
    ang = jnp.arange(seq, dtype=F32)[:, None] * inv[None, :]
    cos, sin = jnp.cos(ang), jnp.sin(ang)
    ones = jnp.ones((seq, MLA_NOPE), F32)
    zl = jnp.zeros((seq, MLA_NOPE), F32)
    zr = jnp.zeros((seq, MLA_PAD - MLA_NOPE - MLA_ROPE), F32)
    return (jnp.concatenate([ones, cos, cos, zr], axis=1), jnp.concatenate([zl, -sin, sin, zr], axis=1))


def kernel(x, norm_g, ev_w_in, ssm_lambda_re, ssm_lambda_im, ssm_log_dt, ssm_b_re, ssm_b_im, ssm_c_re, ssm_c_im, ssm_d, ssm_w_glu, sgu_ln_g, sgu_ln_b, sgu_w, sgu_b, ev_w_out, ffn_w_gate, ffn_w_up, ffn_w_down, od_w_in, conv_w, conv_b, conv_ln_g, conv_ln_b, mla_q_norm_g, mla_w_uq, mla_kv_norm_g, mla_w_ukv, od_w_out, moe_w_router, moe_w_gate, moe_w_up, moe_w_down):
    bsz, seq, d = x.shape
    n = bsz * seq
    assert d == D_MODEL and SUBLANES % bsz == 0 and seq % 512 == 0
    row = lambda v: v.astype(F32).reshape(1, -1)
    h = x.astype(F32).reshape(n, d)
    tm = 512

    g = norm_g[0]
    a_in, proj = _norm_proj(h, row(g[0]), ev_w_in[0].astype(BF16), tm)
    mats = _s5_matrices(ssm_lambda_re[0], ssm_lambda_im[0], ssm_log_dt[0], ssm_b_re[0], ssm_b_im[0],
                        ssm_c_re[0], ssm_c_im[0])
    ys = _s5_mixer(a_in, mats, ssm_d[0], bsz, seq)
    causal = jnp.tril(jnp.ones((SGU_CHUNK, SGU_CHUNK), dtype=bool))
    ws = jnp.where(causal[None], sgu_w[0], 0.0).astype(BF16)
    bias = jnp.repeat(sgu_b[0].astype(F32).T, SGU_HEAD_DIM, axis=1)
    wo = ev_w_out[0].astype(BF16)
    h, z = _even_mix(ys, proj, h, ssm_w_glu[0].astype(BF16), row(sgu_ln_g[0]), row(sgu_ln_b[0]), ws, bias,
                     wo[:SSM_WIDTH], wo[SSM_WIDTH:], row(g[1]), row(g[2]), tm)
    h, z = _dense_ffn(z, ffn_w_gate[0].astype(BF16), ffn_w_up[0].astype(BF16), ffn_w_down[0].astype(BF16),
                      h, row(g[3]), row(norm_g[1][0]), 1024, 1024)

    g = norm_g[1]
    win, wuq, wuqs, wuk, wuv_t, v_one = _odd_weights(od_w_in[0], mla_w_uq[0], mla_w_ukv[0])
    cos_t, sin_t = _rope_tables(seq)
    zc, q, k, vt = _odd_proj(z, win, row(mla_q_norm_g[0]), row(mla_kv_norm_g[0]), wuq, wuqs, wuk, wuv_t, v_one,
                             cos_t, sin_t, seq, tm)
    hp = MLA_HEADS * MLA_PAD
    tm_moe, tf_moe = 1024, 512
    yd, wg_b, wu_b, wd_b = _attention(q.reshape(bsz, seq, hp), k.reshape(bsz, seq, hp), vt, tm,
                                      moe_w_gate[0], moe_w_up[0], moe_w_down[0], tf_moe)
    conv_w_pad = jnp.concatenate([conv_w[0].astype(F32), jnp.zeros((CONV_HALO - CONV_TAPS, CONV_CH), F32)], axis=0)
    yc = _conv_mixer(zc.reshape(bsz, seq, 2 * CONV_CH), conv_w_pad, row(conv_b[0]), row(conv_ln_g[0]),
                     row(conv_ln_b[0]), tm)
    wo = od_w_out[0].astype(BF16)
    wr = jnp.concatenate([moe_w_router[0].astype(F32), jnp.zeros((d, LANES - N_EXPERTS), F32)], axis=1)
    h, z, route = _odd_mix(yc.reshape(n, CONV_CH), yd.reshape(n, MLA_HEADS * MLA_V), h, wo[:CONV_CH], wo[CONV_CH:],
                           row(g[1]), row(g[2]), wr.astype(BF16), tm)
    copy_steps = wg_b.shape[1] - 2
    tok_tab, dst_tab, tile_expert, n_used = _moe_plan(route, tm_moe, copy_steps * -(-tm_moe // copy_steps))
    y = _moe_ffn(tile_expert, n_used, tok_tab, dst_tab, z, wg_b, wu_b, wd_b, n, tm_moe)
    h = _combine(y, route, h, row(g[3]), 256)
    return h.reshape(bsz, seq, d).astype(x.dtype)
```

```python
import functools
import math

import jax
import jax.numpy as jnp
from jax import lax
from jax.experimental import pallas as pl
from jax.experimental.pallas import tpu as pltpu

F32 = jnp.float32
BF16 = jnp.bfloat16

D_MODEL = 1024
NORM_EPS = 1e-6
SSM_WIDTH = 512
SSM_GROUP = 16
SSM_GROUPS = 32
SSM_STATE = 64
SSM_CHUNK = 16
SSM_PAIR = 2 * SSM_GROUP * SSM_CHUNK
SGU_WIDTH = 512
SGU_HEADS = 8
SGU_HEAD_DIM = 64
SGU_CHUNK = 128
CONV_CH = 512
CONV_TAPS = 31
CONV_HALO = 32
MLA_HEADS = 8
MLA_Q_RANK = 256
MLA_KV_RANK = 128
MLA_NOPE = 64
MLA_ROPE = 32
MLA_V = 64
MLA_PAD = 128
MLA_VROWS = 80
ATTN_HEADS = 4
ROPE_THETA = 10000.0
FF_DENSE = 4096
N_EXPERTS = 8
FF_EXPERT = 3584
LANES = 128
SUBLANES = 8
ROW_TILE = D_MODEL // LANES
VMEM_LIMIT = 56 * 1024 * 1024


def _params(sem, vmem=VMEM_LIMIT):
    return pltpu.CompilerParams(dimension_semantics=sem, vmem_limit_bytes=vmem)


def _rms(x, g):
    return x * lax.rsqrt(jnp.mean(x * x, axis=-1, keepdims=True) + NORM_EPS) * g


def _layer_norm(x, g, b):
    mu = jnp.mean(x, axis=-1, keepdims=True)
    xc = x - mu
    return xc * lax.rsqrt(jnp.mean(xc * xc, axis=-1, keepdims=True) + NORM_EPS) * g + b


def _dot(a, b):
    return jnp.dot(a, b, preferred_element_type=F32)


def _norm_proj_kernel(h_ref, g_ref, w_ref, a_ref, b_ref):
    z = _rms(h_ref[...], g_ref[...])
    proj = _dot(z.astype(BF16), w_ref[...])
    for jb in range(SSM_WIDTH // LANES):
        a_ref[jb] = proj[:, jb * LANES:(jb + 1) * LANES]
    b_ref[...] = proj[:, SSM_WIDTH:].astype(b_ref.dtype)


def _norm_proj(h, g, w, tm):
    n, d = h.shape
    cols = w.shape[1]
    return pl.pallas_call(
        _norm_proj_kernel,
        grid=(n // tm,),
        in_specs=[pl.BlockSpec((tm, d), lambda i: (i, 0)),
                  pl.BlockSpec((1, d), lambda i: (0, 0)),
                  pl.BlockSpec((d, cols), lambda i: (0, 0))],
        out_specs=[pl.BlockSpec((SSM_WIDTH // LANES, tm, LANES), lambda i: (0, i, 0)),
                   pl.BlockSpec((tm, cols - SSM_WIDTH), lambda i: (i, 0))],
        out_shape=[jax.ShapeDtypeStruct((SSM_WIDTH // LANES, n, LANES), F32),
                   jax.ShapeDtypeStruct((n, cols - SSM_WIDTH), BF16)],
        compiler_params=_params(("parallel",)),
        name="even_in_proj",
    )(h, g, w)


def _s5_matrices(lam_re, lam_im, log_dt, b_re, b_im, c_re, c_im):
    t = SSM_CHUNK
    lr = jnp.minimum(lam_re.astype(F32), -1e-4)
    li = lam_im.astype(F32)
    dt = jnp.exp(log_dt.astype(F32))[:, None]
    mag = jnp.exp(lr * dt)
    a_re = mag * jnp.cos(li * dt)
    a_im = mag * jnp.sin(li * dt)
    den = lr * lr + li * li
    nr = a_re - 1.0
    coef_re = (nr * lr + a_im * li) / den
    coef_im = (a_im * lr - nr * li) / den
    br = b_re.astype(F32)
    bi = b_im.astype(F32)
    bb_re = coef_re[..., None] * br - coef_im[..., None] * bi
    bb_im = coef_re[..., None] * bi + coef_im[..., None] * br
    cr = c_re.astype(F32)
    ci = c_im.astype(F32)
    pw_re = [jnp.ones_like(a_re)]
    pw_im = [jnp.zeros_like(a_im)]
    for _ in range(t):
        pr, pi = pw_re[-1], pw_im[-1]
        pw_re.append(pr * a_re - pi * a_im)
        pw_im.append(pr * a_im + pi * a_re)
    pw_re = jnp.stack(pw_re)
    pw_im = jnp.stack(pw_im)
    ab_re = pw_re[:t, :, :, None] * bb_re[None] - pw_im[:t, :, :, None] * bb_im[None]
    ab_im = pw_re[:t, :, :, None] * bb_im[None] + pw_im[:t, :, :, None] * bb_re[None]
    hi = lax.Precision.HIGHEST
    k_lag = (jnp.einsum('gnp,tgpm->tgnm', cr, ab_re, precision=hi)
             - jnp.einsum('gnp,tgpm->tgnm', ci, ab_im, precision=hi))
    n_pairs = SSM_GROUPS // 2
    st = 2 * SSM_STATE

    def pair_diag(w):
        w = w.reshape((n_pairs, 2) + w.shape[1:])
        z = jnp.zeros_like(w[:, 0])
        top = jnp.concatenate([w[:, 0], z], axis=-1)
        bot = jnp.concatenate([z, w[:, 1]], axis=-1)
        return jnp.concatenate([top, bot], axis=-2)

    k_blk = pair_diag(k_lag.transpose(1, 0, 3, 2))
    rev_re = pw_re[:t][::-1]
    rev_im = pw_im[:t][::-1]
    ws_re = rev_re[..., None] * bb_re[None] - rev_im[..., None] * bb_im[None]
    ws_im = rev_re[..., None] * bb_im[None] + rev_im[..., None] * bb_re[None]
    ws_re = pair_diag(ws_re.transpose(1, 0, 3, 2)).reshape(n_pairs, SSM_PAIR, st).astype(BF16)
    ws_im = pair_diag(ws_im.transpose(1, 0, 3, 2)).reshape(n_pairs, SSM_PAIR, st).astype(BF16)
    ca_re = cr[None] * pw_re[1:, :, None, :] - ci[None] * pw_im[1:, :, None, :]
    ca_im = cr[None] * pw_im[1:, :, None, :] + ci[None] * pw_re[1:, :, None, :]
    co_re = pair_diag(ca_re.transpose(1, 0, 3, 2))
    co_im = pair_diag((-ca_im).transpose(1, 0, 3, 2))
    w_intra, wo_re, wo_im = _s5_expand(k_blk, co_re, co_im)
    ch_re, ch_im = pw_re[t].reshape(1, -1), pw_im[t].reshape(1, -1)
    tab_re, tab_im = [jnp.ones_like(ch_re)], [jnp.zeros_like(ch_im)]
    for _ in range(SUBLANES):
        pr, pi = tab_re[-1], tab_im[-1]
        tab_re.append(pr * ch_re - pi * ch_im)
        tab_im.append(pr * ch_im + pi * ch_re)
    pad = [jnp.zeros_like(ch_re)] * (2 * SUBLANES - len(tab_re))
    return dict(
        w_intra=w_intra, ws_re=ws_re, ws_im=ws_im, wo_re=wo_re, wo_im=wo_im,
        at_re=jnp.concatenate(tab_re + pad, axis=0), at_im=jnp.concatenate(tab_im + pad, axis=0))


def _s5_expand_kernel(k_ref, cre_ref, cim_ref, wi_ref, wore_ref, woim_ref, kcat_ref):
    pw = 2 * SSM_GROUP
    for tau in range(SSM_CHUNK):
        kcat_ref[:, tau * pw:(tau + 1) * pw] = k_ref[0, tau]
        wore_ref[0, :, tau * pw:(tau + 1) * pw] = cre_ref[0, tau].astype(wore_ref.dtype)
        woim_ref[0, :, tau * pw:(tau + 1) * pw] = cim_ref[0, tau].astype(woim_ref.dtype)
    kcat = kcat_ref[...]
    col = lax.broadcasted_iota(jnp.int32, kcat.shape, 1)
    for s in range(SSM_CHUNK):
        blk = kcat if s == 0 else jnp.where(col >= s * pw, pltpu.roll(kcat, s * pw, 1), 0.0)
        wi_ref[0, s * pw:(s + 1) * pw, :] = blk.astype(wi_ref.dtype)


def _s5_expand(k_blk, co_re, co_im):
    n_pairs = k_blk.shape[0]
    pw = 2 * SSM_GROUP
    st = 2 * SSM_STATE
    return pl.pallas_call(
        _s5_expand_kernel,
        grid=(n_pairs,),
        in_specs=[pl.BlockSpec((1, SSM_CHUNK, pw, pw), lambda q: (q, 0, 0, 0)),
                  pl.BlockSpec((1, SSM_CHUNK, st, pw), lambda q: (q, 0, 0, 0)),
                  pl.BlockSpec((1, SSM_CHUNK, st, pw), lambda q: (q, 0, 0, 0))],
        out_specs=[pl.BlockSpec((1, SSM_PAIR, SSM_PAIR), lambda q: (q, 0, 0)),
                   pl.BlockSpec((1, st, SSM_PAIR), lambda q: (q, 0, 0)),
                   pl.BlockSpec((1, st, SSM_PAIR), lambda q: (q, 0, 0))],
        out_shape=[jax.ShapeDtypeStruct((n_pairs, SSM_PAIR, SSM_PAIR), BF16),
                   jax.ShapeDtypeStruct((n_pairs, st, SSM_PAIR), BF16),
                   jax.ShapeDtypeStruct((n_pairs, st, SSM_PAIR), BF16)],
        scratch_shapes=[pltpu.VMEM((pw, SSM_PAIR), F32)],
        compiler_params=_params(("parallel",)),
        name="s5_expand_weights",
    )(k_blk, co_re, co_im)


S5_LANE_PAIRS = LANES // (2 * SSM_GROUP)
S5_SCAN_LANES = 512


def _s5_state_kernel(u0_ref, u1_ref, u2_ref, u3_ref, wre_ref, wim_ref, are_ref, aim_ref,
                     x_ref, hre_ref, him_ref, sre_ref, sim_ref):
    n_chunks = x_ref.shape[0]
    pw = 2 * SSM_GROUP
    u_refs = (u0_ref, u1_ref, u2_ref, u3_ref)
    for t in range(SSM_CHUNK):
        for j, u_ref in enumerate(u_refs):
            ut = u_ref[pl.ds(t, n_chunks, stride=SSM_CHUNK), :]
            for qq in range(S5_LANE_PAIRS):
                q = j * S5_LANE_PAIRS + qq
                x_ref[:, q * SSM_PAIR + t * pw: q * SSM_PAIR + (t + 1) * pw] = (
                    ut[:, qq * pw:(qq + 1) * pw].astype(x_ref.dtype))
    st = 2 * SSM_STATE
    for q in range(SSM_GROUPS // 2):
        xq = x_ref[:, q * SSM_PAIR:(q + 1) * SSM_PAIR]
        sre_ref[:, q * st:(q + 1) * st] = _dot(xq, wre_ref[q])
        sim_ref[:, q * st:(q + 1) * st] = _dot(xq, wim_ref[q])

    row = lax.broadcasted_iota(jnp.int32, (SUBLANES, S5_SCAN_LANES), 0)
    zero = jnp.zeros((SUBLANES, S5_SCAN_LANES), F32)

    def cmul(ar, ai, xr, xi):
        return ar * xr - ai * xi, ar * xi + ai * xr

    def shift(x, k):
        return jnp.where(row >= k, pltpu.roll(x, k, 0), 0.0)

    for c0 in range(0, sre_ref.shape[1], S5_SCAN_LANES):
        cols = pl.ds(c0, S5_SCAN_LANES)

        def body(k, carry, cols=cols):
            r0 = pl.multiple_of(k * SUBLANES, SUBLANES)
            ir = sre_ref[pl.ds(r0, SUBLANES), cols]
            ii = sim_ref[pl.ds(r0, SUBLANES), cols]
            for step in (1, 2, 4):
                tr, ti = cmul(are_ref[pl.ds(step, 1), cols], aim_ref[pl.ds(step, 1), cols],
                              shift(ir, step), shift(ii, step))
                ir, ii = ir + tr, ii + ti
            cr, ci = carry
            pr, pi = cmul(are_ref[pl.ds(0, SUBLANES), cols], aim_ref[pl.ds(0, SUBLANES), cols], cr, ci)
            hre_ref[pl.ds(r0, SUBLANES), cols] = pr + shift(ir, 1)
            him_ref[pl.ds(r0, SUBLANES), cols] = pi + shift(ii, 1)
            nr, ni = cmul(are_ref[pl.ds(SUBLANES, 1), cols], aim_ref[pl.ds(SUBLANES, 1), cols], cr, ci)
            last_r = jnp.broadcast_to(ir[SUBLANES - 1:SUBLANES], ir.shape)
            last_i = jnp.broadcast_to(ii[SUBLANES - 1:SUBLANES], ii.shape)
            return nr + last_r, ni + last_i

        lax.fori_loop(0, n_chunks // SUBLANES, body, (zero, zero))


def _s5_out_kernel(x_ref, wi_ref, hre_ref, him_ref, wore_ref, woim_ref, d_ref, y_ref, yt_ref):
    n_chunks = x_ref.shape[0]
    pw = 2 * SSM_GROUP
    st = 2 * SSM_STATE
    for qq in range(S5_LANE_PAIRS):
        x = x_ref[:, qq * SSM_PAIR:(qq + 1) * SSM_PAIR]
        y = _dot(x, wi_ref[qq])
        y += _dot(hre_ref[:, qq * st:(qq + 1) * st].astype(BF16), wore_ref[qq])
        y += _dot(him_ref[:, qq * st:(qq + 1) * st].astype(BF16), woim_ref[qq])
        y += d_ref[:, qq * SSM_PAIR:(qq + 1) * SSM_PAIR] * x.astype(F32)
        y = jax.nn.gelu(y)
        for t in range(SSM_CHUNK):
            yt_ref[t, :, qq * pw:(qq + 1) * pw] = y[:, t * pw:(t + 1) * pw]
    for t in range(SSM_CHUNK):
        y_ref[pl.ds(t, n_chunks, stride=SSM_CHUNK), :] = yt_ref[t]


def _s5_mixer(u, mats, d, batch, seq):
    t = SSM_CHUNK
    n_chunks = seq // t
    n_pairs = SSM_GROUPS // 2
    cols = n_pairs * SSM_PAIR
    st = 2 * SSM_STATE
    n_state = n_pairs * st
    n_blk = SSM_WIDTH // LANES
    assert n_blk == 4 and n_chunks % SUBLANES == 0
    once = pl.Buffered(1)
    x, h_re, h_im = pl.pallas_call(
        _s5_state_kernel,
        grid=(batch,),
        in_specs=[pl.BlockSpec((None, seq, LANES), lambda b, j=j: (j, b, 0)) for j in range(n_blk)] + [
            pl.BlockSpec((n_pairs, SSM_PAIR, st), lambda b: (0, 0, 0), pipeline_mode=once),
            pl.BlockSpec((n_pairs, SSM_PAIR, st), lambda b: (0, 0, 0), pipeline_mode=once),
            pl.BlockSpec((2 * SUBLANES, n_state), lambda b: (0, 0)),
            pl.BlockSpec((2 * SUBLANES, n_state), lambda b: (0, 0))],
        out_specs=[pl.BlockSpec((n_chunks, cols), lambda b: (b, 0)),
                   pl.BlockSpec((n_chunks, n_state), lambda b: (b, 0)),
                   pl.BlockSpec((n_chunks, n_state), lambda b: (b, 0))],
        out_shape=[jax.ShapeDtypeStruct((batch * n_chunks, cols), BF16),
                   jax.ShapeDtypeStruct((batch * n_chunks, n_state), F32),
                   jax.ShapeDtypeStruct((batch * n_chunks, n_state), F32)],
        scratch_shapes=[pltpu.VMEM((n_chunks, n_state), F32), pltpu.VMEM((n_chunks, n_state), F32)],
        compiler_params=_params(("parallel",)),
        name="s5_state_scan",
    )(u, u, u, u, mats['ws_re'], mats['ws_im'], mats['at_re'], mats['at_im'])
    lp = S5_LANE_PAIRS
    d_cols = jnp.broadcast_to(d.astype(F32).reshape(n_pairs, 1, 2 * SSM_GROUP),
                              (n_pairs, t, 2 * SSM_GROUP)).reshape(1, cols)
    return pl.pallas_call(
        _s5_out_kernel,
        grid=(batch, n_blk),
        in_specs=[pl.BlockSpec((n_chunks, lp * SSM_PAIR), lambda b, j: (b, j)),
                  pl.BlockSpec((lp, SSM_PAIR, SSM_PAIR), lambda b, j: (j, 0, 0)),
                  pl.BlockSpec((n_chunks, lp * st), lambda b, j: (b, j)),
                  pl.BlockSpec((n_chunks, lp * st), lambda b, j: (b, j)),
                  pl.BlockSpec((lp, st, SSM_PAIR), lambda b, j: (j, 0, 0)),
                  pl.BlockSpec((lp, st, SSM_PAIR), lambda b, j: (j, 0, 0)),
                  pl.BlockSpec((1, lp * SSM_PAIR), lambda b, j: (0, j))],
        out_specs=pl.BlockSpec((None, seq, LANES), lambda b, j: (j, b, 0)),
        out_shape=jax.ShapeDtypeStruct((n_blk, batch * seq, LANES), F32),
        scratch_shapes=[pltpu.VMEM((t, n_chunks, LANES), F32)],
        compiler_params=_params(("parallel", "parallel")),
        name="s5_out",
    )(x, mats['w_intra'], h_re, h_im, mats['wo_re'], mats['wo_im'], d_cols)


def _even_mix_kernel(ys_ref, bu_ref, bv_ref, h_ref, wglu_ref, lng_ref, lnb_ref, ws_ref, bias_ref,
                     wo_a_ref, wo_b_ref, g1_ref, g2_ref, hout_ref, z_ref, s_scr):
    tm = h_ref.shape[0]
    ys = jnp.concatenate([ys_ref[jb] for jb in range(ys_ref.shape[0])], axis=1)
    ya = ys * jax.nn.sigmoid(_dot(ys.astype(BF16), wglu_ref[...]))
    u = jax.nn.gelu(bu_ref[...].astype(F32))
    v = _layer_norm(jax.nn.gelu(bv_ref[...].astype(F32)), lng_ref[...], lnb_ref[...])
    lane = lax.broadcasted_iota(jnp.int32, v.shape, 1)
    left = (lane % LANES) < SGU_HEAD_DIM
    v_l = jnp.where(left, v, 0.0).astype(BF16)
    v_r = jnp.where(left, 0.0, v).astype(BF16)
    for c in range(tm // SGU_CHUNK):
        rows = slice(c * SGU_CHUNK, (c + 1) * SGU_CHUNK)
        for p in range(SGU_HEADS // 2):
            cols = slice(p * LANES, (p + 1) * LANES)
            s_scr[rows, cols] = (_dot(ws_ref[2 * p], v_l[rows, cols]) + _dot(ws_ref[2 * p + 1], v_r[rows, cols]))
    bias = jnp.concatenate([bias_ref[...]] * (tm // SGU_CHUNK), axis=0)
    yb = u * (s_scr[...] + bias)
    mix = _dot(ya.astype(BF16), wo_a_ref[...]) + _dot(yb.astype(BF16), wo_b_ref[...])
    h_new = h_ref[...] + _rms(mix, g1_ref[...])
    hout_ref[...] = h_new
    z_ref[...] = _rms(h_new, g2_ref[...]).astype(z_ref.dtype)


def _even_mix(ys, proj, h, wglu, lng, lnb, ws, bias, wo_a, wo_b, g1, g2, tm):
    n, d = h.shape
    w = SGU_WIDTH
    const = lambda *shape: pl.BlockSpec(shape, lambda i: (0,) * len(shape))
    return pl.pallas_call(
        _even_mix_kernel,
        grid=(n // tm,),
        in_specs=[pl.BlockSpec((w // LANES, tm, LANES), lambda i: (0, i, 0)),
                  pl.BlockSpec((tm, w), lambda i: (i, 0)),
                  pl.BlockSpec((tm, w), lambda i: (i, 1)),
                  pl.BlockSpec((tm, d), lambda i: (i, 0)),
                  const(w, w), const(1, w), const(1, w),
                  const(SGU_HEADS, SGU_CHUNK, SGU_CHUNK), const(SGU_CHUNK, w),
                  const(w, d), const(w, d), const(1, d), const(1, d)],
        out_specs=[pl.BlockSpec((tm, d), lambda i: (i, 0)),
                   pl.BlockSpec((tm, d), lambda i: (i, 0))],
        out_shape=[jax.ShapeDtypeStruct((n, d), F32), jax.ShapeDtypeStruct((n, d), BF16)],
        scratch_shapes=[pltpu.VMEM((tm, w), F32)],
        compiler_params=_params(("parallel",)),
        name="even_mix",
    )(ys, proj, proj, h, wglu, lng, lnb, ws, bias, wo_a, wo_b, g1, g2)


def _ffn_kernel(z_ref, wg_ref, wu_ref, wd_ref, h_ref, g3_ref, gn_ref, hout_ref, zout_ref, acc_ref):
    j = pl.program_id(1)

    @pl.when(j == 0)
    def _():
        acc_ref[...] = jnp.zeros_like(acc_ref)

    z = z_ref[...]
    a = jax.nn.silu(_dot(z, wg_ref[...])) * _dot(z, wu_ref[...])
    acc_ref[...] += _dot(a.astype(BF16), wd_ref[...])

    @pl.when(j == pl.num_programs(1) - 1)
    def _():
        h_new = h_ref[...] + _rms(acc_ref[...], g3_ref[...])
        hout_ref[...] = h_new
        zout_ref[...] = _rms(h_new, gn_ref[...]).astype(zout_ref.dtype)


def _dense_ffn(z, wg, wu, wd, h, g3, g_next, tm, tf):
    n, d = h.shape
    ff = wg.shape[1]
    return pl.pallas_call(
        _ffn_kernel,
        grid=(n // tm, ff // tf),
        in_specs=[pl.BlockSpec((tm, d), lambda i, j: (i, 0)),
                  pl.BlockSpec((d, tf), lambda i, j: (0, j)),
                  pl.BlockSpec((d, tf), lambda i, j: (0, j)),
                  pl.BlockSpec((tf, d), lambda i, j: (j, 0)),
                  pl.BlockSpec((tm, d), lambda i, j: (i, 0)),
                  pl.BlockSpec((1, d), lambda i, j: (0, 0)),
                  pl.BlockSpec((1, d), lambda i, j: (0, 0))],
        out_specs=[pl.BlockSpec((tm, d), lambda i, j: (i, 0)),
                   pl.BlockSpec((tm, d), lambda i, j: (i, 0))],
        out_shape=[jax.ShapeDtypeStruct((n, d), F32), jax.ShapeDtypeStruct((n, d), BF16)],
        scratch_shapes=[pltpu.VMEM((tm, d), F32)],
        compiler_params=_params(("parallel", "arbitrary")),
        name="dense_ffn",
    )(z, wg, wu, wd, h, g3, g_next)


def _odd_proj_kernel(z_ref, win_ref, gq_ref, gkv_ref, wuq_ref, wuqs_ref, wuk_ref, wuv_ref, vone_ref, cos_ref, sin_ref,
                     zc_ref, q_ref, k_ref, v_ref, *, scale):
    z = z_ref[...]
    proj = _dot(z, win_ref[...])
    c0 = 2 * CONV_CH
    c1 = c0 + MLA_Q_RANK
    c2 = c1 + MLA_KV_RANK
    c3 = c2 + MLA_PAD
    zc_ref[...] = proj[:, :c0].astype(zc_ref.dtype)
    cq = _rms(proj[:, c0:c1], gq_ref[...]).astype(BF16)
    ckv = _rms(proj[:, c1:c2], gkv_ref[...]).astype(BF16)
    cos = cos_ref[...]
    sin = sin_ref[...]
    cos_h = jnp.concatenate([cos] * MLA_HEADS, axis=1)
    sin_h = jnp.concatenate([sin] * MLA_HEADS, axis=1)
    q = _dot(cq, wuq_ref[...]) * cos_h + _dot(cq, wuqs_ref[...]) * sin_h
    q_ref[...] = (q * scale).astype(q_ref.dtype)
    kr = proj[:, c2:c3] * cos + proj[:, c3:] * sin
    k = _dot(ckv, wuk_ref[...]) + jnp.concatenate([kr] * MLA_HEADS, axis=1)
    k_ref[...] = k.astype(k_ref.dtype)
    vt = lax.dot_general(wuv_ref[...], ckv, (((1,), (1,)), ((), ())), preferred_element_type=F32)
    v_ref[0] = (vt + vone_ref[...]).astype(v_ref.dtype)


def _odd_proj(z, win, gq, gkv, wuq, wuqs, wuk, wuv_t, v_one, cos_t, sin_t, seq, tm):
    n, d = z.shape
    hp = MLA_HEADS * MLA_PAD
    vr = MLA_HEADS * MLA_VROWS
    n_l = seq // tm
    const = lambda *shape: pl.BlockSpec(shape, lambda i: (0,) * len(shape))
    out = jax.ShapeDtypeStruct((n, hp), BF16)
    scale = float((MLA_NOPE + MLA_ROPE) ** -0.5 * math.log2(math.e))
    return pl.pallas_call(
        functools.partial(_odd_proj_kernel, scale=scale),
        grid=(n // tm,),
        in_specs=[pl.BlockSpec((tm, d), lambda i: (i, 0)),
                  const(d, win.shape[1]), const(1, MLA_Q_RANK), const(1, MLA_KV_RANK),
                  const(MLA_Q_RANK, hp), const(MLA_Q_RANK, hp), const(MLA_KV_RANK, hp), const(vr, MLA_KV_RANK),
                  const(vr, 1),
                  pl.BlockSpec((tm, MLA_PAD), lambda i: (i % n_l, 0)),
                  pl.BlockSpec((tm, MLA_PAD), lambda i: (i % n_l, 0))],
        out_specs=[pl.BlockSpec((tm, 2 * CONV_CH), lambda i: (i, 0)),
                   pl.BlockSpec((tm, hp), lambda i: (i, 0)),
                   pl.BlockSpec((tm, hp), lambda i: (i, 0)),
                   pl.BlockSpec((1, vr, tm), lambda i: (i, 0, 0))],
        out_shape=[jax.ShapeDtypeStruct((n, 2 * CONV_CH), BF16), out, out,
                   jax.ShapeDtypeStruct((n // tm, vr, tm), BF16)],
        compiler_params=_params(("parallel",)),
        name="odd_in_proj",
    )(z, win, gq, gkv, wuq, wuqs, wuk, wuv_t, v_one, cos_t, sin_t)


def _attn_kernel(q_ref, k_ref, vt_ref, o_ref, acc_ref, *, blk):
    i = pl.program_id(2)
    acc_ref[...] = jnp.zeros_like(acc_ref)

    def step(j, m, masked):
        r0 = pl.multiple_of(j * blk, blk)
        scores = []
        for hh in range(ATTN_HEADS):
            q = q_ref[0, :, hh * MLA_PAD:(hh + 1) * MLA_PAD]
            k = k_ref[0, pl.ds(r0, blk), hh * MLA_PAD:(hh + 1) * MLA_PAD]
            st = lax.dot_general(k, q, (((1,), (1,)), ((), ())), preferred_element_type=F32)
            if masked:
                key = lax.broadcasted_iota(jnp.int32, st.shape, 0)
                qry = lax.broadcasted_iota(jnp.int32, st.shape, 1)
                st = jnp.where(key <= qry, st, -1e30)
            scores.append(st)
        soft = []
        for hh in range(ATTN_HEADS):
            m_new = jnp.maximum(m[hh], jnp.max(scores[hh], axis=0, keepdims=True))
            soft.append((m_new, jnp.exp2(m[hh] - m_new), jnp.exp2(scores[hh] - m_new).astype(BF16)))
        for hh in range(ATTN_HEADS):
            vt = vt_ref[j, hh * MLA_VROWS:(hh + 1) * MLA_VROWS, :]
            acc_ref[hh] = soft[hh][1] * acc_ref[hh] + _dot(vt, soft[hh][2])
        return tuple(s[0] for s in soft)

    init = jnp.full((1, blk), -1e30, F32)
    m = lax.fori_loop(0, i, lambda j, m: step(j, m, False), (init,) * ATTN_HEADS)
    step(i, m, True)
    ot = jnp.concatenate([acc_ref[hh][:MLA_V] / acc_ref[hh][MLA_V:MLA_V + 1] for hh in range(ATTN_HEADS)], axis=0)
    o_ref[0] = ot.T.astype(o_ref.dtype)


def _attention(q, k, vt, blk):
    b, seq, _ = q.shape
    n_blk = seq // blk
    n_pairs = MLA_HEADS // ATTN_HEADS
    return pl.pallas_call(
        functools.partial(_attn_kernel, blk=blk),
        grid=(b, n_pairs, n_blk),
        in_specs=[pl.BlockSpec((1, blk, ATTN_HEADS * MLA_PAD), lambda bi, p, i: (bi, i, p)),
                  pl.BlockSpec((1, seq, ATTN_HEADS * MLA_PAD), lambda bi, p, i: (bi, 0, p)),
                  pl.BlockSpec((n_blk, ATTN_HEADS * MLA_VROWS, blk), lambda bi, p, i: (bi, p, 0))],
        out_specs=pl.BlockSpec((1, blk, ATTN_HEADS * MLA_V), lambda bi, p, i: (bi, i, p)),
        out_shape=jax.ShapeDtypeStruct((b, seq, MLA_HEADS * MLA_V), BF16),
        scratch_shapes=[pltpu.VMEM((ATTN_HEADS, MLA_VROWS, blk), F32)],
        compiler_params=_params(("parallel", "parallel", "parallel")),
        name="mla_attention",
    )(q, k, vt)


def _conv_kernel(zc_ref, w_ref, b_ref, lng_ref, lnb_ref, y_ref, buf_ref, part_ref):
    tm = zc_ref.shape[1]

    @pl.when(pl.program_id(1) == 0)
    def _():
        buf_ref[pl.ds(0, CONV_HALO), :] = jnp.zeros((CONV_HALO, CONV_CH), F32)
        buf_ref[pl.ds(CONV_HALO + tm, SUBLANES), :] = jnp.zeros((SUBLANES, CONV_CH), F32)

    zc = zc_ref[0].astype(F32)
    hh = zc[:, :CONV_CH] * jax.nn.sigmoid(zc[:, CONV_CH:])
    buf_ref[pl.ds(CONV_HALO, tm), :] = hh
    off = CONV_HALO - (CONV_TAPS - 1)
    acc = jnp.zeros((tm, CONV_CH), F32) + b_ref[...]
    for b in range(SUBLANES):
        taps = [k for k in range(CONV_TAPS) if (off + k) % SUBLANES == b]
        part = None
        for k in taps:
            term = w_ref[pl.ds(k, 1), :] * buf_ref[pl.ds(off + k - b, tm + SUBLANES), :]
            part = term if part is None else part + term
        if b == 0:
            acc = acc + part[:tm]
        else:
            part_ref[...] = part
            acc = acc + part_ref[pl.ds(b, tm), :]
    buf_ref[pl.ds(0, CONV_HALO), :] = buf_ref[pl.ds(tm, CONV_HALO), :]
    y_ref[0] = jax.nn.silu(_layer_norm(acc, lng_ref[...], lnb_ref[...])).astype(y_ref.dtype)


def _conv_mixer(zc, w, b, lng, lnb, tm):
    bsz, seq, _ = zc.shape
    const = lambda *shape: pl.BlockSpec(shape, lambda bi, i: (0,) * len(shape))
    return pl.pallas_call(
        _conv_kernel,
        grid=(bsz, seq // tm),
        in_specs=[pl.BlockSpec((1, tm, 2 * CONV_CH), lambda bi, i: (bi, i, 0)),
                  const(CONV_HALO, CONV_CH), const(1, CONV_CH), const(1, CONV_CH), const(1, CONV_CH)],
        out_specs=pl.BlockSpec((1, tm, CONV_CH), lambda bi, i: (bi, i, 0)),
        out_shape=jax.ShapeDtypeStruct((bsz, seq, CONV_CH), BF16),
        scratch_shapes=[pltpu.VMEM((CONV_HALO + tm + SUBLANES, CONV_CH), F32),
                        pltpu.VMEM((tm + SUBLANES, CONV_CH), F32)],
        compiler_params=_params(("arbitrary", "arbitrary")),
        name="conv_module",
    )(zc, w, b, lng, lnb)


def _odd_mix_kernel(yc_ref, yd_ref, h_ref, wo_a_ref, wo_b_ref, g1_ref, g2_ref, wr_ref, hout_ref, z_ref, route_ref):
    tm = h_ref.shape[0]
    halves = [slice(0, tm // 2), slice(tm // 2, tm)]
    mixes = [_dot(yc_ref[r, :], wo_a_ref[...]) + _dot(yd_ref[r, :], wo_b_ref[...]) for r in halves]
    zs = []
    for r, mix in zip(halves, mixes):
        h_new = h_ref[r, :] + _rms(mix, g1_ref[...])
        hout_ref[r, :] = h_new
        zs.append(_rms(h_new, g2_ref[...]))
    all_logits = [_dot(z.astype(BF16), wr_ref[...]) for z in zs]
    _store_row_tiles(z_ref, jnp.concatenate(zs, axis=0))
    neg = -jnp.inf
    for r, logits in zip(halves, all_logits):
        lane = lax.broadcasted_iota(jnp.int32, logits.shape, 1)
        logits = jnp.where(lane < N_EXPERTS, logits, neg)
        m1 = jnp.max(logits, axis=-1, keepdims=True)
        i1 = jnp.min(jnp.where(logits == m1, lane, LANES), axis=-1, keepdims=True)
        rest = jnp.where(lane == i1, neg, logits)
        m2 = jnp.max(rest, axis=-1, keepdims=True)
        i2 = jnp.min(jnp.where(rest == m2, lane, LANES), axis=-1, keepdims=True)
        e = jnp.exp(m2 - m1)
        w1 = 1.0 / (1.0 + e)
        w2 = e / (1.0 + e)
        route_ref[r, :] = jnp.where(lane == 0, i1.astype(F32),
                                    jnp.where(lane == 1, i2.astype(F32),
                                              jnp.where(lane == 2, w1, jnp.where(lane == 3, w2, 0.0))))


def _odd_mix(yc, yd, h, wo_a, wo_b, g1, g2, wr, tm):
    n, d = h.shape
    const = lambda *shape: pl.BlockSpec(shape, lambda i: (0,) * len(shape))
    return pl.pallas_call(
        _odd_mix_kernel,
        grid=(n // tm,),
        in_specs=[pl.BlockSpec((tm, yc.shape[1]), lambda i: (i, 0)),
                  pl.BlockSpec((tm, yd.shape[1]), lambda i: (i, 0)),
                  pl.BlockSpec((tm, d), lambda i: (i, 0)),
                  const(*wo_a.shape), const(*wo_b.shape), const(1, d), const(1, d), const(d, LANES)],
        out_specs=[pl.BlockSpec((tm, d), lambda i: (i, 0)),
                   pl.BlockSpec((tm * ROW_TILE, LANES), lambda i: (i, 0)),
                   pl.BlockSpec((tm, LANES), lambda i: (i, 0))],
        out_shape=[jax.ShapeDtypeStruct((n, d), F32), jax.ShapeDtypeStruct((n * ROW_TILE, LANES), F32),
                   jax.ShapeDtypeStruct((n, LANES), F32)],
        compiler_params=_params(("parallel",)),
        name="odd_mix_router",
    )(yc, yd, h, wo_a, wo_b, g1, g2, wr)


def _store_row_tiles(ref, x):
    rows = x.shape[0]
    for s in range(ROW_TILE):
        ref[pl.ds(s, rows, stride=ROW_TILE), :] = x[:, s * LANES:(s + 1) * LANES]


def _load_row_tiles(ref, rows):
    return [ref[pl.ds(s, rows, stride=ROW_TILE), :] for s in range(ROW_TILE)]


def _gather_rows(idx_ref, base, n_rows, src_hbm, dst_ref, sem):
    def body(r, c):
        src = pl.multiple_of(idx_ref[base + r] * ROW_TILE, ROW_TILE)
        dst = pl.multiple_of(r * ROW_TILE, ROW_TILE)
        pltpu.make_async_copy(src_hbm.at[pl.ds(src, ROW_TILE), :], dst_ref.at[pl.ds(dst, ROW_TILE), :], sem).start()
        return c

    lax.fori_loop(0, n_rows, body, 0, unroll=8)


def _wait_rows(src_hbm, dst_ref, sem):
    pltpu.make_async_copy(src_hbm.at[pl.ds(0, dst_ref.shape[0]), :], dst_ref, sem).wait()


def _row_copy(src_ref, src_row, dst_ref, dst_row, sem):
    src = pl.multiple_of(src_row * ROW_TILE, ROW_TILE)
    dst = pl.multiple_of(dst_row * ROW_TILE, ROW_TILE)
    return pltpu.make_async_copy(src_ref.at[pl.ds(src, ROW_TILE), :], dst_ref.at[pl.ds(dst, ROW_TILE), :], sem)


def _moe_ffn_kernel(te_ref, nu_ref, tok_ref, dst_ref, z_hbm, wg_ref, wu_ref, wd_ref, y_hbm,
                    xraw_ref, xb_ref, acc_ref, yst_ref, gsem, ssem, *, rows_per_step):
    i = pl.program_id(0)
    j = pl.program_id(1)
    tm = xb_ref.shape[0]
    stride = yst_ref.shape[0] // ROW_TILE
    n_used = nu_ref[0]
    slot = i % 2
    first = j == 0
    last = j == pl.num_programs(1) - 1

    @pl.when(first & (i == 0))
    def _():
        yst_ref[...] = jnp.zeros_like(yst_ref)
        _gather_rows(tok_ref, 0, stride, z_hbm, xraw_ref.at[0], gsem.at[0])

    @pl.when(first & (i <= n_used))
    def _():
        _wait_rows(z_hbm, xraw_ref.at[slot], gsem.at[slot])

    @pl.when(first & (i < n_used))
    def _():
        for s, blk in enumerate(_load_row_tiles(xraw_ref.at[slot], tm)):
            xb_ref[:, s * LANES:(s + 1) * LANES] = blk.astype(BF16)
        acc_ref[...] = jnp.zeros_like(acc_ref)

    @pl.when(first & (i == n_used))
    def _():
        def body(r, c):
            _row_copy(yst_ref, r, y_hbm, dst_ref[i * stride + r], ssem).start()
            return c
        lax.fori_loop(0, stride, body, 0, unroll=8)

    def multiply(rows, with_copies):
        x = xb_ref[:rows, :]
        g = _dot(x, wg_ref[0].astype(BF16))
        u = _dot(x, wu_ref[0].astype(BF16))
        if with_copies:
            nxt = xraw_ref.at[1 - slot]
            for rr in range(rows_per_step):
                r = j * rows_per_step + rr
                _row_copy(z_hbm, tok_ref[(i + 1) * stride + r], nxt, r, gsem.at[1 - slot]).start(priority=rr % 2)
                _row_copy(yst_ref, r, y_hbm, dst_ref[i * stride + r], ssem).start(priority=rr % 2)
        a = jax.nn.silu(g) * u
        acc_ref[:rows, :] += _dot(a.astype(BF16), wd_ref[0].astype(BF16))

    used = i < n_used
    half = nu_ref[1 + i] <= tm // 2
    for rows, fits in ((tm, jnp.logical_not(half)), (tm // 2, half)):
        @pl.when(used & fits & jnp.logical_not(last))
        def _(rows=rows):
            multiply(rows, True)

        @pl.when(used & fits & last)
        def _(rows=rows):
            multiply(rows, False)

    @pl.when(last & (i <= n_used))
    def _():
        _wait_rows(z_hbm, yst_ref, ssem)

    @pl.when(last & (i < n_used))
    def _():
        _store_row_tiles(yst_ref, acc_ref[...])


def _moe_ffn(tile_expert, n_used, tok_tab, dst_tab, z_tiles, wg, wu, wd, n_tok, tm, tf):
    d, ff = wg.shape[1], wg.shape[2]
    n_f = ff // tf
    copy_steps = n_f - 1
    rows_per_step = -(-tm // copy_steps)
    stride = copy_steps * rows_per_step
    n_tiles = tile_expert.shape[0]
    assert tok_tab.shape[0] == dst_tab.shape[0] == (n_tiles + 1) * stride

    def col(i, j, nu):
        return jnp.where(i < nu[0], j, n_f - 1)

    return pl.pallas_call(
        functools.partial(_moe_ffn_kernel, rows_per_step=rows_per_step),
        grid_spec=pltpu.PrefetchScalarGridSpec(
            num_scalar_prefetch=4,
            grid=(n_tiles, n_f),
            in_specs=[pl.BlockSpec(memory_space=pl.ANY),
                      pl.BlockSpec((1, d, tf), lambda i, j, te, nu, tok, dst: (te[i], 0, col(i, j, nu))),
                      pl.BlockSpec((1, d, tf), lambda i, j, te, nu, tok, dst: (te[i], 0, col(i, j, nu))),
                      pl.BlockSpec((1, tf, d), lambda i, j, te, nu, tok, dst: (te[i], col(i, j, nu), 0))],
            out_specs=pl.BlockSpec(memory_space=pl.ANY),
            scratch_shapes=[pltpu.VMEM((2, stride * ROW_TILE, LANES), F32), pltpu.VMEM((tm, d), BF16),
                            pltpu.VMEM((tm, d), F32), pltpu.VMEM((stride * ROW_TILE, LANES), F32),
                            pltpu.SemaphoreType.DMA((2,)), pltpu.SemaphoreType.DMA(())]),
        out_shape=jax.ShapeDtypeStruct(((2 * n_tok + stride) * ROW_TILE, LANES), F32),
        compiler_params=_params(("arbitrary", "arbitrary")),
        name="moe_grouped_ffn",
    )(tile_expert, n_used, tok_tab, dst_tab, z_tiles, wg, wu, wd)


def _combine_kernel(ya_ref, yb_ref, route_ref, h_ref, g_ref, o_ref):
    tm = h_ref.shape[0]
    route = route_ref[...]
    a = jnp.concatenate(_load_row_tiles(ya_ref, tm), axis=1)
    b = jnp.concatenate(_load_row_tiles(yb_ref, tm), axis=1)
    f = route[:, 2:3] * a + route[:, 3:4] * b
    o_ref[...] = h_ref[...] + _rms(f, g_ref[...])


def _combine(y, route, h, g, tm):
    n, d = h.shape
    n_blk = n // tm
    return pl.pallas_call(
        _combine_kernel,
        grid=(n_blk,),
        in_specs=[pl.BlockSpec((tm * ROW_TILE, LANES), lambda i: (i, 0)),
                  pl.BlockSpec((tm * ROW_TILE, LANES), lambda i: (n_blk + i, 0)),
                  pl.BlockSpec((tm, LANES), lambda i: (i, 0)),
                  pl.BlockSpec((tm, d), lambda i: (i, 0)),
                  pl.BlockSpec((1, d), lambda i: (0, 0))],
        out_specs=pl.BlockSpec((tm, d), lambda i: (i, 0)),
        out_shape=jax.ShapeDtypeStruct((n, d), F32),
        compiler_params=_params(("parallel",)),
        name="moe_combine",
    )(y, y, route, h, g)


def _moe_plan(route, tm, stride):
    n = route.shape[0]
    eids = jnp.concatenate([route[:, 0], route[:, 1]]).astype(jnp.int32)
    onehot = (eids[:, None] == jnp.arange(N_EXPERTS, dtype=jnp.int32)[None, :]).astype(jnp.int32)
    csum = jnp.cumsum(onehot, axis=0)
    rank = jnp.sum(csum * onehot, axis=1) - 1
    counts = csum[-1]
    padded = ((counts + tm - 1) // tm) * tm
    ends = jnp.cumsum(padded)
    starts = ends - padded
    slot = jnp.sum(onehot * starts[None, :], axis=1) + rank
    n_tiles = 2 * n // tm + N_EXPERTS + 1
    copy_of_slot = jnp.full((n_tiles * tm,), -1, jnp.int32).at[slot].set(jnp.arange(2 * n, dtype=jnp.int32))
    copy_tab = jnp.pad(copy_of_slot.reshape(n_tiles, tm), ((0, 1), (0, stride - tm)), constant_values=-1)
    tok_tab = jnp.where(copy_tab >= 0, copy_tab % n, 0)
    dump = 2 * n + jnp.arange(stride, dtype=jnp.int32)[None, :]
    dst_tab = jnp.where(copy_tab >= 0, copy_tab, dump)
    dst_tab = jnp.concatenate([jnp.broadcast_to(dump, (1, stride)), dst_tab[:-1]], axis=0)
    n_used = (ends[-1] // tm).astype(jnp.int32)
    tile_start = jnp.minimum(jnp.arange(n_tiles, dtype=jnp.int32), n_used - 1) * tm
    tile_expert = jnp.sum((tile_start[:, None] >= ends[None, :]).astype(jnp.int32), axis=1)
    onehot_e = (tile_expert[:, None] == jnp.arange(N_EXPERTS, dtype=jnp.int32)[None, :]).astype(jnp.int32)
    run_end = jnp.sum(onehot_e * (starts + counts)[None, :], axis=1)
    tile_rows = jnp.clip(run_end - jnp.arange(n_tiles, dtype=jnp.int32) * tm, 0, tm)
    return tok_tab.reshape(-1), dst_tab.reshape(-1), tile_expert, jnp.concatenate([n_used.reshape(1), tile_rows])


def _odd_weights(od_w_in, mla_w_uq, mla_w_ukv):
    c2 = 2 * CONV_CH + MLA_Q_RANK + MLA_KV_RANK
    half = MLA_ROPE // 2
    w_kr = od_w_in[:, c2:]
    w_kr_sw = jnp.concatenate([w_kr[:, half:], w_kr[:, :half]], axis=1)
    zl = jnp.zeros((D_MODEL, MLA_NOPE), F32)
    zr = jnp.zeros((D_MODEL, MLA_PAD - MLA_NOPE - MLA_ROPE), F32)
    win = jnp.concatenate([od_w_in[:, :c2], zl, w_kr, zr, zl, w_kr_sw, zr], axis=1)
    dk = MLA_NOPE + MLA_ROPE
    wq = mla_w_uq.reshape(MLA_Q_RANK, MLA_HEADS, dk)
    zq = jnp.zeros((MLA_Q_RANK, MLA_HEADS, MLA_PAD - dk), F32)
    wuq = jnp.concatenate([wq, zq], axis=2).reshape(MLA_Q_RANK, MLA_HEADS * MLA_PAD)
    wq_sw = jnp.concatenate([jnp.zeros_like(wq[:, :, :MLA_NOPE]), wq[:, :, MLA_NOPE + half:],
                             wq[:, :, MLA_NOPE:MLA_NOPE + half], zq], axis=2)
    wuqs = wq_sw.reshape(MLA_Q_RANK, MLA_HEADS * MLA_PAD)
    wkv = mla_w_ukv.reshape(MLA_KV_RANK, MLA_HEADS, MLA_NOPE + MLA_V)
    zk = jnp.zeros((MLA_KV_RANK, MLA_HEADS, MLA_PAD - MLA_NOPE), F32)
    wuk = jnp.concatenate([wkv[:, :, :MLA_NOPE], zk], axis=2).reshape(MLA_KV_RANK, MLA_HEADS * MLA_PAD)
    zv = jnp.zeros((MLA_KV_RANK, MLA_HEADS, MLA_VROWS - MLA_V), F32)
    wuv_t = jnp.concatenate([wkv[:, :, MLA_NOPE:], zv], axis=2).reshape(MLA_KV_RANK, MLA_HEADS * MLA_VROWS).T
    v_one = jnp.zeros((MLA_HEADS, MLA_VROWS), F32).at[:, MLA_V].set(1.0).reshape(MLA_HEADS * MLA_VROWS, 1)
    return win.astype(BF16), wuq.astype(BF16), wuqs.astype(BF16), wuk.astype(BF16), wuv_t.astype(BF16), v_one


def _rope_tables(seq):
    inv = 1.0 / (ROPE_THETA ** (jnp.arange(0, MLA_ROPE, 2, dtype=F32) / MLA_ROPE))
    ang = jnp.arange(seq, dtype=F32)[:, None] * inv[None, :]
    cos, sin = jnp.cos(ang), jnp.sin(ang)
    ones = jnp.ones((seq, MLA_NOPE), F32)
    zl = jnp.zeros((seq, MLA_NOPE), F32)
    zr = jnp.zeros((seq, MLA_PAD - MLA_NOPE - MLA_ROPE), F32)
    return (jnp.concatenate([ones, cos, cos, zr], axis=1), jnp.concatenate([zl, -sin, sin, zr], axis=1))


def kernel(x, norm_g, ev_w_in, ssm_lambda_re, ssm_lambda_im, ssm_log_dt, ssm_b_re, ssm_b_im, ssm_c_re, ssm_c_im, ssm_d, ssm_w_glu, sgu_ln_g, sgu_ln_b, sgu_w, sgu_b, ev_w_out, ffn_w_gate, ffn_w_up, ffn_w_down, od_w_in, conv_w, conv_b, conv_ln_g, conv_ln_b, mla_q_norm_g, mla_w_uq, mla_kv_norm_g, mla_w_ukv, od_w_out, moe_w_router, moe_w_gate, moe_w_up, moe_w_down):
    bsz, seq, d = x.shape
    n = bsz * seq
    assert d == D_MODEL and SUBLANES % bsz == 0 and seq % 512 == 0
    row = lambda v: v.astype(F32).reshape(1, -1)
    h = x.astype(F32).reshape(n, d)
    tm = 512

    g = norm_g[0]
    a_in, proj = _norm_proj(h, row(g[0]), ev_w_in[0].astype(BF16), tm)
    mats = _s5_matrices(ssm_lambda_re[0], ssm_lambda_im[0], ssm_log_dt[0], ssm_b_re[0], ssm_b_im[0],
                        ssm_c_re[0], ssm_c_im[0])
    ys = _s5_mixer(a_in, mats, ssm_d[0], bsz, seq)
    causal = jnp.tril(jnp.ones((SGU_CHUNK, SGU_CHUNK), dtype=bool))
    ws = jnp.where(causal[None], sgu_w[0], 0.0).astype(BF16)
    bias = jnp.repeat(sgu_b[0].astype(F32).T, SGU_HEAD_DIM, axis=1)
    wo = ev_w_out[0].astype(BF16)
    h, z = _even_mix(ys, proj, h, ssm_w_glu[0].astype(BF16), row(sgu_ln_g[0]), row(sgu_ln_b[0]), ws, bias,
                     wo[:SSM_WIDTH], wo[SSM_WIDTH:], row(g[1]), row(g[2]), tm)
    h, z = _dense_ffn(z, ffn_w_gate[0].astype(BF16), ffn_w_up[0].astype(BF16), ffn_w_down[0].astype(BF16),
                      h, row(g[3]), row(norm_g[1][0]), 1024, 1024)

    g = norm_g[1]
    win, wuq, wuqs, wuk, wuv_t, v_one = _odd_weights(od_w_in[0], mla_w_uq[0], mla_w_ukv[0])
    cos_t, sin_t = _rope_tables(seq)
    zc, q, k, vt = _odd_proj(z, win, row(mla_q_norm_g[0]), row(mla_kv_norm_g[0]), wuq, wuqs, wuk, wuv_t, v_one,
                             cos_t, sin_t, seq, tm)
    hp = MLA_HEADS * MLA_PAD
    yd = _attention(q.reshape(bsz, seq, hp), k.reshape(bsz, seq, hp), vt, tm)
    conv_w_pad = jnp.concatenate([conv_w[0].astype(F32), jnp.zeros((CONV_HALO - CONV_TAPS, CONV_CH), F32)], axis=0)
    yc = _conv_mixer(zc.reshape(bsz, seq, 2 * CONV_CH), conv_w_pad, row(conv_b[0]), row(conv_ln_g[0]),
                     row(conv_ln_b[0]), tm)
    wo = od_w_out[0].astype(BF16)
    wr = jnp.concatenate([moe_w_router[0].astype(F32), jnp.zeros((d, LANES - N_EXPERTS), F32)], axis=1)
    h, z, route = _odd_mix(yc.reshape(n, CONV_CH), yd.reshape(n, MLA_HEADS * MLA_V), h, wo[:CONV_CH], wo[CONV_CH:],
                           row(g[1]), row(g[2]), wr.astype(BF16), tm)
    tm_moe, tf_moe = 1024, 512
    copy_steps = FF_EXPERT // tf_moe - 1
    tok_tab, dst_tab, tile_expert, n_used = _moe_plan(route, tm_moe, copy_steps * -(-tm_moe // copy_steps))
    y = _moe_ffn(tile_expert, n_used, tok_tab, dst_tab, z, moe_w_gate[0], moe_w_up[0], moe_w_down[0],
                 n, tm_moe, tf_moe)
    h = _combine(y, route, h, row(g[3]), 256)
    return h.reshape(bsz, seq, d).astype(x.dtype)
```

```python
import functools
import math

import jax
import jax.numpy as jnp
from jax import lax
from jax.experimental import pallas as pl
from jax.experimental.pallas import tpu as pltpu

F32 = jnp.float32
BF16 = jnp.bfloat16

D_MODEL = 1024
NORM_EPS = 1e-6
SSM_WIDTH = 512
SSM_GROUP = 16
SSM_GROUPS = 32
SSM_STATE = 64
SSM_CHUNK = 16
SSM_PAIR = 2 * SSM_GROUP * SSM_CHUNK
SGU_WIDTH = 512
SGU_HEADS = 8
SGU_HEAD_DIM = 64
SGU_CHUNK = 128
CONV_CH = 512
CONV_TAPS = 31
CONV_HALO = 32
MLA_HEADS = 8
MLA_Q_RANK = 256
MLA_KV_RANK = 128
MLA_NOPE = 64
MLA_ROPE = 32
MLA_V = 64
MLA_PAD = 128
MLA_VROWS = 80
ATTN_HEADS = 4
ROPE_THETA = 10000.0
FF_DENSE = 4096
N_EXPERTS = 8
FF_EXPERT = 3584
LANES = 128
SUBLANES = 8
ROW_TILE = D_MODEL // LANES
VMEM_LIMIT = 56 * 1024 * 1024


def _params(sem, vmem=VMEM_LIMIT):
    return pltpu.CompilerParams(dimension_semantics=sem, vmem_limit_bytes=vmem)


def _rms(x, g):
    return x * lax.rsqrt(jnp.mean(x * x, axis=-1, keepdims=True) + NORM_EPS) * g


def _layer_norm(x, g, b):
    mu = jnp.mean(x, axis=-1, keepdims=True)
    xc = x - mu
    return xc * lax.rsqrt(jnp.mean(xc * xc, axis=-1, keepdims=True) + NORM_EPS) * g + b


def _dot(a, b):
    return jnp.dot(a, b, preferred_element_type=F32)


def _norm_proj_kernel(h_ref, g_ref, w_ref, a_ref, b_ref):
    z = _rms(h_ref[...], g_ref[...])
    proj = _dot(z.astype(BF16), w_ref[...])
    for jb in range(SSM_WIDTH // LANES):
        a_ref[jb] = proj[:, jb * LANES:(jb + 1) * LANES]
    b_ref[...] = proj[:, SSM_WIDTH:].astype(b_ref.dtype)


def _norm_proj(h, g, w, tm):
    n, d = h.shape
    cols = w.shape[1]
    return pl.pallas_call(
        _norm_proj_kernel,
        grid=(n // tm,),
        in_specs=[pl.BlockSpec((tm, d), lambda i: (i, 0)),
                  pl.BlockSpec((1, d), lambda i: (0, 0)),
                  pl.BlockSpec((d, cols), lambda i: (0, 0))],
        out_specs=[pl.BlockSpec((SSM_WIDTH // LANES, tm, LANES), lambda i: (0, i, 0)),
                   pl.BlockSpec((tm, cols - SSM_WIDTH), lambda i: (i, 0))],
        out_shape=[jax.ShapeDtypeStruct((SSM_WIDTH // LANES, n, LANES), F32),
                   jax.ShapeDtypeStruct((n, cols - SSM_WIDTH), BF16)],
        compiler_params=_params(("parallel",)),
        name="even_in_proj",
    )(h, g, w)


def _s5_matrices(lam_re, lam_im, log_dt, b_re, b_im, c_re, c_im):
    t = SSM_CHUNK
    lr = jnp.minimum(lam_re.astype(F32), -1e-4)
    li = lam_im.astype(F32)
    dt = jnp.exp(log_dt.astype(F32))[:, None]
    mag = jnp.exp(lr * dt)
    a_re = mag * jnp.cos(li * dt)
    a_im = mag * jnp.sin(li * dt)
    den = lr * lr + li * li
    nr = a_re - 1.0
    coef_re = (nr * lr + a_im * li) / den
    coef_im = (a_im * lr - nr * li) / den
    br = b_re.astype(F32)
    bi = b_im.astype(F32)
    bb_re = coef_re[..., None] * br - coef_im[..., None] * bi
    bb_im = coef_re[..., None] * bi + coef_im[..., None] * br
    cr = c_re.astype(F32)
    ci = c_im.astype(F32)
    pw_re = [jnp.ones_like(a_re)]
    pw_im = [jnp.zeros_like(a_im)]
    for _ in range(t):
        pr, pi = pw_re[-1], pw_im[-1]
        pw_re.append(pr * a_re - pi * a_im)
        pw_im.append(pr * a_im + pi * a_re)
    pw_re = jnp.stack(pw_re)
    pw_im = jnp.stack(pw_im)
    ab_re = pw_re[:t, :, :, None] * bb_re[None] - pw_im[:t, :, :, None] * bb_im[None]
    ab_im = pw_re[:t, :, :, None] * bb_im[None] + pw_im[:t, :, :, None] * bb_re[None]
    hi = lax.Precision.HIGHEST
    k_lag = (jnp.einsum('gnp,tgpm->tgnm', cr, ab_re, precision=hi)
             - jnp.einsum('gnp,tgpm->tgnm', ci, ab_im, precision=hi))
    n_pairs = SSM_GROUPS // 2
    st = 2 * SSM_STATE

    def pair_diag(w):
        w = w.reshape((n_pairs, 2) + w.shape[1:])
        z = jnp.zeros_like(w[:, 0])
        top = jnp.concatenate([w[:, 0], z], axis=-1)
        bot = jnp.concatenate([z, w[:, 1]], axis=-1)
        return jnp.concatenate([top, bot], axis=-2)

    k_blk = pair_diag(k_lag.transpose(1, 0, 3, 2))
    rev_re = pw_re[:t][::-1]
    rev_im = pw_im[:t][::-1]
    ws_re = rev_re[..., None] * bb_re[None] - rev_im[..., None] * bb_im[None]
    ws_im = rev_re[..., None] * bb_im[None] + rev_im[..., None] * bb_re[None]
    ws_re = pair_diag(ws_re.transpose(1, 0, 3, 2)).reshape(n_pairs, SSM_PAIR, st).astype(BF16)
    ws_im = pair_diag(ws_im.transpose(1, 0, 3, 2)).reshape(n_pairs, SSM_PAIR, st).astype(BF16)
    ca_re = cr[None] * pw_re[1:, :, None, :] - ci[None] * pw_im[1:, :, None, :]
    ca_im = cr[None] * pw_im[1:, :, None, :] + ci[None] * pw_re[1:, :, None, :]
    co_re = pair_diag(ca_re.transpose(1, 0, 3, 2))
    co_im = pair_diag((-ca_im).transpose(1, 0, 3, 2))
    w_intra, wo_re, wo_im = _s5_expand(k_blk, co_re, co_im)
    ch_re, ch_im = pw_re[t].reshape(1, -1), pw_im[t].reshape(1, -1)
    tab_re, tab_im = [jnp.ones_like(ch_re)], [jnp.zeros_like(ch_im)]
    for _ in range(SUBLANES):
        pr, pi = tab_re[-1], tab_im[-1]
        tab_re.append(pr * ch_re - pi * ch_im)
        tab_im.append(pr * ch_im + pi * ch_re)
    pad = [jnp.zeros_like(ch_re)] * (2 * SUBLANES - len(tab_re))
    return dict(
        w_intra=w_intra, ws_re=ws_re, ws_im=ws_im, wo_re=wo_re, wo_im=wo_im,
        at_re=jnp.concatenate(tab_re + pad, axis=0), at_im=jnp.concatenate(tab_im + pad, axis=0))


def _s5_expand_kernel(k_ref, cre_ref, cim_ref, wi_ref, wore_ref, woim_ref, kcat_ref):
    pw = 2 * SSM_GROUP
    for tau in range(SSM_CHUNK):
        kcat_ref[:, tau * pw:(tau + 1) * pw] = k_ref[0, tau]
        wore_ref[0, :, tau * pw:(tau + 1) * pw] = cre_ref[0, tau].astype(wore_ref.dtype)
        woim_ref[0, :, tau * pw:(tau + 1) * pw] = cim_ref[0, tau].astype(woim_ref.dtype)
    kcat = kcat_ref[...]
    col = lax.broadcasted_iota(jnp.int32, kcat.shape, 1)
    for s in range(SSM_CHUNK):
        blk = kcat if s == 0 else jnp.where(col >= s * pw, pltpu.roll(kcat, s * pw, 1), 0.0)
        wi_ref[0, s * pw:(s + 1) * pw, :] = blk.astype(wi_ref.dtype)


def _s5_expand(k_blk, co_re, co_im):
    n_pairs = k_blk.shape[0]
    pw = 2 * SSM_GROUP
    st = 2 * SSM_STATE
    return pl.pallas_call(
        _s5_expand_kernel,
        grid=(n_pairs,),
        in_specs=[pl.BlockSpec((1, SSM_CHUNK, pw, pw), lambda q: (q, 0, 0, 0)),
                  pl.BlockSpec((1, SSM_CHUNK, st, pw), lambda q: (q, 0, 0, 0)),
                  pl.BlockSpec((1, SSM_CHUNK, st, pw), lambda q: (q, 0, 0, 0))],
        out_specs=[pl.BlockSpec((1, SSM_PAIR, SSM_PAIR), lambda q: (q, 0, 0)),
                   pl.BlockSpec((1, st, SSM_PAIR), lambda q: (q, 0, 0)),
                   pl.BlockSpec((1, st, SSM_PAIR), lambda q: (q, 0, 0))],
        out_shape=[jax.ShapeDtypeStruct((n_pairs, SSM_PAIR, SSM_PAIR), BF16),
                   jax.ShapeDtypeStruct((n_pairs, st, SSM_PAIR), BF16),
                   jax.ShapeDtypeStruct((n_pairs, st, SSM_PAIR), BF16)],
        scratch_shapes=[pltpu.VMEM((pw, SSM_PAIR), F32)],
        compiler_params=_params(("parallel",)),
        name="s5_expand_weights",
    )(k_blk, co_re, co_im)


S5_LANE_PAIRS = LANES // (2 * SSM_GROUP)
S5_SCAN_LANES = 512


def _s5_state_kernel(u0_ref, u1_ref, u2_ref, u3_ref, wre_ref, wim_ref, are_ref, aim_ref,
                     x_ref, hre_ref, him_ref, sre_ref, sim_ref):
    n_chunks = x_ref.shape[0]
    pw = 2 * SSM_GROUP
    u_refs = (u0_ref, u1_ref, u2_ref, u3_ref)
    for t in range(SSM_CHUNK):
        for j, u_ref in enumerate(u_refs):
            ut = u_ref[pl.ds(t, n_chunks, stride=SSM_CHUNK), :]
            for qq in range(S5_LANE_PAIRS):
                q = j * S5_LANE_PAIRS + qq
                x_ref[:, q * SSM_PAIR + t * pw: q * SSM_PAIR + (t + 1) * pw] = (
                    ut[:, qq * pw:(qq + 1) * pw].astype(x_ref.dtype))
    st = 2 * SSM_STATE
    for q in range(SSM_GROUPS // 2):
        xq = x_ref[:, q * SSM_PAIR:(q + 1) * SSM_PAIR]
        sre_ref[:, q * st:(q + 1) * st] = _dot(xq, wre_ref[q])
        sim_ref[:, q * st:(q + 1) * st] = _dot(xq, wim_ref[q])

    row = lax.broadcasted_iota(jnp.int32, (SUBLANES, S5_SCAN_LANES), 0)
    zero = jnp.zeros((SUBLANES, S5_SCAN_LANES), F32)

    def cmul(ar, ai, xr, xi):
        return ar * xr - ai * xi, ar * xi + ai * xr

    def shift(x, k):
        return jnp.where(row >= k, pltpu.roll(x, k, 0), 0.0)

    for c0 in range(0, sre_ref.shape[1], S5_SCAN_LANES):
        cols = pl.ds(c0, S5_SCAN_LANES)

        def body(k, carry, cols=cols):
            r0 = pl.multiple_of(k * SUBLANES, SUBLANES)
            ir = sre_ref[pl.ds(r0, SUBLANES), cols]
            ii = sim_ref[pl.ds(r0, SUBLANES), cols]
            for step in (1, 2, 4):
                tr, ti = cmul(are_ref[pl.ds(step, 1), cols], aim_ref[pl.ds(step, 1), cols],
                              shift(ir, step), shift(ii, step))
                ir, ii = ir + tr, ii + ti
            cr, ci = carry
            pr, pi = cmul(are_ref[pl.ds(0, SUBLANES), cols], aim_ref[pl.ds(0, SUBLANES), cols], cr, ci)
            hre_ref[pl.ds(r0, SUBLANES), cols] = pr + shift(ir, 1)
            him_ref[pl.ds(r0, SUBLANES), cols] = pi + shift(ii, 1)
            nr, ni = cmul(are_ref[pl.ds(SUBLANES, 1), cols], aim_ref[pl.ds(SUBLANES, 1), cols], cr, ci)
            last_r = jnp.broadcast_to(ir[SUBLANES - 1:SUBLANES], ir.shape)
            last_i = jnp.broadcast_to(ii[SUBLANES - 1:SUBLANES], ii.shape)
            return nr + last_r, ni + last_i

        lax.fori_loop(0, n_chunks // SUBLANES, body, (zero, zero))


def _s5_out_kernel(x_ref, wi_ref, hre_ref, him_ref, wore_ref, woim_ref, d_ref, y_ref, yt_ref):
    n_chunks = x_ref.shape[0]
    pw = 2 * SSM_GROUP
    st = 2 * SSM_STATE
    for qq in range(S5_LANE_PAIRS):
        x = x_ref[:, qq * SSM_PAIR:(qq + 1) * SSM_PAIR]
        y = _dot(x, wi_ref[qq])
        y += _dot(hre_ref[:, qq * st:(qq + 1) * st].astype(BF16), wore_ref[qq])
        y += _dot(him_ref[:, qq * st:(qq + 1) * st].astype(BF16), woim_ref[qq])
        y += d_ref[:, qq * SSM_PAIR:(qq + 1) * SSM_PAIR] * x.astype(F32)
        y = jax.nn.gelu(y)
        for t in range(SSM_CHUNK):
            yt_ref[t, :, qq * pw:(qq + 1) * pw] = y[:, t * pw:(t + 1) * pw]
    for t in range(SSM_CHUNK):
        y_ref[pl.ds(t, n_chunks, stride=SSM_CHUNK), :] = yt_ref[t]


def _s5_mixer(u, mats, d, batch, seq):
    t = SSM_CHUNK
    n_chunks = seq // t
    n_pairs = SSM_GROUPS // 2
    cols = n_pairs * SSM_PAIR
    st = 2 * SSM_STATE
    n_state = n_pairs * st
    n_blk = SSM_WIDTH // LANES
    assert n_blk == 4 and n_chunks % SUBLANES == 0
    once = pl.Buffered(1)
    x, h_re, h_im = pl.pallas_call(
        _s5_state_kernel,
        grid=(batch,),
        in_specs=[pl.BlockSpec((None, seq, LANES), lambda b, j=j: (j, b, 0)) for j in range(n_blk)] + [
            pl.BlockSpec((n_pairs, SSM_PAIR, st), lambda b: (0, 0, 0), pipeline_mode=once),
            pl.BlockSpec((n_pairs, SSM_PAIR, st), lambda b: (0, 0, 0), pipeline_mode=once),
            pl.BlockSpec((2 * SUBLANES, n_state), lambda b: (0, 0)),
            pl.BlockSpec((2 * SUBLANES, n_state), lambda b: (0, 0))],
        out_specs=[pl.BlockSpec((n_chunks, cols), lambda b: (b, 0)),
                   pl.BlockSpec((n_chunks, n_state), lambda b: (b, 0)),
                   pl.BlockSpec((n_chunks, n_state), lambda b: (b, 0))],
        out_shape=[jax.ShapeDtypeStruct((batch * n_chunks, cols), BF16),
                   jax.ShapeDtypeStruct((batch * n_chunks, n_state), F32),
                   jax.ShapeDtypeStruct((batch * n_chunks, n_state), F32)],
        scratch_shapes=[pltpu.VMEM((n_chunks, n_state), F32), pltpu.VMEM((n_chunks, n_state), F32)],
        compiler_params=_params(("parallel",)),
        name="s5_state_scan",
    )(u, u, u, u, mats['ws_re'], mats['ws_im'], mats['at_re'], mats['at_im'])
    lp = S5_LANE_PAIRS
    d_cols = jnp.broadcast_to(d.astype(F32).reshape(n_pairs, 1, 2 * SSM_GROUP),
                              (n_pairs, t, 2 * SSM_GROUP)).reshape(1, cols)
    return pl.pallas_call(
        _s5_out_kernel,
        grid=(batch, n_blk),
        in_specs=[pl.BlockSpec((n_chunks, lp * SSM_PAIR), lambda b, j: (b, j)),
                  pl.BlockSpec((lp, SSM_PAIR, SSM_PAIR), lambda b, j: (j, 0, 0)),
                  pl.BlockSpec((n_chunks, lp * st), lambda b, j: (b, j)),
                  pl.BlockSpec((n_chunks, lp * st), lambda b, j: (b, j)),
                  pl.BlockSpec((lp, st, SSM_PAIR), lambda b, j: (j, 0, 0)),
                  pl.BlockSpec((lp, st, SSM_PAIR), lambda b, j: (j, 0, 0)),
                  pl.BlockSpec((1, lp * SSM_PAIR), lambda b, j: (0, j))],
        out_specs=pl.BlockSpec((None, seq, LANES), lambda b, j: (j, b, 0)),
        out_shape=jax.ShapeDtypeStruct((n_blk, batch * seq, LANES), F32),
        scratch_shapes=[pltpu.VMEM((t, n_chunks, LANES), F32)],
        compiler_params=_params(("parallel", "parallel")),
        name="s5_out",
    )(x, mats['w_intra'], h_re, h_im, mats['wo_re'], mats['wo_im'], d_cols)


def _even_mix_kernel(ys_ref, bu_ref, bv_ref, h_ref, wglu_ref, lng_ref, lnb_ref, ws_ref, bias_ref,
                     wo_a_ref, wo_b_ref, g1_ref, g2_ref, hout_ref, z_ref, s_scr):
    tm = h_ref.shape[0]
    ys = jnp.concatenate([ys_ref[jb] for jb in range(ys_ref.shape[0])], axis=1)
    ya = ys * jax.nn.sigmoid(_dot(ys.astype(BF16), wglu_ref[...]))
    u = jax.nn.gelu(bu_ref[...].astype(F32))
    v = _layer_norm(jax.nn.gelu(bv_ref[...].astype(F32)), lng_ref[...], lnb_ref[...])
    lane = lax.broadcasted_iota(jnp.int32, v.shape, 1)
    left = (lane % LANES) < SGU_HEAD_DIM
    v_l = jnp.where(left, v, 0.0).astype(BF16)
    v_r = jnp.where(left, 0.0, v).astype(BF16)
    for c in range(tm // SGU_CHUNK):
        rows = slice(c * SGU_CHUNK, (c + 1) * SGU_CHUNK)
        for p in range(SGU_HEADS // 2):
            cols = slice(p * LANES, (p + 1) * LANES)
            s_scr[rows, cols] = (_dot(ws_ref[2 * p], v_l[rows, cols]) + _dot(ws_ref[2 * p + 1], v_r[rows, cols]))
    bias = jnp.concatenate([bias_ref[...]] * (tm // SGU_CHUNK), axis=0)
    yb = u * (s_scr[...] + bias)
    mix = _dot(ya.astype(BF16), wo_a_ref[...]) + _dot(yb.astype(BF16), wo_b_ref[...])
    h_new = h_ref[...] + _rms(mix, g1_ref[...])
    hout_ref[...] = h_new
    z_ref[...] = _rms(h_new, g2_ref[...]).astype(z_ref.dtype)


def _even_mix(ys, proj, h, wglu, lng, lnb, ws, bias, wo_a, wo_b, g1, g2, tm):
    n, d = h.shape
    w = SGU_WIDTH
    const = lambda *shape: pl.BlockSpec(shape, lambda i: (0,) * len(shape))
    return pl.pallas_call(
        _even_mix_kernel,
        grid=(n // tm,),
        in_specs=[pl.BlockSpec((w // LANES, tm, LANES), lambda i: (0, i, 0)),
                  pl.BlockSpec((tm, w), lambda i: (i, 0)),
                  pl.BlockSpec((tm, w), lambda i: (i, 1)),
                  pl.BlockSpec((tm, d), lambda i: (i, 0)),
                  const(w, w), const(1, w), const(1, w),
                  const(SGU_HEADS, SGU_CHUNK, SGU_CHUNK), const(SGU_CHUNK, w),
                  const(w, d), const(w, d), const(1, d), const(1, d)],
        out_specs=[pl.BlockSpec((tm, d), lambda i: (i, 0)),
                   pl.BlockSpec((tm, d), lambda i: (i, 0))],
        out_shape=[jax.ShapeDtypeStruct((n, d), F32), jax.ShapeDtypeStruct((n, d), BF16)],
        scratch_shapes=[pltpu.VMEM((tm, w), F32)],
        compiler_params=_params(("parallel",)),
        name="even_mix",
    )(ys, proj, proj, h, wglu, lng, lnb, ws, bias, wo_a, wo_b, g1, g2)


def _ffn_kernel(z_ref, wg_ref, wu_ref, wd_ref, h_ref, g3_ref, gn_ref, hout_ref, zout_ref, acc_ref):
    j = pl.program_id(1)

    @pl.when(j == 0)
    def _():
        acc_ref[...] = jnp.zeros_like(acc_ref)

    z = z_ref[...]
    a = jax.nn.silu(_dot(z, wg_ref[...])) * _dot(z, wu_ref[...])
    acc_ref[...] += _dot(a.astype(BF16), wd_ref[...])

    @pl.when(j == pl.num_programs(1) - 1)
    def _():
        h_new = h_ref[...] + _rms(acc_ref[...], g3_ref[...])
        hout_ref[...] = h_new
        zout_ref[...] = _rms(h_new, gn_ref[...]).astype(zout_ref.dtype)


def _dense_ffn(z, wg, wu, wd, h, g3, g_next, tm, tf):
    n, d = h.shape
    ff = wg.shape[1]
    return pl.pallas_call(
        _ffn_kernel,
        grid=(n // tm, ff // tf),
        in_specs=[pl.BlockSpec((tm, d), lambda i, j: (i, 0)),
                  pl.BlockSpec((d, tf), lambda i, j: (0, j)),
                  pl.BlockSpec((d, tf), lambda i, j: (0, j)),
                  pl.BlockSpec((tf, d), lambda i, j: (j, 0)),
                  pl.BlockSpec((tm, d), lambda i, j: (i, 0)),
                  pl.BlockSpec((1, d), lambda i, j: (0, 0)),
                  pl.BlockSpec((1, d), lambda i, j: (0, 0))],
        out_specs=[pl.BlockSpec((tm, d), lambda i, j: (i, 0)),
                   pl.BlockSpec((tm, d), lambda i, j: (i, 0))],
        out_shape=[jax.ShapeDtypeStruct((n, d), F32), jax.ShapeDtypeStruct((n, d), BF16)],
        scratch_shapes=[pltpu.VMEM((tm, d), F32)],
        compiler_params=_params(("parallel", "arbitrary")),
        name="dense_ffn",
    )(z, wg, wu, wd, h, g3, g_next)


def _odd_proj_kernel(z_ref, win_ref, gq_ref, gkv_ref, wuq_ref, wuqs_ref, wuk_ref, wuv_ref, vone_ref, cos_ref, sin_ref,
                     zc_ref, q_ref, k_ref, v_ref, *, scale):
    z = z_ref[...]
    proj = _dot(z, win_ref[...])
    c0 = 2 * CONV_CH
    c1 = c0 + MLA_Q_RANK
    c2 = c1 + MLA_KV_RANK
    c3 = c2 + MLA_PAD
    zc_ref[...] = proj[:, :c0].astype(zc_ref.dtype)
    cq = _rms(proj[:, c0:c1], gq_ref[...]).astype(BF16)
    ckv = _rms(proj[:, c1:c2], gkv_ref[...]).astype(BF16)
    cos = cos_ref[...]
    sin = sin_ref[...]
    cos_h = jnp.concatenate([cos] * MLA_HEADS, axis=1)
    sin_h = jnp.concatenate([sin] * MLA_HEADS, axis=1)
    q = _dot(cq, wuq_ref[...]) * cos_h + _dot(cq, wuqs_ref[...]) * sin_h
    q_ref[...] = (q * scale).astype(q_ref.dtype)
    kr = proj[:, c2:c3] * cos + proj[:, c3:] * sin
    k = _dot(ckv, wuk_ref[...]) + jnp.concatenate([kr] * MLA_HEADS, axis=1)
    k_ref[...] = k.astype(k_ref.dtype)
    vt = lax.dot_general(wuv_ref[...], ckv, (((1,), (1,)), ((), ())), preferred_element_type=F32)
    v_ref[0] = (vt + vone_ref[...]).astype(v_ref.dtype)


def _odd_proj(z, win, gq, gkv, wuq, wuqs, wuk, wuv_t, v_one, cos_t, sin_t, seq, tm):
    n, d = z.shape
    hp = MLA_HEADS * MLA_PAD
    vr = MLA_HEADS * MLA_VROWS
    n_l = seq // tm
    const = lambda *shape: pl.BlockSpec(shape, lambda i: (0,) * len(shape))
    out = jax.ShapeDtypeStruct((n, hp), BF16)
    scale = float((MLA_NOPE + MLA_ROPE) ** -0.5 * math.log2(math.e))
    return pl.pallas_call(
        functools.partial(_odd_proj_kernel, scale=scale),
        grid=(n // tm,),
        in_specs=[pl.BlockSpec((tm, d), lambda i: (i, 0)),
                  const(d, win.shape[1]), const(1, MLA_Q_RANK), const(1, MLA_KV_RANK),
                  const(MLA_Q_RANK, hp), const(MLA_Q_RANK, hp), const(MLA_KV_RANK, hp), const(vr, MLA_KV_RANK),
                  const(vr, 1),
                  pl.BlockSpec((tm, MLA_PAD), lambda i: (i % n_l, 0)),
                  pl.BlockSpec((tm, MLA_PAD), lambda i: (i % n_l, 0))],
        out_specs=[pl.BlockSpec((tm, 2 * CONV_CH), lambda i: (i, 0)),
                   pl.BlockSpec((tm, hp), lambda i: (i, 0)),
                   pl.BlockSpec((tm, hp), lambda i: (i, 0)),
                   pl.BlockSpec((1, vr, tm), lambda i: (i, 0, 0))],
        out_shape=[jax.ShapeDtypeStruct((n, 2 * CONV_CH), BF16), out, out,
                   jax.ShapeDtypeStruct((n // tm, vr, tm), BF16)],
        compiler_params=_params(("parallel",)),
        name="odd_in_proj",
    )(z, win, gq, gkv, wuq, wuqs, wuk, wuv_t, v_one, cos_t, sin_t)


def _attn_kernel(q_ref, k_ref, vt_ref, wg_ref, wu_ref, wd_ref, o_ref, wgb_ref, wub_ref, wdb_ref, acc_ref, *, blk):
    i = pl.program_id(2)
    acc_ref[...] = jnp.zeros_like(acc_ref)
    tf = wgb_ref.shape[3]
    for f in range(wgb_ref.shape[1]):
        wgb_ref[0, f] = wg_ref[0, :, f * tf:(f + 1) * tf].astype(BF16)
        wub_ref[0, f] = wu_ref[0, :, f * tf:(f + 1) * tf].astype(BF16)
    wdb_ref[0] = wd_ref[0].astype(BF16)

    def step(j, m, masked):
        r0 = pl.multiple_of(j * blk, blk)
        scores = []
        for hh in range(ATTN_HEADS):
            q = q_ref[0, :, hh * MLA_PAD:(hh + 1) * MLA_PAD]
            k = k_ref[0, pl.ds(r0, blk), hh * MLA_PAD:(hh + 1) * MLA_PAD]
            st = lax.dot_general(k, q, (((1,), (1,)), ((), ())), preferred_element_type=F32)
            if masked:
                key = lax.broadcasted_iota(jnp.int32, st.shape, 0)
                qry = lax.broadcasted_iota(jnp.int32, st.shape, 1)
                st = jnp.where(key <= qry, st, -1e30)
            scores.append(st)
        soft = []
        for hh in range(ATTN_HEADS):
            m_new = jnp.maximum(m[hh], jnp.max(scores[hh], axis=0, keepdims=True))
            soft.append((m_new, jnp.exp2(m[hh] - m_new), jnp.exp2(scores[hh] - m_new).astype(BF16)))
        for hh in range(ATTN_HEADS):
            vt = vt_ref[j, hh * MLA_VROWS:(hh + 1) * MLA_VROWS, :]
            acc_ref[hh] = soft[hh][1] * acc_ref[hh] + _dot(vt, soft[hh][2])
        return tuple(s[0] for s in soft)

    init = jnp.full((1, blk), -1e30, F32)
    m = lax.fori_loop(0, i, lambda j, m: step(j, m, False), (init,) * ATTN_HEADS)
    step(i, m, True)
    ot = jnp.concatenate([acc_ref[hh][:MLA_V] / acc_ref[hh][MLA_V:MLA_V + 1] for hh in range(ATTN_HEADS)], axis=0)
    o_ref[0] = ot.T.astype(o_ref.dtype)


def _attention(q, k, vt, blk, wg, wu, wd, tf):
    b, seq, _ = q.shape
    n_blk = seq // blk
    n_pairs = MLA_HEADS // ATTN_HEADS
    n_e, d, ff = wg.shape
    steps = b * n_pairs * n_blk
    per_e = steps // n_e
    assert steps == per_e * n_e and d % per_e == 0 and ff % per_e == 0
    rows_in, rows_down = d // per_e, ff // per_e
    assert rows_in % 16 == 0 and rows_down % 16 == 0 and ff % tf == 0

    def lin(bi, p, i):
        return (bi * n_pairs + p) * n_blk + i

    w_in = pl.BlockSpec((1, rows_in, ff), lambda bi, p, i: (lin(bi, p, i) // per_e, lin(bi, p, i) % per_e, 0))
    w_out = pl.BlockSpec((1, ff // tf, rows_in, tf),
                         lambda bi, p, i: (lin(bi, p, i) // per_e, 0, lin(bi, p, i) % per_e, 0))
    w_down = pl.BlockSpec((1, rows_down, d), lambda bi, p, i: (lin(bi, p, i) // per_e, lin(bi, p, i) % per_e, 0))
    return pl.pallas_call(
        functools.partial(_attn_kernel, blk=blk),
        grid=(b, n_pairs, n_blk),
        in_specs=[pl.BlockSpec((1, blk, ATTN_HEADS * MLA_PAD), lambda bi, p, i: (bi, i, p)),
                  pl.BlockSpec((1, seq, ATTN_HEADS * MLA_PAD), lambda bi, p, i: (bi, 0, p)),
                  pl.BlockSpec((n_blk, ATTN_HEADS * MLA_VROWS, blk), lambda bi, p, i: (bi, p, 0)),
                  w_in, w_in, w_down],
        out_specs=[pl.BlockSpec((1, blk, ATTN_HEADS * MLA_V), lambda bi, p, i: (bi, i, p)), w_out, w_out, w_down],
        out_shape=[jax.ShapeDtypeStruct((b, seq, MLA_HEADS * MLA_V), BF16),
                   jax.ShapeDtypeStruct((n_e, ff // tf, d, tf), BF16),
                   jax.ShapeDtypeStruct((n_e, ff // tf, d, tf), BF16),
                   jax.ShapeDtypeStruct((n_e, ff, d), BF16)],
        scratch_shapes=[pltpu.VMEM((ATTN_HEADS, MLA_VROWS, blk), F32)],
        compiler_params=_params(("parallel", "parallel", "parallel")),
        name="mla_attention",
    )(q, k, vt, wg, wu, wd)


def _conv_kernel(zc_ref, w_ref, b_ref, lng_ref, lnb_ref, y_ref, buf_ref, part_ref):
    tm = zc_ref.shape[1]

    @pl.when(pl.program_id(1) == 0)
    def _():
        buf_ref[pl.ds(0, CONV_HALO), :] = jnp.zeros((CONV_HALO, CONV_CH), F32)
        buf_ref[pl.ds(CONV_HALO + tm, SUBLANES), :] = jnp.zeros((SUBLANES, CONV_CH), F32)

    zc = zc_ref[0].astype(F32)
    hh = zc[:, :CONV_CH] * jax.nn.sigmoid(zc[:, CONV_CH:])
    buf_ref[pl.ds(CONV_HALO, tm), :] = hh
    off = CONV_HALO - (CONV_TAPS - 1)
    acc = jnp.zeros((tm, CONV_CH), F32) + b_ref[...]
    for b in range(SUBLANES):
        taps = [k for k in range(CONV_TAPS) if (off + k) % SUBLANES == b]
        part = None
        for k in taps:
            term = w_ref[pl.ds(k, 1), :] * buf_ref[pl.ds(off + k - b, tm + SUBLANES), :]
            part = term if part is None else part + term
        if b == 0:
            acc = acc + part[:tm]
        else:
            part_ref[...] = part
            acc = acc + part_ref[pl.ds(b, tm), :]
    buf_ref[pl.ds(0, CONV_HALO), :] = buf_ref[pl.ds(tm, CONV_HALO), :]
    y_ref[0] = jax.nn.silu(_layer_norm(acc, lng_ref[...], lnb_ref[...])).astype(y_ref.dtype)


def _conv_mixer(zc, w, b, lng, lnb, tm):
    bsz, seq, _ = zc.shape
    const = lambda *shape: pl.BlockSpec(shape, lambda bi, i: (0,) * len(shape))
    return pl.pallas_call(
        _conv_kernel,
        grid=(bsz, seq // tm),
        in_specs=[pl.BlockSpec((1, tm, 2 * CONV_CH), lambda bi, i: (bi, i, 0)),
                  const(CONV_HALO, CONV_CH), const(1, CONV_CH), const(1, CONV_CH), const(1, CONV_CH)],
        out_specs=pl.BlockSpec((1, tm, CONV_CH), lambda bi, i: (bi, i, 0)),
        out_shape=jax.ShapeDtypeStruct((bsz, seq, CONV_CH), BF16),
        scratch_shapes=[pltpu.VMEM((CONV_HALO + tm + SUBLANES, CONV_CH), F32),
                        pltpu.VMEM((tm + SUBLANES, CONV_CH), F32)],
        compiler_params=_params(("arbitrary", "arbitrary")),
        name="conv_module",
    )(zc, w, b, lng, lnb)


def _odd_mix_kernel(yc_ref, yd_ref, h_ref, wo_a_ref, wo_b_ref, g1_ref, g2_ref, wr_ref, hout_ref, z_ref, route_ref):
    tm = h_ref.shape[0]
    halves = [slice(0, tm // 2), slice(tm // 2, tm)]
    mixes = [_dot(yc_ref[r, :], wo_a_ref[...]) + _dot(yd_ref[r, :], wo_b_ref[...]) for r in halves]
    zs = []
    for r, mix in zip(halves, mixes):
        h_new = h_ref[r, :] + _rms(mix, g1_ref[...])
        hout_ref[r, :] = h_new
        zs.append(_rms(h_new, g2_ref[...]))
    all_logits = [_dot(z.astype(BF16), wr_ref[...]) for z in zs]
    _store_row_tiles(z_ref, jnp.concatenate(zs, axis=0))
    neg = -jnp.inf
    for r, logits in zip(halves, all_logits):
        lane = lax.broadcasted_iota(jnp.int32, logits.shape, 1)
        logits = jnp.where(lane < N_EXPERTS, logits, neg)
        m1 = jnp.max(logits, axis=-1, keepdims=True)
        i1 = jnp.min(jnp.where(logits == m1, lane, LANES), axis=-1, keepdims=True)
        rest = jnp.where(lane == i1, neg, logits)
        m2 = jnp.max(rest, axis=-1, keepdims=True)
        i2 = jnp.min(jnp.where(rest == m2, lane, LANES), axis=-1, keepdims=True)
        e = jnp.exp(m2 - m1)
        w1 = 1.0 / (1.0 + e)
        w2 = e / (1.0 + e)
        route_ref[r, :] = jnp.where(lane == 0, i1.astype(F32),
                                    jnp.where(lane == 1, i2.astype(F32),
                                              jnp.where(lane == 2, w1, jnp.where(lane == 3, w2, 0.0))))


def _odd_mix(yc, yd, h, wo_a, wo_b, g1, g2, wr, tm):
    n, d = h.shape
    const = lambda *shape: pl.BlockSpec(shape, lambda i: (0,) * len(shape))
    return pl.pallas_call(
        _odd_mix_kernel,
        grid=(n // tm,),
        in_specs=[pl.BlockSpec((tm, yc.shape[1]), lambda i: (i, 0)),
                  pl.BlockSpec((tm, yd.shape[1]), lambda i: (i, 0)),
                  pl.BlockSpec((tm, d), lambda i: (i, 0)),
                  const(*wo_a.shape), const(*wo_b.shape), const(1, d), const(1, d), const(d, LANES)],
        out_specs=[pl.BlockSpec((tm, d), lambda i: (i, 0)),
                   pl.BlockSpec((tm * ROW_TILE, LANES), lambda i: (i, 0)),
                   pl.BlockSpec((tm, LANES), lambda i: (i, 0))],
        out_shape=[jax.ShapeDtypeStruct((n, d), F32), jax.ShapeDtypeStruct((n * ROW_TILE, LANES), F32),
                   jax.ShapeDtypeStruct((n, LANES), F32)],
        compiler_params=_params(("parallel",)),
        name="odd_mix_router",
    )(yc, yd, h, wo_a, wo_b, g1, g2, wr)


def _store_row_tiles(ref, x):
    rows = x.shape[0]
    for s in range(ROW_TILE):
        ref[pl.ds(s, rows, stride=ROW_TILE), :] = x[:, s * LANES:(s + 1) * LANES]


def _load_row_tiles(ref, rows):
    return [ref[pl.ds(s, rows, stride=ROW_TILE), :] for s in range(ROW_TILE)]


def _gather_rows(idx_ref, base, n_rows, src_hbm, dst_ref, sem):
    def body(r, c):
        src = pl.multiple_of(idx_ref[base + r] * ROW_TILE, ROW_TILE)
        dst = pl.multiple_of(r * ROW_TILE, ROW_TILE)
        pltpu.make_async_copy(src_hbm.at[pl.ds(src, ROW_TILE), :], dst_ref.at[pl.ds(dst, ROW_TILE), :], sem).start()
        return c

    lax.fori_loop(0, n_rows, body, 0, unroll=8)


def _wait_rows(src_hbm, dst_ref, sem):
    pltpu.make_async_copy(src_hbm.at[pl.ds(0, dst_ref.shape[0]), :], dst_ref, sem).wait()


def _row_copy(src_ref, src_row, dst_ref, dst_row, sem):
    src = pl.multiple_of(src_row * ROW_TILE, ROW_TILE)
    dst = pl.multiple_of(dst_row * ROW_TILE, ROW_TILE)
    return pltpu.make_async_copy(src_ref.at[pl.ds(src, ROW_TILE), :], dst_ref.at[pl.ds(dst, ROW_TILE), :], sem)


def _moe_ffn_kernel(te_ref, nu_ref, tok_ref, dst_ref, z_hbm, wg_ref, wu_ref, wd_ref, y_hbm,
                    xraw_ref, xb_ref, acc_ref, yst_ref, gsem, ssem, *, rows_per_step):
    i = pl.program_id(0)
    j = pl.program_id(1)
    tm = xb_ref.shape[0]
    stride = yst_ref.shape[0] // ROW_TILE
    n_used = nu_ref[0]
    slot = i % 2
    first = j == 0
    last = j == pl.num_programs(1) - 1

    @pl.when(first & (i == 0))
    def _():
        yst_ref[...] = jnp.zeros_like(yst_ref)
        _gather_rows(tok_ref, 0, stride, z_hbm, xraw_ref.at[0], gsem.at[0])

    @pl.when(first & (i <= n_used))
    def _():
        _wait_rows(z_hbm, xraw_ref.at[slot], gsem.at[slot])

    @pl.when(first & (i < n_used))
    def _():
        for s, blk in enumerate(_load_row_tiles(xraw_ref.at[slot], tm)):
            xb_ref[:, s * LANES:(s + 1) * LANES] = blk.astype(BF16)
        acc_ref[...] = jnp.zeros_like(acc_ref)

    @pl.when(first & (i == n_used))
    def _():
        def body(r, c):
            _row_copy(yst_ref, r, y_hbm, dst_ref[i * stride + r], ssem).start()
            return c
        lax.fori_loop(0, stride, body, 0, unroll=8)

    def multiply(rows, with_copies):
        x = xb_ref[:rows, :]
        g = _dot(x, wg_ref[0, 0])
        u = _dot(x, wu_ref[0, 0])
        if with_copies:
            nxt = xraw_ref.at[1 - slot]
            for rr in range(rows_per_step):
                r = j * rows_per_step + rr
                _row_copy(z_hbm, tok_ref[(i + 1) * stride + r], nxt, r, gsem.at[1 - slot]).start(priority=rr % 2)
                _row_copy(yst_ref, r, y_hbm, dst_ref[i * stride + r], ssem).start(priority=rr % 2)
        a = jax.nn.silu(g) * u
        acc_ref[:rows, :] += _dot(a.astype(BF16), wd_ref[0])

    used = i < n_used
    half = nu_ref[1 + i] <= tm // 2
    for rows, fits in ((tm, jnp.logical_not(half)), (tm // 2, half)):
        @pl.when(used & fits & jnp.logical_not(last))
        def _(rows=rows):
            multiply(rows, True)

        @pl.when(used & fits & last)
        def _(rows=rows):
            multiply(rows, False)

    @pl.when(last & (i <= n_used))
    def _():
        _wait_rows(z_hbm, yst_ref, ssem)

    @pl.when(last & (i < n_used))
    def _():
        _store_row_tiles(yst_ref, acc_ref[...])


def _moe_ffn(tile_expert, n_used, tok_tab, dst_tab, z_tiles, wg, wu, wd, n_tok, tm):
    n_f, d, tf = wg.shape[1], wg.shape[2], wg.shape[3]
    copy_steps = n_f - 1
    rows_per_step = -(-tm // copy_steps)
    stride = copy_steps * rows_per_step
    n_tiles = tile_expert.shape[0]
    assert tok_tab.shape[0] == dst_tab.shape[0] == (n_tiles + 1) * stride

    def col(i, j, nu):
        return jnp.where(i < nu[0], j, n_f - 1)

    return pl.pallas_call(
        functools.partial(_moe_ffn_kernel, rows_per_step=rows_per_step),
        grid_spec=pltpu.PrefetchScalarGridSpec(
            num_scalar_prefetch=4,
            grid=(n_tiles, n_f),
            in_specs=[pl.BlockSpec(memory_space=pl.ANY),
                      pl.BlockSpec((1, 1, d, tf), lambda i, j, te, nu, tok, dst: (te[i], col(i, j, nu), 0, 0)),
                      pl.BlockSpec((1, 1, d, tf), lambda i, j, te, nu, tok, dst: (te[i], col(i, j, nu), 0, 0)),
                      pl.BlockSpec((1, tf, d), lambda i, j, te, nu, tok, dst: (te[i], col(i, j, nu), 0))],
            out_specs=pl.BlockSpec(memory_space=pl.ANY),
            scratch_shapes=[pltpu.VMEM((2, stride * ROW_TILE, LANES), F32), pltpu.VMEM((tm, d), BF16),
                            pltpu.VMEM((tm, d), F32), pltpu.VMEM((stride * ROW_TILE, LANES), F32),
                            pltpu.SemaphoreType.DMA((2,)), pltpu.SemaphoreType.DMA(())]),
        out_shape=jax.ShapeDtypeStruct(((2 * n_tok + stride) * ROW_TILE, LANES), F32),
        compiler_params=_params(("arbitrary", "arbitrary")),
        name="moe_grouped_ffn",
    )(tile_expert, n_used, tok_tab, dst_tab, z_tiles, wg, wu, wd)


def _combine_kernel(ya_ref, yb_ref, route_ref, h_ref, g_ref, o_ref):
    tm = h_ref.shape[0]
    route = route_ref[...]
    a = jnp.concatenate(_load_row_tiles(ya_ref, tm), axis=1)
    b = jnp.concatenate(_load_row_tiles(yb_ref, tm), axis=1)
    f = route[:, 2:3] * a + route[:, 3:4] * b
    o_ref[...] = h_ref[...] + _rms(f, g_ref[...])


def _combine(y, route, h, g, tm):
    n, d = h.shape
    n_blk = n // tm
    return pl.pallas_call(
        _combine_kernel,
        grid=(n_blk,),
        in_specs=[pl.BlockSpec((tm * ROW_TILE, LANES), lambda i: (i, 0)),
                  pl.BlockSpec((tm * ROW_TILE, LANES), lambda i: (n_blk + i, 0)),
                  pl.BlockSpec((tm, LANES), lambda i: (i, 0)),
                  pl.BlockSpec((tm, d), lambda i: (i, 0)),
                  pl.BlockSpec((1, d), lambda i: (0, 0))],
        out_specs=pl.BlockSpec((tm, d), lambda i: (i, 0)),
        out_shape=jax.ShapeDtypeStruct((n, d), F32),
        compiler_params=_params(("parallel",)),
        name="moe_combine",
    )(y, y, route, h, g)


def _moe_plan(route, tm, stride):
    n = route.shape[0]
    eids = jnp.concatenate([route[:, 0], route[:, 1]]).astype(jnp.int32)
    onehot = (eids[:, None] == jnp.arange(N_EXPERTS, dtype=jnp.int32)[None, :]).astype(jnp.int32)
    csum = jnp.cumsum(onehot, axis=0)
    rank = jnp.sum(csum * onehot, axis=1) - 1
    counts = csum[-1]
    padded = ((counts + tm - 1) // tm) * tm
    ends = jnp.cumsum(padded)
    starts = ends - padded
    slot = jnp.sum(onehot * starts[None, :], axis=1) + rank
    n_tiles = 2 * n // tm + N_EXPERTS + 1
    copy_of_slot = jnp.full((n_tiles * tm,), -1, jnp.int32).at[slot].set(jnp.arange(2 * n, dtype=jnp.int32))
    copy_tab = jnp.pad(copy_of_slot.reshape(n_tiles, tm), ((0, 1), (0, stride - tm)), constant_values=-1)
    tok_tab = jnp.where(copy_tab >= 0, copy_tab % n, 0)
    dump = 2 * n + jnp.arange(stride, dtype=jnp.int32)[None, :]
    dst_tab = jnp.where(copy_tab >= 0, copy_tab, dump)
    dst_tab = jnp.concatenate([jnp.broadcast_to(dump, (1, stride)), dst_tab[:-1]], axis=0)
    n_used = (ends[-1] // tm).astype(jnp.int32)
    tile_start = jnp.minimum(jnp.arange(n_tiles, dtype=jnp.int32), n_used - 1) * tm
    tile_expert = jnp.sum((tile_start[:, None] >= ends[None, :]).astype(jnp.int32), axis=1)
    onehot_e = (tile_expert[:, None] == jnp.arange(N_EXPERTS, dtype=jnp.int32)[None, :]).astype(jnp.int32)
    run_end = jnp.sum(onehot_e * (starts + counts)[None, :], axis=1)
    tile_rows = jnp.clip(run_end - jnp.arange(n_tiles, dtype=jnp.int32) * tm, 0, tm)
    return tok_tab.reshape(-1), dst_tab.reshape(-1), tile_expert, jnp.concatenate([n_used.reshape(1), tile_rows])


def _odd_weights(od_w_in, mla_w_uq, mla_w_ukv):
    c2 = 2 * CONV_CH + MLA_Q_RANK + MLA_KV_RANK
    half = MLA_ROPE // 2
    w_kr = od_w_in[:, c2:]
    w_kr_sw = jnp.concatenate([w_kr[:, half:], w_kr[:, :half]], axis=1)
    zl = jnp.zeros((D_MODEL, MLA_NOPE), F32)
    zr = jnp.zeros((D_MODEL, MLA_PAD - MLA_NOPE - MLA_ROPE), F32)
    win = jnp.concatenate([od_w_in[:, :c2], zl, w_kr, zr, zl, w_kr_sw, zr], axis=1)
    dk = MLA_NOPE + MLA_ROPE
    wq = mla_w_uq.reshape(MLA_Q_RANK, MLA_HEADS, dk)
    zq = jnp.zeros((MLA_Q_RANK, MLA_HEADS, MLA_PAD - dk), F32)
    wuq = jnp.concatenate([wq, zq], axis=2).reshape(MLA_Q_RANK, MLA_HEADS * MLA_PAD)
    wq_sw = jnp.concatenate([jnp.zeros_like(wq[:, :, :MLA_NOPE]), wq[:, :, MLA_NOPE + half:],
                             wq[:, :, MLA_NOPE:MLA_NOPE + half], zq], axis=2)
    wuqs = wq_sw.reshape(MLA_Q_RANK, MLA_HEADS * MLA_PAD)
    wkv = mla_w_ukv.reshape(MLA_KV_RANK, MLA_HEADS, MLA_NOPE + MLA_V)
    zk = jnp.zeros((MLA_KV_RANK, MLA_HEADS, MLA_PAD - MLA_NOPE), F32)
    wuk = jnp.concatenate([wkv[:, :, :MLA_NOPE], zk], axis=2).reshape(MLA_KV_RANK, MLA_HEADS * MLA_PAD)
    zv = jnp.zeros((MLA_KV_RANK, MLA_HEADS, MLA_VROWS - MLA_V), F32)
    wuv_t = jnp.concatenate([wkv[:, :, MLA_NOPE:], zv], axis=2).reshape(MLA_KV_RANK, MLA_HEADS * MLA_VROWS).T
    v_one = jnp.zeros((MLA_HEADS, MLA_VROWS), F32).at[:, MLA_V].set(1.0).reshape(MLA_HEADS * MLA_VROWS, 1)
    return win.astype(BF16), wuq.astype(BF16), wuqs.astype(BF16), wuk.astype(BF16), wuv_t.astype(BF16), v_one


def _rope_tables(seq):
    inv = 1.0 / (ROPE_THETA ** (jnp.arange(0, MLA_ROPE, 2, dtype=F32) / MLA_ROPE))
    ang = jnp.arange(seq, dtype=F32)[:, None] * inv[None, :]
    cos, sin = jnp.cos(ang), jnp.sin(ang)
    ones = jnp.ones((seq, MLA_NOPE), F32)
    zl = jnp.zeros((seq, MLA_NOPE), F32)
    zr = jnp.zeros((seq, MLA_PAD - MLA_NOPE - MLA_ROPE), F32)
    return (jnp.concatenate([ones, cos, cos, zr], axis=1), jnp.concatenate([zl, -sin, sin, zr], axis=1))


def kernel(x, norm_g, ev_w_in, ssm_lambda_re, ssm_lambda_im, ssm_log_dt, ssm_b_re, ssm_b_im, ssm_c_re, ssm_c_im, ssm_d, ssm_w_glu, sgu_ln_g, sgu_ln_b, sgu_w, sgu_b, ev_w_out, ffn_w_gate, ffn_w_up, ffn_w_down, od_w_in, conv_w, conv_b, conv_ln_g, conv_ln_b, mla_q_norm_g, mla_w_uq, mla_kv_norm_g, mla_w_ukv, od_w_out, moe_w_router, moe_w_gate, moe_w_up, moe_w_down):
    bsz, seq, d = x.shape
    n = bsz * seq
    assert d == D_MODEL and SUBLANES % bsz == 0 and seq % 512 == 0
    row = lambda v: v.astype(F32).reshape(1, -1)
    h = x.astype(F32).reshape(n, d)
    tm = 512

    g = norm_g[0]
    a_in, proj = _norm_proj(h, row(g[0]), ev_w_in[0].astype(BF16), 2 * tm)
    mats = _s5_matrices(ssm_lambda_re[0], ssm_lambda_im[0], ssm_log_dt[0], ssm_b_re[0], ssm_b_im[0],
                        ssm_c_re[0], ssm_c_im[0])
    ys = _s5_mixer(a_in, mats, ssm_d[0], bsz, seq)
    causal = jnp.tril(jnp.ones((SGU_CHUNK, SGU_CHUNK), dtype=bool))
    ws = jnp.where(causal[None], sgu_w[0], 0.0).astype(BF16)
    bias = jnp.repeat(sgu_b[0].astype(F32).T, SGU_HEAD_DIM, axis=1)
    wo = ev_w_out[0].astype(BF16)
    h, z = _even_mix(ys, proj, h, ssm_w_glu[0].astype(BF16), row(sgu_ln_g[0]), row(sgu_ln_b[0]), ws, bias,
                     wo[:SSM_WIDTH], wo[SSM_WIDTH:], row(g[1]), row(g[2]), tm)
    h, z = _dense_ffn(z, ffn_w_gate[0].astype(BF16), ffn_w_up[0].astype(BF16), ffn_w_down[0].astype(BF16),
                      h, row(g[3]), row(norm_g[1][0]), 1024, 1024)

    g = norm_g[1]
    win, wuq, wuqs, wuk, wuv_t, v_one = _odd_weights(od_w_in[0], mla_w_uq[0], mla_w_ukv[0])
    cos_t, sin_t = _rope_tables(seq)
    zc, q, k, vt = _odd_proj(z, win, row(mla_q_norm_g[0]), row(mla_kv_norm_g[0]), wuq, wuqs, wuk, wuv_t, v_one,
                             cos_t, sin_t, seq, tm)
    hp = MLA_HEADS * MLA_PAD
    tm_moe, tf_moe = 1024, 512
    yd, wg_b, wu_b, wd_b = _attention(q.reshape(bsz, seq, hp), k.reshape(bsz, seq, hp), vt, tm,
                                      moe_w_gate[0], moe_w_up[0], moe_w_down[0], tf_moe)
    conv_w_pad = jnp.concatenate([conv_w[0].astype(F32), jnp.zeros((CONV_HALO - CONV_TAPS, CONV_CH), F32)], axis=0)
    yc = _conv_mixer(zc.reshape(bsz, seq, 2 * CONV_CH), conv_w_pad, row(conv_b[0]), row(conv_ln_g[0]),
                     row(conv_ln_b[0]), tm)
    wo = od_w_out[0].astype(BF16)
    wr = jnp.concatenate([moe_w_router[0].astype(F32), jnp.zeros((d, LANES - N_EXPERTS), F32)], axis=1)
    h, z, route = _odd_mix(yc.reshape(n, CONV_CH), yd.reshape(n, MLA_HEADS * MLA_V), h, wo[:CONV_CH], wo[CONV_CH:],
                           row(g[1]), row(g[2]), wr.astype(BF16), tm)
    copy_steps = wg_b.shape[1] - 1
    tok_tab, dst_tab, tile_expert, n_used = _moe_plan(route, tm_moe, copy_steps * -(-tm_moe // copy_steps))
    y = _moe_ffn(tile_expert, n_used, tok_tab, dst_tab, z, wg_b, wu_b, wd_b, n, tm_moe)
    h = _combine(y, route, h, row(g[3]), 256)
    return h.reshape(bsz, seq, d).astype(x.dtype)
```

```python
import functools
import math

import jax
import jax.numpy as jnp
from jax import lax
from jax.experimental import pallas as pl
from jax.experimental.pallas import tpu as pltpu

F32 = jnp.float32
BF16 = jnp.bfloat16

D_MODEL = 1024
NORM_EPS = 1e-6
SSM_WIDTH = 512
SSM_GROUP = 16
SSM_GROUPS = 32
SSM_STATE = 64
SSM_CHUNK = 16
SSM_PAIR = 2 * SSM_GROUP * SSM_CHUNK
SGU_WIDTH = 512
SGU_HEADS = 8
SGU_HEAD_DIM = 64
SGU_CHUNK = 128
CONV_CH = 512
CONV_TAPS = 31
CONV_HALO = 32
MLA_HEADS = 8
MLA_Q_RANK = 256
MLA_KV_RANK = 128
MLA_NOPE = 64
MLA_ROPE = 32
MLA_V = 64
MLA_PAD = 128
MLA_VROWS = 80
ATTN_HEADS = 4
ROPE_THETA = 10000.0
FF_DENSE = 4096
N_EXPERTS = 8
FF_EXPERT = 3584
LANES = 128
SUBLANES = 8
ROW_TILE = D_MODEL // LANES
VMEM_LIMIT = 56 * 1024 * 1024


def _params(sem, vmem=VMEM_LIMIT):
    return pltpu.CompilerParams(dimension_semantics=sem, vmem_limit_bytes=vmem)


def _rms(x, g):
    return x * lax.rsqrt(jnp.mean(x * x, axis=-1, keepdims=True) + NORM_EPS) * g


def _layer_norm(x, g, b):
    mu = jnp.mean(x, axis=-1, keepdims=True)
    xc = x - mu
    return xc * lax.rsqrt(jnp.mean(xc * xc, axis=-1, keepdims=True) + NORM_EPS) * g + b


def _dot(a, b):
    return jnp.dot(a, b, preferred_element_type=F32)


def _norm_proj_kernel(h_ref, g_ref, w_ref, a_ref, b_ref):
    z = _rms(h_ref[...], g_ref[...])
    proj = _dot(z.astype(BF16), w_ref[...])
    for jb in range(SSM_WIDTH // LANES):
        a_ref[jb] = proj[:, jb * LANES:(jb + 1) * LANES]
    b_ref[...] = proj[:, SSM_WIDTH:].astype(b_ref.dtype)


def _norm_proj(h, g, w, tm):
    n, d = h.shape
    cols = w.shape[1]
    return pl.pallas_call(
        _norm_proj_kernel,
        grid=(n // tm,),
        in_specs=[pl.BlockSpec((tm, d), lambda i: (i, 0)),
                  pl.BlockSpec((1, d), lambda i: (0, 0)),
                  pl.BlockSpec((d, cols), lambda i: (0, 0))],
        out_specs=[pl.BlockSpec((SSM_WIDTH // LANES, tm, LANES), lambda i: (0, i, 0)),
                   pl.BlockSpec((tm, cols - SSM_WIDTH), lambda i: (i, 0))],
        out_shape=[jax.ShapeDtypeStruct((SSM_WIDTH // LANES, n, LANES), F32),
                   jax.ShapeDtypeStruct((n, cols - SSM_WIDTH), BF16)],
        compiler_params=_params(("parallel",)),
        name="even_in_proj",
    )(h, g, w)


def _s5_matrices(lam_re, lam_im, log_dt, b_re, b_im, c_re, c_im):
    t = SSM_CHUNK
    lr = jnp.minimum(lam_re.astype(F32), -1e-4)
    li = lam_im.astype(F32)
    dt = jnp.exp(log_dt.astype(F32))[:, None]
    mag = jnp.exp(lr * dt)
    a_re = mag * jnp.cos(li * dt)
    a_im = mag * jnp.sin(li * dt)
    den = lr * lr + li * li
    nr = a_re - 1.0
    coef_re = (nr * lr + a_im * li) / den
    coef_im = (a_im * lr - nr * li) / den
    br = b_re.astype(F32)
    bi = b_im.astype(F32)
    bb_re = coef_re[..., None] * br - coef_im[..., None] * bi
    bb_im = coef_re[..., None] * bi + coef_im[..., None] * br
    cr = c_re.astype(F32)
    ci = c_im.astype(F32)
    pw_re = [jnp.ones_like(a_re)]
    pw_im = [jnp.zeros_like(a_im)]
    for _ in range(t):
        pr, pi = pw_re[-1], pw_im[-1]
        pw_re.append(pr * a_re - pi * a_im)
        pw_im.append(pr * a_im + pi * a_re)
    pw_re = jnp.stack(pw_re)
    pw_im = jnp.stack(pw_im)
    ab_re = pw_re[:t, :, :, None] * bb_re[None] - pw_im[:t, :, :, None] * bb_im[None]
    ab_im = pw_re[:t, :, :, None] * bb_im[None] + pw_im[:t, :, :, None] * bb_re[None]
    hi = lax.Precision.HIGHEST
    k_lag = (jnp.einsum('gnp,tgpm->tgnm', cr, ab_re, precision=hi)
             - jnp.einsum('gnp,tgpm->tgnm', ci, ab_im, precision=hi))
    n_pairs = SSM_GROUPS // 2
    st = 2 * SSM_STATE

    def pair_diag(w):
        w = w.reshape((n_pairs, 2) + w.shape[1:])
        z = jnp.zeros_like(w[:, 0])
        top = jnp.concatenate([w[:, 0], z], axis=-1)
        bot = jnp.concatenate([z, w[:, 1]], axis=-1)
        return jnp.concatenate([top, bot], axis=-2)

    k_blk = pair_diag(k_lag.transpose(1, 0, 3, 2))
    rev_re = pw_re[:t][::-1]
    rev_im = pw_im[:t][::-1]
    ws_re = rev_re[..., None] * bb_re[None] - rev_im[..., None] * bb_im[None]
    ws_im = rev_re[..., None] * bb_im[None] + rev_im[..., None] * bb_re[None]
    ws_re = pair_diag(ws_re.transpose(1, 0, 3, 2)).reshape(n_pairs, SSM_PAIR, st).astype(BF16)
    ws_im = pair_diag(ws_im.transpose(1, 0, 3, 2)).reshape(n_pairs, SSM_PAIR, st).astype(BF16)
    ca_re = cr[None] * pw_re[1:, :, None, :] - ci[None] * pw_im[1:, :, None, :]
    ca_im = cr[None] * pw_im[1:, :, None, :] + ci[None] * pw_re[1:, :, None, :]
    co_re = pair_diag(ca_re.transpose(1, 0, 3, 2))
    co_im = pair_diag((-ca_im).transpose(1, 0, 3, 2))
    w_intra, wo_re, wo_im = _s5_expand(k_blk, co_re, co_im)
    return dict(
        w_intra=w_intra, ws_re=ws_re, ws_im=ws_im, wo_re=wo_re, wo_im=wo_im,
        at_re=pw_re[t].reshape(1, SSM_GROUPS * SSM_STATE), at_im=pw_im[t].reshape(1, SSM_GROUPS * SSM_STATE))


def _s5_expand_kernel(k_ref, cre_ref, cim_ref, wi_ref, wore_ref, woim_ref, kcat_ref):
    pw = 2 * SSM_GROUP
    for tau in range(SSM_CHUNK):
        kcat_ref[:, tau * pw:(tau + 1) * pw] = k_ref[0, tau]
        wore_ref[0, :, tau * pw:(tau + 1) * pw] = cre_ref[0, tau].astype(wore_ref.dtype)
        woim_ref[0, :, tau * pw:(tau + 1) * pw] = cim_ref[0, tau].astype(woim_ref.dtype)
    kcat = kcat_ref[...]
    col = lax.broadcasted_iota(jnp.int32, kcat.shape, 1)
    for s in range(SSM_CHUNK):
        blk = kcat if s == 0 else jnp.where(col >= s * pw, pltpu.roll(kcat, s * pw, 1), 0.0)
        wi_ref[0, s * pw:(s + 1) * pw, :] = blk.astype(wi_ref.dtype)


def _s5_expand(k_blk, co_re, co_im):
    n_pairs = k_blk.shape[0]
    pw = 2 * SSM_GROUP
    st = 2 * SSM_STATE
    return pl.pallas_call(
        _s5_expand_kernel,
        grid=(n_pairs,),
        in_specs=[pl.BlockSpec((1, SSM_CHUNK, pw, pw), lambda q: (q, 0, 0, 0)),
                  pl.BlockSpec((1, SSM_CHUNK, st, pw), lambda q: (q, 0, 0, 0)),
                  pl.BlockSpec((1, SSM_CHUNK, st, pw), lambda q: (q, 0, 0, 0))],
        out_specs=[pl.BlockSpec((1, SSM_PAIR, SSM_PAIR), lambda q: (q, 0, 0)),
                   pl.BlockSpec((1, st, SSM_PAIR), lambda q: (q, 0, 0)),
                   pl.BlockSpec((1, st, SSM_PAIR), lambda q: (q, 0, 0))],
        out_shape=[jax.ShapeDtypeStruct((n_pairs, SSM_PAIR, SSM_PAIR), BF16),
                   jax.ShapeDtypeStruct((n_pairs, st, SSM_PAIR), BF16),
                   jax.ShapeDtypeStruct((n_pairs, st, SSM_PAIR), BF16)],
        scratch_shapes=[pltpu.VMEM((pw, SSM_PAIR), F32)],
        compiler_params=_params(("parallel",)),
        name="s5_expand_weights",
    )(k_blk, co_re, co_im)


S5_LANE_PAIRS = LANES // (2 * SSM_GROUP)
S5_SCAN_LANES = 512


def _s5_state_kernel(u0_ref, u1_ref, u2_ref, u3_ref, wre_ref, wim_ref, are_ref, aim_ref,
                     x_ref, hre_ref, him_ref, sre_ref, sim_ref):
    n_chunks = x_ref.shape[0]
    pw = 2 * SSM_GROUP
    u_refs = (u0_ref, u1_ref, u2_ref, u3_ref)
    for t in range(SSM_CHUNK):
        for j, u_ref in enumerate(u_refs):
            ut = u_ref[pl.ds(t, n_chunks, stride=SSM_CHUNK), :]
            for qq in range(S5_LANE_PAIRS):
                q = j * S5_LANE_PAIRS + qq
                x_ref[:, q * SSM_PAIR + t * pw: q * SSM_PAIR + (t + 1) * pw] = (
                    ut[:, qq * pw:(qq + 1) * pw].astype(x_ref.dtype))
    st = 2 * SSM_STATE
    for q in range(SSM_GROUPS // 2):
        xq = x_ref[:, q * SSM_PAIR:(q + 1) * SSM_PAIR]
        sre_ref[:, q * st:(q + 1) * st] = _dot(xq, wre_ref[q])
        sim_ref[:, q * st:(q + 1) * st] = _dot(xq, wim_ref[q])

    row = lax.broadcasted_iota(jnp.int32, (SUBLANES, S5_SCAN_LANES), 0)
    zero = jnp.zeros((SUBLANES, S5_SCAN_LANES), F32)
    for c0 in range(0, sre_ref.shape[1], S5_SCAN_LANES):
        cols = pl.ds(c0, S5_SCAN_LANES)
        ar = are_ref[:, cols]
        ai = aim_ref[:, cols]

        def body(k, carry, cols=cols, ar=ar, ai=ai):
            r0 = pl.multiple_of(k * SUBLANES, SUBLANES)
            sr = sre_ref[pl.ds(r0, SUBLANES), cols]
            si = sim_ref[pl.ds(r0, SUBLANES), cols]
            out_r, out_i = carry
            for i in range(1, SUBLANES + 1):
                tr = ar * out_r - ai * out_i + sr
                ti = ar * out_i + ai * out_r + si
                tr = pltpu.roll(tr, 1, 0)
                ti = pltpu.roll(ti, 1, 0)
                if i < SUBLANES:
                    out_r = jnp.where(row == i, tr, out_r)
                    out_i = jnp.where(row == i, ti, out_i)
            hre_ref[pl.ds(r0, SUBLANES), cols] = out_r
            him_ref[pl.ds(r0, SUBLANES), cols] = out_i
            return tr, ti

        lax.fori_loop(0, n_chunks // SUBLANES, body, (zero, zero))


def _s5_out_kernel(x_ref, wi_ref, hre_ref, him_ref, wore_ref, woim_ref, d_ref, y_ref, yt_ref):
    n_chunks = x_ref.shape[0]
    pw = 2 * SSM_GROUP
    st = 2 * SSM_STATE
    for qq in range(S5_LANE_PAIRS):
        x = x_ref[:, qq * SSM_PAIR:(qq + 1) * SSM_PAIR]
        y = _dot(x, wi_ref[qq])
        y += _dot(hre_ref[:, qq * st:(qq + 1) * st].astype(BF16), wore_ref[qq])
        y += _dot(him_ref[:, qq * st:(qq + 1) * st].astype(BF16), woim_ref[qq])
        y += d_ref[:, qq * SSM_PAIR:(qq + 1) * SSM_PAIR] * x.astype(F32)
        y = jax.nn.gelu(y)
        for t in range(SSM_CHUNK):
            yt_ref[t, :, qq * pw:(qq + 1) * pw] = y[:, t * pw:(t + 1) * pw]
    for t in range(SSM_CHUNK):
        y_ref[pl.ds(t, n_chunks, stride=SSM_CHUNK), :] = yt_ref[t]


def _s5_mixer(u, mats, d, batch, seq):
    t = SSM_CHUNK
    n_chunks = seq // t
    n_pairs = SSM_GROUPS // 2
    cols = n_pairs * SSM_PAIR
    st = 2 * SSM_STATE
    n_state = n_pairs * st
    n_blk = SSM_WIDTH // LANES
    assert n_blk == 4 and n_chunks % SUBLANES == 0
    once = pl.Buffered(1)
    x, h_re, h_im = pl.pallas_call(
        _s5_state_kernel,
        grid=(batch,),
        in_specs=[pl.BlockSpec((None, seq, LANES), lambda b, j=j: (j, b, 0)) for j in range(n_blk)] + [
            pl.BlockSpec((n_pairs, SSM_PAIR, st), lambda b: (0, 0, 0), pipeline_mode=once),
            pl.BlockSpec((n_pairs, SSM_PAIR, st), lambda b: (0, 0, 0), pipeline_mode=once),
            pl.BlockSpec((1, n_state), lambda b: (0, 0)),
            pl.BlockSpec((1, n_state), lambda b: (0, 0))],
        out_specs=[pl.BlockSpec((n_chunks, cols), lambda b: (b, 0)),
                   pl.BlockSpec((n_chunks, n_state), lambda b: (b, 0)),
                   pl.BlockSpec((n_chunks, n_state), lambda b: (b, 0))],
        out_shape=[jax.ShapeDtypeStruct((batch * n_chunks, cols), BF16),
                   jax.ShapeDtypeStruct((batch * n_chunks, n_state), F32),
                   jax.ShapeDtypeStruct((batch * n_chunks, n_state), F32)],
        scratch_shapes=[pltpu.VMEM((n_chunks, n_state), F32), pltpu.VMEM((n_chunks, n_state), F32)],
        compiler_params=_params(("parallel",)),
        name="s5_state_scan",
    )(u, u, u, u, mats['ws_re'], mats['ws_im'], mats['at_re'], mats['at_im'])
    lp = S5_LANE_PAIRS
    d_cols = jnp.broadcast_to(d.astype(F32).reshape(n_pairs, 1, 2 * SSM_GROUP),
                              (n_pairs, t, 2 * SSM_GROUP)).reshape(1, cols)
    return pl.pallas_call(
        _s5_out_kernel,
        grid=(batch, n_blk),
        in_specs=[pl.BlockSpec((n_chunks, lp * SSM_PAIR), lambda b, j: (b, j)),
                  pl.BlockSpec((lp, SSM_PAIR, SSM_PAIR), lambda b, j: (j, 0, 0)),
                  pl.BlockSpec((n_chunks, lp * st), lambda b, j: (b, j)),
                  pl.BlockSpec((n_chunks, lp * st), lambda b, j: (b, j)),
                  pl.BlockSpec((lp, st, SSM_PAIR), lambda b, j: (j, 0, 0)),
                  pl.BlockSpec((lp, st, SSM_PAIR), lambda b, j: (j, 0, 0)),
                  pl.BlockSpec((1, lp * SSM_PAIR), lambda b, j: (0, j))],
        out_specs=pl.BlockSpec((None, seq, LANES), lambda b, j: (j, b, 0)),
        out_shape=jax.ShapeDtypeStruct((n_blk, batch * seq, LANES), F32),
        scratch_shapes=[pltpu.VMEM((t, n_chunks, LANES), F32)],
        compiler_params=_params(("parallel", "parallel")),
        name="s5_out",
    )(x, mats['w_intra'], h_re, h_im, mats['wo_re'], mats['wo_im'], d_cols)


def _even_mix_kernel(ys_ref, bu_ref, bv_ref, h_ref, wglu_ref, lng_ref, lnb_ref, ws_ref, bias_ref,
                     wo_a_ref, wo_b_ref, g1_ref, g2_ref, hout_ref, z_ref, s_scr):
    tm = h_ref.shape[0]
    ys = jnp.concatenate([ys_ref[jb] for jb in range(ys_ref.shape[0])], axis=1)
    ya = ys * jax.nn.sigmoid(_dot(ys.astype(BF16), wglu_ref[...]))
    u = jax.nn.gelu(bu_ref[...].astype(F32))
    v = _layer_norm(jax.nn.gelu(bv_ref[...].astype(F32)), lng_ref[...], lnb_ref[...])
    lane = lax.broadcasted_iota(jnp.int32, v.shape, 1)
    left = (lane % LANES) < SGU_HEAD_DIM
    v_l = jnp.where(left, v, 0.0).astype(BF16)
    v_r = jnp.where(left, 0.0, v).astype(BF16)
    for c in range(tm // SGU_CHUNK):
        rows = slice(c * SGU_CHUNK, (c + 1) * SGU_CHUNK)
        for p in range(SGU_HEADS // 2):
            cols = slice(p * LANES, (p + 1) * LANES)
            s_scr[rows, cols] = (_dot(ws_ref[2 * p], v_l[rows, cols]) + _dot(ws_ref[2 * p + 1], v_r[rows, cols]))
    bias = jnp.concatenate([bias_ref[...]] * (tm // SGU_CHUNK), axis=0)
    yb = u * (s_scr[...] + bias)
    mix = _dot(ya.astype(BF16), wo_a_ref[...]) + _dot(yb.astype(BF16), wo_b_ref[...])
    h_new = h_ref[...] + _rms(mix, g1_ref[...])
    hout_ref[...] = h_new
    z_ref[...] = _rms(h_new, g2_ref[...]).astype(z_ref.dtype)


def _even_mix(ys, proj, h, wglu, lng, lnb, ws, bias, wo_a, wo_b, g1, g2, tm):
    n, d = h.shape
    w = SGU_WIDTH
    const = lambda *shape: pl.BlockSpec(shape, lambda i: (0,) * len(shape))
    return pl.pallas_call(
        _even_mix_kernel,
        grid=(n // tm,),
        in_specs=[pl.BlockSpec((w // LANES, tm, LANES), lambda i: (0, i, 0)),
                  pl.BlockSpec((tm, w), lambda i: (i, 0)),
                  pl.BlockSpec((tm, w), lambda i: (i, 1)),
                  pl.BlockSpec((tm, d), lambda i: (i, 0)),
                  const(w, w), const(1, w), const(1, w),
                  const(SGU_HEADS, SGU_CHUNK, SGU_CHUNK), const(SGU_CHUNK, w),
                  const(w, d), const(w, d), const(1, d), const(1, d)],
        out_specs=[pl.BlockSpec((tm, d), lambda i: (i, 0)),
                   pl.BlockSpec((tm, d), lambda i: (i, 0))],
        out_shape=[jax.ShapeDtypeStruct((n, d), F32), jax.ShapeDtypeStruct((n, d), BF16)],
        scratch_shapes=[pltpu.VMEM((tm, w), F32)],
        compiler_params=_params(("parallel",)),
        name="even_mix",
    )(ys, proj, proj, h, wglu, lng, lnb, ws, bias, wo_a, wo_b, g1, g2)


def _ffn_kernel(z_ref, wg_ref, wu_ref, wd_ref, h_ref, g3_ref, gn_ref, hout_ref, zout_ref, acc_ref):
    j = pl.program_id(1)

    @pl.when(j == 0)
    def _():
        acc_ref[...] = jnp.zeros_like(acc_ref)

    z = z_ref[...]
    a = jax.nn.silu(_dot(z, wg_ref[...])) * _dot(z, wu_ref[...])
    acc_ref[...] += _dot(a.astype(BF16), wd_ref[...])

    @pl.when(j == pl.num_programs(1) - 1)
    def _():
        h_new = h_ref[...] + _rms(acc_ref[...], g3_ref[...])
        hout_ref[...] = h_new
        zout_ref[...] = _rms(h_new, gn_ref[...]).astype(zout_ref.dtype)


def _dense_ffn(z, wg, wu, wd, h, g3, g_next, tm, tf):
    n, d = h.shape
    ff = wg.shape[1]
    return pl.pallas_call(
        _ffn_kernel,
        grid=(n // tm, ff // tf),
        in_specs=[pl.BlockSpec((tm, d), lambda i, j: (i, 0)),
                  pl.BlockSpec((d, tf), lambda i, j: (0, j)),
                  pl.BlockSpec((d, tf), lambda i, j: (0, j)),
                  pl.BlockSpec((tf, d), lambda i, j: (j, 0)),
                  pl.BlockSpec((tm, d), lambda i, j: (i, 0)),
                  pl.BlockSpec((1, d), lambda i, j: (0, 0)),
                  pl.BlockSpec((1, d), lambda i, j: (0, 0))],
        out_specs=[pl.BlockSpec((tm, d), lambda i, j: (i, 0)),
                   pl.BlockSpec((tm, d), lambda i, j: (i, 0))],
        out_shape=[jax.ShapeDtypeStruct((n, d), F32), jax.ShapeDtypeStruct((n, d), BF16)],
        scratch_shapes=[pltpu.VMEM((tm, d), F32)],
        compiler_params=_params(("parallel", "arbitrary")),
        name="dense_ffn",
    )(z, wg, wu, wd, h, g3, g_next)


def _odd_proj_kernel(z_ref, win_ref, gq_ref, gkv_ref, wuq_ref, wuqs_ref, wuk_ref, wuv_ref, vone_ref, cos_ref, sin_ref,
                     zc_ref, q_ref, k_ref, v_ref, *, scale):
    z = z_ref[...]
    proj = _dot(z, win_ref[...])
    c0 = 2 * CONV_CH
    c1 = c0 + MLA_Q_RANK
    c2 = c1 + MLA_KV_RANK
    c3 = c2 + MLA_PAD
    zc_ref[...] = proj[:, :c0].astype(zc_ref.dtype)
    cq = _rms(proj[:, c0:c1], gq_ref[...]).astype(BF16)
    ckv = _rms(proj[:, c1:c2], gkv_ref[...]).astype(BF16)
    cos = cos_ref[...]
    sin = sin_ref[...]
    cos_h = jnp.concatenate([cos] * MLA_HEADS, axis=1)
    sin_h = jnp.concatenate([sin] * MLA_HEADS, axis=1)
    q = _dot(cq, wuq_ref[...]) * cos_h + _dot(cq, wuqs_ref[...]) * sin_h
    q_ref[...] = (q * scale).astype(q_ref.dtype)
    kr = proj[:, c2:c3] * cos + proj[:, c3:] * sin
    k = _dot(ckv, wuk_ref[...]) + jnp.concatenate([kr] * MLA_HEADS, axis=1)
    k_ref[...] = k.astype(k_ref.dtype)
    vt = lax.dot_general(wuv_ref[...], ckv, (((1,), (1,)), ((), ())), preferred_element_type=F32)
    v_ref[0] = (vt + vone_ref[...]).astype(v_ref.dtype)


def _odd_proj(z, win, gq, gkv, wuq, wuqs, wuk, wuv_t, v_one, cos_t, sin_t, seq, tm):
    n, d = z.shape
    hp = MLA_HEADS * MLA_PAD
    vr = MLA_HEADS * MLA_VROWS
    n_l = seq // tm
    const = lambda *shape: pl.BlockSpec(shape, lambda i: (0,) * len(shape))
    out = jax.ShapeDtypeStruct((n, hp), BF16)
    scale = float((MLA_NOPE + MLA_ROPE) ** -0.5 * math.log2(math.e))
    return pl.pallas_call(
        functools.partial(_odd_proj_kernel, scale=scale),
        grid=(n // tm,),
        in_specs=[pl.BlockSpec((tm, d), lambda i: (i, 0)),
                  const(d, win.shape[1]), const(1, MLA_Q_RANK), const(1, MLA_KV_RANK),
                  const(MLA_Q_RANK, hp), const(MLA_Q_RANK, hp), const(MLA_KV_RANK, hp), const(vr, MLA_KV_RANK),
                  const(vr, 1),
                  pl.BlockSpec((tm, MLA_PAD), lambda i: (i % n_l, 0)),
                  pl.BlockSpec((tm, MLA_PAD), lambda i: (i % n_l, 0))],
        out_specs=[pl.BlockSpec((tm, 2 * CONV_CH), lambda i: (i, 0)),
                   pl.BlockSpec((tm, hp), lambda i: (i, 0)),
                   pl.BlockSpec((tm, hp), lambda i: (i, 0)),
                   pl.BlockSpec((1, vr, tm), lambda i: (i, 0, 0))],
        out_shape=[jax.ShapeDtypeStruct((n, 2 * CONV_CH), BF16), out, out,
                   jax.ShapeDtypeStruct((n // tm, vr, tm), BF16)],
        compiler_params=_params(("parallel",)),
        name="odd_in_proj",
    )(z, win, gq, gkv, wuq, wuqs, wuk, wuv_t, v_one, cos_t, sin_t)


def _attn_kernel(q_ref, k_ref, vt_ref, wg_ref, wu_ref, wd_ref, o_ref, wgb_ref, wub_ref, wdb_ref, acc_ref, *, blk):
    i = pl.program_id(2)
    acc_ref[...] = jnp.zeros_like(acc_ref)
    tf = wgb_ref.shape[3]
    for f in range(wgb_ref.shape[1]):
        wgb_ref[0, f] = wg_ref[0, :, f * tf:(f + 1) * tf].astype(BF16)
        wub_ref[0, f] = wu_ref[0, :, f * tf:(f + 1) * tf].astype(BF16)
    wdb_ref[0] = wd_ref[0].astype(BF16)

    def step(j, m, masked):
        r0 = pl.multiple_of(j * blk, blk)
        scores = []
        for hh in range(ATTN_HEADS):
            q = q_ref[0, :, hh * MLA_PAD:(hh + 1) * MLA_PAD]
            k = k_ref[0, pl.ds(r0, blk), hh * MLA_PAD:(hh + 1) * MLA_PAD]
            st = lax.dot_general(k, q, (((1,), (1,)), ((), ())), preferred_element_type=F32)
            if masked:
                key = lax.broadcasted_iota(jnp.int32, st.shape, 0)
                qry = lax.broadcasted_iota(jnp.int32, st.shape, 1)
                st = jnp.where(key <= qry, st, -1e30)
            scores.append(st)
        soft = []
        for hh in range(ATTN_HEADS):
            m_new = jnp.maximum(m[hh], jnp.max(scores[hh], axis=0, keepdims=True))
            soft.append((m_new, jnp.exp2(m[hh] - m_new), jnp.exp2(scores[hh] - m_new).astype(BF16)))
        for hh in range(ATTN_HEADS):
            vt = vt_ref[j, hh * MLA_VROWS:(hh + 1) * MLA_VROWS, :]
            acc_ref[hh] = soft[hh][1] * acc_ref[hh] + _dot(vt, soft[hh][2])
        return tuple(s[0] for s in soft)

    init = jnp.full((1, blk), -1e30, F32)
    m = lax.fori_loop(0, i, lambda j, m: step(j, m, False), (init,) * ATTN_HEADS)
    step(i, m, True)
    ot = jnp.concatenate([acc_ref[hh][:MLA_V] / acc_ref[hh][MLA_V:MLA_V + 1] for hh in range(ATTN_HEADS)], axis=0)
    o_ref[0] = ot.T.astype(o_ref.dtype)


def _attention(q, k, vt, blk, wg, wu, wd, tf):
    b, seq, _ = q.shape
    n_blk = seq // blk
    n_pairs = MLA_HEADS // ATTN_HEADS
    n_e, d, ff = wg.shape
    steps = b * n_pairs * n_blk
    per_e = steps // n_e
    assert steps == per_e * n_e and d % per_e == 0 and ff % per_e == 0
    rows_in, rows_down = d // per_e, ff // per_e
    assert rows_in % 16 == 0 and rows_down % 16 == 0 and ff % tf == 0

    def lin(bi, p, i):
        return (bi * n_pairs + p) * n_blk + i

    w_in = pl.BlockSpec((1, rows_in, ff), lambda bi, p, i: (lin(bi, p, i) // per_e, lin(bi, p, i) % per_e, 0))
    w_out = pl.BlockSpec((1, ff // tf, rows_in, tf),
                         lambda bi, p, i: (lin(bi, p, i) // per_e, 0, lin(bi, p, i) % per_e, 0))
    w_down = pl.BlockSpec((1, rows_down, d), lambda bi, p, i: (lin(bi, p, i) // per_e, lin(bi, p, i) % per_e, 0))
    return pl.pallas_call(
        functools.partial(_attn_kernel, blk=blk),
        grid=(b, n_pairs, n_blk),
        in_specs=[pl.BlockSpec((1, blk, ATTN_HEADS * MLA_PAD), lambda bi, p, i: (bi, i, p)),
                  pl.BlockSpec((1, seq, ATTN_HEADS * MLA_PAD), lambda bi, p, i: (bi, 0, p)),
                  pl.BlockSpec((n_blk, ATTN_HEADS * MLA_VROWS, blk), lambda bi, p, i: (bi, p, 0)),
                  w_in, w_in, w_down],
        out_specs=[pl.BlockSpec((1, blk, ATTN_HEADS * MLA_V), lambda bi, p, i: (bi, i, p)), w_out, w_out, w_down],
        out_shape=[jax.ShapeDtypeStruct((b, seq, MLA_HEADS * MLA_V), BF16),
                   jax.ShapeDtypeStruct((n_e, ff // tf, d, tf), BF16),
                   jax.ShapeDtypeStruct((n_e, ff // tf, d, tf), BF16),
                   jax.ShapeDtypeStruct((n_e, ff, d), BF16)],
        scratch_shapes=[pltpu.VMEM((ATTN_HEADS, MLA_VROWS, blk), F32)],
        compiler_params=_params(("parallel", "parallel", "parallel")),
        name="mla_attention",
    )(q, k, vt, wg, wu, wd)


def _conv_kernel(zc_ref, w_ref, b_ref, lng_ref, lnb_ref, y_ref, buf_ref, part_ref):
    tm = zc_ref.shape[1]

    @pl.when(pl.program_id(1) == 0)
    def _():
        buf_ref[pl.ds(0, CONV_HALO), :] = jnp.zeros((CONV_HALO, CONV_CH), F32)
        buf_ref[pl.ds(CONV_HALO + tm, SUBLANES), :] = jnp.zeros((SUBLANES, CONV_CH), F32)

    zc = zc_ref[0].astype(F32)
    hh = zc[:, :CONV_CH] * jax.nn.sigmoid(zc[:, CONV_CH:])
    buf_ref[pl.ds(CONV_HALO, tm), :] = hh
    off = CONV_HALO - (CONV_TAPS - 1)
    acc = jnp.zeros((tm, CONV_CH), F32) + b_ref[...]
    for b in range(SUBLANES):
        taps = [k for k in range(CONV_TAPS) if (off + k) % SUBLANES == b]
        part = None
        for k in taps:
            term = w_ref[pl.ds(k, 1), :] * buf_ref[pl.ds(off + k - b, tm + SUBLANES), :]
            part = term if part is None else part + term
        if b == 0:
            acc = acc + part[:tm]
        else:
            part_ref[...] = part
            acc = acc + part_ref[pl.ds(b, tm), :]
    buf_ref[pl.ds(0, CONV_HALO), :] = buf_ref[pl.ds(tm, CONV_HALO), :]
    y_ref[0] = jax.nn.silu(_layer_norm(acc, lng_ref[...], lnb_ref[...])).astype(y_ref.dtype)


def _conv_mixer(zc, w, b, lng, lnb, tm):
    bsz, seq, _ = zc.shape
    const = lambda *shape: pl.BlockSpec(shape, lambda bi, i: (0,) * len(shape))
    return pl.pallas_call(
        _conv_kernel,
        grid=(bsz, seq // tm),
        in_specs=[pl.BlockSpec((1, tm, 2 * CONV_CH), lambda bi, i: (bi, i, 0)),
                  const(CONV_HALO, CONV_CH), const(1, CONV_CH), const(1, CONV_CH), const(1, CONV_CH)],
        out_specs=pl.BlockSpec((1, tm, CONV_CH), lambda bi, i: (bi, i, 0)),
        out_shape=jax.ShapeDtypeStruct((bsz, seq, CONV_CH), BF16),
        scratch_shapes=[pltpu.VMEM((CONV_HALO + tm + SUBLANES, CONV_CH), F32),
                        pltpu.VMEM((tm + SUBLANES, CONV_CH), F32)],
        compiler_params=_params(("arbitrary", "arbitrary")),
        name="conv_module",
    )(zc, w, b, lng, lnb)


def _odd_mix_kernel(yc_ref, yd_ref, h_ref, wo_a_ref, wo_b_ref, g1_ref, g2_ref, wr_ref, hout_ref, z_ref, route_ref):
    tm = h_ref.shape[0]
    halves = [slice(0, tm // 2), slice(tm // 2, tm)]
    mixes = [_dot(yc_ref[r, :], wo_a_ref[...]) + _dot(yd_ref[r, :], wo_b_ref[...]) for r in halves]
    zs = []
    for r, mix in zip(halves, mixes):
        h_new = h_ref[r, :] + _rms(mix, g1_ref[...])
        hout_ref[r, :] = h_new
        zs.append(_rms(h_new, g2_ref[...]))
    all_logits = [_dot(z.astype(BF16), wr_ref[...]) for z in zs]
    _store_row_tiles(z_ref, jnp.concatenate(zs, axis=0))
    neg = -jnp.inf
    for r, logits in zip(halves, all_logits):
        lane = lax.broadcasted_iota(jnp.int32, logits.shape, 1)
        logits = jnp.where(lane < N_EXPERTS, logits, neg)
        m1 = jnp.max(logits, axis=-1, keepdims=True)
        i1 = jnp.min(jnp.where(logits == m1, lane, LANES), axis=-1, keepdims=True)
        rest = jnp.where(lane == i1, neg, logits)
        m2 = jnp.max(rest, axis=-1, keepdims=True)
        i2 = jnp.min(jnp.where(rest == m2, lane, LANES), axis=-1, keepdims=True)
        e = jnp.exp(m2 - m1)
        w1 = 1.0 / (1.0 + e)
        w2 = e / (1.0 + e)
        route_ref[r, :] = jnp.where(lane == 0, i1.astype(F32),
                                    jnp.where(lane == 1, i2.astype(F32),
                                              jnp.where(lane == 2, w1, jnp.where(lane == 3, w2, 0.0))))


def _odd_mix(yc, yd, h, wo_a, wo_b, g1, g2, wr, tm):
    n, d = h.shape
    const = lambda *shape: pl.BlockSpec(shape, lambda i: (0,) * len(shape))
    return pl.pallas_call(
        _odd_mix_kernel,
        grid=(n // tm,),
        in_specs=[pl.BlockSpec((tm, yc.shape[1]), lambda i: (i, 0)),
                  pl.BlockSpec((tm, yd.shape[1]), lambda i: (i, 0)),
                  pl.BlockSpec((tm, d), lambda i: (i, 0)),
                  const(*wo_a.shape), const(*wo_b.shape), const(1, d), const(1, d), const(d, LANES)],
        out_specs=[pl.BlockSpec((tm, d), lambda i: (i, 0)),
                   pl.BlockSpec((tm * ROW_TILE, LANES), lambda i: (i, 0)),
                   pl.BlockSpec((tm, LANES), lambda i: (i, 0))],
        out_shape=[jax.ShapeDtypeStruct((n, d), F32), jax.ShapeDtypeStruct((n * ROW_TILE, LANES), F32),
                   jax.ShapeDtypeStruct((n, LANES), F32)],
        compiler_params=_params(("parallel",)),
        name="odd_mix_router",
    )(yc, yd, h, wo_a, wo_b, g1, g2, wr)


def _store_row_tiles(ref, x):
    rows = x.shape[0]
    for s in range(ROW_TILE):
        ref[pl.ds(s, rows, stride=ROW_TILE), :] = x[:, s * LANES:(s + 1) * LANES]


def _load_row_tiles(ref, rows):
    return [ref[pl.ds(s, rows, stride=ROW_TILE), :] for s in range(ROW_TILE)]


def _gather_rows(idx_ref, base, n_rows, src_hbm, dst_ref, sem):
    def body(r, c):
        src = pl.multiple_of(idx_ref[base + r] * ROW_TILE, ROW_TILE)
        dst = pl.multiple_of(r * ROW_TILE, ROW_TILE)
        pltpu.make_async_copy(src_hbm.at[pl.ds(src, ROW_TILE), :], dst_ref.at[pl.ds(dst, ROW_TILE), :], sem).start()
        return c

    lax.fori_loop(0, n_rows, body, 0, unroll=8)


def _wait_rows(src_hbm, dst_ref, sem):
    pltpu.make_async_copy(src_hbm.at[pl.ds(0, dst_ref.shape[0]), :], dst_ref, sem).wait()


def _row_copy(src_ref, src_row, dst_ref, dst_row, sem):
    src = pl.multiple_of(src_row * ROW_TILE, ROW_TILE)
    dst = pl.multiple_of(dst_row * ROW_TILE, ROW_TILE)
    return pltpu.make_async_copy(src_ref.at[pl.ds(src, ROW_TILE), :], dst_ref.at[pl.ds(dst, ROW_TILE), :], sem)


def _moe_ffn_kernel(te_ref, nu_ref, tok_ref, dst_ref, z_hbm, wg_ref, wu_ref, wd_ref, y_hbm,
                    xraw_ref, xb_ref, acc_ref, yst_ref, gsem, ssem, *, rows_per_step):
    i = pl.program_id(0)
    j = pl.program_id(1)
    tm = xb_ref.shape[0]
    stride = yst_ref.shape[0] // ROW_TILE
    n_used = nu_ref[0]
    slot = i % 2
    first = j == 0
    last = j == pl.num_programs(1) - 1

    @pl.when(first & (i == 0))
    def _():
        yst_ref[...] = jnp.zeros_like(yst_ref)
        _gather_rows(tok_ref, 0, stride, z_hbm, xraw_ref.at[0], gsem.at[0])

    @pl.when(first & (i <= n_used))
    def _():
        _wait_rows(z_hbm, xraw_ref.at[slot], gsem.at[slot])

    @pl.when(first & (i < n_used))
    def _():
        for s, blk in enumerate(_load_row_tiles(xraw_ref.at[slot], tm)):
            xb_ref[:, s * LANES:(s + 1) * LANES] = blk.astype(BF16)
        acc_ref[...] = jnp.zeros_like(acc_ref)

    @pl.when(first & (i == n_used))
    def _():
        def body(r, c):
            _row_copy(yst_ref, r, y_hbm, dst_ref[i * stride + r], ssem).start()
            return c
        lax.fori_loop(0, stride, body, 0, unroll=8)

    def multiply(rows, with_copies):
        x = xb_ref[:rows, :]
        g = _dot(x, wg_ref[0, 0])
        u = _dot(x, wu_ref[0, 0])
        if with_copies:
            nxt = xraw_ref.at[1 - slot]
            for rr in range(rows_per_step):
                r = j * rows_per_step + rr
                _row_copy(z_hbm, tok_ref[(i + 1) * stride + r], nxt, r, gsem.at[1 - slot]).start(priority=rr % 2)
                _row_copy(yst_ref, r, y_hbm, dst_ref[i * stride + r], ssem).start(priority=rr % 2)
        a = jax.nn.silu(g) * u
        acc_ref[:rows, :] += _dot(a.astype(BF16), wd_ref[0])

    used = i < n_used
    real = nu_ref[1 + i]
    quarter = tm // 4
    for rows in range(quarter, tm + 1, quarter):
        fits = (real > rows - quarter) & (real <= rows)

        @pl.when(used & fits & jnp.logical_not(last))
        def _(rows=rows):
            multiply(rows, True)

        @pl.when(used & fits & last)
        def _(rows=rows):
            multiply(rows, False)

    @pl.when(last & (i <= n_used))
    def _():
        _wait_rows(z_hbm, yst_ref, ssem)

    @pl.when(last & (i < n_used))
    def _():
        _store_row_tiles(yst_ref, acc_ref[...])


def _moe_ffn(tile_expert, n_used, tok_tab, dst_tab, z_tiles, wg, wu, wd, n_tok, tm):
    n_f, d, tf = wg.shape[1], wg.shape[2], wg.shape[3]
    copy_steps = n_f - 1
    rows_per_step = -(-tm // copy_steps)
    stride = copy_steps * rows_per_step
    n_tiles = tile_expert.shape[0]
    assert tok_tab.shape[0] == dst_tab.shape[0] == (n_tiles + 1) * stride

    def col(i, j, nu):
        return jnp.where(i < nu[0], j, n_f - 1)

    return pl.pallas_call(
        functools.partial(_moe_ffn_kernel, rows_per_step=rows_per_step),
        grid_spec=pltpu.PrefetchScalarGridSpec(
            num_scalar_prefetch=4,
            grid=(n_tiles, n_f),
            in_specs=[pl.BlockSpec(memory_space=pl.ANY),
                      pl.BlockSpec((1, 1, d, tf), lambda i, j, te, nu, tok, dst: (te[i], col(i, j, nu), 0, 0)),
                      pl.BlockSpec((1, 1, d, tf), lambda i, j, te, nu, tok, dst: (te[i], col(i, j, nu), 0, 0)),
                      pl.BlockSpec((1, tf, d), lambda i, j, te, nu, tok, dst: (te[i], col(i, j, nu), 0))],
            out_specs=pl.BlockSpec(memory_space=pl.ANY),
            scratch_shapes=[pltpu.VMEM((2, stride * ROW_TILE, LANES), F32), pltpu.VMEM((tm, d), BF16),
                            pltpu.VMEM((tm, d), F32), pltpu.VMEM((stride * ROW_TILE, LANES), F32),
                            pltpu.SemaphoreType.DMA((2,)), pltpu.SemaphoreType.DMA(())]),
        out_shape=jax.ShapeDtypeStruct(((2 * n_tok + stride) * ROW_TILE, LANES), F32),
        compiler_params=_params(("arbitrary", "arbitrary")),
        name="moe_grouped_ffn",
    )(tile_expert, n_used, tok_tab, dst_tab, z_tiles, wg, wu, wd)


def _combine_kernel(ya_ref, yb_ref, route_ref, h_ref, g_ref, o_ref):
    tm = h_ref.shape[0]
    route = route_ref[...]
    a = jnp.concatenate(_load_row_tiles(ya_ref, tm), axis=1)
    b = jnp.concatenate(_load_row_tiles(yb_ref, tm), axis=1)
    f = route[:, 2:3] * a + route[:, 3:4] * b
    o_ref[...] = h_ref[...] + _rms(f, g_ref[...])


def _combine(y, route, h, g, tm):
    n, d = h.shape
    n_blk = n // tm
    return pl.pallas_call(
        _combine_kernel,
        grid=(n_blk,),
        in_specs=[pl.BlockSpec((tm * ROW_TILE, LANES), lambda i: (i, 0)),
                  pl.BlockSpec((tm * ROW_TILE, LANES), lambda i: (n_blk + i, 0)),
                  pl.BlockSpec((tm, LANES), lambda i: (i, 0)),
                  pl.BlockSpec((tm, d), lambda i: (i, 0)),
                  pl.BlockSpec((1, d), lambda i: (0, 0))],
        out_specs=pl.BlockSpec((tm, d), lambda i: (i, 0)),
        out_shape=jax.ShapeDtypeStruct((n, d), F32),
        compiler_params=_params(("parallel",)),
        name="moe_combine",
    )(y, y, route, h, g)


def _moe_plan(route, tm, stride):
    n = route.shape[0]
    eids = jnp.concatenate([route[:, 0], route[:, 1]]).astype(jnp.int32)
    onehot = (eids[:, None] == jnp.arange(N_EXPERTS, dtype=jnp.int32)[None, :]).astype(jnp.int32)
    csum = jnp.cumsum(onehot, axis=0)
    rank = jnp.sum(csum * onehot, axis=1) - 1
    counts = csum[-1]
    padded = ((counts + tm - 1) // tm) * tm
    ends = jnp.cumsum(padded)
    starts = ends - padded
    slot = jnp.sum(onehot * starts[None, :], axis=1) + rank
    n_tiles = 2 * n // tm + N_EXPERTS + 1
    copy_of_slot = jnp.full((n_tiles * tm,), -1, jnp.int32).at[slot].set(jnp.arange(2 * n, dtype=jnp.int32))
    copy_tab = jnp.pad(copy_of_slot.reshape(n_tiles, tm), ((0, 1), (0, stride - tm)), constant_values=-1)
    tok_tab = jnp.where(copy_tab >= 0, copy_tab % n, 0)
    dump = 2 * n + jnp.arange(stride, dtype=jnp.int32)[None, :]
    dst_tab = jnp.where(copy_tab >= 0, copy_tab, dump)
    dst_tab = jnp.concatenate([jnp.broadcast_to(dump, (1, stride)), dst_tab[:-1]], axis=0)
    n_used = (ends[-1] // tm).astype(jnp.int32)
    tile_start = jnp.minimum(jnp.arange(n_tiles, dtype=jnp.int32), n_used - 1) * tm
    tile_expert = jnp.sum((tile_start[:, None] >= ends[None, :]).astype(jnp.int32), axis=1)
    onehot_e = (tile_expert[:, None] == jnp.arange(N_EXPERTS, dtype=jnp.int32)[None, :]).astype(jnp.int32)
    run_end = jnp.sum(onehot_e * (starts + counts)[None, :], axis=1)
    tile_rows = jnp.clip(run_end - jnp.arange(n_tiles, dtype=jnp.int32) * tm, 0, tm)
    return tok_tab.reshape(-1), dst_tab.reshape(-1), tile_expert, jnp.concatenate([n_used.reshape(1), tile_rows])


def _odd_weights(od_w_in, mla_w_uq, mla_w_ukv):
    c2 = 2 * CONV_CH + MLA_Q_RANK + MLA_KV_RANK
    half = MLA_ROPE // 2
    w_kr = od_w_in[:, c2:]
    w_kr_sw = jnp.concatenate([w_kr[:, half:], w_kr[:, :half]], axis=1)
    zl = jnp.zeros((D_MODEL, MLA_NOPE), F32)
    zr = jnp.zeros((D_MODEL, MLA_PAD - MLA_NOPE - MLA_ROPE), F32)
    win = jnp.concatenate([od_w_in[:, :c2], zl, w_kr, zr, zl, w_kr_sw, zr], axis=1)
    dk = MLA_NOPE + MLA_ROPE
    wq = mla_w_uq.reshape(MLA_Q_RANK, MLA_HEADS, dk)
    zq = jnp.zeros((MLA_Q_RANK, MLA_HEADS, MLA_PAD - dk), F32)
    wuq = jnp.concatenate([wq, zq], axis=2).reshape(MLA_Q_RANK, MLA_HEADS * MLA_PAD)
    wq_sw = jnp.concatenate([jnp.zeros_like(wq[:, :, :MLA_NOPE]), wq[:, :, MLA_NOPE + half:],
                             wq[:, :, MLA_NOPE:MLA_NOPE + half], zq], axis=2)
    wuqs = wq_sw.reshape(MLA_Q_RANK, MLA_HEADS * MLA_PAD)
    wkv = mla_w_ukv.reshape(MLA_KV_RANK, MLA_HEADS, MLA_NOPE + MLA_V)
    zk = jnp.zeros((MLA_KV_RANK, MLA_HEADS, MLA_PAD - MLA_NOPE), F32)
    wuk = jnp.concatenate([wkv[:, :, :MLA_NOPE], zk], axis=2).reshape(MLA_KV_RANK, MLA_HEADS * MLA_PAD)
    zv = jnp.zeros((MLA_KV_RANK, MLA_HEADS, MLA_VROWS - MLA_V), F32)
    wuv_t = jnp.concatenate([wkv[:, :, MLA_NOPE:], zv], axis=2).reshape(MLA_KV_RANK, MLA_HEADS * MLA_VROWS).T
    v_one = jnp.zeros((MLA_HEADS, MLA_VROWS), F32).at[:, MLA_V].set(1.0).reshape(MLA_HEADS * MLA_VROWS, 1)
    return win.astype(BF16), wuq.astype(BF16), wuqs.astype(BF16), wuk.astype(BF16), wuv_t.astype(BF16), v_one


def _rope_tables(seq):
    inv = 1.0 / (ROPE_THETA ** (jnp.arange(0, MLA_ROPE, 2, dtype=F32) / MLA_ROPE))
    ang = jnp.arange(seq, dtype=F32)[:, None] * inv[None, :]
    cos, sin = jnp.cos(ang), jnp.sin(ang)
    ones = jnp.ones((seq, MLA_NOPE), F32)
    zl = jnp.zeros((seq, MLA_NOPE), F32)
    zr = jnp.zeros((seq, MLA_PAD - MLA_NOPE - MLA_ROPE), F32)
    return (jnp.concatenate([ones, cos, cos, zr], axis=1), jnp.concatenate([zl, -sin, sin, zr], axis=1))


def kernel(x, norm_g, ev_w_in, ssm_lambda_re, ssm_lambda_im, ssm_log_dt, ssm_b_re, ssm_b_im, ssm_c_re, ssm_c_im, ssm_d, ssm_w_glu, sgu_ln_g, sgu_ln_b, sgu_w, sgu_b, ev_w_out, ffn_w_gate, ffn_w_up, ffn_w_down, od_w_in, conv_w, conv_b, conv_ln_g, conv_ln_b, mla_q_norm_g, mla_w_uq, mla_kv_norm_g, mla_w_ukv, od_w_out, moe_w_router, moe_w_gate, moe_w_up, moe_w_down):
    bsz, seq, d = x.shape
    n = bsz * seq
    assert d == D_MODEL and SUBLANES % bsz == 0 and seq % 512 == 0
    row = lambda v: v.astype(F32).reshape(1, -1)
    h = x.astype(F32).reshape(n, d)
    tm = 512

    g = norm_g[0]
    a_in, proj = _norm_proj(h, row(g[0]), ev_w_in[0].astype(BF16), 2 * tm)
    mats = _s5_matrices(ssm_lambda_re[0], ssm_lambda_im[0], ssm_log_dt[0], ssm_b_re[0], ssm_b_im[0],
                        ssm_c_re[0], ssm_c_im[0])
    ys = _s5_mixer(a_in, mats, ssm_d[0], bsz, seq)
    causal = jnp.tril(jnp.ones((SGU_CHUNK, SGU_CHUNK), dtype=bool))
    ws = jnp.where(causal[None], sgu_w[0], 0.0).astype(BF16)
    bias = jnp.repeat(sgu_b[0].astype(F32).T, SGU_HEAD_DIM, axis=1)
    wo = ev_w_out[0].astype(BF16)
    h, z = _even_mix(ys, proj, h, ssm_w_glu[0].astype(BF16), row(sgu_ln_g[0]), row(sgu_ln_b[0]), ws, bias,
                     wo[:SSM_WIDTH], wo[SSM_WIDTH:], row(g[1]), row(g[2]), tm)
    h, z = _dense_ffn(z, ffn_w_gate[0].astype(BF16), ffn_w_up[0].astype(BF16), ffn_w_down[0].astype(BF16),
                      h, row(g[3]), row(norm_g[1][0]), 1024, 1024)

    g = norm_g[1]
    win, wuq, wuqs, wuk, wuv_t, v_one = _odd_weights(od_w_in[0], mla_w_uq[0], mla_w_ukv[0])
    cos_t, sin_t = _rope_tables(seq)
    zc, q, k, vt = _odd_proj(z, win, row(mla_q_norm_g[0]), row(mla_kv_norm_g[0]), wuq, wuqs, wuk, wuv_t, v_one,
                             cos_t, sin_t, seq, tm)
    hp = MLA_HEADS * MLA_PAD
    tm_moe, tf_moe = 1024, 512
    yd, wg_b, wu_b, wd_b = _attention(q.reshape(bsz, seq, hp), k.reshape(bsz, seq, hp), vt, tm,
                                      moe_w_gate[0], moe_w_up[0], moe_w_down[0], tf_moe)
    conv_w_pad = jnp.concatenate([conv_w[0].astype(F32), jnp.zeros((CONV_HALO - CONV_TAPS, CONV_CH), F32)], axis=0)
    yc = _conv_mixer(zc.reshape(bsz, seq, 2 * CONV_CH), conv_w_pad, row(conv_b[0]), row(conv_ln_g[0]),
                     row(conv_ln_b[0]), tm)
    wo = od_w_out[0].astype(BF16)
    wr = jnp.concatenate([moe_w_router[0].astype(F32), jnp.zeros((d, LANES - N_EXPERTS), F32)], axis=1)
    h, z, route = _odd_mix(yc.reshape(n, CONV_CH), yd.reshape(n, MLA_HEADS * MLA_V), h, wo[:CONV_CH], wo[CONV_CH:],
                           row(g[1]), row(g[2]), wr.astype(BF16), tm)
    copy_steps = wg_b.shape[1] - 1
    tok_tab, dst_tab, tile_expert, n_used = _moe_plan(route, tm_moe, copy_steps * -(-tm_moe // copy_steps))
    y = _moe_ffn(tile_expert, n_used, tok_tab, dst_tab, z, wg_b, wu_b, wd_b, n, tm_moe)
    h = _combine(y, route, h, row(g[3]), 256)
    return h.reshape(bsz, seq, d).astype(x.dtype)
```

```python
import functools
import math

import jax
import jax.numpy as jnp
from jax import lax
from jax.experimental import pallas as pl
from jax.experimental.pallas import tpu as pltpu

F32 = jnp.float32
BF16 = jnp.bfloat16

D_MODEL = 1024
NORM_EPS = 1e-6
SSM_WIDTH = 512
SSM_GROUP = 16
SSM_GROUPS = 32
SSM_STATE = 64
SSM_CHUNK = 16
SSM_PAIR = 2 * SSM_GROUP * SSM_CHUNK
SGU_WIDTH = 512
SGU_HEADS = 8
SGU_HEAD_DIM = 64
SGU_CHUNK = 128
CONV_CH = 512
CONV_TAPS = 31
CONV_HALO = 32
MLA_HEADS = 8
MLA_Q_RANK = 256
MLA_KV_RANK = 128
MLA_NOPE = 64
MLA_ROPE = 32
MLA_V = 64
MLA_PAD = 128
MLA_VROWS = 80
ATTN_HEADS = 4
ROPE_THETA = 10000.0
FF_DENSE = 4096
N_EXPERTS = 8
FF_EXPERT = 3584
LANES = 128
SUBLANES = 8
ROW_TILE = D_MODEL // LANES
VMEM_LIMIT = 56 * 1024 * 1024


def _params(sem, vmem=VMEM_LIMIT):
    return pltpu.CompilerParams(dimension_semantics=sem, vmem_limit_bytes=vmem)


def _rms(x, g):
    return x * lax.rsqrt(jnp.mean(x * x, axis=-1, keepdims=True) + NORM_EPS) * g


def _layer_norm(x, g, b):
    mu = jnp.mean(x, axis=-1, keepdims=True)
    xc = x - mu
    return xc * lax.rsqrt(jnp.mean(xc * xc, axis=-1, keepdims=True) + NORM_EPS) * g + b


def _dot(a, b):
    return jnp.dot(a, b, preferred_element_type=F32)


def _norm_proj_kernel(h_ref, g_ref, w_ref, a_ref, b_ref):
    z = _rms(h_ref[...], g_ref[...])
    proj = _dot(z.astype(BF16), w_ref[...])
    for jb in range(SSM_WIDTH // LANES):
        a_ref[jb] = proj[:, jb * LANES:(jb + 1) * LANES]
    b_ref[...] = proj[:, SSM_WIDTH:].astype(b_ref.dtype)


def _norm_proj(h, g, w, tm):
    n, d = h.shape
    cols = w.shape[1]
    return pl.pallas_call(
        _norm_proj_kernel,
        grid=(n // tm,),
        in_specs=[pl.BlockSpec((tm, d), lambda i: (i, 0)),
                  pl.BlockSpec((1, d), lambda i: (0, 0)),
                  pl.BlockSpec((d, cols), lambda i: (0, 0))],
        out_specs=[pl.BlockSpec((SSM_WIDTH // LANES, tm, LANES), lambda i: (0, i, 0)),
                   pl.BlockSpec((tm, cols - SSM_WIDTH), lambda i: (i, 0))],
        out_shape=[jax.ShapeDtypeStruct((SSM_WIDTH // LANES, n, LANES), F32),
                   jax.ShapeDtypeStruct((n, cols - SSM_WIDTH), BF16)],
        compiler_params=_params(("parallel",)),
        name="even_in_proj",
    )(h, g, w)


def _s5_matrices(lam_re, lam_im, log_dt, b_re, b_im, c_re, c_im):
    t = SSM_CHUNK
    lr = jnp.minimum(lam_re.astype(F32), -1e-4)
    li = lam_im.astype(F32)
    dt = jnp.exp(log_dt.astype(F32))[:, None]
    mag = jnp.exp(lr * dt)
    a_re = mag * jnp.cos(li * dt)
    a_im = mag * jnp.sin(li * dt)
    den = lr * lr + li * li
    nr = a_re - 1.0
    coef_re = (nr * lr + a_im * li) / den
    coef_im = (a_im * lr - nr * li) / den
    br = b_re.astype(F32)
    bi = b_im.astype(F32)
    bb_re = coef_re[..., None] * br - coef_im[..., None] * bi
    bb_im = coef_re[..., None] * bi + coef_im[..., None] * br
    cr = c_re.astype(F32)
    ci = c_im.astype(F32)
    pw_re = [jnp.ones_like(a_re)]
    pw_im = [jnp.zeros_like(a_im)]
    for _ in range(t):
        pr, pi = pw_re[-1], pw_im[-1]
        pw_re.append(pr * a_re - pi * a_im)
        pw_im.append(pr * a_im + pi * a_re)
    pw_re = jnp.stack(pw_re)
    pw_im = jnp.stack(pw_im)
    ab_re = pw_re[:t, :, :, None] * bb_re[None] - pw_im[:t, :, :, None] * bb_im[None]
    ab_im = pw_re[:t, :, :, None] * bb_im[None] + pw_im[:t, :, :, None] * bb_re[None]
    hi = lax.Precision.HIGHEST
    k_lag = (jnp.einsum('gnp,tgpm->tgnm', cr, ab_re, precision=hi)
             - jnp.einsum('gnp,tgpm->tgnm', ci, ab_im, precision=hi))
    n_pairs = SSM_GROUPS // 2
    st = 2 * SSM_STATE

    def pair_diag(w):
        w = w.reshape((n_pairs, 2) + w.shape[1:])
        z = jnp.zeros_like(w[:, 0])
        top = jnp.concatenate([w[:, 0], z], axis=-1)
        bot = jnp.concatenate([z, w[:, 1]], axis=-1)
        return jnp.concatenate([top, bot], axis=-2)

    k_blk = pair_diag(k_lag.transpose(1, 0, 3, 2))
    rev_re = pw_re[:t][::-1]
    rev_im = pw_im[:t][::-1]
    ws_re = rev_re[..., None] * bb_re[None] - rev_im[..., None] * bb_im[None]
    ws_im = rev_re[..., None] * bb_im[None] + rev_im[..., None] * bb_re[None]
    ws_re = pair_diag(ws_re.transpose(1, 0, 3, 2)).reshape(n_pairs, SSM_PAIR, st).astype(BF16)
    ws_im = pair_diag(ws_im.transpose(1, 0, 3, 2)).reshape(n_pairs, SSM_PAIR, st).astype(BF16)
    ca_re = cr[None] * pw_re[1:, :, None, :] - ci[None] * pw_im[1:, :, None, :]
    ca_im = cr[None] * pw_im[1:, :, None, :] + ci[None] * pw_re[1:, :, None, :]
    co_re = pair_diag(ca_re.transpose(1, 0, 3, 2))
    co_im = pair_diag((-ca_im).transpose(1, 0, 3, 2))
    w_intra, wo_re, wo_im = _s5_expand(k_blk, co_re, co_im)
    return dict(
        w_intra=w_intra, ws_re=ws_re, ws_im=ws_im, wo_re=wo_re, wo_im=wo_im,
        at_re=pw_re[t].reshape(1, SSM_GROUPS * SSM_STATE), at_im=pw_im[t].reshape(1, SSM_GROUPS * SSM_STATE))


def _s5_expand_kernel(k_ref, cre_ref, cim_ref, wi_ref, wore_ref, woim_ref, kcat_ref):
    pw = 2 * SSM_GROUP
    for tau in range(SSM_CHUNK):
        kcat_ref[:, tau * pw:(tau + 1) * pw] = k_ref[0, tau]
        wore_ref[0, :, tau * pw:(tau + 1) * pw] = cre_ref[0, tau].astype(wore_ref.dtype)
        woim_ref[0, :, tau * pw:(tau + 1) * pw] = cim_ref[0, tau].astype(woim_ref.dtype)
    kcat = kcat_ref[...]
    col = lax.broadcasted_iota(jnp.int32, kcat.shape, 1)
    for s in range(SSM_CHUNK):
        blk = kcat if s == 0 else jnp.where(col >= s * pw, pltpu.roll(kcat, s * pw, 1), 0.0)
        wi_ref[0, s * pw:(s + 1) * pw, :] = blk.astype(wi_ref.dtype)


def _s5_expand(k_blk, co_re, co_im):
    n_pairs = k_blk.shape[0]
    pw = 2 * SSM_GROUP
    st = 2 * SSM_STATE
    return pl.pallas_call(
        _s5_expand_kernel,
        grid=(n_pairs,),
        in_specs=[pl.BlockSpec((1, SSM_CHUNK, pw, pw), lambda q: (q, 0, 0, 0)),
                  pl.BlockSpec((1, SSM_CHUNK, st, pw), lambda q: (q, 0, 0, 0)),
                  pl.BlockSpec((1, SSM_CHUNK, st, pw), lambda q: (q, 0, 0, 0))],
        out_specs=[pl.BlockSpec((1, SSM_PAIR, SSM_PAIR), lambda q: (q, 0, 0)),
                   pl.BlockSpec((1, st, SSM_PAIR), lambda q: (q, 0, 0)),
                   pl.BlockSpec((1, st, SSM_PAIR), lambda q: (q, 0, 0))],
        out_shape=[jax.ShapeDtypeStruct((n_pairs, SSM_PAIR, SSM_PAIR), BF16),
                   jax.ShapeDtypeStruct((n_pairs, st, SSM_PAIR), BF16),
                   jax.ShapeDtypeStruct((n_pairs, st, SSM_PAIR), BF16)],
        scratch_shapes=[pltpu.VMEM((pw, SSM_PAIR), F32)],
        compiler_params=_params(("parallel",)),
        name="s5_expand_weights",
    )(k_blk, co_re, co_im)


S5_LANE_PAIRS = LANES // (2 * SSM_GROUP)
S5_SCAN_LANES = 512


def _s5_state_kernel(u0_ref, u1_ref, u2_ref, u3_ref, wre_ref, wim_ref, are_ref, aim_ref,
                     x_ref, hre_ref, him_ref, sre_ref, sim_ref):
    n_chunks = x_ref.shape[0]
    pw = 2 * SSM_GROUP
    u_refs = (u0_ref, u1_ref, u2_ref, u3_ref)
    for t in range(SSM_CHUNK):
        for j, u_ref in enumerate(u_refs):
            ut = u_ref[pl.ds(t, n_chunks, stride=SSM_CHUNK), :]
            for qq in range(S5_LANE_PAIRS):
                q = j * S5_LANE_PAIRS + qq
                x_ref[:, q * SSM_PAIR + t * pw: q * SSM_PAIR + (t + 1) * pw] = (
                    ut[:, qq * pw:(qq + 1) * pw].astype(x_ref.dtype))
    st = 2 * SSM_STATE
    for q in range(SSM_GROUPS // 2):
        xq = x_ref[:, q * SSM_PAIR:(q + 1) * SSM_PAIR]
        sre_ref[:, q * st:(q + 1) * st] = _dot(xq, wre_ref[q])
        sim_ref[:, q * st:(q + 1) * st] = _dot(xq, wim_ref[q])

    row = lax.broadcasted_iota(jnp.int32, (SUBLANES, S5_SCAN_LANES), 0)
    zero = jnp.zeros((SUBLANES, S5_SCAN_LANES), F32)
    for c0 in range(0, sre_ref.shape[1], S5_SCAN_LANES):
        cols = pl.ds(c0, S5_SCAN_LANES)
        ar = are_ref[:, cols]
        ai = aim_ref[:, cols]

        def body(k, carry, cols=cols, ar=ar, ai=ai):
            r0 = pl.multiple_of(k * SUBLANES, SUBLANES)
            sr = sre_ref[pl.ds(r0, SUBLANES), cols]
            si = sim_ref[pl.ds(r0, SUBLANES), cols]
            out_r, out_i = carry
            for i in range(1, SUBLANES + 1):
                tr = ar * out_r - ai * out_i + sr
                ti = ar * out_i + ai * out_r + si
                tr = pltpu.roll(tr, 1, 0)
                ti = pltpu.roll(ti, 1, 0)
                if i < SUBLANES:
                    out_r = jnp.where(row == i, tr, out_r)
                    out_i = jnp.where(row == i, ti, out_i)
            hre_ref[pl.ds(r0, SUBLANES), cols] = out_r
            him_ref[pl.ds(r0, SUBLANES), cols] = out_i
            return tr, ti

        lax.fori_loop(0, n_chunks // SUBLANES, body, (zero, zero))


def _s5_out_kernel(x_ref, wi_ref, hre_ref, him_ref, wore_ref, woim_ref, d_ref, y_ref, yt_ref):
    n_chunks = x_ref.shape[0]
    pw = 2 * SSM_GROUP
    st = 2 * SSM_STATE
    for qq in range(S5_LANE_PAIRS):
        x = x_ref[:, qq * SSM_PAIR:(qq + 1) * SSM_PAIR]
        y = _dot(x, wi_ref[qq])
        y += _dot(hre_ref[:, qq * st:(qq + 1) * st].astype(BF16), wore_ref[qq])
        y += _dot(him_ref[:, qq * st:(qq + 1) * st].astype(BF16), woim_ref[qq])
        y += d_ref[:, qq * SSM_PAIR:(qq + 1) * SSM_PAIR] * x.astype(F32)
        y = jax.nn.gelu(y)
        for t in range(SSM_CHUNK):
            yt_ref[t, :, qq * pw:(qq + 1) * pw] = y[:, t * pw:(t + 1) * pw]
    for t in range(SSM_CHUNK):
        y_ref[pl.ds(t, n_chunks, stride=SSM_CHUNK), :] = yt_ref[t]


def _s5_mixer(u, mats, d, batch, seq):
    t = SSM_CHUNK
    n_chunks = seq // t
    n_pairs = SSM_GROUPS // 2
    cols = n_pairs * SSM_PAIR
    st = 2 * SSM_STATE
    n_state = n_pairs * st
    n_blk = SSM_WIDTH // LANES
    assert n_blk == 4 and n_chunks % SUBLANES == 0
    once = pl.Buffered(1)
    x, h_re, h_im = pl.pallas_call(
        _s5_state_kernel,
        grid=(batch,),
        in_specs=[pl.BlockSpec((None, seq, LANES), lambda b, j=j: (j, b, 0)) for j in range(n_blk)] + [
            pl.BlockSpec((n_pairs, SSM_PAIR, st), lambda b: (0, 0, 0), pipeline_mode=once),
            pl.BlockSpec((n_pairs, SSM_PAIR, st), lambda b: (0, 0, 0), pipeline_mode=once),
            pl.BlockSpec((1, n_state), lambda b: (0, 0)),
            pl.BlockSpec((1, n_state), lambda b: (0, 0))],
        out_specs=[pl.BlockSpec((n_chunks, cols), lambda b: (b, 0)),
                   pl.BlockSpec((n_chunks, n_state), lambda b: (b, 0)),
                   pl.BlockSpec((n_chunks, n_state), lambda b: (b, 0))],
        out_shape=[jax.ShapeDtypeStruct((batch * n_chunks, cols), BF16),
                   jax.ShapeDtypeStruct((batch * n_chunks, n_state), F32),
                   jax.ShapeDtypeStruct((batch * n_chunks, n_state), F32)],
        scratch_shapes=[pltpu.VMEM((n_chunks, n_state), F32), pltpu.VMEM((n_chunks, n_state), F32)],
        compiler_params=_params(("parallel",)),
        name="s5_state_scan",
    )(u, u, u, u, mats['ws_re'], mats['ws_im'], mats['at_re'], mats['at_im'])
    lp = S5_LANE_PAIRS
    d_cols = jnp.broadcast_to(d.astype(F32).reshape(n_pairs, 1, 2 * SSM_GROUP),
                              (n_pairs, t, 2 * SSM_GROUP)).reshape(1, cols)
    return pl.pallas_call(
        _s5_out_kernel,
        grid=(batch, n_blk),
        in_specs=[pl.BlockSpec((n_chunks, lp * SSM_PAIR), lambda b, j: (b, j)),
                  pl.BlockSpec((lp, SSM_PAIR, SSM_PAIR), lambda b, j: (j, 0, 0)),
                  pl.BlockSpec((n_chunks, lp * st), lambda b, j: (b, j)),
                  pl.BlockSpec((n_chunks, lp * st), lambda b, j: (b, j)),
                  pl.BlockSpec((lp, st, SSM_PAIR), lambda b, j: (j, 0, 0)),
                  pl.BlockSpec((lp, st, SSM_PAIR), lambda b, j: (j, 0, 0)),
                  pl.BlockSpec((1, lp * SSM_PAIR), lambda b, j: (0, j))],
        out_specs=pl.BlockSpec((None, seq, LANES), lambda b, j: (j, b, 0)),
        out_shape=jax.ShapeDtypeStruct((n_blk, batch * seq, LANES), F32),
        scratch_shapes=[pltpu.VMEM((t, n_chunks, LANES), F32)],
        compiler_params=_params(("parallel", "parallel")),
        name="s5_out",
    )(x, mats['w_intra'], h_re, h_im, mats['wo_re'], mats['wo_im'], d_cols)


def _even_mix_kernel(ys_ref, bu_ref, bv_ref, h_ref, wglu_ref, lng_ref, lnb_ref, ws_ref, bias_ref,
                     wo_a_ref, wo_b_ref, g1_ref, g2_ref, hout_ref, z_ref, s_scr):
    tm = h_ref.shape[0]
    ys = jnp.concatenate([ys_ref[jb] for jb in range(ys_ref.shape[0])], axis=1)
    ya = ys * jax.nn.sigmoid(_dot(ys.astype(BF16), wglu_ref[...]))
    u = jax.nn.gelu(bu_ref[...].astype(F32))
    v = _layer_norm(jax.nn.gelu(bv_ref[...].astype(F32)), lng_ref[...], lnb_ref[...])
    lane = lax.broadcasted_iota(jnp.int32, v.shape, 1)
    left = (lane % LANES) < SGU_HEAD_DIM
    v_l = jnp.where(left, v, 0.0).astype(BF16)
    v_r = jnp.where(left, 0.0, v).astype(BF16)
    for c in range(tm // SGU_CHUNK):
        rows = slice(c * SGU_CHUNK, (c + 1) * SGU_CHUNK)
        for p in range(SGU_HEADS // 2):
            cols = slice(p * LANES, (p + 1) * LANES)
            s_scr[rows, cols] = (_dot(ws_ref[2 * p], v_l[rows, cols]) + _dot(ws_ref[2 * p + 1], v_r[rows, cols]))
    bias = jnp.concatenate([bias_ref[...]] * (tm // SGU_CHUNK), axis=0)
    yb = u * (s_scr[...] + bias)
    mix = _dot(ya.astype(BF16), wo_a_ref[...]) + _dot(yb.astype(BF16), wo_b_ref[...])
    h_new = h_ref[...] + _rms(mix, g1_ref[...])
    hout_ref[...] = h_new
    z_ref[...] = _rms(h_new, g2_ref[...]).astype(z_ref.dtype)


def _even_mix(ys, proj, h, wglu, lng, lnb, ws, bias, wo_a, wo_b, g1, g2, tm):
    n, d = h.shape
    w = SGU_WIDTH
    const = lambda *shape: pl.BlockSpec(shape, lambda i: (0,) * len(shape))
    return pl.pallas_call(
        _even_mix_kernel,
        grid=(n // tm,),
        in_specs=[pl.BlockSpec((w // LANES, tm, LANES), lambda i: (0, i, 0)),
                  pl.BlockSpec((tm, w), lambda i: (i, 0)),
                  pl.BlockSpec((tm, w), lambda i: (i, 1)),
                  pl.BlockSpec((tm, d), lambda i: (i, 0)),
                  const(w, w), const(1, w), const(1, w),
                  const(SGU_HEADS, SGU_CHUNK, SGU_CHUNK), const(SGU_CHUNK, w),
                  const(w, d), const(w, d), const(1, d), const(1, d)],
        out_specs=[pl.BlockSpec((tm, d), lambda i: (i, 0)),
                   pl.BlockSpec((tm, d), lambda i: (i, 0))],
        out_shape=[jax.ShapeDtypeStruct((n, d), F32), jax.ShapeDtypeStruct((n, d), BF16)],
        scratch_shapes=[pltpu.VMEM((tm, w), F32)],
        compiler_params=_params(("parallel",)),
        name="even_mix",
    )(ys, proj, proj, h, wglu, lng, lnb, ws, bias, wo_a, wo_b, g1, g2)


def _ffn_kernel(z_ref, wg_ref, wu_ref, wd_ref, h_ref, g3_ref, gn_ref, hout_ref, zout_ref, acc_ref):
    j = pl.program_id(1)

    @pl.when(j == 0)
    def _():
        acc_ref[...] = jnp.zeros_like(acc_ref)

    z = z_ref[...]
    a = jax.nn.silu(_dot(z, wg_ref[...])) * _dot(z, wu_ref[...])
    acc_ref[...] += _dot(a.astype(BF16), wd_ref[...])

    @pl.when(j == pl.num_programs(1) - 1)
    def _():
        h_new = h_ref[...] + _rms(acc_ref[...], g3_ref[...])
        hout_ref[...] = h_new
        zout_ref[...] = _rms(h_new, gn_ref[...]).astype(zout_ref.dtype)


def _dense_ffn(z, wg, wu, wd, h, g3, g_next, tm, tf):
    n, d = h.shape
    ff = wg.shape[1]
    return pl.pallas_call(
        _ffn_kernel,
        grid=(n // tm, ff // tf),
        in_specs=[pl.BlockSpec((tm, d), lambda i, j: (i, 0)),
                  pl.BlockSpec((d, tf), lambda i, j: (0, j)),
                  pl.BlockSpec((d, tf), lambda i, j: (0, j)),
                  pl.BlockSpec((tf, d), lambda i, j: (j, 0)),
                  pl.BlockSpec((tm, d), lambda i, j: (i, 0)),
                  pl.BlockSpec((1, d), lambda i, j: (0, 0)),
                  pl.BlockSpec((1, d), lambda i, j: (0, 0))],
        out_specs=[pl.BlockSpec((tm, d), lambda i, j: (i, 0)),
                   pl.BlockSpec((tm, d), lambda i, j: (i, 0))],
        out_shape=[jax.ShapeDtypeStruct((n, d), F32), jax.ShapeDtypeStruct((n, d), BF16)],
        scratch_shapes=[pltpu.VMEM((tm, d), F32)],
        compiler_params=_params(("parallel", "arbitrary")),
        name="dense_ffn",
    )(z, wg, wu, wd, h, g3, g_next)


def _odd_proj_kernel(z_ref, win_ref, gq_ref, gkv_ref, wuq_ref, wuqs_ref, wuk_ref, wuv_ref, vone_ref, cos_ref, sin_ref,
                     zc_ref, q_ref, k_ref, v_ref, *, scale):
    z = z_ref[...]
    proj = _dot(z, win_ref[...])
    c0 = 2 * CONV_CH
    c1 = c0 + MLA_Q_RANK
    c2 = c1 + MLA_KV_RANK
    c3 = c2 + MLA_PAD
    zc_ref[...] = proj[:, :c0].astype(zc_ref.dtype)
    cq = _rms(proj[:, c0:c1], gq_ref[...]).astype(BF16)
    ckv = _rms(proj[:, c1:c2], gkv_ref[...]).astype(BF16)
    cos = cos_ref[...]
    sin = sin_ref[...]
    cos_h = jnp.concatenate([cos] * MLA_HEADS, axis=1)
    sin_h = jnp.concatenate([sin] * MLA_HEADS, axis=1)
    q = _dot(cq, wuq_ref[...]) * cos_h + _dot(cq, wuqs_ref[...]) * sin_h
    q_ref[...] = (q * scale).astype(q_ref.dtype)
    kr = proj[:, c2:c3] * cos + proj[:, c3:] * sin
    k = _dot(ckv, wuk_ref[...]) + jnp.concatenate([kr] * MLA_HEADS, axis=1)
    k_ref[...] = k.astype(k_ref.dtype)
    vt = lax.dot_general(wuv_ref[...], ckv, (((1,), (1,)), ((), ())), preferred_element_type=F32)
    v_ref[0] = (vt + vone_ref[...]).astype(v_ref.dtype)


def _odd_proj(z, win, gq, gkv, wuq, wuqs, wuk, wuv_t, v_one, cos_t, sin_t, seq, tm):
    n, d = z.shape
    hp = MLA_HEADS * MLA_PAD
    vr = MLA_HEADS * MLA_VROWS
    n_l = seq // tm
    const = lambda *shape: pl.BlockSpec(shape, lambda i: (0,) * len(shape))
    out = jax.ShapeDtypeStruct((n, hp), BF16)
    scale = float((MLA_NOPE + MLA_ROPE) ** -0.5 * math.log2(math.e))
    return pl.pallas_call(
        functools.partial(_odd_proj_kernel, scale=scale),
        grid=(n // tm,),
        in_specs=[pl.BlockSpec((tm, d), lambda i: (i, 0)),
                  const(d, win.shape[1]), const(1, MLA_Q_RANK), const(1, MLA_KV_RANK),
                  const(MLA_Q_RANK, hp), const(MLA_Q_RANK, hp), const(MLA_KV_RANK, hp), const(vr, MLA_KV_RANK),
                  const(vr, 1),
                  pl.BlockSpec((tm, MLA_PAD), lambda i: (i % n_l, 0)),
                  pl.BlockSpec((tm, MLA_PAD), lambda i: (i % n_l, 0))],
        out_specs=[pl.BlockSpec((tm, 2 * CONV_CH), lambda i: (i, 0)),
                   pl.BlockSpec((tm, hp), lambda i: (i, 0)),
                   pl.BlockSpec((tm, hp), lambda i: (i, 0)),
                   pl.BlockSpec((1, vr, tm), lambda i: (i, 0, 0))],
        out_shape=[jax.ShapeDtypeStruct((n, 2 * CONV_CH), BF16), out, out,
                   jax.ShapeDtypeStruct((n // tm, vr, tm), BF16)],
        compiler_params=_params(("parallel",)),
        name="odd_in_proj",
    )(z, win, gq, gkv, wuq, wuqs, wuk, wuv_t, v_one, cos_t, sin_t)


def _attn_kernel(q_ref, k_ref, vt_ref, wg_ref, wu_ref, wd_ref, o_ref, wgb_ref, wub_ref, wdb_ref, acc_ref, *, blk):
    i = pl.program_id(2)
    acc_ref[...] = jnp.zeros_like(acc_ref)
    tf = wgb_ref.shape[3]
    for f in range(wgb_ref.shape[1]):
        wgb_ref[0, f] = wg_ref[0, :, f * tf:(f + 1) * tf].astype(BF16)
        wub_ref[0, f] = wu_ref[0, :, f * tf:(f + 1) * tf].astype(BF16)
    wdb_ref[0] = wd_ref[0].astype(BF16)

    def step(j, m, masked):
        r0 = pl.multiple_of(j * blk, blk)
        scores = []
        for hh in range(ATTN_HEADS):
            q = q_ref[0, :, hh * MLA_PAD:(hh + 1) * MLA_PAD]
            k = k_ref[0, pl.ds(r0, blk), hh * MLA_PAD:(hh + 1) * MLA_PAD]
            st = lax.dot_general(k, q, (((1,), (1,)), ((), ())), preferred_element_type=F32)
            if masked:
                key = lax.broadcasted_iota(jnp.int32, st.shape, 0)
                qry = lax.broadcasted_iota(jnp.int32, st.shape, 1)
                st = jnp.where(key <= qry, st, -1e30)
            scores.append(st)
        soft = []
        for hh in range(ATTN_HEADS):
            m_new = jnp.maximum(m[hh], jnp.max(scores[hh], axis=0, keepdims=True))
            soft.append((m_new, jnp.exp2(m[hh] - m_new), jnp.exp2(scores[hh] - m_new).astype(BF16)))
        for hh in range(ATTN_HEADS):
            vt = vt_ref[j, hh * MLA_VROWS:(hh + 1) * MLA_VROWS, :]
            acc_ref[hh] = soft[hh][1] * acc_ref[hh] + _dot(vt, soft[hh][2])
        return tuple(s[0] for s in soft)

    init = jnp.full((1, blk), -1e30, F32)
    m = lax.fori_loop(0, i, lambda j, m: step(j, m, False), (init,) * ATTN_HEADS)
    step(i, m, True)
    ot = jnp.concatenate([acc_ref[hh][:MLA_V] / acc_ref[hh][MLA_V:MLA_V + 1] for hh in range(ATTN_HEADS)], axis=0)
    o_ref[0] = ot.T.astype(o_ref.dtype)


def _attention(q, k, vt, blk, wg, wu, wd, tf):
    b, seq, _ = q.shape
    n_blk = seq // blk
    n_pairs = MLA_HEADS // ATTN_HEADS
    n_e, d, ff = wg.shape
    steps = b * n_pairs * n_blk
    per_e = steps // n_e
    assert steps == per_e * n_e and d % per_e == 0 and ff % per_e == 0
    rows_in, rows_down = d // per_e, ff // per_e
    assert rows_in % 16 == 0 and rows_down % 16 == 0 and ff % tf == 0

    def lin(bi, p, i):
        return (bi * n_pairs + p) * n_blk + i

    w_in = pl.BlockSpec((1, rows_in, ff), lambda bi, p, i: (lin(bi, p, i) // per_e, lin(bi, p, i) % per_e, 0))
    w_out = pl.BlockSpec((1, ff // tf, rows_in, tf),
                         lambda bi, p, i: (lin(bi, p, i) // per_e, 0, lin(bi, p, i) % per_e, 0))
    w_down = pl.BlockSpec((1, rows_down, d), lambda bi, p, i: (lin(bi, p, i) // per_e, lin(bi, p, i) % per_e, 0))
    return pl.pallas_call(
        functools.partial(_attn_kernel, blk=blk),
        grid=(b, n_pairs, n_blk),
        in_specs=[pl.BlockSpec((1, blk, ATTN_HEADS * MLA_PAD), lambda bi, p, i: (bi, i, p)),
                  pl.BlockSpec((1, seq, ATTN_HEADS * MLA_PAD), lambda bi, p, i: (bi, 0, p)),
                  pl.BlockSpec((n_blk, ATTN_HEADS * MLA_VROWS, blk), lambda bi, p, i: (bi, p, 0)),
                  w_in, w_in, w_down],
        out_specs=[pl.BlockSpec((1, blk, ATTN_HEADS * MLA_V), lambda bi, p, i: (bi, i, p)), w_out, w_out, w_down],
        out_shape=[jax.ShapeDtypeStruct((b, seq, MLA_HEADS * MLA_V), BF16),
                   jax.ShapeDtypeStruct((n_e, ff // tf, d, tf), BF16),
                   jax.ShapeDtypeStruct((n_e, ff // tf, d, tf), BF16),
                   jax.ShapeDtypeStruct((n_e, ff, d), BF16)],
        scratch_shapes=[pltpu.VMEM((ATTN_HEADS, MLA_VROWS, blk), F32)],
        compiler_params=_params(("parallel", "parallel", "parallel")),
        name="mla_attention",
    )(q, k, vt, wg, wu, wd)


def _conv_kernel(zc_ref, w_ref, b_ref, lng_ref, lnb_ref, y_ref, buf_ref, part_ref):
    tm = zc_ref.shape[1]

    @pl.when(pl.program_id(1) == 0)
    def _():
        buf_ref[pl.ds(0, CONV_HALO), :] = jnp.zeros((CONV_HALO, CONV_CH), F32)
        buf_ref[pl.ds(CONV_HALO + tm, SUBLANES), :] = jnp.zeros((SUBLANES, CONV_CH), F32)

    zc = zc_ref[0].astype(F32)
    hh = zc[:, :CONV_CH] * jax.nn.sigmoid(zc[:, CONV_CH:])
    buf_ref[pl.ds(CONV_HALO, tm), :] = hh
    off = CONV_HALO - (CONV_TAPS - 1)
    acc = jnp.zeros((tm, CONV_CH), F32) + b_ref[...]
    for b in range(SUBLANES):
        taps = [k for k in range(CONV_TAPS) if (off + k) % SUBLANES == b]
        part = None
        for k in taps:
            term = w_ref[pl.ds(k, 1), :] * buf_ref[pl.ds(off + k - b, tm + SUBLANES), :]
            part = term if part is None else part + term
        if b == 0:
            acc = acc + part[:tm]
        else:
            part_ref[...] = part
            acc = acc + part_ref[pl.ds(b, tm), :]
    buf_ref[pl.ds(0, CONV_HALO), :] = buf_ref[pl.ds(tm, CONV_HALO), :]
    y_ref[0] = jax.nn.silu(_layer_norm(acc, lng_ref[...], lnb_ref[...])).astype(y_ref.dtype)


def _conv_mixer(zc, w, b, lng, lnb, tm):
    bsz, seq, _ = zc.shape
    const = lambda *shape: pl.BlockSpec(shape, lambda bi, i: (0,) * len(shape))
    return pl.pallas_call(
        _conv_kernel,
        grid=(bsz, seq // tm),
        in_specs=[pl.BlockSpec((1, tm, 2 * CONV_CH), lambda bi, i: (bi, i, 0)),
                  const(CONV_HALO, CONV_CH), const(1, CONV_CH), const(1, CONV_CH), const(1, CONV_CH)],
        out_specs=pl.BlockSpec((1, tm, CONV_CH), lambda bi, i: (bi, i, 0)),
        out_shape=jax.ShapeDtypeStruct((bsz, seq, CONV_CH), BF16),
        scratch_shapes=[pltpu.VMEM((CONV_HALO + tm + SUBLANES, CONV_CH), F32),
                        pltpu.VMEM((tm + SUBLANES, CONV_CH), F32)],
        compiler_params=_params(("arbitrary", "arbitrary")),
        name="conv_module",
    )(zc, w, b, lng, lnb)


def _odd_mix_kernel(yc_ref, yd_ref, h_ref, wo_a_ref, wo_b_ref, g1_ref, g2_ref, wr_ref, hout_ref, z_ref, route_ref):
    tm = h_ref.shape[0]
    halves = [slice(0, tm // 2), slice(tm // 2, tm)]
    mixes = [_dot(yc_ref[r, :], wo_a_ref[...]) + _dot(yd_ref[r, :], wo_b_ref[...]) for r in halves]
    zs = []
    for r, mix in zip(halves, mixes):
        h_new = h_ref[r, :] + _rms(mix, g1_ref[...])
        hout_ref[r, :] = h_new
        zs.append(_rms(h_new, g2_ref[...]))
    all_logits = [_dot(z.astype(BF16), wr_ref[...]) for z in zs]
    _store_row_tiles(z_ref, jnp.concatenate(zs, axis=0))
    neg = -jnp.inf
    for r, logits in zip(halves, all_logits):
        lane = lax.broadcasted_iota(jnp.int32, logits.shape, 1)
        logits = jnp.where(lane < N_EXPERTS, logits, neg)
        m1 = jnp.max(logits, axis=-1, keepdims=True)
        i1 = jnp.min(jnp.where(logits == m1, lane, LANES), axis=-1, keepdims=True)
        rest = jnp.where(lane == i1, neg, logits)
        m2 = jnp.max(rest, axis=-1, keepdims=True)
        i2 = jnp.min(jnp.where(rest == m2, lane, LANES), axis=-1, keepdims=True)
        e = jnp.exp(m2 - m1)
        w1 = 1.0 / (1.0 + e)
        w2 = e / (1.0 + e)
        route_ref[r, :] = jnp.where(lane == 0, i1.astype(F32),
                                    jnp.where(lane == 1, i2.astype(F32),
                                              jnp.where(lane == 2, w1, jnp.where(lane == 3, w2, 0.0))))


def _odd_mix(yc, yd, h, wo_a, wo_b, g1, g2, wr, tm):
    n, d = h.shape
    const = lambda *shape: pl.BlockSpec(shape, lambda i: (0,) * len(shape))
    return pl.pallas_call(
        _odd_mix_kernel,
        grid=(n // tm,),
        in_specs=[pl.BlockSpec((tm, yc.shape[1]), lambda i: (i, 0)),
                  pl.BlockSpec((tm, yd.shape[1]), lambda i: (i, 0)),
                  pl.BlockSpec((tm, d), lambda i: (i, 0)),
                  const(*wo_a.shape), const(*wo_b.shape), const(1, d), const(1, d), const(d, LANES)],
        out_specs=[pl.BlockSpec((tm, d), lambda i: (i, 0)),
                   pl.BlockSpec((tm * ROW_TILE, LANES), lambda i: (i, 0)),
                   pl.BlockSpec((tm, LANES), lambda i: (i, 0))],
        out_shape=[jax.ShapeDtypeStruct((n, d), F32), jax.ShapeDtypeStruct((n * ROW_TILE, LANES), F32),
                   jax.ShapeDtypeStruct((n, LANES), F32)],
        compiler_params=_params(("parallel",)),
        name="odd_mix_router",
    )(yc, yd, h, wo_a, wo_b, g1, g2, wr)


def _store_row_tiles(ref, x):
    rows = x.shape[0]
    for s in range(ROW_TILE):
        ref[pl.ds(s, rows, stride=ROW_TILE), :] = x[:, s * LANES:(s + 1) * LANES]


def _load_row_tiles(ref, rows):
    return [ref[pl.ds(s, rows, stride=ROW_TILE), :] for s in range(ROW_TILE)]


def _gather_rows(idx_ref, base, n_rows, src_hbm, dst_ref, sem):
    def body(r, c):
        src = pl.multiple_of(idx_ref[base + r] * ROW_TILE, ROW_TILE)
        dst = pl.multiple_of(r * ROW_TILE, ROW_TILE)
        pltpu.make_async_copy(src_hbm.at[pl.ds(src, ROW_TILE), :], dst_ref.at[pl.ds(dst, ROW_TILE), :], sem).start()
        return c

    lax.fori_loop(0, n_rows, body, 0, unroll=8)


def _wait_rows(src_hbm, dst_ref, sem):
    pltpu.make_async_copy(src_hbm.at[pl.ds(0, dst_ref.shape[0]), :], dst_ref, sem).wait()


def _row_copy(src_ref, src_row, dst_ref, dst_row, sem):
    src = pl.multiple_of(src_row * ROW_TILE, ROW_TILE)
    dst = pl.multiple_of(dst_row * ROW_TILE, ROW_TILE)
    return pltpu.make_async_copy(src_ref.at[pl.ds(src, ROW_TILE), :], dst_ref.at[pl.ds(dst, ROW_TILE), :], sem)


def _moe_ffn_kernel(te_ref, nu_ref, tok_ref, dst_ref, z_hbm, wg_ref, wu_ref, wd_ref, y_hbm,
                    xraw_ref, xb_ref, acc_ref, yst_ref, gsem, ssem, *, rows_per_step):
    i = pl.program_id(0)
    j = pl.program_id(1)
    tm = xb_ref.shape[0]
    stride = yst_ref.shape[0] // ROW_TILE
    n_used = nu_ref[0]
    slot = i % 2
    first = j == 0
    last = j == pl.num_programs(1) - 1

    @pl.when(first & (i == 0))
    def _():
        yst_ref[...] = jnp.zeros_like(yst_ref)
        _gather_rows(tok_ref, 0, stride, z_hbm, xraw_ref.at[0], gsem.at[0])

    @pl.when(first & (i <= n_used))
    def _():
        _wait_rows(z_hbm, xraw_ref.at[slot], gsem.at[slot])

    @pl.when(first & (i < n_used))
    def _():
        for s, blk in enumerate(_load_row_tiles(xraw_ref.at[slot], tm)):
            xb_ref[:, s * LANES:(s + 1) * LANES] = blk.astype(BF16)
        acc_ref[...] = jnp.zeros_like(acc_ref)

    @pl.when(first & (i == n_used))
    def _():
        def body(r, c):
            _row_copy(yst_ref, r, y_hbm, dst_ref[i * stride + r], ssem).start()
            return c
        lax.fori_loop(0, stride, body, 0, unroll=8)

    def multiply(rows, with_copies):
        x = xb_ref[:rows, :]
        g = _dot(x, wg_ref[0, 0])
        u = _dot(x, wu_ref[0, 0])
        if with_copies:
            nxt = xraw_ref.at[1 - slot]
            for rr in range(rows_per_step):
                r = j * rows_per_step + rr
                _row_copy(z_hbm, tok_ref[(i + 1) * stride + r], nxt, r, gsem.at[1 - slot]).start(priority=rr % 2)
                _row_copy(yst_ref, r, y_hbm, dst_ref[i * stride + r], ssem).start(priority=rr % 2)
        a = jax.nn.silu(g) * u
        acc_ref[:rows, :] += _dot(a.astype(BF16), wd_ref[0])

    used = i < n_used
    real = nu_ref[1 + i]
    quarter = tm // 4
    for rows in range(quarter, tm + 1, quarter):
        fits = (real > rows - quarter) & (real <= rows)

        @pl.when(used & fits & jnp.logical_not(last))
        def _(rows=rows):
            multiply(rows, True)

        @pl.when(used & fits & last)
        def _(rows=rows):
            multiply(rows, False)

    @pl.when(last & (i <= n_used))
    def _():
        _wait_rows(z_hbm, yst_ref, ssem)

    @pl.when(last & (i < n_used))
    def _():
        _store_row_tiles(yst_ref, acc_ref[...])


def _moe_ffn(tile_expert, n_used, tok_tab, dst_tab, z_tiles, wg, wu, wd, n_tok, tm):
    n_f, d, tf = wg.shape[1], wg.shape[2], wg.shape[3]
    copy_steps = n_f - 1
    rows_per_step = -(-tm // copy_steps)
    stride = copy_steps * rows_per_step
    n_tiles = tile_expert.shape[0]
    assert tok_tab.shape[0] == dst_tab.shape[0] == (n_tiles + 1) * stride

    def col(i, j, nu):
        return jnp.where(i < nu[0], j, n_f - 1)

    return pl.pallas_call(
        functools.partial(_moe_ffn_kernel, rows_per_step=rows_per_step),
        grid_spec=pltpu.PrefetchScalarGridSpec(
            num_scalar_prefetch=4,
            grid=(n_tiles, n_f),
            in_specs=[pl.BlockSpec(memory_space=pl.ANY),
                      pl.BlockSpec((1, 1, d, tf), lambda i, j, te, nu, tok, dst: (te[i], col(i, j, nu), 0, 0)),
                      pl.BlockSpec((1, 1, d, tf), lambda i, j, te, nu, tok, dst: (te[i], col(i, j, nu), 0, 0)),
                      pl.BlockSpec((1, tf, d), lambda i, j, te, nu, tok, dst: (te[i], col(i, j, nu), 0))],
            out_specs=pl.BlockSpec(memory_space=pl.ANY),
            scratch_shapes=[pltpu.VMEM((2, stride * ROW_TILE, LANES), F32), pltpu.VMEM((tm, d), BF16),
                            pltpu.VMEM((tm, d), F32), pltpu.VMEM((stride * ROW_TILE, LANES), F32),
                            pltpu.SemaphoreType.DMA((2,)), pltpu.SemaphoreType.DMA(())]),
        out_shape=jax.ShapeDtypeStruct(((2 * n_tok + stride) * ROW_TILE, LANES), F32),
        compiler_params=_params(("arbitrary", "arbitrary")),
        name="moe_grouped_ffn",
    )(tile_expert, n_used, tok_tab, dst_tab, z_tiles, wg, wu, wd)


def _combine_kernel(ya_ref, yb_ref, route_ref, h_ref, g_ref, o_ref):
    tm = h_ref.shape[0]
    route = route_ref[...]
    a = jnp.concatenate(_load_row_tiles(ya_ref, tm), axis=1)
    b = jnp.concatenate(_load_row_tiles(yb_ref, tm), axis=1)
    f = route[:, 2:3] * a + route[:, 3:4] * b
    o_ref[...] = h_ref[...] + _rms(f, g_ref[...])


def _combine(y, route, h, g, tm):
    n, d = h.shape
    n_blk = n // tm
    return pl.pallas_call(
        _combine_kernel,
        grid=(n_blk,),
        in_specs=[pl.BlockSpec((tm * ROW_TILE, LANES), lambda i: (i, 0)),
                  pl.BlockSpec((tm * ROW_TILE, LANES), lambda i: (n_blk + i, 0)),
                  pl.BlockSpec((tm, LANES), lambda i: (i, 0)),
                  pl.BlockSpec((tm, d), lambda i: (i, 0)),
                  pl.BlockSpec((1, d), lambda i: (0, 0))],
        out_specs=pl.BlockSpec((tm, d), lambda i: (i, 0)),
        out_shape=jax.ShapeDtypeStruct((n, d), F32),
        compiler_params=_params(("parallel",)),
        name="moe_combine",
    )(y, y, route, h, g)


def _moe_plan(route, tm, stride):
    n = route.shape[0]
    eids = jnp.concatenate([route[:, 0], route[:, 1]]).astype(jnp.int32)
    onehot = (eids[:, None] == jnp.arange(N_EXPERTS, dtype=jnp.int32)[None, :]).astype(jnp.int32)
    csum = jnp.cumsum(onehot, axis=0)
    rank = jnp.sum(csum * onehot, axis=1) - 1
    counts = csum[-1]
    padded = ((counts + tm - 1) // tm) * tm
    ends = jnp.cumsum(padded)
    starts = ends - padded
    slot = jnp.sum(onehot * starts[None, :], axis=1) + rank
    n_tiles = 2 * n // tm + N_EXPERTS + 1
    copy_of_slot = jnp.full((n_tiles * tm,), -1, jnp.int32).at[slot].set(jnp.arange(2 * n, dtype=jnp.int32))
    copy_tab = jnp.pad(copy_of_slot.reshape(n_tiles, tm), ((0, 1), (0, stride - tm)), constant_values=-1)
    tok_tab = jnp.where(copy_tab >= 0, copy_tab % n, 0)
    dump = 2 * n + jnp.arange(stride, dtype=jnp.int32)[None, :]
    dst_tab = jnp.where(copy_tab >= 0, copy_tab, dump)
    dst_tab = jnp.concatenate([jnp.broadcast_to(dump, (1, stride)), dst_tab[:-1]], axis=0)
    n_used = (ends[-1] // tm).astype(jnp.int32)
    tile_start = jnp.minimum(jnp.arange(n_tiles, dtype=jnp.int32), n_used - 1) * tm
    tile_expert = jnp.sum((tile_start[:, None] >= ends[None, :]).astype(jnp.int32), axis=1)
    onehot_e = (tile_expert[:, None] == jnp.arange(N_EXPERTS, dtype=jnp.int32)[None, :]).astype(jnp.int32)
    run_end = jnp.sum(onehot_e * (starts + counts)[None, :], axis=1)
    tile_rows = jnp.clip(run_end - jnp.arange(n_tiles, dtype=jnp.int32) * tm, 0, tm)
    return tok_tab.reshape(-1), dst_tab.reshape(-1), tile_expert, jnp.concatenate([n_used.reshape(1), tile_rows])


def _odd_weights(od_w_in, mla_w_uq, mla_w_ukv):
    c2 = 2 * CONV_CH + MLA_Q_RANK + MLA_KV_RANK
    half = MLA_ROPE // 2
    w_kr = od_w_in[:, c2:]
    w_kr_sw = jnp.concatenate([w_kr[:, half:], w_kr[:, :half]], axis=1)
    zl = jnp.zeros((D_MODEL, MLA_NOPE), F32)
    zr = jnp.zeros((D_MODEL, MLA_PAD - MLA_NOPE - MLA_ROPE), F32)
    win = jnp.concatenate([od_w_in[:, :c2], zl, w_kr, zr, zl, w_kr_sw, zr], axis=1)
    dk = MLA_NOPE + MLA_ROPE
    wq = mla_w_uq.reshape(MLA_Q_RANK, MLA_HEADS, dk)
    zq = jnp.zeros((MLA_Q_RANK, MLA_HEADS, MLA_PAD - dk), F32)
    wuq = jnp.concatenate([wq, zq], axis=2).reshape(MLA_Q_RANK, MLA_HEADS * MLA_PAD)
    wq_sw = jnp.concatenate([jnp.zeros_like(wq[:, :, :MLA_NOPE]), wq[:, :, MLA_NOPE + half:],
                             wq[:, :, MLA_NOPE:MLA_NOPE + half], zq], axis=2)
    wuqs = wq_sw.reshape(MLA_Q_RANK, MLA_HEADS * MLA_PAD)
    wkv = mla_w_ukv.reshape(MLA_KV_RANK, MLA_HEADS, MLA_NOPE + MLA_V)
    zk = jnp.zeros((MLA_KV_RANK, MLA_HEADS, MLA_PAD - MLA_NOPE), F32)
    wuk = jnp.concatenate([wkv[:, :, :MLA_NOPE], zk], axis=2).reshape(MLA_KV_RANK, MLA_HEADS * MLA_PAD)
    zv = jnp.zeros((MLA_KV_RANK, MLA_HEADS, MLA_VROWS - MLA_V), F32)
    wuv_t = jnp.concatenate([wkv[:, :, MLA_NOPE:], zv], axis=2).reshape(MLA_KV_RANK, MLA_HEADS * MLA_VROWS).T
    v_one = jnp.zeros((MLA_HEADS, MLA_VROWS), F32).at[:, MLA_V].set(1.0).reshape(MLA_HEADS * MLA_VROWS, 1)
    return win.astype(BF16), wuq.astype(BF16), wuqs.astype(BF16), wuk.astype(BF16), wuv_t.astype(BF16), v_one


def _rope_tables(seq):
    inv = 1.0 / (ROPE_THETA ** (jnp.arange(0, MLA_ROPE, 2, dtype=F32) / MLA_ROPE))
    ang = jnp.arange(seq, dtype=F32)[:, None] * inv[None, :]
    cos, sin = jnp.cos(ang), jnp.sin(ang)
    ones = jnp.ones((seq, MLA_NOPE), F32)
    zl = jnp.zeros((seq, MLA_NOPE), F32)
    zr = jnp.zeros((seq, MLA_PAD - MLA_NOPE - MLA_ROPE), F32)
    return (jnp.concatenate([ones, cos, cos, zr], axis=1), jnp.concatenate([zl, -sin, sin, zr], axis=1))


def kernel(x, norm_g, ev_w_in, ssm_lambda_re, ssm_lambda_im, ssm_log_dt, ssm_b_re, ssm_b_im, ssm_c_re, ssm_c_im, ssm_d, ssm_w_glu, sgu_ln_g, sgu_ln_b, sgu_w, sgu_b, ev_w_out, ffn_w_gate, ffn_w_up, ffn_w_down, od_w_in, conv_w, conv_b, conv_ln_g, conv_ln_b, mla_q_norm_g, mla_w_uq, mla_kv_norm_g, mla_w_ukv, od_w_out, moe_w_router, moe_w_gate, moe_w_up, moe_w_down):
    bsz, seq, d = x.shape
    n = bsz * seq
    assert d == D_MODEL and SUBLANES % bsz == 0 and seq % 512 == 0
    row = lambda v: v.astype(F32).reshape(1, -1)
    h = x.astype(F32).reshape(n, d)
    tm = 512

    g = norm_g[0]
    a_in, proj = _norm_proj(h, row(g[0]), ev_w_in[0].astype(BF16), 2 * tm)
    mats = _s5_matrices(ssm_lambda_re[0], ssm_lambda_im[0], ssm_log_dt[0], ssm_b_re[0], ssm_b_im[0],
                        ssm_c_re[0], ssm_c_im[0])
    ys = _s5_mixer(a_in, mats, ssm_d[0], bsz, seq)
    causal = jnp.tril(jnp.ones((SGU_CHUNK, SGU_CHUNK), dtype=bool))
    ws = jnp.where(causal[None], sgu_w[0], 0.0).astype(BF16)
    bias = jnp.repeat(sgu_b[0].astype(F32).T, SGU_HEAD_DIM, axis=1)
    wo = ev_w_out[0].astype(BF16)
    h, z = _even_mix(ys, proj, h, ssm_w_glu[0].astype(BF16), row(sgu_ln_g[0]), row(sgu_ln_b[0]), ws, bias,
                     wo[:SSM_WIDTH], wo[SSM_WIDTH:], row(g[1]), row(g[2]), tm)
    h, z = _dense_ffn(z, ffn_w_gate[0].astype(BF16), ffn_w_up[0].astype(BF16), ffn_w_down[0].astype(BF16),
                      h, row(g[3]), row(norm_g[1][0]), 1024, 1024)

    g = norm_g[1]
    win, wuq, wuqs, wuk, wuv_t, v_one = _odd_weights(od_w_in[0], mla_w_uq[0], mla_w_ukv[0])
    cos_t, sin_t = _rope_tables(seq)
    zc, q, k, vt = _odd_proj(z, win, row(mla_q_norm_g[0]), row(mla_kv_norm_g[0]), wuq, wuqs, wuk, wuv_t, v_one,
                             cos_t, sin_t, seq, tm)
    hp = MLA_HEADS * MLA_PAD
    tm_moe, tf_moe = 1024, 512
    yd, wg_b, wu_b, wd_b = _attention(q.reshape(bsz, seq, hp), k.reshape(bsz, seq, hp), vt, tm,
                                      moe_w_gate[0], moe_w_up[0], moe_w_down[0], tf_moe)
    conv_w_pad = jnp.concatenate([conv_w[0].astype(F32), jnp.zeros((CONV_HALO - CONV_TAPS, CONV_CH), F32)], axis=0)
    yc = _conv_mixer(zc.reshape(bsz, seq, 2 * CONV_CH), conv_w_pad, row(conv_b[0]), row(conv_ln_g[0]),
                     row(conv_ln_b[0]), tm)
    wo = od_w_out[0].astype(BF16)
    wr = jnp.concatenate([moe_w_router[0].astype(F32), jnp.zeros((d, LANES - N_EXPERTS), F32)], axis=1)
    h, z, route = _odd_mix(yc.reshape(n, CONV_CH), yd.reshape(n, MLA_HEADS * MLA_V), h, wo[:CONV_CH], wo[CONV_CH:],
                           row(g[1]), row(g[2]), wr.astype(BF16), 2 * tm)
    copy_steps = wg_b.shape[1] - 1
    tok_tab, dst_tab, tile_expert, n_used = _moe_plan(route, tm_moe, copy_steps * -(-tm_moe // copy_steps))
    y = _moe_ffn(tile_expert, n_used, tok_tab, dst_tab, z, wg_b, wu_b, wd_b, n, tm_moe)
    h = _combine(y, route, h, row(g[3]), tm)
    return h.reshape(bsz, seq, d).astype(x.dtype)
```

```python
import functools
import math

import jax
import jax.numpy as jnp
from jax import lax
from jax.experimental import pallas as pl
from jax.experimental.pallas import tpu as pltpu

F32 = jnp.float32
BF16 = jnp.bfloat16

D_MODEL = 1024
NORM_EPS = 1e-6
SSM_WIDTH = 512
SSM_GROUP = 16
SSM_GROUPS = 32
SSM_STATE = 64
SSM_CHUNK = 16
SSM_PAIR = 2 * SSM_GROUP * SSM_CHUNK
SGU_WIDTH = 512
SGU_HEADS = 8
SGU_HEAD_DIM = 64
SGU_CHUNK = 128
CONV_CH = 512
CONV_TAPS = 31
CONV_HALO = 32
MLA_HEADS = 8
MLA_Q_RANK = 256
MLA_KV_RANK = 128
MLA_NOPE = 64
MLA_ROPE = 32
MLA_V = 64
MLA_PAD = 128
MLA_VROWS = 80
ATTN_HEADS = 4
ROPE_THETA = 10000.0
FF_DENSE = 4096
N_EXPERTS = 8
FF_EXPERT = 3584
LANES = 128
SUBLANES = 8
ROW_TILE = D_MODEL // LANES
VMEM_LIMIT = 56 * 1024 * 1024


def _params(sem, vmem=VMEM_LIMIT):
    return pltpu.CompilerParams(dimension_semantics=sem, vmem_limit_bytes=vmem)


def _rms(x, g):
    return x * lax.rsqrt(jnp.mean(x * x, axis=-1, keepdims=True) + NORM_EPS) * g


def _layer_norm(x, g, b):
    mu = jnp.mean(x, axis=-1, keepdims=True)
    xc = x - mu
    return xc * lax.rsqrt(jnp.mean(xc * xc, axis=-1, keepdims=True) + NORM_EPS) * g + b


def _dot(a, b):
    return jnp.dot(a, b, preferred_element_type=F32)


def _norm_proj_kernel(h_ref, g_ref, w_ref, a_ref, b_ref):
    z = _rms(h_ref[...], g_ref[...])
    proj = _dot(z.astype(BF16), w_ref[...])
    for jb in range(SSM_WIDTH // LANES):
        a_ref[jb] = proj[:, jb * LANES:(jb + 1) * LANES]
    b_ref[...] = proj[:, SSM_WIDTH:].astype(b_ref.dtype)


def _norm_proj(h, g, w, tm):
    n, d = h.shape
    cols = w.shape[1]
    return pl.pallas_call(
        _norm_proj_kernel,
        grid=(n // tm,),
        in_specs=[pl.BlockSpec((tm, d), lambda i: (i, 0)),
                  pl.BlockSpec((1, d), lambda i: (0, 0)),
                  pl.BlockSpec((d, cols), lambda i: (0, 0))],
        out_specs=[pl.BlockSpec((SSM_WIDTH // LANES, tm, LANES), lambda i: (0, i, 0)),
                   pl.BlockSpec((tm, cols - SSM_WIDTH), lambda i: (i, 0))],
        out_shape=[jax.ShapeDtypeStruct((SSM_WIDTH // LANES, n, LANES), F32),
                   jax.ShapeDtypeStruct((n, cols - SSM_WIDTH), BF16)],
        compiler_params=_params(("parallel",)),
        name="even_in_proj",
    )(h, g, w)


def _s5_matrices(lam_re, lam_im, log_dt, b_re, b_im, c_re, c_im):
    t = SSM_CHUNK
    lr = jnp.minimum(lam_re.astype(F32), -1e-4)
    li = lam_im.astype(F32)
    dt = jnp.exp(log_dt.astype(F32))[:, None]
    mag = jnp.exp(lr * dt)
    a_re = mag * jnp.cos(li * dt)
    a_im = mag * jnp.sin(li * dt)
    den = lr * lr + li * li
    nr = a_re - 1.0
    coef_re = (nr * lr + a_im * li) / den
    coef_im = (a_im * lr - nr * li) / den
    br = b_re.astype(F32)
    bi = b_im.astype(F32)
    bb_re = coef_re[..., None] * br - coef_im[..., None] * bi
    bb_im = coef_re[..., None] * bi + coef_im[..., None] * br
    cr = c_re.astype(F32)
    ci = c_im.astype(F32)
    pw_re = [jnp.ones_like(a_re)]
    pw_im = [jnp.zeros_like(a_im)]
    for _ in range(t):
        pr, pi = pw_re[-1], pw_im[-1]
        pw_re.append(pr * a_re - pi * a_im)
        pw_im.append(pr * a_im + pi * a_re)
    pw_re = jnp.stack(pw_re)
    pw_im = jnp.stack(pw_im)
    ab_re = pw_re[:t, :, :, None] * bb_re[None] - pw_im[:t, :, :, None] * bb_im[None]
    ab_im = pw_re[:t, :, :, None] * bb_im[None] + pw_im[:t, :, :, None] * bb_re[None]
    hi = lax.Precision.HIGHEST
    k_lag = (jnp.einsum('gnp,tgpm->tgnm', cr, ab_re, precision=hi)
             - jnp.einsum('gnp,tgpm->tgnm', ci, ab_im, precision=hi))
    n_pairs = SSM_GROUPS // 2
    st = 2 * SSM_STATE

    def pair_diag(w):
        w = w.reshape((n_pairs, 2) + w.shape[1:])
        z = jnp.zeros_like(w[:, 0])
        top = jnp.concatenate([w[:, 0], z], axis=-1)
        bot = jnp.concatenate([z, w[:, 1]], axis=-1)
        return jnp.concatenate([top, bot], axis=-2)

    k_blk = pair_diag(k_lag.transpose(1, 0, 3, 2))
    rev_re = pw_re[:t][::-1]
    rev_im = pw_im[:t][::-1]
    ws_re = rev_re[..., None] * bb_re[None] - rev_im[..., None] * bb_im[None]
    ws_im = rev_re[..., None] * bb_im[None] + rev_im[..., None] * bb_re[None]
    ws_re = pair_diag(ws_re.transpose(1, 0, 3, 2)).reshape(n_pairs, SSM_PAIR, st).astype(BF16)
    ws_im = pair_diag(ws_im.transpose(1, 0, 3, 2)).reshape(n_pairs, SSM_PAIR, st).astype(BF16)
    ca_re = cr[None] * pw_re[1:, :, None, :] - ci[None] * pw_im[1:, :, None, :]
    ca_im = cr[None] * pw_im[1:, :, None, :] + ci[None] * pw_re[1:, :, None, :]
    co_re = pair_diag(ca_re.transpose(1, 0, 3, 2))
    co_im = pair_diag((-ca_im).transpose(1, 0, 3, 2))
    w_intra, wo_re, wo_im = _s5_expand(k_blk, co_re, co_im)
    return dict(
        w_intra=w_intra, ws_re=ws_re, ws_im=ws_im, wo_re=wo_re, wo_im=wo_im,
        at_re=pw_re[t].reshape(1, SSM_GROUPS * SSM_STATE), at_im=pw_im[t].reshape(1, SSM_GROUPS * SSM_STATE))


def _s5_expand_kernel(k_ref, cre_ref, cim_ref, wi_ref, wore_ref, woim_ref, kcat_ref):
    pw = 2 * SSM_GROUP
    for tau in range(SSM_CHUNK):
        kcat_ref[:, tau * pw:(tau + 1) * pw] = k_ref[0, tau]
        wore_ref[0, :, tau * pw:(tau + 1) * pw] = cre_ref[0, tau].astype(wore_ref.dtype)
        woim_ref[0, :, tau * pw:(tau + 1) * pw] = cim_ref[0, tau].astype(woim_ref.dtype)
    kcat = kcat_ref[...]
    col = lax.broadcasted_iota(jnp.int32, kcat.shape, 1)
    for s in range(SSM_CHUNK):
        blk = kcat if s == 0 else jnp.where(col >= s * pw, pltpu.roll(kcat, s * pw, 1), 0.0)
        wi_ref[0, s * pw:(s + 1) * pw, :] = blk.astype(wi_ref.dtype)


def _s5_expand(k_blk, co_re, co_im):
    n_pairs = k_blk.shape[0]
    pw = 2 * SSM_GROUP
    st = 2 * SSM_STATE
    return pl.pallas_call(
        _s5_expand_kernel,
        grid=(n_pairs,),
        in_specs=[pl.BlockSpec((1, SSM_CHUNK, pw, pw), lambda q: (q, 0, 0, 0)),
                  pl.BlockSpec((1, SSM_CHUNK, st, pw), lambda q: (q, 0, 0, 0)),
                  pl.BlockSpec((1, SSM_CHUNK, st, pw), lambda q: (q, 0, 0, 0))],
        out_specs=[pl.BlockSpec((1, SSM_PAIR, SSM_PAIR), lambda q: (q, 0, 0)),
                   pl.BlockSpec((1, st, SSM_PAIR), lambda q: (q, 0, 0)),
                   pl.BlockSpec((1, st, SSM_PAIR), lambda q: (q, 0, 0))],
        out_shape=[jax.ShapeDtypeStruct((n_pairs, SSM_PAIR, SSM_PAIR), BF16),
                   jax.ShapeDtypeStruct((n_pairs, st, SSM_PAIR), BF16),
                   jax.ShapeDtypeStruct((n_pairs, st, SSM_PAIR), BF16)],
        scratch_shapes=[pltpu.VMEM((pw, SSM_PAIR), F32)],
        compiler_params=_params(("parallel",)),
        name="s5_expand_weights",
    )(k_blk, co_re, co_im)


S5_LANE_PAIRS = LANES // (2 * SSM_GROUP)
S5_SCAN_LANES = 512


def _s5_state_kernel(u0_ref, u1_ref, u2_ref, u3_ref, wre_ref, wim_ref, are_ref, aim_ref,
                     x_ref, hre_ref, him_ref, sre_ref, sim_ref):
    n_chunks = x_ref.shape[0]
    pw = 2 * SSM_GROUP
    u_refs = (u0_ref, u1_ref, u2_ref, u3_ref)
    for t in range(SSM_CHUNK):
        for j, u_ref in enumerate(u_refs):
            ut = u_ref[pl.ds(t, n_chunks, stride=SSM_CHUNK), :]
            for qq in range(S5_LANE_PAIRS):
                q = j * S5_LANE_PAIRS + qq
                x_ref[:, q * SSM_PAIR + t * pw: q * SSM_PAIR + (t + 1) * pw] = (
                    ut[:, qq * pw:(qq + 1) * pw].astype(x_ref.dtype))
    st = 2 * SSM_STATE
    for q in range(SSM_GROUPS // 2):
        xq = x_ref[:, q * SSM_PAIR:(q + 1) * SSM_PAIR]
        sre_ref[:, q * st:(q + 1) * st] = _dot(xq, wre_ref[q])
        sim_ref[:, q * st:(q + 1) * st] = _dot(xq, wim_ref[q])

    row = lax.broadcasted_iota(jnp.int32, (SUBLANES, S5_SCAN_LANES), 0)
    zero = jnp.zeros((SUBLANES, S5_SCAN_LANES), F32)
    for c0 in range(0, sre_ref.shape[1], S5_SCAN_LANES):
        cols = pl.ds(c0, S5_SCAN_LANES)
        ar = are_ref[:, cols]
        ai = aim_ref[:, cols]

        def body(k, carry, cols=cols, ar=ar, ai=ai):
            r0 = pl.multiple_of(k * SUBLANES, SUBLANES)
            sr = sre_ref[pl.ds(r0, SUBLANES), cols]
            si = sim_ref[pl.ds(r0, SUBLANES), cols]
            out_r, out_i = carry
            for i in range(1, SUBLANES + 1):
                tr = ar * out_r - ai * out_i + sr
                ti = ar * out_i + ai * out_r + si
                tr = pltpu.roll(tr, 1, 0)
                ti = pltpu.roll(ti, 1, 0)
                if i < SUBLANES:
                    out_r = jnp.where(row == i, tr, out_r)
                    out_i = jnp.where(row == i, ti, out_i)
            hre_ref[pl.ds(r0, SUBLANES), cols] = out_r
            him_ref[pl.ds(r0, SUBLANES), cols] = out_i
            return tr, ti

        lax.fori_loop(0, n_chunks // SUBLANES, body, (zero, zero))


def _s5_out_kernel(x_ref, wi_ref, hre_ref, him_ref, wore_ref, woim_ref, d_ref, y_ref, yt_ref):
    n_chunks = x_ref.shape[0]
    pw = 2 * SSM_GROUP
    st = 2 * SSM_STATE
    for qq in range(S5_LANE_PAIRS):
        x = x_ref[:, qq * SSM_PAIR:(qq + 1) * SSM_PAIR]
        y = _dot(x, wi_ref[qq])
        y += _dot(hre_ref[:, qq * st:(qq + 1) * st].astype(BF16), wore_ref[qq])
        y += _dot(him_ref[:, qq * st:(qq + 1) * st].astype(BF16), woim_ref[qq])
        y += d_ref[:, qq * SSM_PAIR:(qq + 1) * SSM_PAIR] * x.astype(F32)
        y = jax.nn.gelu(y)
        for t in range(SSM_CHUNK):
            yt_ref[t, :, qq * pw:(qq + 1) * pw] = y[:, t * pw:(t + 1) * pw]
    for t in range(SSM_CHUNK):
        y_ref[pl.ds(t, n_chunks, stride=SSM_CHUNK), :] = yt_ref[t]


def _s5_mixer(u, mats, d, batch, seq):
    t = SSM_CHUNK
    n_chunks = seq // t
    n_pairs = SSM_GROUPS // 2
    cols = n_pairs * SSM_PAIR
    st = 2 * SSM_STATE
    n_state = n_pairs * st
    n_blk = SSM_WIDTH // LANES
    assert n_blk == 4 and n_chunks % SUBLANES == 0
    once = pl.Buffered(1)
    x, h_re, h_im = pl.pallas_call(
        _s5_state_kernel,
        grid=(batch,),
        in_specs=[pl.BlockSpec((None, seq, LANES), lambda b, j=j: (j, b, 0)) for j in range(n_blk)] + [
            pl.BlockSpec((n_pairs, SSM_PAIR, st), lambda b: (0, 0, 0), pipeline_mode=once),
            pl.BlockSpec((n_pairs, SSM_PAIR, st), lambda b: (0, 0, 0), pipeline_mode=once),
            pl.BlockSpec((1, n_state), lambda b: (0, 0)),
            pl.BlockSpec((1, n_state), lambda b: (0, 0))],
        out_specs=[pl.BlockSpec((n_chunks, cols), lambda b: (b, 0)),
                   pl.BlockSpec((n_chunks, n_state), lambda b: (b, 0)),
                   pl.BlockSpec((n_chunks, n_state), lambda b: (b, 0))],
        out_shape=[jax.ShapeDtypeStruct((batch * n_chunks, cols), BF16),
                   jax.ShapeDtypeStruct((batch * n_chunks, n_state), F32),
                   jax.ShapeDtypeStruct((batch * n_chunks, n_state), F32)],
        scratch_shapes=[pltpu.VMEM((n_chunks, n_state), F32), pltpu.VMEM((n_chunks, n_state), F32)],
        compiler_params=_params(("parallel",)),
        name="s5_state_scan",
    )(u, u, u, u, mats['ws_re'], mats['ws_im'], mats['at_re'], mats['at_im'])
    lp = S5_LANE_PAIRS
    d_cols = jnp.broadcast_to(d.astype(F32).reshape(n_pairs, 1, 2 * SSM_GROUP),
                              (n_pairs, t, 2 * SSM_GROUP)).reshape(1, cols)
    return pl.pallas_call(
        _s5_out_kernel,
        grid=(batch, n_blk),
        in_specs=[pl.BlockSpec((n_chunks, lp * SSM_PAIR), lambda b, j: (b, j)),
                  pl.BlockSpec((lp, SSM_PAIR, SSM_PAIR), lambda b, j: (j, 0, 0)),
                  pl.BlockSpec((n_chunks, lp * st), lambda b, j: (b, j)),
                  pl.BlockSpec((n_chunks, lp * st), lambda b, j: (b, j)),
                  pl.BlockSpec((lp, st, SSM_PAIR), lambda b, j: (j, 0, 0)),
                  pl.BlockSpec((lp, st, SSM_PAIR), lambda b, j: (j, 0, 0)),
                  pl.BlockSpec((1, lp * SSM_PAIR), lambda b, j: (0, j))],
        out_specs=pl.BlockSpec((None, seq, LANES), lambda b, j: (j, b, 0)),
        out_shape=jax.ShapeDtypeStruct((n_blk, batch * seq, LANES), F32),
        scratch_shapes=[pltpu.VMEM((t, n_chunks, LANES), F32)],
        compiler_params=_params(("parallel", "parallel")),
        name="s5_out",
    )(x, mats['w_intra'], h_re, h_im, mats['wo_re'], mats['wo_im'], d_cols)


def _even_mix_kernel(ys_ref, bu_ref, bv_ref, h_ref, wglu_ref, lng_ref, lnb_ref, ws_ref, bias_ref,
                     wo_a_ref, wo_b_ref, g1_ref, g2_ref, hout_ref, z_ref, s_scr):
    tm = h_ref.shape[0]
    ys = jnp.concatenate([ys_ref[jb] for jb in range(ys_ref.shape[0])], axis=1)
    ya = ys * jax.nn.sigmoid(_dot(ys.astype(BF16), wglu_ref[...]))
    u = jax.nn.gelu(bu_ref[...].astype(F32))
    v = _layer_norm(jax.nn.gelu(bv_ref[...].astype(F32)), lng_ref[...], lnb_ref[...])
    lane = lax.broadcasted_iota(jnp.int32, v.shape, 1)
    left = (lane % LANES) < SGU_HEAD_DIM
    v_l = jnp.where(left, v, 0.0).astype(BF16)
    v_r = jnp.where(left, 0.0, v).astype(BF16)
    for c in range(tm // SGU_CHUNK):
        rows = slice(c * SGU_CHUNK, (c + 1) * SGU_CHUNK)
        for p in range(SGU_HEADS // 2):
            cols = slice(p * LANES, (p + 1) * LANES)
            s_scr[rows, cols] = (_dot(ws_ref[2 * p], v_l[rows, cols]) + _dot(ws_ref[2 * p + 1], v_r[rows, cols]))
    bias = jnp.concatenate([bias_ref[...]] * (tm // SGU_CHUNK), axis=0)
    yb = u * (s_scr[...] + bias)
    mix = _dot(ya.astype(BF16), wo_a_ref[...]) + _dot(yb.astype(BF16), wo_b_ref[...])
    h_new = h_ref[...] + _rms(mix, g1_ref[...])
    hout_ref[...] = h_new
    z_ref[...] = _rms(h_new, g2_ref[...]).astype(z_ref.dtype)


def _even_mix(ys, proj, h, wglu, lng, lnb, ws, bias, wo_a, wo_b, g1, g2, tm):
    n, d = h.shape
    w = SGU_WIDTH
    const = lambda *shape: pl.BlockSpec(shape, lambda i: (0,) * len(shape))
    return pl.pallas_call(
        _even_mix_kernel,
        grid=(n // tm,),
        in_specs=[pl.BlockSpec((w // LANES, tm, LANES), lambda i: (0, i, 0)),
                  pl.BlockSpec((tm, w), lambda i: (i, 0)),
                  pl.BlockSpec((tm, w), lambda i: (i, 1)),
                  pl.BlockSpec((tm, d), lambda i: (i, 0)),
                  const(w, w), const(1, w), const(1, w),
                  const(SGU_HEADS, SGU_CHUNK, SGU_CHUNK), const(SGU_CHUNK, w),
                  const(w, d), const(w, d), const(1, d), const(1, d)],
        out_specs=[pl.BlockSpec((tm, d), lambda i: (i, 0)),
                   pl.BlockSpec((tm, d), lambda i: (i, 0))],
        out_shape=[jax.ShapeDtypeStruct((n, d), F32), jax.ShapeDtypeStruct((n, d), BF16)],
        scratch_shapes=[pltpu.VMEM((tm, w), F32)],
        compiler_params=_params(("parallel",)),
        name="even_mix",
    )(ys, proj, proj, h, wglu, lng, lnb, ws, bias, wo_a, wo_b, g1, g2)


def _ffn_kernel(z_ref, wg_ref, wu_ref, wd_ref, h_ref, g3_ref, gn_ref, hout_ref, zout_ref, acc_ref):
    j = pl.program_id(1)

    @pl.when(j == 0)
    def _():
        acc_ref[...] = jnp.zeros_like(acc_ref)

    z = z_ref[...]
    a = jax.nn.silu(_dot(z, wg_ref[...])) * _dot(z, wu_ref[...])
    acc_ref[...] += _dot(a.astype(BF16), wd_ref[...])

    @pl.when(j == pl.num_programs(1) - 1)
    def _():
        h_new = h_ref[...] + _rms(acc_ref[...], g3_ref[...])
        hout_ref[...] = h_new
        zout_ref[...] = _rms(h_new, gn_ref[...]).astype(zout_ref.dtype)


def _dense_ffn(z, wg, wu, wd, h, g3, g_next, tm, tf):
    n, d = h.shape
    ff = wg.shape[1]
    return pl.pallas_call(
        _ffn_kernel,
        grid=(n // tm, ff // tf),
        in_specs=[pl.BlockSpec((tm, d), lambda i, j: (i, 0)),
                  pl.BlockSpec((d, tf), lambda i, j: (0, j)),
                  pl.BlockSpec((d, tf), lambda i, j: (0, j)),
                  pl.BlockSpec((tf, d), lambda i, j: (j, 0)),
                  pl.BlockSpec((tm, d), lambda i, j: (i, 0)),
                  pl.BlockSpec((1, d), lambda i, j: (0, 0)),
                  pl.BlockSpec((1, d), lambda i, j: (0, 0))],
        out_specs=[pl.BlockSpec((tm, d), lambda i, j: (i, 0)),
                   pl.BlockSpec((tm, d), lambda i, j: (i, 0))],
        out_shape=[jax.ShapeDtypeStruct((n, d), F32), jax.ShapeDtypeStruct((n, d), BF16)],
        scratch_shapes=[pltpu.VMEM((tm, d), F32)],
        compiler_params=_params(("parallel", "arbitrary")),
        name="dense_ffn",
    )(z, wg, wu, wd, h, g3, g_next)


def _odd_proj_kernel(z_ref, win_ref, gq_ref, gkv_ref, wuq_ref, wuqs_ref, wuk_ref, wuv_ref, vone_ref, cos_ref, sin_ref,
                     zc_ref, q_ref, k_ref, v_ref, *, scale):
    z = z_ref[...]
    proj = _dot(z, win_ref[...])
    c0 = 2 * CONV_CH
    c1 = c0 + MLA_Q_RANK
    c2 = c1 + MLA_KV_RANK
    c3 = c2 + MLA_PAD
    zc_ref[...] = proj[:, :c0].astype(zc_ref.dtype)
    cq = _rms(proj[:, c0:c1], gq_ref[...]).astype(BF16)
    ckv = _rms(proj[:, c1:c2], gkv_ref[...]).astype(BF16)
    cos = cos_ref[...]
    sin = sin_ref[...]
    cos_h = jnp.concatenate([cos] * MLA_HEADS, axis=1)
    sin_h = jnp.concatenate([sin] * MLA_HEADS, axis=1)
    q = _dot(cq, wuq_ref[...]) * cos_h + _dot(cq, wuqs_ref[...]) * sin_h
    q_ref[...] = (q * scale).astype(q_ref.dtype)
    kr = proj[:, c2:c3] * cos + proj[:, c3:] * sin
    k = _dot(ckv, wuk_ref[...]) + jnp.concatenate([kr] * MLA_HEADS, axis=1)
    k_ref[...] = k.astype(k_ref.dtype)
    vt = lax.dot_general(wuv_ref[...], ckv, (((1,), (1,)), ((), ())), preferred_element_type=F32)
    v_ref[0] = (vt + vone_ref[...]).astype(v_ref.dtype)


def _odd_proj(z, win, gq, gkv, wuq, wuqs, wuk, wuv_t, v_one, cos_t, sin_t, seq, tm):
    n, d = z.shape
    hp = MLA_HEADS * MLA_PAD
    vr = MLA_HEADS * MLA_VROWS
    n_l = seq // tm
    const = lambda *shape: pl.BlockSpec(shape, lambda i: (0,) * len(shape))
    out = jax.ShapeDtypeStruct((n, hp), BF16)
    scale = float((MLA_NOPE + MLA_ROPE) ** -0.5 * math.log2(math.e))
    return pl.pallas_call(
        functools.partial(_odd_proj_kernel, scale=scale),
        grid=(n // tm,),
        in_specs=[pl.BlockSpec((tm, d), lambda i: (i, 0)),
                  const(d, win.shape[1]), const(1, MLA_Q_RANK), const(1, MLA_KV_RANK),
                  const(MLA_Q_RANK, hp), const(MLA_Q_RANK, hp), const(MLA_KV_RANK, hp), const(vr, MLA_KV_RANK),
                  const(vr, 1),
                  pl.BlockSpec((tm, MLA_PAD), lambda i: (i % n_l, 0)),
                  pl.BlockSpec((tm, MLA_PAD), lambda i: (i % n_l, 0))],
        out_specs=[pl.BlockSpec((tm, 2 * CONV_CH), lambda i: (i, 0)),
                   pl.BlockSpec((tm, hp), lambda i: (i, 0)),
                   pl.BlockSpec((tm, hp), lambda i: (i, 0)),
                   pl.BlockSpec((1, vr, tm), lambda i: (i, 0, 0))],
        out_shape=[jax.ShapeDtypeStruct((n, 2 * CONV_CH), BF16), out, out,
                   jax.ShapeDtypeStruct((n // tm, vr, tm), BF16)],
        compiler_params=_params(("parallel",)),
        name="odd_in_proj",
    )(z, win, gq, gkv, wuq, wuqs, wuk, wuv_t, v_one, cos_t, sin_t)


def _attn_kernel(q_ref, k_ref, vt_ref, wg_ref, wu_ref, wd_ref, o_ref, wgb_ref, wub_ref, wdb_ref, acc_ref, *, blk):
    i = pl.program_id(2)
    acc_ref[...] = jnp.zeros_like(acc_ref)
    tf = wgb_ref.shape[3]
    for f in range(wgb_ref.shape[1]):
        wgb_ref[0, f] = wg_ref[0, :, f * tf:(f + 1) * tf].astype(BF16)
        wub_ref[0, f] = wu_ref[0, :, f * tf:(f + 1) * tf].astype(BF16)
    wdb_ref[0] = wd_ref[0].astype(BF16)

    def step(j, m, masked):
        r0 = pl.multiple_of(j * blk, blk)
        scores = []
        for hh in range(ATTN_HEADS):
            q = q_ref[0, :, hh * MLA_PAD:(hh + 1) * MLA_PAD]
            k = k_ref[0, pl.ds(r0, blk), hh * MLA_PAD:(hh + 1) * MLA_PAD]
            st = lax.dot_general(k, q, (((1,), (1,)), ((), ())), preferred_element_type=F32)
            if masked:
                key = lax.broadcasted_iota(jnp.int32, st.shape, 0)
                qry = lax.broadcasted_iota(jnp.int32, st.shape, 1)
                st = jnp.where(key <= qry, st, -1e30)
            scores.append(st)
        soft = []
        for hh in range(ATTN_HEADS):
            m_new = jnp.maximum(m[hh], jnp.max(scores[hh], axis=0, keepdims=True))
            soft.append((m_new, jnp.exp2(m[hh] - m_new), jnp.exp2(scores[hh] - m_new).astype(BF16)))
        for hh in range(ATTN_HEADS):
            vt = vt_ref[j, hh * MLA_VROWS:(hh + 1) * MLA_VROWS, :]
            acc_ref[hh] = soft[hh][1] * acc_ref[hh] + _dot(vt, soft[hh][2])
        return tuple(s[0] for s in soft)

    init = jnp.full((1, blk), -1e30, F32)
    m = lax.fori_loop(0, i, lambda j, m: step(j, m, False), (init,) * ATTN_HEADS)
    step(i, m, True)
    ot = jnp.concatenate([acc_ref[hh][:MLA_V] / acc_ref[hh][MLA_V:MLA_V + 1] for hh in range(ATTN_HEADS)], axis=0)
    o_ref[0] = ot.T.astype(o_ref.dtype)


def _attention(q, k, vt, blk, wg, wu, wd, tf):
    b, seq, _ = q.shape
    n_blk = seq // blk
    n_pairs = MLA_HEADS // ATTN_HEADS
    n_e, d, ff = wg.shape
    steps = b * n_pairs * n_blk
    per_e = steps // n_e
    assert steps == per_e * n_e and d % per_e == 0 and ff % per_e == 0
    rows_in, rows_down = d // per_e, ff // per_e
    assert rows_in % 16 == 0 and rows_down % 16 == 0 and ff % tf == 0

    def lin(bi, p, i):
        return (bi * n_pairs + p) * n_blk + i

    w_in = pl.BlockSpec((1, rows_in, ff), lambda bi, p, i: (lin(bi, p, i) // per_e, lin(bi, p, i) % per_e, 0))
    w_out = pl.BlockSpec((1, ff // tf, rows_in, tf),
                         lambda bi, p, i: (lin(bi, p, i) // per_e, 0, lin(bi, p, i) % per_e, 0))
    w_down = pl.BlockSpec((1, rows_down, d), lambda bi, p, i: (lin(bi, p, i) // per_e, lin(bi, p, i) % per_e, 0))
    return pl.pallas_call(
        functools.partial(_attn_kernel, blk=blk),
        grid=(b, n_pairs, n_blk),
        in_specs=[pl.BlockSpec((1, blk, ATTN_HEADS * MLA_PAD), lambda bi, p, i: (bi, i, p)),
                  pl.BlockSpec((1, seq, ATTN_HEADS * MLA_PAD), lambda bi, p, i: (bi, 0, p)),
                  pl.BlockSpec((n_blk, ATTN_HEADS * MLA_VROWS, blk), lambda bi, p, i: (bi, p, 0)),
                  w_in, w_in, w_down],
        out_specs=[pl.BlockSpec((1, blk, ATTN_HEADS * MLA_V), lambda bi, p, i: (bi, i, p)), w_out, w_out, w_down],
        out_shape=[jax.ShapeDtypeStruct((b, seq, MLA_HEADS * MLA_V), BF16),
                   jax.ShapeDtypeStruct((n_e, ff // tf, d, tf), BF16),
                   jax.ShapeDtypeStruct((n_e, ff // tf, d, tf), BF16),
                   jax.ShapeDtypeStruct((n_e, ff, d), BF16)],
        scratch_shapes=[pltpu.VMEM((ATTN_HEADS, MLA_VROWS, blk), F32)],
        compiler_params=_params(("parallel", "parallel", "parallel")),
        name="mla_attention",
    )(q, k, vt, wg, wu, wd)


def _conv_kernel(zc_ref, w_ref, b_ref, lng_ref, lnb_ref, y_ref, buf_ref, part_ref):
    tm = zc_ref.shape[1]

    @pl.when(pl.program_id(1) == 0)
    def _():
        buf_ref[pl.ds(0, CONV_HALO), :] = jnp.zeros((CONV_HALO, CONV_CH), F32)
        buf_ref[pl.ds(CONV_HALO + tm, SUBLANES), :] = jnp.zeros((SUBLANES, CONV_CH), F32)

    zc = zc_ref[0].astype(F32)
    hh = zc[:, :CONV_CH] * jax.nn.sigmoid(zc[:, CONV_CH:])
    buf_ref[pl.ds(CONV_HALO, tm), :] = hh
    off = CONV_HALO - (CONV_TAPS - 1)
    acc = jnp.zeros((tm, CONV_CH), F32) + b_ref[...]
    for b in range(SUBLANES):
        taps = [k for k in range(CONV_TAPS) if (off + k) % SUBLANES == b]
        part = None
        for k in taps:
            term = w_ref[pl.ds(k, 1), :] * buf_ref[pl.ds(off + k - b, tm + SUBLANES), :]
            part = term if part is None else part + term
        if b == 0:
            acc = acc + part[:tm]
        else:
            part_ref[...] = part
            acc = acc + part_ref[pl.ds(b, tm), :]
    buf_ref[pl.ds(0, CONV_HALO), :] = buf_ref[pl.ds(tm, CONV_HALO), :]
    y_ref[0] = jax.nn.silu(_layer_norm(acc, lng_ref[...], lnb_ref[...])).astype(y_ref.dtype)


def _conv_mixer(zc, w, b, lng, lnb, tm):
    bsz, seq, _ = zc.shape
    const = lambda *shape: pl.BlockSpec(shape, lambda bi, i: (0,) * len(shape))
    return pl.pallas_call(
        _conv_kernel,
        grid=(bsz, seq // tm),
        in_specs=[pl.BlockSpec((1, tm, 2 * CONV_CH), lambda bi, i: (bi, i, 0)),
                  const(CONV_HALO, CONV_CH), const(1, CONV_CH), const(1, CONV_CH), const(1, CONV_CH)],
        out_specs=pl.BlockSpec((1, tm, CONV_CH), lambda bi, i: (bi, i, 0)),
        out_shape=jax.ShapeDtypeStruct((bsz, seq, CONV_CH), BF16),
        scratch_shapes=[pltpu.VMEM((CONV_HALO + tm + SUBLANES, CONV_CH), F32),
                        pltpu.VMEM((tm + SUBLANES, CONV_CH), F32)],
        compiler_params=_params(("arbitrary", "arbitrary")),
        name="conv_module",
    )(zc, w, b, lng, lnb)


def _odd_mix_kernel(yc_ref, yd_ref, h_ref, wo_a_ref, wo_b_ref, g1_ref, g2_ref, wr_ref, hout_ref, z_ref, route_ref):
    tm = h_ref.shape[0]
    halves = [slice(0, tm // 2), slice(tm // 2, tm)]
    mixes = [_dot(yc_ref[r, :], wo_a_ref[...]) + _dot(yd_ref[r, :], wo_b_ref[...]) for r in halves]
    zs = []
    for r, mix in zip(halves, mixes):
        h_new = h_ref[r, :] + _rms(mix, g1_ref[...])
        hout_ref[r, :] = h_new
        zs.append(_rms(h_new, g2_ref[...]))
    all_logits = [_dot(z.astype(BF16), wr_ref[...]) for z in zs]
    _store_row_tiles(z_ref, jnp.concatenate(zs, axis=0))
    neg = -jnp.inf
    for r, logits in zip(halves, all_logits):
        lane = lax.broadcasted_iota(jnp.int32, logits.shape, 1)
        logits = jnp.where(lane < N_EXPERTS, logits, neg)
        m1 = jnp.max(logits, axis=-1, keepdims=True)
        i1 = jnp.min(jnp.where(logits == m1, lane, LANES), axis=-1, keepdims=True)
        rest = jnp.where(lane == i1, neg, logits)
        m2 = jnp.max(rest, axis=-1, keepdims=True)
        i2 = jnp.min(jnp.where(rest == m2, lane, LANES), axis=-1, keepdims=True)
        e = jnp.exp(m2 - m1)
        w1 = 1.0 / (1.0 + e)
        w2 = e / (1.0 + e)
        route_ref[r, :] = jnp.where(lane == 0, i1.astype(F32),
                                    jnp.where(lane == 1, i2.astype(F32),
                                              jnp.where(lane == 2, w1, jnp.where(lane == 3, w2, 0.0))))


def _odd_mix(yc, yd, h, wo_a, wo_b, g1, g2, wr, tm):
    n, d = h.shape
    const = lambda *shape: pl.BlockSpec(shape, lambda i: (0,) * len(shape))
    return pl.pallas_call(
        _odd_mix_kernel,
        grid=(n // tm,),
        in_specs=[pl.BlockSpec((tm, yc.shape[1]), lambda i: (i, 0)),
                  pl.BlockSpec((tm, yd.shape[1]), lambda i: (i, 0)),
                  pl.BlockSpec((tm, d), lambda i: (i, 0)),
                  const(*wo_a.shape), const(*wo_b.shape), const(1, d), const(1, d), const(d, LANES)],
        out_specs=[pl.BlockSpec((tm, d), lambda i: (i, 0)),
                   pl.BlockSpec((tm * ROW_TILE, LANES), lambda i: (i, 0)),
                   pl.BlockSpec((tm, LANES), lambda i: (i, 0))],
        out_shape=[jax.ShapeDtypeStruct((n, d), F32), jax.ShapeDtypeStruct((n * ROW_TILE, LANES), F32),
                   jax.ShapeDtypeStruct((n, LANES), F32)],
        compiler_params=_params(("parallel",)),
        name="odd_mix_router",
    )(yc, yd, h, wo_a, wo_b, g1, g2, wr)


def _store_row_tiles(ref, x):
    rows = x.shape[0]
    for s in range(ROW_TILE):
        ref[pl.ds(s, rows, stride=ROW_TILE), :] = x[:, s * LANES:(s + 1) * LANES]


def _load_row_tiles(ref, rows):
    return [ref[pl.ds(s, rows, stride=ROW_TILE), :] for s in range(ROW_TILE)]


def _gather_rows(idx_ref, base, n_rows, src_hbm, dst_ref, sem):
    def body(r, c):
        src = pl.multiple_of(idx_ref[base + r] * ROW_TILE, ROW_TILE)
        dst = pl.multiple_of(r * ROW_TILE, ROW_TILE)
        pltpu.make_async_copy(src_hbm.at[pl.ds(src, ROW_TILE), :], dst_ref.at[pl.ds(dst, ROW_TILE), :], sem).start()
        return c

    lax.fori_loop(0, n_rows, body, 0, unroll=8)


def _wait_rows(src_hbm, dst_ref, sem):
    pltpu.make_async_copy(src_hbm.at[pl.ds(0, dst_ref.shape[0]), :], dst_ref, sem).wait()


def _row_copy(src_ref, src_row, dst_ref, dst_row, sem):
    src = pl.multiple_of(src_row * ROW_TILE, ROW_TILE)
    dst = pl.multiple_of(dst_row * ROW_TILE, ROW_TILE)
    return pltpu.make_async_copy(src_ref.at[pl.ds(src, ROW_TILE), :], dst_ref.at[pl.ds(dst, ROW_TILE), :], sem)


def _moe_ffn_kernel(te_ref, nu_ref, tok_ref, dst_ref, z_hbm, wg_ref, wu_ref, wd_ref, y_hbm,
                    xraw_ref, xb_ref, acc_ref, yst_ref, gsem, ssem, *, rows_per_step):
    i = pl.program_id(0)
    j = pl.program_id(1)
    tm = xb_ref.shape[0]
    stride = yst_ref.shape[0] // ROW_TILE
    n_used = nu_ref[0]
    slot = i % 2
    first = j == 0
    last = j == pl.num_programs(1) - 1

    @pl.when(first & (i == 0))
    def _():
        yst_ref[...] = jnp.zeros_like(yst_ref)
        _gather_rows(tok_ref, 0, stride, z_hbm, xraw_ref.at[0], gsem.at[0])

    @pl.when(first & (i <= n_used))
    def _():
        _wait_rows(z_hbm, xraw_ref.at[slot], gsem.at[slot])

    @pl.when(first & (i < n_used))
    def _():
        for s, blk in enumerate(_load_row_tiles(xraw_ref.at[slot], tm)):
            xb_ref[:, s * LANES:(s + 1) * LANES] = blk.astype(BF16)
        acc_ref[...] = jnp.zeros_like(acc_ref)

    @pl.when(first & (i == n_used))
    def _():
        def body(r, c):
            _row_copy(yst_ref, r, y_hbm, dst_ref[i * stride + r], ssem).start()
            return c
        lax.fori_loop(0, stride, body, 0, unroll=8)

    def multiply(rows, with_copies):
        x = xb_ref[:rows, :]
        g = _dot(x, wg_ref[0, 0])
        u = _dot(x, wu_ref[0, 0])
        if with_copies:
            nxt = xraw_ref.at[1 - slot]
            for rr in range(rows_per_step):
                r = j * rows_per_step + rr
                _row_copy(z_hbm, tok_ref[(i + 1) * stride + r], nxt, r, gsem.at[1 - slot]).start(priority=rr % 2)
                _row_copy(yst_ref, r, y_hbm, dst_ref[i * stride + r], ssem).start(priority=rr % 2)
        a = jax.nn.silu(g) * u
        acc_ref[:rows, :] += _dot(a.astype(BF16), wd_ref[0])

    used = i < n_used
    real = nu_ref[1 + i]
    quarter = tm // 4
    for rows in range(quarter, tm + 1, quarter):
        fits = (real > rows - quarter) & (real <= rows)

        @pl.when(used & fits & jnp.logical_not(last))
        def _(rows=rows):
            multiply(rows, True)

        @pl.when(used & fits & last)
        def _(rows=rows):
            multiply(rows, False)

    @pl.when(last & (i <= n_used))
    def _():
        _wait_rows(z_hbm, yst_ref, ssem)

    @pl.when(last & (i < n_used))
    def _():
        _store_row_tiles(yst_ref, acc_ref[...])


def _moe_ffn(tile_expert, n_used, tok_tab, dst_tab, z_tiles, wg, wu, wd, n_tok, tm):
    n_f, d, tf = wg.shape[1], wg.shape[2], wg.shape[3]
    copy_steps = n_f - 1
    rows_per_step = -(-tm // copy_steps)
    stride = copy_steps * rows_per_step
    n_tiles = tile_expert.shape[0]
    assert tok_tab.shape[0] == dst_tab.shape[0] == (n_tiles + 1) * stride

    def col(i, j, nu):
        return jnp.where(i < nu[0], j, n_f - 1)

    return pl.pallas_call(
        functools.partial(_moe_ffn_kernel, rows_per_step=rows_per_step),
        grid_spec=pltpu.PrefetchScalarGridSpec(
            num_scalar_prefetch=4,
            grid=(n_tiles, n_f),
            in_specs=[pl.BlockSpec(memory_space=pl.ANY),
                      pl.BlockSpec((1, 1, d, tf), lambda i, j, te, nu, tok, dst: (te[i], col(i, j, nu), 0, 0)),
                      pl.BlockSpec((1, 1, d, tf), lambda i, j, te, nu, tok, dst: (te[i], col(i, j, nu), 0, 0)),
                      pl.BlockSpec((1, tf, d), lambda i, j, te, nu, tok, dst: (te[i], col(i, j, nu), 0))],
            out_specs=pl.BlockSpec(memory_space=pl.ANY),
            scratch_shapes=[pltpu.VMEM((2, stride * ROW_TILE, LANES), F32), pltpu.VMEM((tm, d), BF16),
                            pltpu.VMEM((tm, d), F32), pltpu.VMEM((stride * ROW_TILE, LANES), F32),
                            pltpu.SemaphoreType.DMA((2,)), pltpu.SemaphoreType.DMA(())]),
        out_shape=jax.ShapeDtypeStruct(((2 * n_tok + stride) * ROW_TILE, LANES), F32),
        compiler_params=_params(("arbitrary", "arbitrary")),
        name="moe_grouped_ffn",
    )(tile_expert, n_used, tok_tab, dst_tab, z_tiles, wg, wu, wd)


def _combine_kernel(ya_ref, yb_ref, route_ref, h_ref, g_ref, o_ref):
    tm = h_ref.shape[0]
    route = route_ref[...]
    a = jnp.concatenate(_load_row_tiles(ya_ref, tm), axis=1)
    b = jnp.concatenate(_load_row_tiles(yb_ref, tm), axis=1)
    f = route[:, 2:3] * a + route[:, 3:4] * b
    o_ref[...] = h_ref[...] + _rms(f, g_ref[...])


def _combine(y, route, h, g, tm):
    n, d = h.shape
    n_blk = n // tm
    return pl.pallas_call(
        _combine_kernel,
        grid=(n_blk,),
        in_specs=[pl.BlockSpec((tm * ROW_TILE, LANES), lambda i: (i, 0)),
                  pl.BlockSpec((tm * ROW_TILE, LANES), lambda i: (n_blk + i, 0)),
                  pl.BlockSpec((tm, LANES), lambda i: (i, 0)),
                  pl.BlockSpec((tm, d), lambda i: (i, 0)),
                  pl.BlockSpec((1, d), lambda i: (0, 0))],
        out_specs=pl.BlockSpec((tm, d), lambda i: (i, 0)),
        out_shape=jax.ShapeDtypeStruct((n, d), F32),
        compiler_params=_params(("parallel",)),
        name="moe_combine",
    )(y, y, route, h, g)


def _moe_plan(route, tm, stride):
    n = route.shape[0]
    eids = jnp.concatenate([route[:, 0], route[:, 1]]).astype(jnp.int32)
    onehot = (eids[:, None] == jnp.arange(N_EXPERTS, dtype=jnp.int32)[None, :]).astype(jnp.int32)
    csum = jnp.cumsum(onehot, axis=0)
    rank = jnp.sum(csum * onehot, axis=1) - 1
    counts = csum[-1]
    padded = ((counts + tm - 1) // tm) * tm
    ends = jnp.cumsum(padded)
    starts = ends - padded
    slot = jnp.sum(onehot * starts[None, :], axis=1) + rank
    n_tiles = 2 * n // tm + N_EXPERTS + 1
    copy_of_slot = jnp.full((n_tiles * tm,), -1, jnp.int32).at[slot].set(jnp.arange(2 * n, dtype=jnp.int32))
    copy_tab = jnp.pad(copy_of_slot.reshape(n_tiles, tm), ((0, 1), (0, stride - tm)), constant_values=-1)
    tok_tab = jnp.where(copy_tab >= 0, copy_tab % n, 0)
    dump = 2 * n + jnp.arange(stride, dtype=jnp.int32)[None, :]
    dst_tab = jnp.where(copy_tab >= 0, copy_tab, dump)
    dst_tab = jnp.concatenate([jnp.broadcast_to(dump, (1, stride)), dst_tab[:-1]], axis=0)
    n_used = (ends[-1] // tm).astype(jnp.int32)
    tile_start = jnp.minimum(jnp.arange(n_tiles, dtype=jnp.int32), n_used - 1) * tm
    tile_expert = jnp.sum((tile_start[:, None] >= ends[None, :]).astype(jnp.int32), axis=1)
    onehot_e = (tile_expert[:, None] == jnp.arange(N_EXPERTS, dtype=jnp.int32)[None, :]).astype(jnp.int32)
    run_end = jnp.sum(onehot_e * (starts + counts)[None, :], axis=1)
    tile_rows = jnp.clip(run_end - jnp.arange(n_tiles, dtype=jnp.int32) * tm, 0, tm)
    return tok_tab.reshape(-1), dst_tab.reshape(-1), tile_expert, jnp.concatenate([n_used.reshape(1), tile_rows])


def _odd_weights(od_w_in, mla_w_uq, mla_w_ukv):
    c2 = 2 * CONV_CH + MLA_Q_RANK + MLA_KV_RANK
    half = MLA_ROPE // 2
    w_kr = od_w_in[:, c2:]
    w_kr_sw = jnp.concatenate([w_kr[:, half:], w_kr[:, :half]], axis=1)
    zl = jnp.zeros((D_MODEL, MLA_NOPE), F32)
    zr = jnp.zeros((D_MODEL, MLA_PAD - MLA_NOPE - MLA_ROPE), F32)
    win = jnp.concatenate([od_w_in[:, :c2], zl, w_kr, zr, zl, w_kr_sw, zr], axis=1)
    dk = MLA_NOPE + MLA_ROPE
    wq = mla_w_uq.reshape(MLA_Q_RANK, MLA_HEADS, dk)
    zq = jnp.zeros((MLA_Q_RANK, MLA_HEADS, MLA_PAD - dk), F32)
    wuq = jnp.concatenate([wq, zq], axis=2).reshape(MLA_Q_RANK, MLA_HEADS * MLA_PAD)
    wq_sw = jnp.concatenate([jnp.zeros_like(wq[:, :, :MLA_NOPE]), wq[:, :, MLA_NOPE + half:],
                             wq[:, :, MLA_NOPE:MLA_NOPE + half], zq], axis=2)
    wuqs = wq_sw.reshape(MLA_Q_RANK, MLA_HEADS * MLA_PAD)
    wkv = mla_w_ukv.reshape(MLA_KV_RANK, MLA_HEADS, MLA_NOPE + MLA_V)
    zk = jnp.zeros((MLA_KV_RANK, MLA_HEADS, MLA_PAD - MLA_NOPE), F32)
    wuk = jnp.concatenate([wkv[:, :, :MLA_NOPE], zk], axis=2).reshape(MLA_KV_RANK, MLA_HEADS * MLA_PAD)
    zv = jnp.zeros((MLA_KV_RANK, MLA_HEADS, MLA_VROWS - MLA_V), F32)
    wuv_t = jnp.concatenate([wkv[:, :, MLA_NOPE:], zv], axis=2).reshape(MLA_KV_RANK, MLA_HEADS * MLA_VROWS).T
    v_one = jnp.zeros((MLA_HEADS, MLA_VROWS), F32).at[:, MLA_V].set(1.0).reshape(MLA_HEADS * MLA_VROWS, 1)
    return win.astype(BF16), wuq.astype(BF16), wuqs.astype(BF16), wuk.astype(BF16), wuv_t.astype(BF16), v_one


def _rope_tables(seq):
    inv = 1.0 / (ROPE_THETA ** (jnp.arange(0, MLA_ROPE, 2, dtype=F32) / MLA_ROPE))
    ang = jnp.arange(seq, dtype=F32)[:, None] * inv[None, :]
    cos, sin = jnp.cos(ang), jnp.sin(ang)
    ones = jnp.ones((seq, MLA_NOPE), F32)
    zl = jnp.zeros((seq, MLA_NOPE), F32)
    zr = jnp.zeros((seq, MLA_PAD - MLA_NOPE - MLA_ROPE), F32)
    return (jnp.concatenate([ones, cos, cos, zr], axis=1), jnp.concatenate([zl, -sin, sin, zr], axis=1))


def kernel(x, norm_g, ev_w_in, ssm_lambda_re, ssm_lambda_im, ssm_log_dt, ssm_b_re, ssm_b_im, ssm_c_re, ssm_c_im, ssm_d, ssm_w_glu, sgu_ln_g, sgu_ln_b, sgu_w, sgu_b, ev_w_out, ffn_w_gate, ffn_w_up, ffn_w_down, od_w_in, conv_w, conv_b, conv_ln_g, conv_ln_b, mla_q_norm_g, mla_w_uq, mla_kv_norm_g, mla_w_ukv, od_w_out, moe_w_router, moe_w_gate, moe_w_up, moe_w_down):
    bsz, seq, d = x.shape
    n = bsz * seq
    assert d == D_MODEL and SUBLANES % bsz == 0 and seq % 512 == 0
    row = lambda v: v.astype(F32).reshape(1, -1)
    h = x.astype(F32).reshape(n, d)
    tm = 512

    g = norm_g[0]
    a_in, proj = _norm_proj(h, row(g[0]), ev_w_in[0].astype(BF16), 2 * tm)
    mats = _s5_matrices(ssm_lambda_re[0], ssm_lambda_im[0], ssm_log_dt[0], ssm_b_re[0], ssm_b_im[0],
                        ssm_c_re[0], ssm_c_im[0])
    ys = _s5_mixer(a_in, mats, ssm_d[0], bsz, seq)
    causal = jnp.tril(jnp.ones((SGU_CHUNK, SGU_CHUNK), dtype=bool))
    ws = jnp.where(causal[None], sgu_w[0], 0.0).astype(BF16)
    bias = jnp.repeat(sgu_b[0].astype(F32).T, SGU_HEAD_DIM, axis=1)
    wo = ev_w_out[0].astype(BF16)
    h, z = _even_mix(ys, proj, h, ssm_w_glu[0].astype(BF16), row(sgu_ln_g[0]), row(sgu_ln_b[0]), ws, bias,
                     wo[:SSM_WIDTH], wo[SSM_WIDTH:], row(g[1]), row(g[2]), tm)
    h, z = _dense_ffn(z, ffn_w_gate[0].astype(BF16), ffn_w_up[0].astype(BF16), ffn_w_down[0].astype(BF16),
                      h, row(g[3]), row(norm_g[1][0]), 1024, 1024)

    g = norm_g[1]
    win, wuq, wuqs, wuk, wuv_t, v_one = _odd_weights(od_w_in[0], mla_w_uq[0], mla_w_ukv[0])
    cos_t, sin_t = _rope_tables(seq)
    zc, q, k, vt = _odd_proj(z, win, row(mla_q_norm_g[0]), row(mla_kv_norm_g[0]), wuq, wuqs, wuk, wuv_t, v_one,
                             cos_t, sin_t, seq, tm)
    hp = MLA_HEADS * MLA_PAD
    tm_moe, tf_moe = 1024, 512
    yd, wg_b, wu_b, wd_b = _attention(q.reshape(bsz, seq, hp), k.reshape(bsz, seq, hp), vt, tm,
                                      moe_w_gate[0], moe_w_up[0], moe_w_down[0], tf_moe)
    conv_w_pad = jnp.concatenate([conv_w[0].astype(F32), jnp.zeros((CONV_HALO - CONV_TAPS, CONV_CH), F32)], axis=0)
    yc = _conv_mixer(zc.reshape(bsz, seq, 2 * CONV_CH), conv_w_pad, row(conv_b[0]), row(conv_ln_g[0]),
                     row(conv_ln_b[0]), tm)
    wo = od_w_out[0].astype(BF16)
    wr = jnp.concatenate([moe_w_router[0].astype(F32), jnp.zeros((d, LANES - N_EXPERTS), F32)], axis=1)
    h, z, route = _odd_mix(yc.reshape(n, CONV_CH), yd.reshape(n, MLA_HEADS * MLA_V), h, wo[:CONV_CH], wo[CONV_CH:],
                           row(g[1]), row(g[2]), wr.astype(BF16), 2 * tm)
    copy_steps = wg_b.shape[1] - 1
    tok_tab, dst_tab, tile_expert, n_used = _moe_plan(route, tm_moe, copy_steps * -(-tm_moe // copy_steps))
    y = _moe_ffn(tile_expert, n_used, tok_tab, dst_tab, z, wg_b, wu_b, wd_b, n, tm_moe)
    h = _combine(y, route, h, row(g[3]), 2 * tm)
    return h.reshape(bsz, seq, d).astype(x.dtype)
```

```python
import functools
import math

import jax
import jax.numpy as jnp
from jax import lax
from jax.experimental import pallas as pl
from jax.experimental.pallas import tpu as pltpu

F32 = jnp.float32
BF16 = jnp.bfloat16

D_MODEL = 1024
NORM_EPS = 1e-6
SSM_WIDTH = 512
SSM_GROUP = 16
SSM_GROUPS = 32
SSM_STATE = 64
SSM_CHUNK = 16
SSM_PAIR = 2 * SSM_GROUP * SSM_CHUNK
SGU_WIDTH = 512
SGU_HEADS = 8
SGU_HEAD_DIM = 64
SGU_CHUNK = 128
CONV_CH = 512
CONV_TAPS = 31
CONV_HALO = 32
MLA_HEADS = 8
MLA_Q_RANK = 256
MLA_KV_RANK = 128
MLA_NOPE = 64
MLA_ROPE = 32
MLA_V = 64
MLA_PAD = 128
MLA_VROWS = 80
ATTN_HEADS = 4
ROPE_THETA = 10000.0
FF_DENSE = 4096
N_EXPERTS = 8
FF_EXPERT = 3584
LANES = 128
SUBLANES = 8
ROW_TILE = D_MODEL // LANES
VMEM_LIMIT = 56 * 1024 * 1024


def _params(sem, vmem=VMEM_LIMIT):
    return pltpu.CompilerParams(dimension_semantics=sem, vmem_limit_bytes=vmem)


def _rms(x, g):
    return x * lax.rsqrt(jnp.mean(x * x, axis=-1, keepdims=True) + NORM_EPS) * g


def _layer_norm(x, g, b):
    mu = jnp.mean(x, axis=-1, keepdims=True)
    xc = x - mu
    return xc * lax.rsqrt(jnp.mean(xc * xc, axis=-1, keepdims=True) + NORM_EPS) * g + b


def _dot(a, b):
    return jnp.dot(a, b, preferred_element_type=F32)


def _norm_proj_kernel(h_ref, g_ref, w_ref, a_ref, b_ref):
    z = _rms(h_ref[...], g_ref[...])
    proj = _dot(z.astype(BF16), w_ref[...])
    for jb in range(SSM_WIDTH // LANES):
        a_ref[jb] = proj[:, jb * LANES:(jb + 1) * LANES]
    b_ref[...] = proj[:, SSM_WIDTH:].astype(b_ref.dtype)


def _norm_proj(h, g, w, tm):
    n, d = h.shape
    cols = w.shape[1]
    return pl.pallas_call(
        _norm_proj_kernel,
        grid=(n // tm,),
        in_specs=[pl.BlockSpec((tm, d), lambda i: (i, 0)),
                  pl.BlockSpec((1, d), lambda i: (0, 0)),
                  pl.BlockSpec((d, cols), lambda i: (0, 0))],
        out_specs=[pl.BlockSpec((SSM_WIDTH // LANES, tm, LANES), lambda i: (0, i, 0)),
                   pl.BlockSpec((tm, cols - SSM_WIDTH), lambda i: (i, 0))],
        out_shape=[jax.ShapeDtypeStruct((SSM_WIDTH // LANES, n, LANES), F32),
                   jax.ShapeDtypeStruct((n, cols - SSM_WIDTH), BF16)],
        compiler_params=_params(("parallel",)),
        name="even_in_proj",
    )(h, g, w)


def _s5_matrices(lam_re, lam_im, log_dt, b_re, b_im, c_re, c_im):
    t = SSM_CHUNK
    lr = jnp.minimum(lam_re.astype(F32), -1e-4)
    li = lam_im.astype(F32)
    dt = jnp.exp(log_dt.astype(F32))[:, None]
    mag = jnp.exp(lr * dt)
    a_re = mag * jnp.cos(li * dt)
    a_im = mag * jnp.sin(li * dt)
    den = lr * lr + li * li
    nr = a_re - 1.0
    coef_re = (nr * lr + a_im * li) / den
    coef_im = (a_im * lr - nr * li) / den
    br = b_re.astype(F32)
    bi = b_im.astype(F32)
    bb_re = coef_re[..., None] * br - coef_im[..., None] * bi
    bb_im = coef_re[..., None] * bi + coef_im[..., None] * br
    cr = c_re.astype(F32)
    ci = c_im.astype(F32)
    pw_re = [jnp.ones_like(a_re)]
    pw_im = [jnp.zeros_like(a_im)]
    for _ in range(t):
        pr, pi = pw_re[-1], pw_im[-1]
        pw_re.append(pr * a_re - pi * a_im)
        pw_im.append(pr * a_im + pi * a_re)
    pw_re = jnp.stack(pw_re)
    pw_im = jnp.stack(pw_im)
    ab_re = pw_re[:t, :, :, None] * bb_re[None] - pw_im[:t, :, :, None] * bb_im[None]
    ab_im = pw_re[:t, :, :, None] * bb_im[None] + pw_im[:t, :, :, None] * bb_re[None]
    hi = lax.Precision.HIGHEST
    k_lag = (jnp.einsum('gnp,tgpm->tgnm', cr, ab_re, precision=hi)
             - jnp.einsum('gnp,tgpm->tgnm', ci, ab_im, precision=hi))
    n_pairs = SSM_GROUPS // 2
    st = 2 * SSM_STATE

    def pair_diag(w):
        w = w.reshape((n_pairs, 2) + w.shape[1:])
        z = jnp.zeros_like(w[:, 0])
        top = jnp.concatenate([w[:, 0], z], axis=-1)
        bot = jnp.concatenate([z, w[:, 1]], axis=-1)
        return jnp.concatenate([top, bot], axis=-2)

    k_blk = pair_diag(k_lag.transpose(1, 0, 3, 2))
    rev_re = pw_re[:t][::-1]
    rev_im = pw_im[:t][::-1]
    ws_re = rev_re[..., None] * bb_re[None] - rev_im[..., None] * bb_im[None]
    ws_im = rev_re[..., None] * bb_im[None] + rev_im[..., None] * bb_re[None]
    ws_re = pair_diag(ws_re.transpose(1, 0, 3, 2)).reshape(n_pairs, SSM_PAIR, st).astype(BF16)
    ws_im = pair_diag(ws_im.transpose(1, 0, 3, 2)).reshape(n_pairs, SSM_PAIR, st).astype(BF16)
    ca_re = cr[None] * pw_re[1:, :, None, :] - ci[None] * pw_im[1:, :, None, :]
    ca_im = cr[None] * pw_im[1:, :, None, :] + ci[None] * pw_re[1:, :, None, :]
    co_re = pair_diag(ca_re.transpose(1, 0, 3, 2))
    co_im = pair_diag((-ca_im).transpose(1, 0, 3, 2))
    w_intra, wo_re, wo_im = _s5_expand(k_blk, co_re, co_im)
    return dict(
        w_intra=w_intra, ws_re=ws_re, ws_im=ws_im, wo_re=wo_re, wo_im=wo_im,
        at_re=pw_re[t].reshape(1, SSM_GROUPS * SSM_STATE), at_im=pw_im[t].reshape(1, SSM_GROUPS * SSM_STATE))


def _s5_expand_kernel(k_ref, cre_ref, cim_ref, wi_ref, wore_ref, woim_ref, kcat_ref):
    pw = 2 * SSM_GROUP
    for tau in range(SSM_CHUNK):
        kcat_ref[:, tau * pw:(tau + 1) * pw] = k_ref[0, tau]
        wore_ref[0, :, tau * pw:(tau + 1) * pw] = cre_ref[0, tau].astype(wore_ref.dtype)
        woim_ref[0, :, tau * pw:(tau + 1) * pw] = cim_ref[0, tau].astype(woim_ref.dtype)
    kcat = kcat_ref[...]
    col = lax.broadcasted_iota(jnp.int32, kcat.shape, 1)
    for s in range(SSM_CHUNK):
        blk = kcat if s == 0 else jnp.where(col >= s * pw, pltpu.roll(kcat, s * pw, 1), 0.0)
        wi_ref[0, s * pw:(s + 1) * pw, :] = blk.astype(wi_ref.dtype)


def _s5_expand(k_blk, co_re, co_im):
    n_pairs = k_blk.shape[0]
    pw = 2 * SSM_GROUP
    st = 2 * SSM_STATE
    return pl.pallas_call(
        _s5_expand_kernel,
        grid=(n_pairs,),
        in_specs=[pl.BlockSpec((1, SSM_CHUNK, pw, pw), lambda q: (q, 0, 0, 0)),
                  pl.BlockSpec((1, SSM_CHUNK, st, pw), lambda q: (q, 0, 0, 0)),
                  pl.BlockSpec((1, SSM_CHUNK, st, pw), lambda q: (q, 0, 0, 0))],
        out_specs=[pl.BlockSpec((1, SSM_PAIR, SSM_PAIR), lambda q: (q, 0, 0)),
                   pl.BlockSpec((1, st, SSM_PAIR), lambda q: (q, 0, 0)),
                   pl.BlockSpec((1, st, SSM_PAIR), lambda q: (q, 0, 0))],
        out_shape=[jax.ShapeDtypeStruct((n_pairs, SSM_PAIR, SSM_PAIR), BF16),
                   jax.ShapeDtypeStruct((n_pairs, st, SSM_PAIR), BF16),
                   jax.ShapeDtypeStruct((n_pairs, st, SSM_PAIR), BF16)],
        scratch_shapes=[pltpu.VMEM((pw, SSM_PAIR), F32)],
        compiler_params=_params(("parallel",)),
        name="s5_expand_weights",
    )(k_blk, co_re, co_im)


S5_LANE_PAIRS = LANES // (2 * SSM_GROUP)
S5_SCAN_LANES = 512


def _s5_state_kernel(u0_ref, u1_ref, u2_ref, u3_ref, wre_ref, wim_ref, are_ref, aim_ref,
                     x_ref, hre_ref, him_ref, sre_ref, sim_ref):
    n_chunks = x_ref.shape[0]
    pw = 2 * SSM_GROUP
    u_refs = (u0_ref, u1_ref, u2_ref, u3_ref)
    for t in range(SSM_CHUNK):
        for j, u_ref in enumerate(u_refs):
            ut = u_ref[pl.ds(t, n_chunks, stride=SSM_CHUNK), :]
            for qq in range(S5_LANE_PAIRS):
                q = j * S5_LANE_PAIRS + qq
                x_ref[:, q * SSM_PAIR + t * pw: q * SSM_PAIR + (t + 1) * pw] = (
                    ut[:, qq * pw:(qq + 1) * pw].astype(x_ref.dtype))
    st = 2 * SSM_STATE
    for q in range(SSM_GROUPS // 2):
        xq = x_ref[:, q * SSM_PAIR:(q + 1) * SSM_PAIR]
        sre_ref[:, q * st:(q + 1) * st] = _dot(xq, wre_ref[q])
        sim_ref[:, q * st:(q + 1) * st] = _dot(xq, wim_ref[q])

    row = lax.broadcasted_iota(jnp.int32, (SUBLANES, S5_SCAN_LANES), 0)
    zero = jnp.zeros((SUBLANES, S5_SCAN_LANES), F32)
    for c0 in range(0, sre_ref.shape[1], S5_SCAN_LANES):
        cols = pl.ds(c0, S5_SCAN_LANES)
        ar = are_ref[:, cols]
        ai = aim_ref[:, cols]

        def body(k, carry, cols=cols, ar=ar, ai=ai):
            r0 = pl.multiple_of(k * SUBLANES, SUBLANES)
            sr = sre_ref[pl.ds(r0, SUBLANES), cols]
            si = sim_ref[pl.ds(r0, SUBLANES), cols]
            out_r, out_i = carry
            for i in range(1, SUBLANES + 1):
                tr = ar * out_r - ai * out_i + sr
                ti = ar * out_i + ai * out_r + si
                tr = pltpu.roll(tr, 1, 0)
                ti = pltpu.roll(ti, 1, 0)
                if i < SUBLANES:
                    out_r = jnp.where(row == i, tr, out_r)
                    out_i = jnp.where(row == i, ti, out_i)
            hre_ref[pl.ds(r0, SUBLANES), cols] = out_r
            him_ref[pl.ds(r0, SUBLANES), cols] = out_i
            return tr, ti

        lax.fori_loop(0, n_chunks // SUBLANES, body, (zero, zero))


def _s5_out_kernel(x_ref, wi_ref, hre_ref, him_ref, wore_ref, woim_ref, d_ref, y_ref, yt_ref):
    n_chunks = x_ref.shape[0]
    pw = 2 * SSM_GROUP
    st = 2 * SSM_STATE
    for qq in range(S5_LANE_PAIRS):
        x = x_ref[:, qq * SSM_PAIR:(qq + 1) * SSM_PAIR]
        y = _dot(x, wi_ref[qq])
        y += _dot(hre_ref[:, qq * st:(qq + 1) * st].astype(BF16), wore_ref[qq])
        y += _dot(him_ref[:, qq * st:(qq + 1) * st].astype(BF16), woim_ref[qq])
        y += d_ref[:, qq * SSM_PAIR:(qq + 1) * SSM_PAIR] * x.astype(F32)
        y = jax.nn.gelu(y)
        for t in range(SSM_CHUNK):
            yt_ref[t, :, qq * pw:(qq + 1) * pw] = y[:, t * pw:(t + 1) * pw]
    for t in range(SSM_CHUNK):
        y_ref[pl.ds(t, n_chunks, stride=SSM_CHUNK), :] = yt_ref[t]


def _s5_mixer(u, mats, d, batch, seq):
    t = SSM_CHUNK
    n_chunks = seq // t
    n_pairs = SSM_GROUPS // 2
    cols = n_pairs * SSM_PAIR
    st = 2 * SSM_STATE
    n_state = n_pairs * st
    n_blk = SSM_WIDTH // LANES
    assert n_blk == 4 and n_chunks % SUBLANES == 0
    once = pl.Buffered(1)
    x, h_re, h_im = pl.pallas_call(
        _s5_state_kernel,
        grid=(batch,),
        in_specs=[pl.BlockSpec((None, seq, LANES), lambda b, j=j: (j, b, 0)) for j in range(n_blk)] + [
            pl.BlockSpec((n_pairs, SSM_PAIR, st), lambda b: (0, 0, 0), pipeline_mode=once),
            pl.BlockSpec((n_pairs, SSM_PAIR, st), lambda b: (0, 0, 0), pipeline_mode=once),
            pl.BlockSpec((1, n_state), lambda b: (0, 0)),
            pl.BlockSpec((1, n_state), lambda b: (0, 0))],
        out_specs=[pl.BlockSpec((n_chunks, cols), lambda b: (b, 0)),
                   pl.BlockSpec((n_chunks, n_state), lambda b: (b, 0)),
                   pl.BlockSpec((n_chunks, n_state), lambda b: (b, 0))],
        out_shape=[jax.ShapeDtypeStruct((batch * n_chunks, cols), BF16),
                   jax.ShapeDtypeStruct((batch * n_chunks, n_state), F32),
                   jax.ShapeDtypeStruct((batch * n_chunks, n_state), F32)],
        scratch_shapes=[pltpu.VMEM((n_chunks, n_state), F32), pltpu.VMEM((n_chunks, n_state), F32)],
        compiler_params=_params(("parallel",)),
        name="s5_state_scan",
    )(u, u, u, u, mats['ws_re'], mats['ws_im'], mats['at_re'], mats['at_im'])
    lp = S5_LANE_PAIRS
    d_cols = jnp.broadcast_to(d.astype(F32).reshape(n_pairs, 1, 2 * SSM_GROUP),
                              (n_pairs, t, 2 * SSM_GROUP)).reshape(1, cols)
    return pl.pallas_call(
        _s5_out_kernel,
        grid=(batch, n_blk),
        in_specs=[pl.BlockSpec((n_chunks, lp * SSM_PAIR), lambda b, j: (b, j)),
                  pl.BlockSpec((lp, SSM_PAIR, SSM_PAIR), lambda b, j: (j, 0, 0)),
                  pl.BlockSpec((n_chunks, lp * st), lambda b, j: (b, j)),
                  pl.BlockSpec((n_chunks, lp * st), lambda b, j: (b, j)),
                  pl.BlockSpec((lp, st, SSM_PAIR), lambda b, j: (j, 0, 0)),
                  pl.BlockSpec((lp, st, SSM_PAIR), lambda b, j: (j, 0, 0)),
                  pl.BlockSpec((1, lp * SSM_PAIR), lambda b, j: (0, j))],
        out_specs=pl.BlockSpec((None, seq, LANES), lambda b, j: (j, b, 0)),
        out_shape=jax.ShapeDtypeStruct((n_blk, batch * seq, LANES), F32),
        scratch_shapes=[pltpu.VMEM((t, n_chunks, LANES), F32)],
        compiler_params=_params(("parallel", "parallel")),
        name="s5_out",
    )(x, mats['w_intra'], h_re, h_im, mats['wo_re'], mats['wo_im'], d_cols)


def _even_mix_kernel(ys_ref, bu_ref, bv_ref, h_ref, wglu_ref, lng_ref, lnb_ref, ws_ref, bias_ref,
                     wo_a_ref, wo_b_ref, g1_ref, g2_ref, hout_ref, z_ref, s_scr):
    tm = h_ref.shape[0]
    ys = jnp.concatenate([ys_ref[jb] for jb in range(ys_ref.shape[0])], axis=1)
    ya = ys * jax.nn.sigmoid(_dot(ys.astype(BF16), wglu_ref[...]))
    u = jax.nn.gelu(bu_ref[...].astype(F32))
    v = _layer_norm(jax.nn.gelu(bv_ref[...].astype(F32)), lng_ref[...], lnb_ref[...])
    lane = lax.broadcasted_iota(jnp.int32, v.shape, 1)
    left = (lane % LANES) < SGU_HEAD_DIM
    v_l = jnp.where(left, v, 0.0).astype(BF16)
    v_r = jnp.where(left, 0.0, v).astype(BF16)
    for c in range(tm // SGU_CHUNK):
        rows = slice(c * SGU_CHUNK, (c + 1) * SGU_CHUNK)
        for p in range(SGU_HEADS // 2):
            cols = slice(p * LANES, (p + 1) * LANES)
            s_scr[rows, cols] = (_dot(ws_ref[2 * p], v_l[rows, cols]) + _dot(ws_ref[2 * p + 1], v_r[rows, cols]))
    bias = jnp.concatenate([bias_ref[...]] * (tm // SGU_CHUNK), axis=0)
    yb = u * (s_scr[...] + bias)
    mix = _dot(ya.astype(BF16), wo_a_ref[...]) + _dot(yb.astype(BF16), wo_b_ref[...])
    h_new = h_ref[...] + _rms(mix, g1_ref[...])
    hout_ref[...] = h_new
    z_ref[...] = _rms(h_new, g2_ref[...]).astype(z_ref.dtype)


def _even_mix(ys, proj, h, wglu, lng, lnb, ws, bias, wo_a, wo_b, g1, g2, tm):
    n, d = h.shape
    w = SGU_WIDTH
    const = lambda *shape: pl.BlockSpec(shape, lambda i: (0,) * len(shape))
    return pl.pallas_call(
        _even_mix_kernel,
        grid=(n // tm,),
        in_specs=[pl.BlockSpec((w // LANES, tm, LANES), lambda i: (0, i, 0)),
                  pl.BlockSpec((tm, w), lambda i: (i, 0)),
                  pl.BlockSpec((tm, w), lambda i: (i, 1)),
                  pl.BlockSpec((tm, d), lambda i: (i, 0)),
                  const(w, w), const(1, w), const(1, w),
                  const(SGU_HEADS, SGU_CHUNK, SGU_CHUNK), const(SGU_CHUNK, w),
                  const(w, d), const(w, d), const(1, d), const(1, d)],
        out_specs=[pl.BlockSpec((tm, d), lambda i: (i, 0)),
                   pl.BlockSpec((tm, d), lambda i: (i, 0))],
        out_shape=[jax.ShapeDtypeStruct((n, d), F32), jax.ShapeDtypeStruct((n, d), BF16)],
        scratch_shapes=[pltpu.VMEM((tm, w), F32)],
        compiler_params=_params(("parallel",)),
        name="even_mix",
    )(ys, proj, proj, h, wglu, lng, lnb, ws, bias, wo_a, wo_b, g1, g2)


def _ffn_kernel(z_ref, wg_ref, wu_ref, wd_ref, h_ref, g3_ref, gn_ref, hout_ref, zout_ref, acc_ref):
    j = pl.program_id(1)

    @pl.when(j == 0)
    def _():
        acc_ref[...] = jnp.zeros_like(acc_ref)

    z = z_ref[...]
    a = jax.nn.silu(_dot(z, wg_ref[...])) * _dot(z, wu_ref[...])
    acc_ref[...] += _dot(a.astype(BF16), wd_ref[...])

    @pl.when(j == pl.num_programs(1) - 1)
    def _():
        h_new = h_ref[...] + _rms(acc_ref[...], g3_ref[...])
        hout_ref[...] = h_new
        zout_ref[...] = _rms(h_new, gn_ref[...]).astype(zout_ref.dtype)


def _dense_ffn(z, wg, wu, wd, h, g3, g_next, tm, tf):
    n, d = h.shape
    ff = wg.shape[1]
    return pl.pallas_call(
        _ffn_kernel,
        grid=(n // tm, ff // tf),
        in_specs=[pl.BlockSpec((tm, d), lambda i, j: (i, 0)),
                  pl.BlockSpec((d, tf), lambda i, j: (0, j)),
                  pl.BlockSpec((d, tf), lambda i, j: (0, j)),
                  pl.BlockSpec((tf, d), lambda i, j: (j, 0)),
                  pl.BlockSpec((tm, d), lambda i, j: (i, 0)),
                  pl.BlockSpec((1, d), lambda i, j: (0, 0)),
                  pl.BlockSpec((1, d), lambda i, j: (0, 0))],
        out_specs=[pl.BlockSpec((tm, d), lambda i, j: (i, 0)),
                   pl.BlockSpec((tm, d), lambda i, j: (i, 0))],
        out_shape=[jax.ShapeDtypeStruct((n, d), F32), jax.ShapeDtypeStruct((n, d), BF16)],
        scratch_shapes=[pltpu.VMEM((tm, d), F32)],
        compiler_params=_params(("parallel", "arbitrary")),
        name="dense_ffn",
    )(z, wg, wu, wd, h, g3, g_next)


def _odd_proj_kernel(z_ref, win_ref, gq_ref, gkv_ref, wuq_ref, wuqs_ref, wuk_ref, wuv_ref, vone_ref, cos_ref, sin_ref,
                     zc_ref, q_ref, k_ref, v_ref, *, scale):
    z = z_ref[...]
    proj = _dot(z, win_ref[...])
    c0 = 2 * CONV_CH
    c1 = c0 + MLA_Q_RANK
    c2 = c1 + MLA_KV_RANK
    c3 = c2 + MLA_PAD
    zc_ref[...] = proj[:, :c0].astype(zc_ref.dtype)
    cq = _rms(proj[:, c0:c1], gq_ref[...]).astype(BF16)
    ckv = _rms(proj[:, c1:c2], gkv_ref[...]).astype(BF16)
    cos = cos_ref[...]
    sin = sin_ref[...]
    cos_h = jnp.concatenate([cos] * MLA_HEADS, axis=1)
    sin_h = jnp.concatenate([sin] * MLA_HEADS, axis=1)
    q = _dot(cq, wuq_ref[...]) * cos_h + _dot(cq, wuqs_ref[...]) * sin_h
    q_ref[...] = (q * scale).astype(q_ref.dtype)
    kr = proj[:, c2:c3] * cos + proj[:, c3:] * sin
    k = _dot(ckv, wuk_ref[...]) + jnp.concatenate([kr] * MLA_HEADS, axis=1)
    k_ref[...] = k.astype(k_ref.dtype)
    vt = lax.dot_general(wuv_ref[...], ckv, (((1,), (1,)), ((), ())), preferred_element_type=F32)
    v_ref[0] = (vt + vone_ref[...]).astype(v_ref.dtype)


def _odd_proj(z, win, gq, gkv, wuq, wuqs, wuk, wuv_t, v_one, cos_t, sin_t, seq, tm):
    n, d = z.shape
    hp = MLA_HEADS * MLA_PAD
    vr = MLA_HEADS * MLA_VROWS
    n_l = seq // tm
    const = lambda *shape: pl.BlockSpec(shape, lambda i: (0,) * len(shape))
    out = jax.ShapeDtypeStruct((n, hp), BF16)
    scale = float((MLA_NOPE + MLA_ROPE) ** -0.5 * math.log2(math.e))
    return pl.pallas_call(
        functools.partial(_odd_proj_kernel, scale=scale),
        grid=(n // tm,),
        in_specs=[pl.BlockSpec((tm, d), lambda i: (i, 0)),
                  const(d, win.shape[1]), const(1, MLA_Q_RANK), const(1, MLA_KV_RANK),
                  const(MLA_Q_RANK, hp), const(MLA_Q_RANK, hp), const(MLA_KV_RANK, hp), const(vr, MLA_KV_RANK),
                  const(vr, 1),
                  pl.BlockSpec((tm, MLA_PAD), lambda i: (i % n_l, 0)),
                  pl.BlockSpec((tm, MLA_PAD), lambda i: (i % n_l, 0))],
        out_specs=[pl.BlockSpec((tm, 2 * CONV_CH), lambda i: (i, 0)),
                   pl.BlockSpec((tm, hp), lambda i: (i, 0)),
                   pl.BlockSpec((tm, hp), lambda i: (i, 0)),
                   pl.BlockSpec((1, vr, tm), lambda i: (i, 0, 0))],
        out_shape=[jax.ShapeDtypeStruct((n, 2 * CONV_CH), BF16), out, out,
                   jax.ShapeDtypeStruct((n // tm, vr, tm), BF16)],
        compiler_params=_params(("parallel",)),
        name="odd_in_proj",
    )(z, win, gq, gkv, wuq, wuqs, wuk, wuv_t, v_one, cos_t, sin_t)


def _attn_kernel(q_ref, k_ref, vt_ref, wg_ref, wu_ref, wd_ref, o_ref, wgb_ref, wub_ref, wdb_ref, acc_ref, *, blk):
    i = pl.program_id(2)
    acc_ref[...] = jnp.zeros_like(acc_ref)
    tf = wgb_ref.shape[3]
    for f in range(wgb_ref.shape[1]):
        wgb_ref[0, f] = wg_ref[0, :, f * tf:(f + 1) * tf].astype(BF16)
        wub_ref[0, f] = wu_ref[0, :, f * tf:(f + 1) * tf].astype(BF16)
    wdb_ref[0] = wd_ref[0].astype(BF16)

    def step(j, m, masked):
        r0 = pl.multiple_of(j * blk, blk)
        scores = []
        for hh in range(ATTN_HEADS):
            q = q_ref[0, :, hh * MLA_PAD:(hh + 1) * MLA_PAD]
            k = k_ref[0, pl.ds(r0, blk), hh * MLA_PAD:(hh + 1) * MLA_PAD]
            st = lax.dot_general(k, q, (((1,), (1,)), ((), ())), preferred_element_type=F32)
            if masked:
                key = lax.broadcasted_iota(jnp.int32, st.shape, 0)
                qry = lax.broadcasted_iota(jnp.int32, st.shape, 1)
                st = jnp.where(key <= qry, st, -1e30)
            scores.append(st)
        soft = []
        for hh in range(ATTN_HEADS):
            m_new = jnp.maximum(m[hh], jnp.max(scores[hh], axis=0, keepdims=True))
            soft.append((m_new, jnp.exp2(m[hh] - m_new), jnp.exp2(scores[hh] - m_new).astype(BF16)))
        for hh in range(ATTN_HEADS):
            vt = vt_ref[j, hh * MLA_VROWS:(hh + 1) * MLA_VROWS, :]
            acc_ref[hh] = soft[hh][1] * acc_ref[hh] + _dot(vt, soft[hh][2])
        return tuple(s[0] for s in soft)

    init = jnp.full((1, blk), -1e30, F32)
    m = lax.fori_loop(0, i, lambda j, m: step(j, m, False), (init,) * ATTN_HEADS)
    step(i, m, True)
    ot = jnp.concatenate([acc_ref[hh][:MLA_V] / acc_ref[hh][MLA_V:MLA_V + 1] for hh in range(ATTN_HEADS)], axis=0)
    o_ref[0] = ot.T.astype(o_ref.dtype)


def _attention(q, k, vt, blk, wg, wu, wd, tf):
    b, seq, _ = q.shape
    n_blk = seq // blk
    n_pairs = MLA_HEADS // ATTN_HEADS
    n_e, d, ff = wg.shape
    steps = b * n_pairs * n_blk
    per_e = steps // n_e
    assert steps == per_e * n_e and d % per_e == 0 and ff % per_e == 0
    rows_in, rows_down = d // per_e, ff // per_e
    assert rows_in % 16 == 0 and rows_down % 16 == 0 and ff % tf == 0

    def lin(bi, p, i):
        return (bi * n_pairs + p) * n_blk + i

    w_in = pl.BlockSpec((1, rows_in, ff), lambda bi, p, i: (lin(bi, p, i) // per_e, lin(bi, p, i) % per_e, 0))
    w_out = pl.BlockSpec((1, ff // tf, rows_in, tf),
                         lambda bi, p, i: (lin(bi, p, i) // per_e, 0, lin(bi, p, i) % per_e, 0))
    w_down = pl.BlockSpec((1, rows_down, d), lambda bi, p, i: (lin(bi, p, i) // per_e, lin(bi, p, i) % per_e, 0))
    return pl.pallas_call(
        functools.partial(_attn_kernel, blk=blk),
        grid=(b, n_pairs, n_blk),
        in_specs=[pl.BlockSpec((1, blk, ATTN_HEADS * MLA_PAD), lambda bi, p, i: (bi, i, p)),
                  pl.BlockSpec((1, seq, ATTN_HEADS * MLA_PAD), lambda bi, p, i: (bi, 0, p)),
                  pl.BlockSpec((n_blk, ATTN_HEADS * MLA_VROWS, blk), lambda bi, p, i: (bi, p, 0)),
                  w_in, w_in, w_down],
        out_specs=[pl.BlockSpec((1, blk, ATTN_HEADS * MLA_V), lambda bi, p, i: (bi, i, p)), w_out, w_out, w_down],
        out_shape=[jax.ShapeDtypeStruct((b, seq, MLA_HEADS * MLA_V), BF16),
                   jax.ShapeDtypeStruct((n_e, ff // tf, d, tf), BF16),
                   jax.ShapeDtypeStruct((n_e, ff // tf, d, tf), BF16),
                   jax.ShapeDtypeStruct((n_e, ff, d), BF16)],
        scratch_shapes=[pltpu.VMEM((ATTN_HEADS, MLA_VROWS, blk), F32)],
        compiler_params=_params(("parallel", "parallel", "parallel")),
        name="mla_attention",
    )(q, k, vt, wg, wu, wd)


def _conv_kernel(zc_ref, w_ref, b_ref, lng_ref, lnb_ref, y_ref, buf_ref, part_ref):
    tm = zc_ref.shape[1]

    @pl.when(pl.program_id(1) == 0)
    def _():
        buf_ref[pl.ds(0, CONV_HALO), :] = jnp.zeros((CONV_HALO, CONV_CH), F32)
        buf_ref[pl.ds(CONV_HALO + tm, SUBLANES), :] = jnp.zeros((SUBLANES, CONV_CH), F32)

    zc = zc_ref[0].astype(F32)
    hh = zc[:, :CONV_CH] * jax.nn.sigmoid(zc[:, CONV_CH:])
    buf_ref[pl.ds(CONV_HALO, tm), :] = hh
    off = CONV_HALO - (CONV_TAPS - 1)
    acc = jnp.zeros((tm, CONV_CH), F32) + b_ref[...]
    for b in range(SUBLANES):
        taps = [k for k in range(CONV_TAPS) if (off + k) % SUBLANES == b]
        part = None
        for k in taps:
            term = w_ref[pl.ds(k, 1), :] * buf_ref[pl.ds(off + k - b, tm + SUBLANES), :]
            part = term if part is None else part + term
        if b == 0:
            acc = acc + part[:tm]
        else:
            part_ref[...] = part
            acc = acc + part_ref[pl.ds(b, tm), :]
    buf_ref[pl.ds(0, CONV_HALO), :] = buf_ref[pl.ds(tm, CONV_HALO), :]
    y_ref[0] = jax.nn.silu(_layer_norm(acc, lng_ref[...], lnb_ref[...])).astype(y_ref.dtype)


def _conv_mixer(zc, w, b, lng, lnb, tm):
    bsz, seq, _ = zc.shape
    const = lambda *shape: pl.BlockSpec(shape, lambda bi, i: (0,) * len(shape))
    return pl.pallas_call(
        _conv_kernel,
        grid=(bsz, seq // tm),
        in_specs=[pl.BlockSpec((1, tm, 2 * CONV_CH), lambda bi, i: (bi, i, 0)),
                  const(CONV_HALO, CONV_CH), const(1, CONV_CH), const(1, CONV_CH), const(1, CONV_CH)],
        out_specs=pl.BlockSpec((1, tm, CONV_CH), lambda bi, i: (bi, i, 0)),
        out_shape=jax.ShapeDtypeStruct((bsz, seq, CONV_CH), BF16),
        scratch_shapes=[pltpu.VMEM((CONV_HALO + tm + SUBLANES, CONV_CH), F32),
                        pltpu.VMEM((tm + SUBLANES, CONV_CH), F32)],
        compiler_params=_params(("arbitrary", "arbitrary")),
        name="conv_module",
    )(zc, w, b, lng, lnb)


def _odd_mix_kernel(yc_ref, yd_ref, h_ref, wo_a_ref, wo_b_ref, g1_ref, g2_ref, wr_ref, hout_ref, z_ref, route_ref):
    tm = h_ref.shape[0]
    halves = [slice(0, tm // 2), slice(tm // 2, tm)]
    mixes = [_dot(yc_ref[r, :], wo_a_ref[...]) + _dot(yd_ref[r, :], wo_b_ref[...]) for r in halves]
    zs = []
    for r, mix in zip(halves, mixes):
        h_new = h_ref[r, :] + _rms(mix, g1_ref[...])
        hout_ref[r, :] = h_new
        zs.append(_rms(h_new, g2_ref[...]))
    all_logits = [_dot(z.astype(BF16), wr_ref[...]) for z in zs]
    _store_row_tiles(z_ref, jnp.concatenate(zs, axis=0))
    neg = -jnp.inf
    for r, logits in zip(halves, all_logits):
        lane = lax.broadcasted_iota(jnp.int32, logits.shape, 1)
        logits = jnp.where(lane < N_EXPERTS, logits, neg)
        m1 = jnp.max(logits, axis=-1, keepdims=True)
        i1 = jnp.min(jnp.where(logits == m1, lane, LANES), axis=-1, keepdims=True)
        rest = jnp.where(lane == i1, neg, logits)
        m2 = jnp.max(rest, axis=-1, keepdims=True)
        i2 = jnp.min(jnp.where(rest == m2, lane, LANES), axis=-1, keepdims=True)
        e = jnp.exp(m2 - m1)
        w1 = 1.0 / (1.0 + e)
        w2 = e / (1.0 + e)
        route_ref[r, :] = jnp.where(lane == 0, i1.astype(F32),
                                    jnp.where(lane == 1, i2.astype(F32),
                                              jnp.where(lane == 2, w1, jnp.where(lane == 3, w2, 0.0))))


def _odd_mix(yc, yd, h, wo_a, wo_b, g1, g2, wr, tm):
    n, d = h.shape
    const = lambda *shape: pl.BlockSpec(shape, lambda i: (0,) * len(shape))
    return pl.pallas_call(
        _odd_mix_kernel,
        grid=(n // tm,),
        in_specs=[pl.BlockSpec((tm, yc.shape[1]), lambda i: (i, 0)),
                  pl.BlockSpec((tm, yd.shape[1]), lambda i: (i, 0)),
                  pl.BlockSpec((tm, d), lambda i: (i, 0)),
                  const(*wo_a.shape), const(*wo_b.shape), const(1, d), const(1, d), const(d, LANES)],
        out_specs=[pl.BlockSpec((tm, d), lambda i: (i, 0)),
                   pl.BlockSpec((tm * ROW_TILE, LANES), lambda i: (i, 0)),
                   pl.BlockSpec((tm, LANES), lambda i: (i, 0))],
        out_shape=[jax.ShapeDtypeStruct((n, d), F32), jax.ShapeDtypeStruct((n * ROW_TILE, LANES), F32),
                   jax.ShapeDtypeStruct((n, LANES), F32)],
        compiler_params=_params(("parallel",)),
        name="odd_mix_router",
    )(yc, yd, h, wo_a, wo_b, g1, g2, wr)


def _store_row_tiles(ref, x):
    rows = x.shape[0]
    for s in range(ROW_TILE):
        ref[pl.ds(s, rows, stride=ROW_TILE), :] = x[:, s * LANES:(s + 1) * LANES]


def _load_row_tiles(ref, rows):
    return [ref[pl.ds(s, rows, stride=ROW_TILE), :] for s in range(ROW_TILE)]


def _gather_rows(idx_ref, base, n_rows, src_hbm, dst_ref, sem):
    def body(r, c):
        src = pl.multiple_of(idx_ref[base + r] * ROW_TILE, ROW_TILE)
        dst = pl.multiple_of(r * ROW_TILE, ROW_TILE)
        pltpu.make_async_copy(src_hbm.at[pl.ds(src, ROW_TILE), :], dst_ref.at[pl.ds(dst, ROW_TILE), :], sem).start()
        return c

    lax.fori_loop(0, n_rows, body, 0, unroll=8)


def _wait_rows(src_hbm, dst_ref, sem):
    pltpu.make_async_copy(src_hbm.at[pl.ds(0, dst_ref.shape[0]), :], dst_ref, sem).wait()


def _row_copy(src_ref, src_row, dst_ref, dst_row, sem):
    src = pl.multiple_of(src_row * ROW_TILE, ROW_TILE)
    dst = pl.multiple_of(dst_row * ROW_TILE, ROW_TILE)
    return pltpu.make_async_copy(src_ref.at[pl.ds(src, ROW_TILE), :], dst_ref.at[pl.ds(dst, ROW_TILE), :], sem)


def _moe_ffn_kernel(te_ref, nu_ref, tok_ref, dst_ref, z_hbm, wg_ref, wu_ref, wd_ref, y_hbm,
                    xraw_ref, xb_ref, acc_ref, yst_ref, gsem, ssem, *, rows_per_step):
    i = pl.program_id(0)
    j = pl.program_id(1)
    tm = xb_ref.shape[0]
    stride = yst_ref.shape[0] // ROW_TILE
    n_used = nu_ref[0]
    slot = i % 2
    first = j == 0
    last = j == pl.num_programs(1) - 1

    @pl.when(first & (i == 0))
    def _():
        yst_ref[...] = jnp.zeros_like(yst_ref)
        _gather_rows(tok_ref, 0, stride, z_hbm, xraw_ref.at[0], gsem.at[0])

    @pl.when(first & (i <= n_used))
    def _():
        _wait_rows(z_hbm, xraw_ref.at[slot], gsem.at[slot])

    @pl.when(first & (i < n_used))
    def _():
        for s, blk in enumerate(_load_row_tiles(xraw_ref.at[slot], tm)):
            xb_ref[:, s * LANES:(s + 1) * LANES] = blk.astype(BF16)
        acc_ref[...] = jnp.zeros_like(acc_ref)

    @pl.when(first & (i == n_used))
    def _():
        def body(r, c):
            _row_copy(yst_ref, r, y_hbm, dst_ref[i * stride + r], ssem).start()
            return c
        lax.fori_loop(0, stride, body, 0, unroll=8)

    def multiply(rows, with_copies):
        x = xb_ref[:rows, :]
        g = _dot(x, wg_ref[0, 0])
        u = _dot(x, wu_ref[0, 0])
        if with_copies:
            nxt = xraw_ref.at[1 - slot]
            for rr in range(rows_per_step):
                r = j * rows_per_step + rr
                _row_copy(z_hbm, tok_ref[(i + 1) * stride + r], nxt, r, gsem.at[1 - slot]).start(priority=rr % 2)
                _row_copy(yst_ref, r, y_hbm, dst_ref[i * stride + r], ssem).start(priority=rr % 2)
        a = jax.nn.silu(g) * u
        acc_ref[:rows, :] += _dot(a.astype(BF16), wd_ref[0])

    used = i < n_used
    real = nu_ref[1 + i]
    quarter = tm // 4
    for rows in range(quarter, tm + 1, quarter):
        fits = (real > rows - quarter) & (real <= rows)

        @pl.when(used & fits & jnp.logical_not(last))
        def _(rows=rows):
            multiply(rows, True)

        @pl.when(used & fits & last)
        def _(rows=rows):
            multiply(rows, False)

    @pl.when(last & (i <= n_used))
    def _():
        _wait_rows(z_hbm, yst_ref, ssem)

    @pl.when(last & (i < n_used))
    def _():
        _store_row_tiles(yst_ref, acc_ref[...])


def _moe_ffn(tile_expert, n_used, tok_tab, dst_tab, z_tiles, wg, wu, wd, n_tok, tm):
    n_f, d, tf = wg.shape[1], wg.shape[2], wg.shape[3]
    copy_steps = n_f - 1
    rows_per_step = -(-tm // copy_steps)
    stride = copy_steps * rows_per_step
    n_tiles = tile_expert.shape[0]
    assert tok_tab.shape[0] == dst_tab.shape[0] == (n_tiles + 1) * stride

    def col(i, j, nu):
        return jnp.where(i < nu[0], j, n_f - 1)

    return pl.pallas_call(
        functools.partial(_moe_ffn_kernel, rows_per_step=rows_per_step),
        grid_spec=pltpu.PrefetchScalarGridSpec(
            num_scalar_prefetch=4,
            grid=(n_tiles, n_f),
            in_specs=[pl.BlockSpec(memory_space=pl.ANY),
                      pl.BlockSpec((1, 1, d, tf), lambda i, j, te, nu, tok, dst: (te[i], col(i, j, nu), 0, 0)),
                      pl.BlockSpec((1, 1, d, tf), lambda i, j, te, nu, tok, dst: (te[i], col(i, j, nu), 0, 0)),
                      pl.BlockSpec((1, tf, d), lambda i, j, te, nu, tok, dst: (te[i], col(i, j, nu), 0))],
            out_specs=pl.BlockSpec(memory_space=pl.ANY),
            scratch_shapes=[pltpu.VMEM((2, stride * ROW_TILE, LANES), F32), pltpu.VMEM((tm, d), BF16),
                            pltpu.VMEM((tm, d), F32), pltpu.VMEM((stride * ROW_TILE, LANES), F32),
                            pltpu.SemaphoreType.DMA((2,)), pltpu.SemaphoreType.DMA(())]),
        out_shape=jax.ShapeDtypeStruct(((2 * n_tok + stride) * ROW_TILE, LANES), F32),
        compiler_params=_params(("arbitrary", "arbitrary")),
        name="moe_grouped_ffn",
    )(tile_expert, n_used, tok_tab, dst_tab, z_tiles, wg, wu, wd)


def _combine_kernel(ya_ref, yb_ref, route_ref, h_ref, g_ref, o_ref):
    tm = h_ref.shape[0]
    route = route_ref[...]
    a = jnp.concatenate(_load_row_tiles(ya_ref, tm), axis=1)
    b = jnp.concatenate(_load_row_tiles(yb_ref, tm), axis=1)
    f = route[:, 2:3] * a + route[:, 3:4] * b
    o_ref[...] = h_ref[...] + _rms(f, g_ref[...])


def _combine(y, route, h, g, tm):
    n, d = h.shape
    n_blk = n // tm
    return pl.pallas_call(
        _combine_kernel,
        grid=(n_blk,),
        in_specs=[pl.BlockSpec((tm * ROW_TILE, LANES), lambda i: (i, 0)),
                  pl.BlockSpec((tm * ROW_TILE, LANES), lambda i: (n_blk + i, 0)),
                  pl.BlockSpec((tm, LANES), lambda i: (i, 0)),
                  pl.BlockSpec((tm, d), lambda i: (i, 0)),
                  pl.BlockSpec((1, d), lambda i: (0, 0))],
        out_specs=pl.BlockSpec((tm, d), lambda i: (i, 0)),
        out_shape=jax.ShapeDtypeStruct((n, d), F32),
        compiler_params=_params(("parallel",)),
        name="moe_combine",
    )(y, y, route, h, g)


def _invert_slots_kernel(slot_ref, out_ref):
    def fill(s, c):
        out_ref[s] = jnp.int32(-1)
        return c

    lax.fori_loop(0, out_ref.shape[0], fill, 0, unroll=8)

    def place(p, c):
        out_ref[slot_ref[p]] = p
        return c

    lax.fori_loop(0, slot_ref.shape[0], place, 0, unroll=8)


def _invert_slots(slot, n_slots):
    return pl.pallas_call(
        _invert_slots_kernel,
        in_specs=[pl.BlockSpec(memory_space=pltpu.SMEM)],
        out_specs=pl.BlockSpec(memory_space=pltpu.SMEM),
        out_shape=jax.ShapeDtypeStruct((n_slots,), jnp.int32),
        name="moe_invert_slots",
    )(slot)


def _moe_plan(route, tm, stride):
    n = route.shape[0]
    eids = jnp.concatenate([route[:, 0], route[:, 1]]).astype(jnp.int32)
    onehot = (eids[:, None] == jnp.arange(N_EXPERTS, dtype=jnp.int32)[None, :]).astype(jnp.int32)
    csum = jnp.cumsum(onehot, axis=0)
    rank = jnp.sum(csum * onehot, axis=1) - 1
    counts = csum[-1]
    padded = ((counts + tm - 1) // tm) * tm
    ends = jnp.cumsum(padded)
    starts = ends - padded
    slot = jnp.sum(onehot * starts[None, :], axis=1) + rank
    n_tiles = 2 * n // tm + N_EXPERTS + 1
    copy_of_slot = _invert_slots(slot, n_tiles * tm)
    copy_tab = jnp.pad(copy_of_slot.reshape(n_tiles, tm), ((0, 1), (0, stride - tm)), constant_values=-1)
    tok_tab = jnp.where(copy_tab >= 0, copy_tab % n, 0)
    dump = 2 * n + jnp.arange(stride, dtype=jnp.int32)[None, :]
    dst_tab = jnp.where(copy_tab >= 0, copy_tab, dump)
    dst_tab = jnp.concatenate([jnp.broadcast_to(dump, (1, stride)), dst_tab[:-1]], axis=0)
    n_used = (ends[-1] // tm).astype(jnp.int32)
    tile_start = jnp.minimum(jnp.arange(n_tiles, dtype=jnp.int32), n_used - 1) * tm
    tile_expert = jnp.sum((tile_start[:, None] >= ends[None, :]).astype(jnp.int32), axis=1)
    onehot_e = (tile_expert[:, None] == jnp.arange(N_EXPERTS, dtype=jnp.int32)[None, :]).astype(jnp.int32)
    run_end = jnp.sum(onehot_e * (starts + counts)[None, :], axis=1)
    tile_rows = jnp.clip(run_end - jnp.arange(n_tiles, dtype=jnp.int32) * tm, 0, tm)
    return tok_tab.reshape(-1), dst_tab.reshape(-1), tile_expert, jnp.concatenate([n_used.reshape(1), tile_rows])


def _odd_weights(od_w_in, mla_w_uq, mla_w_ukv):
    c2 = 2 * CONV_CH + MLA_Q_RANK + MLA_KV_RANK
    half = MLA_ROPE // 2
    w_kr = od_w_in[:, c2:]
    w_kr_sw = jnp.concatenate([w_kr[:, half:], w_kr[:, :half]], axis=1)
    zl = jnp.zeros((D_MODEL, MLA_NOPE), F32)
    zr = jnp.zeros((D_MODEL, MLA_PAD - MLA_NOPE - MLA_ROPE), F32)
    win = jnp.concatenate([od_w_in[:, :c2], zl, w_kr, zr, zl, w_kr_sw, zr], axis=1)
    dk = MLA_NOPE + MLA_ROPE
    wq = mla_w_uq.reshape(MLA_Q_RANK, MLA_HEADS, dk)
    zq = jnp.zeros((MLA_Q_RANK, MLA_HEADS, MLA_PAD - dk), F32)
    wuq = jnp.concatenate([wq, zq], axis=2).reshape(MLA_Q_RANK, MLA_HEADS * MLA_PAD)
    wq_sw = jnp.concatenate([jnp.zeros_like(wq[:, :, :MLA_NOPE]), wq[:, :, MLA_NOPE + half:],
                             wq[:, :, MLA_NOPE:MLA_NOPE + half], zq], axis=2)
    wuqs = wq_sw.reshape(MLA_Q_RANK, MLA_HEADS * MLA_PAD)
    wkv = mla_w_ukv.reshape(MLA_KV_RANK, MLA_HEADS, MLA_NOPE + MLA_V)
    zk = jnp.zeros((MLA_KV_RANK, MLA_HEADS, MLA_PAD - MLA_NOPE), F32)
    wuk = jnp.concatenate([wkv[:, :, :MLA_NOPE], zk], axis=2).reshape(MLA_KV_RANK, MLA_HEADS * MLA_PAD)
    zv = jnp.zeros((MLA_KV_RANK, MLA_HEADS, MLA_VROWS - MLA_V), F32)
    wuv_t = jnp.concatenate([wkv[:, :, MLA_NOPE:], zv], axis=2).reshape(MLA_KV_RANK, MLA_HEADS * MLA_VROWS).T
    v_one = jnp.zeros((MLA_HEADS, MLA_VROWS), F32).at[:, MLA_V].set(1.0).reshape(MLA_HEADS * MLA_VROWS, 1)
    return win.astype(BF16), wuq.astype(BF16), wuqs.astype(BF16), wuk.astype(BF16), wuv_t.astype(BF16), v_one


def _rope_tables(seq):
    inv = 1.0 / (ROPE_THETA ** (jnp.arange(0, MLA_ROPE, 2, dtype=F32) / MLA_ROPE))
    ang = jnp.arange(seq, dtype=F32)[:, None] * inv[None, :]
    cos, sin = jnp.cos(ang), jnp.sin(ang)
    ones = jnp.ones((seq, MLA_NOPE), F32)
    zl = jnp.zeros((seq, MLA_NOPE), F32)
    zr = jnp.zeros((seq, MLA_PAD - MLA_NOPE - MLA_ROPE), F32)
    return (jnp.concatenate([ones, cos, cos, zr], axis=1), jnp.concatenate([zl, -sin, sin, zr], axis=1))


def kernel(x, norm_g, ev_w_in, ssm_lambda_re, ssm_lambda_im, ssm_log_dt, ssm_b_re, ssm_b_im, ssm_c_re, ssm_c_im, ssm_d, ssm_w_glu, sgu_ln_g, sgu_ln_b, sgu_w, sgu_b, ev_w_out, ffn_w_gate, ffn_w_up, ffn_w_down, od_w_in, conv_w, conv_b, conv_ln_g, conv_ln_b, mla_q_norm_g, mla_w_uq, mla_kv_norm_g, mla_w_ukv, od_w_out, moe_w_router, moe_w_gate, moe_w_up, moe_w_down):
    bsz, seq, d = x.shape
    n = bsz * seq
    assert d == D_MODEL and SUBLANES % bsz == 0 and seq % 512 == 0
    row = lambda v: v.astype(F32).reshape(1, -1)
    h = x.astype(F32).reshape(n, d)
    tm = 512

    g = norm_g[0]
    a_in, proj = _norm_proj(h, row(g[0]), ev_w_in[0].astype(BF16), 2 * tm)
    mats = _s5_matrices(ssm_lambda_re[0], ssm_lambda_im[0], ssm_log_dt[0], ssm_b_re[0], ssm_b_im[0],
                        ssm_c_re[0], ssm_c_im[0])
    ys = _s5_mixer(a_in, mats, ssm_d[0], bsz, seq)
    causal = jnp.tril(jnp.ones((SGU_CHUNK, SGU_CHUNK), dtype=bool))
    ws = jnp.where(causal[None], sgu_w[0], 0.0).astype(BF16)
    bias = jnp.repeat(sgu_b[0].astype(F32).T, SGU_HEAD_DIM, axis=1)
    wo = ev_w_out[0].astype(BF16)
    h, z = _even_mix(ys, proj, h, ssm_w_glu[0].astype(BF16), row(sgu_ln_g[0]), row(sgu_ln_b[0]), ws, bias,
                     wo[:SSM_WIDTH], wo[SSM_WIDTH:], row(g[1]), row(g[2]), tm)
    h, z = _dense_ffn(z, ffn_w_gate[0].astype(BF16), ffn_w_up[0].astype(BF16), ffn_w_down[0].astype(BF16),
                      h, row(g[3]), row(norm_g[1][0]), 1024, 1024)

    g = norm_g[1]
    win, wuq, wuqs, wuk, wuv_t, v_one = _odd_weights(od_w_in[0], mla_w_uq[0], mla_w_ukv[0])
    cos_t, sin_t = _rope_tables(seq)
    zc, q, k, vt = _odd_proj(z, win, row(mla_q_norm_g[0]), row(mla_kv_norm_g[0]), wuq, wuqs, wuk, wuv_t, v_one,
                             cos_t, sin_t, seq, tm)
    hp = MLA_HEADS * MLA_PAD
    tm_moe, tf_moe = 1024, 512
    yd, wg_b, wu_b, wd_b = _attention(q.reshape(bsz, seq, hp), k.reshape(bsz, seq, hp), vt, tm,
                                      moe_w_gate[0], moe_w_up[0], moe_w_down[0], tf_moe)
    conv_w_pad = jnp.concatenate([conv_w[0].astype(F32), jnp.zeros((CONV_HALO - CONV_TAPS, CONV_CH), F32)], axis=0)
    yc = _conv_mixer(zc.reshape(bsz, seq, 2 * CONV_CH), conv_w_pad, row(conv_b[0]), row(conv_ln_g[0]),
                     row(conv_ln_b[0]), tm)
    wo = od_w_out[0].astype(BF16)
    wr = jnp.concatenate([moe_w_router[0].astype(F32), jnp.zeros((d, LANES - N_EXPERTS), F32)], axis=1)
    h, z, route = _odd_mix(yc.reshape(n, CONV_CH), yd.reshape(n, MLA_HEADS * MLA_V), h, wo[:CONV_CH], wo[CONV_CH:],
                           row(g[1]), row(g[2]), wr.astype(BF16), 2 * tm)
    copy_steps = wg_b.shape[1] - 1
    tok_tab, dst_tab, tile_expert, n_used = _moe_plan(route, tm_moe, copy_steps * -(-tm_moe // copy_steps))
    y = _moe_ffn(tile_expert, n_used, tok_tab, dst_tab, z, wg_b, wu_b, wd_b, n, tm_moe)
    h = _combine(y, route, h, row(g[3]), 2 * tm)
    return h.reshape(bsz, seq, d).astype(x.dtype)
```
